```python
import jax
import jax.numpy as jnp
from jax import lax
import numpy as np

D_MODEL = 1024
BATCH = 16
SEQ = 2048
DEPTH = 1

GRID_W = 64
CTX_LEN = 256
NA_HEADS = 8
NA_HEAD_DIM = 64
NA_WIDTH = NA_HEADS * NA_HEAD_DIM
NA_WIN_H = 8
NA_WIN_W = 16
NA_QBLOCK = 16
NA_KBLOCK = NA_QBLOCK + NA_WIN_W
GLA_HEADS = 4
GLA_DK = D_MODEL // 2
GLA_DV = D_MODEL
GLA_HK = GLA_DK // GLA_HEADS
GLA_HV = GLA_DV // GLA_HEADS
GLA_GATE_RANK = 16
GLA_GATE_NORM = 16.0
GLA_CHUNK = 64
ROPE_BASE = 10000.0
N_GROUPS = 4
EXPERTS_PER_GROUP = 8
N_EXPERTS = N_GROUPS * EXPERTS_PER_GROUP
TOP_K = 2
D_EXPERT = D_MODEL // 2
MOE_BLOCK = 128
D_IN = 3 * NA_WIDTH + 2 * GLA_DK + 2 * GLA_DV + 2 * GLA_GATE_RANK + 2 * D_MODEL
EPS = 1e-6
NEG_INF = -1e30

kernel_name = "hybrid_na_gla_hmoe_dit_block"


def rmsnorm(x, g):
    xf = x.astype(jnp.float32)
    xf = xf * lax.rsqrt(jnp.mean(xf * xf, axis=-1, keepdims=True) + EPS)
    return xf.astype(x.dtype) * g


def split_proj(p):
    sizes = (NA_WIDTH, NA_WIDTH, NA_WIDTH, GLA_DK, GLA_DK, GLA_DV, GLA_DV, 2 * GLA_GATE_RANK, D_MODEL, D_MODEL)
    return jnp.split(p, np.cumsum(sizes)[:-1].tolist(), axis=-1)


def heads(t, n):
    return t.reshape(t.shape[0], t.shape[1], n, t.shape[-1] // n)


def _rotate_half(x, ang):
    m = x.shape[-1] // 2
    cos = jnp.cos(ang)[:, None, :].astype(x.dtype)
    sin = jnp.sin(ang)[:, None, :].astype(x.dtype)
    x1, x2 = x[..., :m], x[..., m:]
    return jnp.concatenate([x1 * cos - x2 * sin, x1 * sin + x2 * cos], axis=-1)


def rope_2d(x, pos_r, pos_c):
    half = x.shape[-1] // 2
    nf = half // 2
    inv = ROPE_BASE ** (-jnp.arange(nf, dtype=jnp.float32) / nf)
    return jnp.concatenate([_rotate_half(x[..., :half], pos_r[:, None] * inv),
                            _rotate_half(x[..., half:], pos_c[:, None] * inv)], axis=-1)


def na_latent(q, k, v, k_ctx, v_ctx, rpb):
    B_, S, H, dh = q.shape
    rows = S // GRID_W
    kh = min(NA_WIN_H, rows)
    nb = GRID_W // NA_QBLOCK
    kc0 = [min(max(j * NA_QBLOCK - NA_WIN_W // 2, 0), GRID_W - NA_KBLOCK) for j in range(nb)]
    qcol = np.arange(GRID_W).reshape(nb, NA_QBLOCK)
    kcol = np.array(kc0)[:, None] + np.arange(NA_KBLOCK)[None, :]
    cstart = np.clip(qcol - NA_WIN_W // 2, 0, GRID_W - NA_WIN_W)
    col_valid = (kcol[:, None, :] >= cstart[:, :, None]) & (kcol[:, None, :] < cstart[:, :, None] + NA_WIN_W)
    col_idx = np.clip(kcol[:, None, :] - qcol[:, :, None] + NA_WIN_W - 1, 0, 2 * NA_WIN_W - 2)
    col_valid = jnp.asarray(col_valid)[:, :, None, :]
    scale = dh ** -0.5
    k_grid = k.reshape(B_, rows, GRID_W, H, dh)
    v_grid = v.reshape(B_, rows, GRID_W, H, dh)
    q_rows = q.reshape(B_, rows, nb, NA_QBLOCK, H, dh).transpose(1, 0, 2, 3, 4, 5)

    def row_block(inp):
        r, q_r = inp
        rs = jnp.clip(r - kh // 2, 0, rows - kh)
        k_r = lax.dynamic_slice_in_dim(k_grid, rs, kh, axis=1)
        v_r = lax.dynamic_slice_in_dim(v_grid, rs, kh, axis=1)
        k_blk = jnp.stack([k_r[:, :, c0:c0 + NA_KBLOCK] for c0 in kc0], axis=1)
        v_blk = jnp.stack([v_r[:, :, c0:c0 + NA_KBLOCK] for c0 in kc0], axis=1)
        row_idx = rs + jnp.arange(kh) - r + NA_WIN_H - 1
        bias = rpb[:, row_idx][:, :, col_idx].transpose(0, 2, 3, 1, 4)
        s_win = jnp.einsum('bjqhd,bjikhd->bhjqik', q_r, k_blk).astype(jnp.float32) * scale
        s_win = jnp.where(col_valid, s_win + bias.astype(jnp.float32), NEG_INF)
        s_win = s_win.reshape(B_, H, nb, NA_QBLOCK, kh * NA_KBLOCK)
        s_ctx = jnp.einsum('bjqhd,bkhd->bhjqk', q_r, k_ctx).astype(jnp.float32) * scale
        p = jax.nn.softmax(jnp.concatenate([s_win, s_ctx], axis=-1), axis=-1).astype(v.dtype)
        p_win = p[..., :kh * NA_KBLOCK].reshape(B_, H, nb, NA_QBLOCK, kh, NA_KBLOCK)
        p_ctx = p[..., kh * NA_KBLOCK:]
        return (jnp.einsum('bhjqik,bjikhd->bjqhd', p_win, v_blk)
                + jnp.einsum('bhjqk,bkhd->bjqhd', p_ctx, v_ctx))

    o = lax.map(row_block, (jnp.arange(rows), q_rows))
    return o.transpose(1, 0, 2, 3, 4, 5).reshape(B_, S, H * dh)


def na_context(q, k, v):
    B_, T, H, dh = q.shape
    s = jnp.einsum('bqhd,bkhd->bhqk', q, k).astype(jnp.float32) * dh ** -0.5
    p = jax.nn.softmax(s, axis=-1).astype(v.dtype)
    return jnp.einsum('bhqk,bkhd->bqhd', p, v).reshape(B_, T, H * dh)


def gla_chunked(q, k, v, g, s0):
    B_, T, H, _ = q.shape
    dv = v.shape[-1]
    n = T // GLA_CHUNK

    def to_chunks(a):
        return a.astype(jnp.float32).reshape(B_, n, GLA_CHUNK, H, a.shape[-1]).transpose(1, 0, 3, 2, 4)

    tril = jnp.tril(jnp.ones((GLA_CHUNK, GLA_CHUNK), dtype=bool))

    def step(s, inp):
        qc, kc, vc, gc = inp
        b = jnp.cumsum(gc, axis=-2)
        b_last = b[..., -1:, :]
        qe = qc * jnp.exp(b)
        ke = kc * jnp.exp(-b)
        att = jnp.where(tril, jnp.einsum('bhtd,bhsd->bhts', qe, ke), 0.0)
        o = jnp.einsum('bhts,bhse->bhte', att, vc) + jnp.einsum('bhtd,bhde->bhte', qe, s)
        s = (jnp.exp(b_last[..., 0, :])[..., None] * s
             + jnp.einsum('bhsd,bhse->bhde', kc * jnp.exp(b_last - b), vc))
        return s, o

    s_fin, o = lax.scan(step, s0, (to_chunks(q), to_chunks(k), to_chunks(v), to_chunks(g)))
    return o.transpose(1, 0, 3, 2, 4).reshape(B_, T, H, dv), s_fin


def gla_bidir(q, k, v, g_f, g_b, s_f0, s_b0):
    o_f, s_f = gla_chunked(q, k, v, g_f, s_f0)
    o_b, s_b = gla_chunked(q[:, ::-1], k[:, ::-1], v[:, ::-1], g_b[:, ::-1], s_b0)
    return o_f + o_b[:, ::-1], s_f, s_b


def hybrid_mixer(a_lat, a_ctx, pos_r, pos_c, w_in, w_gla_a2, b_gla_a2, gla_norm_g, na_rpb,
                 w_na_o, w_gla_o, w_out, need_ctx_out):
    B_ = a_lat.shape[0]
    P_lat = split_proj(a_lat @ w_in)
    P_ctx = split_proj(a_ctx @ w_in)

    def gla_streams(P, rotary):
        q = heads(P[3], GLA_HEADS) * (GLA_HK ** -0.5)
        k = heads(P[4], GLA_HEADS)
        if rotary:
            q = rope_2d(q, pos_r, pos_c)
            k = rope_2d(k, pos_r, pos_c)
        v = heads(P[5], GLA_HEADS)
        a = P[7]
        z_f = (a[..., :GLA_GATE_RANK] @ w_gla_a2[0] + b_gla_a2[0]).astype(jnp.float32)
        z_b = (a[..., GLA_GATE_RANK:] @ w_gla_a2[1] + b_gla_a2[1]).astype(jnp.float32)
        g_f = heads(jax.nn.log_sigmoid(z_f) / GLA_GATE_NORM, GLA_HEADS)
        g_b = heads(jax.nn.log_sigmoid(z_b) / GLA_GATE_NORM, GLA_HEADS)
        return q, k, v, g_f, g_b

    def gla_readout(o, P):
        o = rmsnorm(o.astype(P[6].dtype), gla_norm_g) * jax.nn.silu(heads(P[6], GLA_HEADS))
        return o.reshape(o.shape[0], o.shape[1], GLA_DV)

    def merge(P, o_na, o_gla):
        y = jax.nn.sigmoid(P[8]) * (o_na @ w_na_o) + jax.nn.sigmoid(P[9]) * (o_gla @ w_gla_o)
        return y @ w_out

    q_na, k_na, v_na = [heads(t, NA_HEADS) for t in P_lat[:3]]
    qc_na, kc_na, vc_na = [heads(t, NA_HEADS) for t in P_ctx[:3]]
    o_na = na_latent(q_na, k_na, v_na, kc_na, vc_na, na_rpb)

    s0 = jnp.zeros((B_, GLA_HEADS, GLA_HK, GLA_HV), jnp.float32)
    oc_gla, s_f, s_b = gla_bidir(*gla_streams(P_ctx, False), s0, s0)
    o_gla, _, _ = gla_bidir(*gla_streams(P_lat, True), s_f, s_b)

    y_lat = merge(P_lat, o_na, gla_readout(o_gla, P_lat))
    y_ctx = None
    if need_ctx_out:
        y_ctx = merge(P_ctx, na_context(qc_na, kc_na, vc_na), gla_readout(oc_gla, P_ctx))
    return y_lat, y_ctx


def hier_moe(x2d, w_group, b_group, w_expert, b_expert, w_gate, w_up, w_down):
    T, D = x2d.shape
    lg = (x2d @ w_group).astype(jnp.float32) + b_group
    grp = jnp.argmax(lg, axis=-1)
    p_grp = jnp.take_along_axis(jax.nn.softmax(lg, axis=-1), grp[:, None], axis=-1)
    le = ((x2d @ w_expert).astype(jnp.float32) + b_expert).reshape(T, N_GROUPS, EXPERTS_PER_GROUP)
    le_g = jnp.take_along_axis(le, grp[:, None, None], axis=1)[:, 0]
    top_p, top_i = lax.top_k(jax.nn.softmax(le_g, axis=-1), TOP_K)
    weights = p_grp * top_p / jnp.sum(top_p, axis=-1, keepdims=True)
    expert_id = grp[:, None] * EXPERTS_PER_GROUP + top_i

    ids = expert_id.reshape(-1).astype(jnp.int32)
    wts = weights.reshape(-1)
    tok = jnp.repeat(jnp.arange(T, dtype=jnp.int32), TOP_K)
    order = jnp.argsort(ids)
    ids_s, tok_s, w_s = ids[order], tok[order], wts[order]
    counts = jax.ops.segment_sum(jnp.ones_like(ids), ids, num_segments=N_EXPERTS)
    padded = (counts + MOE_BLOCK - 1) // MOE_BLOCK * MOE_BLOCK
    start = jnp.cumsum(counts) - counts
    pend = jnp.cumsum(padded)
    pstart = pend - padded
    dest = pstart[ids_s] + (jnp.arange(T * TOP_K, dtype=jnp.int32) - start[ids_s])
    n_pad = (T * TOP_K + N_EXPERTS * (MOE_BLOCK - 1) + MOE_BLOCK - 1) // MOE_BLOCK * MOE_BLOCK
    n_blk = n_pad // MOE_BLOCK
    tok_buf = jnp.full((n_pad,), T, jnp.int32).at[dest].set(tok_s)
    w_buf = jnp.zeros((n_pad,), jnp.float32).at[dest].set(w_s)
    blk_start = jnp.arange(n_blk, dtype=jnp.int32) * MOE_BLOCK
    blk_expert = jnp.clip(jnp.searchsorted(pend, blk_start, side='right'), 0, N_EXPERTS - 1)
    x_pad = jnp.concatenate([x2d, jnp.zeros((1, D), x2d.dtype)], axis=0)
    xb = x_pad[tok_buf].reshape(n_blk, MOE_BLOCK, D)

    def expert_block(inp):
        xe, e = inp
        hdn = jax.nn.silu(xe @ w_gate[e]) * (xe @ w_up[e])
        return hdn @ w_down[e]

    yb = lax.map(expert_block, (xb, blk_expert)).reshape(n_pad, D)
    y = jax.ops.segment_sum(yb * w_buf[:, None].astype(yb.dtype), tok_buf, num_segments=T + 1)
    return y[:T]


def setup_inputs(seed: int = 0) -> dict:
    key = jax.random.key(seed)
    ks = jax.random.split(key, 24)
    f32 = jnp.float32
    L = DEPTH

    def nrm(k, shape, scale):
        return jax.random.normal(k, shape, f32) * scale

    return {
        "x": nrm(ks[0], (BATCH, SEQ, D_MODEL), 1.0),
        "c": nrm(ks[1], (BATCH, D_MODEL), 1.0),
        "ctx": nrm(ks[2], (BATCH, CTX_LEN, D_MODEL), 1.0),
        "c_ctx": nrm(ks[3], (D_MODEL,), 1.0),
        "w_mod": nrm(ks[4], (L, D_MODEL, 6 * D_MODEL), 0.5 * D_MODEL ** -0.5),
        "b_mod": nrm(ks[5], (L, 6 * D_MODEL), 0.02),
        "norm_attn_g": 1.0 + nrm(ks[6], (L, D_MODEL), 0.02),
        "norm_ffn_g": 1.0 + nrm(ks[7], (L, D_MODEL), 0.02),
        "w_in": nrm(ks[8], (L, D_MODEL, D_IN), D_MODEL ** -0.5),
        "w_gla_a2": nrm(ks[9], (L, 2, GLA_GATE_RANK, GLA_DK), GLA_GATE_RANK ** -0.5),
        "b_gla_a2": nrm(ks[10], (L, 2, GLA_DK), 0.1),
        "gla_norm_g": 1.0 + nrm(ks[11], (L, GLA_HV), 0.02),
        "na_rpb": nrm(ks[12], (L, NA_HEADS, 2 * NA_WIN_H - 1, 2 * NA_WIN_W - 1), 0.1),
        "w_na_o": nrm(ks[13], (L, NA_WIDTH, D_MODEL), NA_WIDTH ** -0.5),
        "w_gla_o": nrm(ks[14], (L, GLA_DV, D_MODEL), GLA_DV ** -0.5),
        "w_out": nrm(ks[15], (L, D_MODEL, D_MODEL), D_MODEL ** -0.5),
        "w_group": nrm(ks[16], (L, D_MODEL, N_GROUPS), D_MODEL ** -0.5),
        "b_group": nrm(ks[17], (L, N_GROUPS), 0.01),
        "w_expert": nrm(ks[18], (L, D_MODEL, N_EXPERTS), D_MODEL ** -0.5),
        "b_expert": nrm(ks[19], (L, N_EXPERTS), 0.01),
        "w_exp_gate": nrm(ks[20], (L, N_EXPERTS, D_MODEL, D_EXPERT), D_MODEL ** -0.5),
        "w_exp_up": nrm(ks[21], (L, N_EXPERTS, D_MODEL, D_EXPERT), D_MODEL ** -0.5),
        "w_exp_down": nrm(ks[22], (L, N_EXPERTS, D_EXPERT, D_MODEL), D_EXPERT ** -0.5),
        "final_norm_g": 1.0 + nrm(ks[23], (D_MODEL,), 0.02),
    }


def reference(x, c, ctx, c_ctx, w_mod, b_mod, norm_attn_g, norm_ffn_g, w_in, w_gla_a2, b_gla_a2,
              gla_norm_g, na_rpb, w_na_o, w_gla_o, w_out, w_group, b_group, w_expert, b_expert,
              w_exp_gate, w_exp_up, w_exp_down, final_norm_g):
    B_, S, D = x.shape
    t = jnp.arange(S)
    pos_r = (t // GRID_W).astype(jnp.float32)
    pos_c = (t % GRID_W).astype(jnp.float32)
    h, hc = x, ctx
    for l in range(DEPTH):
        ctx_continues = l < DEPTH - 1
        mod = jax.nn.silu(c) @ w_mod[l] + b_mod[l]
        sh1, sc1, g1, sh2, sc2, g2 = jnp.split(mod[:, None, :], 6, axis=-1)
        csh1, csc1, cg1, csh2, csc2, cg2 = jnp.split(jax.nn.silu(c_ctx) @ w_mod[l] + b_mod[l], 6, axis=-1)
        a_lat = rmsnorm(h, norm_attn_g[l]) * (1 + sc1) + sh1
        a_ctx = rmsnorm(hc, norm_attn_g[l]) * (1 + csc1) + csh1
        y_lat, y_ctx = hybrid_mixer(a_lat, a_ctx, pos_r, pos_c, w_in[l], w_gla_a2[l], b_gla_a2[l],
                                    gla_norm_g[l], na_rpb[l], w_na_o[l], w_gla_o[l], w_out[l],
                                    ctx_continues)
        h = h + g1 * y_lat
        f_lat = rmsnorm(h, norm_ffn_g[l]) * (1 + sc2) + sh2
        h = h + g2 * hier_moe(f_lat.reshape(-1, D), w_group[l], b_group[l], w_expert[l], b_expert[l],
                              w_exp_gate[l], w_exp_up[l], w_exp_down[l]).reshape(B_, S, D)
        if ctx_continues:
            hc = hc + cg1 * y_ctx
            f_ctx = rmsnorm(hc, norm_ffn_g[l]) * (1 + csc2) + csh2
            hc = hc + cg2 * hier_moe(f_ctx.reshape(-1, D), w_group[l], b_group[l], w_expert[l], b_expert[l],
                                     w_exp_gate[l], w_exp_up[l], w_exp_down[l]).reshape(hc.shape)
    return rmsnorm(h, final_norm_g)
```

```python
import functools

import jax
import jax.numpy as jnp
import numpy as np
from jax import lax
from jax.experimental import pallas as pl
from jax.experimental.pallas import tpu as pltpu

F32 = jnp.float32
BF16 = jnp.bfloat16

D_MODEL = 1024
GRID_W = 64
NA_HEADS = 8
NA_HEAD_DIM = 64
NA_WIDTH = NA_HEADS * NA_HEAD_DIM
NA_WIN_H = 8
NA_WIN_W = 16
GLA_HEADS = 4
GLA_DK = D_MODEL // 2
GLA_DV = D_MODEL
GLA_HK = GLA_DK // GLA_HEADS
GLA_HV = GLA_DV // GLA_HEADS
GLA_GATE_RANK = 16
GLA_GATE_NORM = 16.0
GLA_CHUNK = 64
ROPE_BASE = 10000.0
N_GROUPS = 4
EXPERTS_PER_GROUP = 8
N_EXPERTS = N_GROUPS * EXPERTS_PER_GROUP
D_EXPERT = D_MODEL // 2
EPS = 1e-6
NEG_INF = -1e30

LANES = 128
LR_PAD = LANES
ROUTE_LANES = LANES
EXPERT_LANE0 = N_GROUPS
MOE_ROWS = 256
VMEM_LIMIT = 56 * 1024 * 1024

_NT = (((1,), (1,)), ((), ()))
_TN = (((0,), (0,)), ((), ()))


def _params(sem, vmem=VMEM_LIMIT):
    return pltpu.CompilerParams(dimension_semantics=sem, vmem_limit_bytes=vmem)


def _mod_kernel(c_ref, w_ref, b_ref, o_ref):
    c = c_ref[...]
    s = c * jax.nn.sigmoid(c)
    o_ref[...] = jnp.dot(s.astype(BF16), w_ref[...].astype(BF16),
                         preferred_element_type=F32) + b_ref[...]


def _modulation(cs, w_mod, b_mod):
    rows, d = cs.shape
    n = w_mod.shape[1]
    bn = 1024
    return pl.pallas_call(
        _mod_kernel,
        grid=(n // bn,),
        in_specs=[pl.BlockSpec((rows, d), lambda j: (0, 0)),
                  pl.BlockSpec((d, bn), lambda j: (0, j)),
                  pl.BlockSpec((1, bn), lambda j: (0, j))],
        out_specs=pl.BlockSpec((rows, bn), lambda j: (0, j)),
        out_shape=jax.ShapeDtypeStruct((rows, n), F32),
        compiler_params=_params(("arbitrary",)),
        name="mod",
    )(cs, w_mod, b_mod.reshape(1, n))


def _inproj_kernel(specs, use_rope, n_out, x_ref, mod_ref, g_ref, *rest):
    if use_rope:
        cos_ref, sin_ref, w_ref = rest[:3]
        rest = rest[3:]
    else:
        w_ref = rest[0]
        rest = rest[1:]
    outs = rest[:n_out]
    a_scr = rest[n_out]
    x = x_ref[...]
    ms = jnp.mean(x * x, axis=-1, keepdims=True)
    a = x * lax.rsqrt(ms + EPS) * g_ref[...] * (1.0 + mod_ref[0, 1:2, :]) + mod_ref[0, 0:1, :]
    a_scr[...] = a.astype(BF16)
    for (oi, oc, wc, width, kind) in specs:
        acc = jnp.dot(a_scr[...], w_ref[:, wc:wc + width], preferred_element_type=F32)
        if kind in ("rope", "rope_scaled"):
            if kind == "rope_scaled":
                acc = acc * (GLA_HK ** -0.5)
            lane = lax.broadcasted_iota(jnp.int32, acc.shape, 1)
            rot = jnp.where((lane & 63) < 32,
                            pltpu.roll(acc, width - 32, axis=1),
                            pltpu.roll(acc, 32, axis=1))
            acc = acc * cos_ref[...] + rot * sin_ref[...]
        elif kind == "silu":
            acc = acc * jax.nn.sigmoid(acc)
        elif kind == "sigmoid":
            acc = jax.nn.sigmoid(acc)
        outs[oi][:, oc:oc + width] = acc.astype(outs[oi].dtype)


def _inproj(x2d, mod3, mod_row_fn, g, w_packed, specs, out_widths, tm, rope=None, name="inproj"):
    n, d = x2d.shape
    wcols = w_packed.shape[1]
    use_rope = rope is not None
    in_specs = [pl.BlockSpec((tm, d), lambda i: (i, 0)),
                pl.BlockSpec((1, 6, d), lambda i: (mod_row_fn(i), 0, 0)),
                pl.BlockSpec((1, d), lambda i: (0, 0))]
    args = [x2d, mod3, g.reshape(1, d)]
    if use_rope:
        cos, sin = rope
        nb = cos.shape[0] // tm
        in_specs += [pl.BlockSpec((tm, cos.shape[1]), lambda i: (i % nb, 0)),
                     pl.BlockSpec((tm, cos.shape[1]), lambda i: (i % nb, 0))]
        args += [cos, sin]
    in_specs.append(pl.BlockSpec((d, wcols), lambda i: (0, 0), pipeline_mode=pl.Buffered(1)))
    args.append(w_packed)
    out_specs = [pl.BlockSpec((tm, w), lambda i: (i, 0)) for w in out_widths]
    out_shape = [jax.ShapeDtypeStruct((n, w), BF16) for w in out_widths]
    return pl.pallas_call(
        functools.partial(_inproj_kernel, specs, use_rope, len(out_widths)),
        grid=(n // tm,),
        in_specs=in_specs,
        out_specs=out_specs,
        out_shape=out_shape,
        scratch_shapes=[pltpu.VMEM((tm, d), BF16)],
        compiler_params=_params(("arbitrary",)),
        name=name,
    )(*args)


def _na_kernel(rows_per_step, n_rows, q_ref, k_ref, v_ref, kc_ref, vc_ref, bias_ref, o_ref):
    j = pl.program_id(1)
    lane = lax.broadcasted_iota(jnp.int32, (GRID_W, LANES), 1)
    first_head = lane < NA_HEAD_DIM
    scale = NA_HEAD_DIM ** -0.5
    win_keys = NA_WIN_H * GRID_W

    def row_body(rr, carry):
        r = j * rows_per_step + rr
        rs = jnp.clip(r - NA_WIN_H // 2, 0, n_rows - NA_WIN_H)
        d = r - rs
        t0 = pl.multiple_of(rr * GRID_W, GRID_W)
        ks = pl.multiple_of(rs * GRID_W, GRID_W)
        for p in range(NA_HEADS // 2):
            ln = slice(p * LANES, (p + 1) * LANES)
            qp = q_ref[0, pl.ds(t0, GRID_W), ln]
            kp = k_ref[0, pl.ds(ks, win_keys), ln]
            vp = v_ref[0, pl.ds(ks, win_keys), ln]
            kcp = kc_ref[0, :, ln]
            vcp = vc_ref[0, :, ln]
            heads_out = []
            for hh in range(2):
                sel = first_head if hh == 0 else jnp.logical_not(first_head)
                qm = jnp.where(sel, qp, jnp.zeros_like(qp))
                sw = lax.dot_general(qm, kp, _NT, preferred_element_type=F32) * scale
                sw = sw + bias_ref[2 * p + hh, d]
                sc = lax.dot_general(qm, kcp, _NT, preferred_element_type=F32) * scale
                m = jnp.maximum(jnp.max(sw, axis=-1, keepdims=True),
                                jnp.max(sc, axis=-1, keepdims=True))
                pw = jnp.exp(sw - m)
                pc = jnp.exp(sc - m)
                den = jnp.sum(pw, axis=-1, keepdims=True) + jnp.sum(pc, axis=-1, keepdims=True)
                o2 = (jnp.dot(pw.astype(BF16), vp, preferred_element_type=F32)
                      + jnp.dot(pc.astype(BF16), vcp, preferred_element_type=F32))
                heads_out.append(o2 / den)
            o_ref[0, pl.ds(t0, GRID_W), ln] = jnp.where(first_head, heads_out[0],
                                                        heads_out[1]).astype(o_ref.dtype)
        return carry

    lax.fori_loop(0, rows_per_step, row_body, 0)


def _na_bias_table(rpb):
    c = np.arange(GRID_W)
    cstart = np.clip(c - NA_WIN_W // 2, 0, GRID_W - NA_WIN_W)
    kc = np.arange(GRID_W)
    valid = (kc[None, :] >= cstart[:, None]) & (kc[None, :] < cstart[:, None] + NA_WIN_W)
    col_idx = np.clip(kc[None, :] - c[:, None] + NA_WIN_W - 1, 0, 2 * NA_WIN_W - 2)
    t = jnp.where(jnp.asarray(valid)[None, None], rpb[:, :, col_idx].astype(F32), NEG_INF)
    dr = (NA_WIN_H - 1) - np.arange(NA_WIN_H)[:, None] + np.arange(NA_WIN_H)[None, :]
    full = t[:, dr]
    full = full.transpose(0, 1, 3, 2, 4)
    return full.reshape(rpb.shape[0], NA_WIN_H, GRID_W, NA_WIN_H * GRID_W)


def _na_attention(q, k, v, kc, vc, bias, rows_per_step=4):
    b, s, w = q.shape
    n_rows = s // GRID_W
    ctx_len = kc.shape[1]
    blk = rows_per_step * GRID_W
    return pl.pallas_call(
        functools.partial(_na_kernel, rows_per_step, n_rows),
        grid=(b, n_rows // rows_per_step),
        in_specs=[pl.BlockSpec((1, blk, w), lambda bi, j: (bi, j, 0)),
                  pl.BlockSpec((1, s, w), lambda bi, j: (bi, 0, 0)),
                  pl.BlockSpec((1, s, w), lambda bi, j: (bi, 0, 0)),
                  pl.BlockSpec((1, ctx_len, w), lambda bi, j: (bi, 0, 0)),
                  pl.BlockSpec((1, ctx_len, w), lambda bi, j: (bi, 0, 0)),
                  pl.BlockSpec(bias.shape, lambda bi, j: (0, 0, 0, 0))],
        out_specs=pl.BlockSpec((1, blk, w), lambda bi, j: (bi, j, 0)),
        out_shape=jax.ShapeDtypeStruct((b, s, w), BF16),
        compiler_params=_params(("arbitrary", "arbitrary")),
        name="na_attn",
    )(q, k, v, kc, vc, bias)


def _gla_kernel(n_lat, n_ctx, gq_ref, gk_ref, gv_ref, lr_ref, sog_ref, ck_ref, cv_ref, clr_ref,
                wa_ref, ba_ref, gn_ref, o_ref, st_scr, of_scr):
    c_len = GLA_CHUNK
    row = lax.broadcasted_iota(jnp.int32, (c_len, c_len), 0)
    col = lax.broadcasted_iota(jnp.int32, (c_len, c_len), 1)
    masks = (col <= row, col >= row)

    def chunk(t0, k_ref, v_ref, l_ref, direction, mode):
        mask = masks[direction]
        tri = mask.astype(BF16)
        z = jnp.dot(l_ref[0, pl.ds(t0, c_len), :], wa_ref[direction],
                    preferred_element_type=F32) + ba_ref[direction:direction + 1, :]
        g = (jnp.minimum(z, 0.0) - jnp.log(1.0 + jnp.exp(-jnp.abs(z)))) * (1.0 / GLA_GATE_NORM)
        g1 = g.astype(BF16)
        r1 = g - g1.astype(F32)
        g2 = r1.astype(BF16)
        g3 = (r1 - g2.astype(F32)).astype(BF16)
        b = (jnp.dot(tri, g1, preferred_element_type=F32)
             + jnp.dot(tri, g2, preferred_element_type=F32)
             + jnp.dot(tri, g3, preferred_element_type=F32))
        b_last = b[c_len - 1:c_len, :] if direction == 0 else b[0:1, :]
        kf = k_ref[0, pl.ds(t0, c_len), :].astype(F32)
        kd = (kf * jnp.exp(b_last - b)).astype(BF16)
        dec = jnp.exp(b_last)
        if mode != "state":
            qe = (gq_ref[0, pl.ds(t0, c_len), :].astype(F32) * jnp.exp(b)).astype(BF16)
            ke = (kf * jnp.exp(-b)).astype(BF16)
        for h in range(GLA_HEADS):
            kl = slice(h * GLA_HK, (h + 1) * GLA_HK)
            vl = slice(h * GLA_HV, (h + 1) * GLA_HV)
            v_h = v_ref[0, pl.ds(t0, c_len), vl]
            st = st_scr[h]
            if mode != "state":
                att = lax.dot_general(qe[:, kl], ke[:, kl], _NT, preferred_element_type=F32)
                att = jnp.where(mask, att, 0.0).astype(BF16)
                o_h = (jnp.dot(att, v_h, preferred_element_type=F32)
                       + lax.dot_general(qe[:, kl], st.astype(BF16), _NT, preferred_element_type=F32))
                if mode == "fwd":
                    of_scr[pl.ds(t0, c_len), vl] = o_h
                else:
                    o_t = of_scr[pl.ds(t0, c_len), vl] + o_h
                    ms = jnp.mean(o_t * o_t, axis=-1, keepdims=True)
                    o_n = o_t * lax.rsqrt(ms + EPS) * gn_ref[:, vl]
                    o_n = o_n * sog_ref[0, pl.ds(t0, c_len), vl].astype(F32)
                    o_ref[0, pl.ds(t0, c_len), vl] = o_n.astype(o_ref.dtype)
            upd = lax.dot_general(v_h, kd[:, kl], _TN, preferred_element_type=F32)
            st_scr[h] = st * dec[:, kl] + upd

    def scan(direction):
        st_scr[...] = jnp.zeros_like(st_scr)

        def ctx_body(i, carry):
            c = i if direction == 0 else n_ctx - 1 - i
            chunk(pl.multiple_of(c * c_len, c_len), ck_ref, cv_ref, clr_ref, direction, "state")
            return carry

        def lat_body(i, carry):
            c = i if direction == 0 else n_lat - 1 - i
            chunk(pl.multiple_of(c * c_len, c_len), gk_ref, gv_ref, lr_ref, direction,
                  "fwd" if direction == 0 else "bwd")
            return carry

        lax.fori_loop(0, n_ctx, ctx_body, 0)
        lax.fori_loop(0, n_lat, lat_body, 0)

    scan(0)
    scan(1)


def _gla(gq, gk, gv, lr, sog, ck, cv, clr, wa, ba, gn):
    b, s, _ = gq.shape
    ctx_len = ck.shape[1]
    n_lat = s // GLA_CHUNK
    n_ctx = ctx_len // GLA_CHUNK

    def per_batch(t, w):
        return pl.BlockSpec((1, t, w), lambda bi: (bi, 0, 0))

    return pl.pallas_call(
        functools.partial(_gla_kernel, n_lat, n_ctx),
        grid=(b,),
        in_specs=[per_batch(s, GLA_DK), per_batch(s, GLA_DK), per_batch(s, GLA_DV),
                  per_batch(s, LR_PAD), per_batch(s, GLA_DV),
                  per_batch(ctx_len, GLA_DK), per_batch(ctx_len, GLA_DV), per_batch(ctx_len, LR_PAD),
                  pl.BlockSpec(wa.shape, lambda bi: (0, 0, 0)),
                  pl.BlockSpec(ba.shape, lambda bi: (0, 0)),
                  pl.BlockSpec(gn.shape, lambda bi: (0, 0))],
        out_specs=per_batch(s, GLA_DV),
        out_shape=jax.ShapeDtypeStruct((b, s, GLA_DV), BF16),
        scratch_shapes=[pltpu.VMEM((GLA_HEADS, GLA_HV, GLA_HK), F32),
                        pltpu.VMEM((s, GLA_DV), F32)],
        compiler_params=_params(("arbitrary",)),
        name="gla",
    )(gq, gk, gv, lr, sog, ck, cv, clr, wa, ba, gn)


def _merge_kernel(ona_ref, ogla_ref, sgn_ref, sgg_ref, x_ref, mod_ref, gffn_ref,
                  wna_ref, wgla_ref, wout_ref, wr_ref, br_ref,
                  h_ref, f_ref, route_ref, cnt_ref):
    i = pl.program_id(0)
    tm = x_ref.shape[0]

    @pl.when(i == 0)
    def _():
        cnt_ref[...] = jnp.zeros_like(cnt_ref)

    y1 = (sgn_ref[...].astype(F32) * jnp.dot(ona_ref[...], wna_ref[...], preferred_element_type=F32)
          + sgg_ref[...].astype(F32) * jnp.dot(ogla_ref[...], wgla_ref[...], preferred_element_type=F32))
    y = jnp.dot(y1.astype(BF16), wout_ref[...], preferred_element_type=F32)
    h = x_ref[...] + mod_ref[0, 2:3, :] * y
    h_ref[...] = h
    ms = jnp.mean(h * h, axis=-1, keepdims=True)
    f = h * lax.rsqrt(ms + EPS) * gffn_ref[...] * (1.0 + mod_ref[0, 4:5, :]) + mod_ref[0, 3:4, :]
    f_ref[...] = f

    logits = jnp.dot(f, wr_ref[...], preferred_element_type=F32,
                     precision=lax.Precision.HIGHEST) + br_ref[...]
    lane = lax.broadcasted_iota(jnp.int32, (tm, ROUTE_LANES), 1)
    big = jnp.int32(ROUTE_LANES)

    def first_argmax(vals):
        m = jnp.max(vals, axis=-1, keepdims=True)
        idx = jnp.min(jnp.where(vals == m, lane, big), axis=-1, keepdims=True)
        return m, idx

    is_grp = lane < N_GROUPS
    lg = jnp.where(is_grp, logits, -jnp.inf)
    mg, grp = first_argmax(lg)
    p_grp = 1.0 / jnp.sum(jnp.where(is_grp, jnp.exp(lg - mg), 0.0), axis=-1, keepdims=True)
    lo = EXPERT_LANE0 + grp * EXPERTS_PER_GROUP
    in_grp = (lane >= lo) & (lane < lo + EXPERTS_PER_GROUP)
    le = jnp.where(in_grp, logits, -jnp.inf)
    m1, i1 = first_argmax(le)
    m2, i2 = first_argmax(jnp.where(lane == i1, -jnp.inf, le))
    t = jnp.exp(m2 - m1)
    w1 = p_grp / (1.0 + t)
    w2 = p_grp * t / (1.0 + t)

    hot1 = lane == i1
    hot2 = lane == i2
    onehot = (hot1 | hot2).astype(BF16)
    r_i = lax.broadcasted_iota(jnp.int32, (tm, tm), 0)
    c_i = lax.broadcasted_iota(jnp.int32, (tm, tm), 1)
    before = (c_i < r_i).astype(BF16)
    rank = jnp.dot(before, onehot, preferred_element_type=F32) + cnt_ref[...]
    rank1 = jnp.sum(jnp.where(hot1, rank, 0.0), axis=-1, keepdims=True)
    rank2 = jnp.sum(jnp.where(hot2, rank, 0.0), axis=-1, keepdims=True)
    cnt_ref[...] = cnt_ref[...] + jnp.sum(onehot.astype(F32), axis=0, keepdims=True)

    e1 = (i1 - EXPERT_LANE0).astype(F32)
    e2 = (i2 - EXPERT_LANE0).astype(F32)
    route = jnp.where(lane == 0, e1, 0.0)
    route = jnp.where(lane == 1, e2, route)
    route = jnp.where(lane == 2, w1, route)
    route = jnp.where(lane == 3, w2, route)
    route = jnp.where(lane == 4, rank1, route)
    route = jnp.where(lane == 5, rank2, route)
    route_ref[...] = route


def _merge_route(ona, ogla, sgn, sgg, x2d, mod3, gffn, wna, wgla, wout, wr, br, tokens_per_batch, tm):
    n, d = x2d.shape
    per_b = tokens_per_batch // tm

    def tile(w):
        return pl.BlockSpec((tm, w), lambda i: (i, 0))

    def whole(a):
        return pl.BlockSpec(a.shape, lambda i: (0,) * a.ndim)

    return pl.pallas_call(
        _merge_kernel,
        grid=(n // tm,),
        in_specs=[tile(NA_WIDTH), tile(GLA_DV), tile(d), tile(d), tile(d),
                  pl.BlockSpec((1, 6, d), lambda i: (i // per_b, 0, 0)),
                  whole(gffn), whole(wna), whole(wgla), whole(wout), whole(wr), whole(br)],
        out_specs=[tile(d), tile(d), tile(ROUTE_LANES),
                   pl.BlockSpec((1, ROUTE_LANES), lambda i: (0, 0))],
        out_shape=[jax.ShapeDtypeStruct((n, d), F32),
                   jax.ShapeDtypeStruct((n, d), F32),
                   jax.ShapeDtypeStruct((n, ROUTE_LANES), F32),
                   jax.ShapeDtypeStruct((1, ROUTE_LANES), F32)],
        compiler_params=_params(("arbitrary",)),
        name="merge_route",
    )(ona, ogla, sgn, sgg, x2d, mod3, gffn, wna, wgla, wout, wr, br)


def _row_copy(src_ref, src_row, dst_ref, dst_row, sem):
    return pltpu.make_async_copy(src_ref.at[pl.ds(src_row, 1), :], dst_ref.at[pl.ds(dst_row, 1), :], sem)


def _dispatch_kernel(dest_ref, f_ref, xb_in_ref, xb_ref, sem):
    del xb_in_ref
    tm = f_ref.shape[0]

    def start(t, carry):
        for k in range(2):
            _row_copy(f_ref, t, xb_ref, dest_ref[0, 0, k * tm + t], sem).start()
        return carry

    def wait(t, carry):
        for k in range(2):
            _row_copy(f_ref, t, xb_ref, dest_ref[0, 0, k * tm + t], sem).wait()
        return carry

    lax.fori_loop(0, tm, start, 0)
    lax.fori_loop(0, tm, wait, 0)


def _dispatch(dest3, f2d, n_pad, tm):
    n, d = f2d.shape
    xb0 = jnp.zeros((n_pad, d), F32)
    return pl.pallas_call(
        _dispatch_kernel,
        grid=(n // tm,),
        in_specs=[pl.BlockSpec((1, 1, 2 * tm), lambda i: (i, 0, 0), memory_space=pltpu.SMEM),
                  pl.BlockSpec((tm, d), lambda i: (i, 0)),
                  pl.BlockSpec(memory_space=pl.ANY)],
        out_specs=pl.BlockSpec(memory_space=pl.ANY),
        out_shape=jax.ShapeDtypeStruct((n_pad, d), F32),
        scratch_shapes=[pltpu.SemaphoreType.DMA],
        input_output_aliases={2: 0},
        compiler_params=_params(("arbitrary",)),
        name="dispatch",
    )(dest3, f2d, xb0)


def _expert_kernel(be_ref, nu_ref, x_ref, wg_ref, wu_ref, wd_ref, y_ref, wgu_scr, wd_scr):
    i = pl.program_id(0)
    prev = be_ref[jnp.maximum(i - 1, 0)]

    @pl.when((i == 0) | (be_ref[i] != prev))
    def _():
        wgu_scr[:, :D_EXPERT] = wg_ref[0].astype(BF16)
        wgu_scr[:, D_EXPERT:] = wu_ref[0].astype(BF16)
        wd_scr[...] = wd_ref[0].astype(BF16)

    @pl.when(i < nu_ref[0])
    def _():
        x = x_ref[...].astype(BF16)
        gu = jnp.dot(x, wgu_scr[...], preferred_element_type=F32)
        gate = gu[:, :D_EXPERT]
        hdn = gate * jax.nn.sigmoid(gate) * gu[:, D_EXPERT:]
        y_ref[...] = jnp.dot(hdn.astype(BF16), wd_scr[...], preferred_element_type=F32)

    @pl.when(i >= nu_ref[0])
    def _():
        y_ref[...] = jnp.zeros_like(y_ref)


def _experts(blk_expert, n_used, xb, wg, wu, wd):
    n_pad, d = xb.shape
    n_blk = n_pad // MOE_ROWS
    grid_spec = pltpu.PrefetchScalarGridSpec(
        num_scalar_prefetch=2,
        grid=(n_blk,),
        in_specs=[pl.BlockSpec((MOE_ROWS, d), lambda i, be, nu: (i, 0)),
                  pl.BlockSpec((1, d, D_EXPERT), lambda i, be, nu: (be[i], 0, 0)),
                  pl.BlockSpec((1, d, D_EXPERT), lambda i, be, nu: (be[i], 0, 0)),
                  pl.BlockSpec((1, D_EXPERT, d), lambda i, be, nu: (be[i], 0, 0))],
        out_specs=pl.BlockSpec((MOE_ROWS, d), lambda i, be, nu: (i, 0)),
        scratch_shapes=[pltpu.VMEM((d, 2 * D_EXPERT), BF16),
                        pltpu.VMEM((D_EXPERT, d), BF16)],
    )
    return pl.pallas_call(
        _expert_kernel,
        grid_spec=grid_spec,
        out_shape=jax.ShapeDtypeStruct((n_pad, d), F32),
        compiler_params=_params(("arbitrary",)),
        name="experts",
    )(blk_expert, n_used, xb, wg, wu, wd)


def _combine_kernel(dest_ref, h_ref, route_ref, mod_ref, fg_ref, yb_ref, o_ref, ybuf, sem):
    tm = h_ref.shape[0]

    def start(t, carry):
        for k in range(2):
            _row_copy(yb_ref, dest_ref[0, 0, k * tm + t], ybuf.at[k], t, sem).start()
        return carry

    def wait(t, carry):
        for k in range(2):
            _row_copy(yb_ref, dest_ref[0, 0, k * tm + t], ybuf.at[k], t, sem).wait()
        return carry

    lax.fori_loop(0, tm, start, 0)
    lax.fori_loop(0, tm, wait, 0)
    moe = route_ref[:, 2:3] * ybuf[0] + route_ref[:, 3:4] * ybuf[1]
    h = h_ref[...] + mod_ref[0, 5:6, :] * moe
    ms = jnp.mean(h * h, axis=-1, keepdims=True)
    o_ref[...] = h * lax.rsqrt(ms + EPS) * fg_ref[...]


def _combine(dest3, h2d, route, mod3, fg, yb, tokens_per_batch, tm):
    n, d = h2d.shape
    per_b = tokens_per_batch // tm
    return pl.pallas_call(
        _combine_kernel,
        grid=(n // tm,),
        in_specs=[pl.BlockSpec((1, 1, 2 * tm), lambda i: (i, 0, 0), memory_space=pltpu.SMEM),
                  pl.BlockSpec((tm, d), lambda i: (i, 0)),
                  pl.BlockSpec((tm, ROUTE_LANES), lambda i: (i, 0)),
                  pl.BlockSpec((1, 6, d), lambda i: (i // per_b, 0, 0)),
                  pl.BlockSpec((1, d), lambda i: (0, 0)),
                  pl.BlockSpec(memory_space=pl.ANY)],
        out_specs=pl.BlockSpec((tm, d), lambda i: (i, 0)),
        out_shape=jax.ShapeDtypeStruct((n, d), F32),
        scratch_shapes=[pltpu.VMEM((2, tm, d), F32), pltpu.SemaphoreType.DMA],
        compiler_params=_params(("arbitrary",)),
        name="combine",
    )(dest3, h2d, route, mod3, fg, yb)


def _rope_tables(s):
    t = np.arange(s)
    pos_r = jnp.asarray(t // GRID_W, F32)
    pos_c = jnp.asarray(t % GRID_W, F32)
    nf = GLA_HK // 4
    inv = ROPE_BASE ** (-jnp.arange(nf, dtype=F32) / nf)
    ang_r = pos_r[:, None] * inv
    ang_c = pos_c[:, None] * inv

    def half(ang):
        return (jnp.concatenate([jnp.cos(ang), jnp.cos(ang)], axis=-1),
                jnp.concatenate([-jnp.sin(ang), jnp.sin(ang)], axis=-1))

    cr, sr = half(ang_r)
    cc, sc = half(ang_c)
    cos = jnp.tile(jnp.concatenate([cr, cc], axis=-1), (1, GLA_HEADS))
    sin = jnp.tile(jnp.concatenate([sr, sc], axis=-1), (1, GLA_HEADS))
    return cos.astype(F32), sin.astype(F32)


def _dest_blocks(dest, tm):
    n = dest.shape[0]
    return dest.reshape(n // tm, tm, 2).transpose(0, 2, 1).reshape(n // tm, 1, 2 * tm)


def kernel(x, c, ctx, c_ctx, w_mod, b_mod, norm_attn_g, norm_ffn_g, w_in, w_gla_a2, b_gla_a2, gla_norm_g, na_rpb, w_na_o, w_gla_o, w_out, w_group, b_group, w_expert, b_expert, w_exp_gate, w_exp_up, w_exp_down, final_norm_g):
    b, s, d = x.shape
    ctx_len = ctx.shape[1]
    n = b * s
    assert w_mod.shape[0] == 1, "single-layer block"
    assert s % (GRID_W * NA_WIN_H) == 0 and ctx_len % GLA_CHUNK == 0

    mod_rows = -(-(b + 1) // 8) * 8
    cs = jnp.zeros((mod_rows, d), F32).at[:b].set(c).at[b].set(c_ctx)
    mod3 = _modulation(cs, w_mod[0], b_mod[0]).reshape(mod_rows, 6, d)

    w = w_in[0].astype(BF16)
    o_q, o_k, o_v = 0, NA_WIDTH, 2 * NA_WIDTH
    o_gq = 3 * NA_WIDTH
    o_gk = o_gq + GLA_DK
    o_gv = o_gk + GLA_DK
    o_og = o_gv + GLA_DV
    o_lr = o_og + GLA_DV
    o_mn = o_lr + 2 * GLA_GATE_RANK
    o_mg = o_mn + d
    w_lr = jnp.pad(w[:, o_lr:o_mn], ((0, 0), (0, LR_PAD - 2 * GLA_GATE_RANK)))
    w_lat = jnp.concatenate([w[:, :o_lr], w[:, o_mn:], w_lr], axis=1)
    cw = 512
    lat_specs = [(0, 0, o_q, cw, "plain"), (1, 0, o_k, cw, "plain"), (2, 0, o_v, cw, "plain"),
                 (3, 0, o_gq, cw, "rope_scaled"), (4, 0, o_gk, cw, "rope"),
                 (5, 0, o_gv, cw, "plain"), (5, cw, o_gv + cw, cw, "plain"),
                 (6, 0, o_og, cw, "silu"), (6, cw, o_og + cw, cw, "silu"),
                 (7, 0, o_lr, cw, "sigmoid"), (7, cw, o_lr + cw, cw, "sigmoid"),
                 (8, 0, o_lr + d, cw, "sigmoid"), (8, cw, o_lr + d + cw, cw, "sigmoid"),
                 (9, 0, o_lr + 2 * d, LR_PAD, "plain")]
    lat_widths = [NA_WIDTH, NA_WIDTH, NA_WIDTH, GLA_DK, GLA_DK, GLA_DV, GLA_DV, d, d, LR_PAD]
    tm = 512
    per_b = s // tm
    q_na, k_na, v_na, gq, gk, gv, sog, sgn, sgg, lr = _inproj(
        x.reshape(n, d), mod3, lambda i: i // per_b, norm_attn_g[0], w_lat, lat_specs, lat_widths, tm,
        rope=_rope_tables(s), name="inproj_lat")

    w_ctx = jnp.concatenate([w[:, o_k:o_gq], w[:, o_gk:o_og], w_lr], axis=1)
    ctx_specs = [(0, 0, 0, cw, "plain"), (1, 0, cw, cw, "plain"), (2, 0, 2 * cw, cw, "plain"),
                 (3, 0, 3 * cw, cw, "plain"), (3, cw, 4 * cw, cw, "plain"),
                 (4, 0, 5 * cw, LR_PAD, "plain")]
    ctx_widths = [NA_WIDTH, NA_WIDTH, GLA_DK, GLA_DV, LR_PAD]
    kc_na, vc_na, ck, cv, clr = _inproj(
        ctx.reshape(b * ctx_len, d), mod3, lambda i: b, norm_attn_g[0], w_ctx, ctx_specs, ctx_widths,
        ctx_len, name="inproj_ctx")

    def lat3(a):
        return a.reshape(b, s, a.shape[-1])

    def ctx3(a):
        return a.reshape(b, ctx_len, a.shape[-1])

    o_na = _na_attention(lat3(q_na), lat3(k_na), lat3(v_na), ctx3(kc_na), ctx3(vc_na),
                         _na_bias_table(na_rpb[0]))

    wa = jnp.zeros((2, LR_PAD, GLA_DK), F32)
    wa = wa.at[0, :GLA_GATE_RANK].set(w_gla_a2[0, 0]).at[1, GLA_GATE_RANK:2 * GLA_GATE_RANK].set(w_gla_a2[0, 1])
    gn = jnp.tile(gla_norm_g[0], GLA_HEADS).reshape(1, GLA_DV)
    o_gla = _gla(lat3(gq), lat3(gk), lat3(gv), lat3(lr), lat3(sog), ctx3(ck), ctx3(cv), ctx3(clr),
                 wa.astype(BF16), b_gla_a2[0], gn)

    wr = jnp.zeros((d, ROUTE_LANES), F32)
    wr = wr.at[:, :N_GROUPS].set(w_group[0]).at[:, EXPERT_LANE0:EXPERT_LANE0 + N_EXPERTS].set(w_expert[0])
    br = jnp.zeros((1, ROUTE_LANES), F32)
    br = br.at[0, :N_GROUPS].set(b_group[0]).at[0, EXPERT_LANE0:EXPERT_LANE0 + N_EXPERTS].set(b_expert[0])
    h, f_lat, route, counts = _merge_route(
        o_na.reshape(n, NA_WIDTH), o_gla.reshape(n, GLA_DV), sgn, sgg, x.reshape(n, d), mod3, norm_ffn_g[0].reshape(1, d),
        w_na_o[0].astype(BF16), w_gla_o[0].astype(BF16), w_out[0].astype(BF16), wr, br, s, tm)

    cnt = counts[0, EXPERT_LANE0:EXPERT_LANE0 + N_EXPERTS].astype(jnp.int32)
    padded = (cnt + MOE_ROWS - 1) // MOE_ROWS * MOE_ROWS
    pend = jnp.cumsum(padded)
    pstart = pend - padded
    n_pad = (2 * n + N_EXPERTS * (MOE_ROWS - 1) + MOE_ROWS - 1) // MOE_ROWS * MOE_ROWS
    n_blk = n_pad // MOE_ROWS
    eid = route[:, 0:2].astype(jnp.int32)
    rank = route[:, 4:6].astype(jnp.int32)
    onehot = eid[:, :, None] == jnp.arange(N_EXPERTS, dtype=jnp.int32)
    dest = rank + jnp.sum(jnp.where(onehot, pstart, 0), axis=-1)
    blk_start = jnp.arange(n_blk, dtype=jnp.int32) * MOE_ROWS
    blk_expert = jnp.minimum(jnp.sum(blk_start[:, None] >= pend[None, :], axis=-1), N_EXPERTS - 1)
    n_used = (pend[-1] // MOE_ROWS).reshape(1)

    tg = 256
    dest3 = _dest_blocks(dest.astype(jnp.int32), tg)
    xb = _dispatch(dest3, f_lat, n_pad, tg)
    yb = _experts(blk_expert.astype(jnp.int32), n_used.astype(jnp.int32), xb,
                  w_exp_gate[0], w_exp_up[0], w_exp_down[0])
    out = _combine(dest3, h, route, mod3, final_norm_g.reshape(1, d), yb, s, tg)
    return out.reshape(b, s, d)
```

```python
import functools

import jax
import jax.numpy as jnp
import numpy as np
from jax import lax
from jax.experimental import pallas as pl
from jax.experimental.pallas import tpu as pltpu

F32 = jnp.float32
BF16 = jnp.bfloat16

D_MODEL = 1024
GRID_W = 64
NA_HEADS = 8
NA_HEAD_DIM = 64
NA_WIDTH = NA_HEADS * NA_HEAD_DIM
NA_WIN_H = 8
NA_WIN_W = 16
GLA_HEADS = 4
GLA_DK = D_MODEL // 2
GLA_DV = D_MODEL
GLA_HK = GLA_DK // GLA_HEADS
GLA_HV = GLA_DV // GLA_HEADS
GLA_GATE_RANK = 16
GLA_GATE_NORM = 16.0
GLA_CHUNK = 64
ROPE_BASE = 10000.0
N_GROUPS = 4
EXPERTS_PER_GROUP = 8
N_EXPERTS = N_GROUPS * EXPERTS_PER_GROUP
D_EXPERT = D_MODEL // 2
EPS = 1e-6
NEG_INF = -1e30

LANES = 128
LR_PAD = LANES
ROUTE_LANES = LANES
EXPERT_LANE0 = N_GROUPS
MOE_ROWS = 256
VMEM_LIMIT = 56 * 1024 * 1024

_NT = (((1,), (1,)), ((), ()))
_TN = (((0,), (0,)), ((), ()))


def _params(sem, vmem=VMEM_LIMIT):
    return pltpu.CompilerParams(dimension_semantics=sem, vmem_limit_bytes=vmem)


def _mod_kernel(c_ref, w_ref, b_ref, o_ref):
    c = c_ref[...]
    s = c * jax.nn.sigmoid(c)
    o_ref[...] = jnp.dot(s.astype(BF16), w_ref[...].astype(BF16),
                         preferred_element_type=F32) + b_ref[...]


def _modulation(cs, w_mod, b_mod):
    rows, d = cs.shape
    n = w_mod.shape[1]
    bn = 1024
    return pl.pallas_call(
        _mod_kernel,
        grid=(n // bn,),
        in_specs=[pl.BlockSpec((rows, d), lambda j: (0, 0)),
                  pl.BlockSpec((d, bn), lambda j: (0, j)),
                  pl.BlockSpec((1, bn), lambda j: (0, j))],
        out_specs=pl.BlockSpec((rows, bn), lambda j: (0, j)),
        out_shape=jax.ShapeDtypeStruct((rows, n), F32),
        compiler_params=_params(("arbitrary",)),
        name="mod",
    )(cs, w_mod, b_mod.reshape(1, n))


def _inproj_kernel(specs, use_rope, n_out, x_ref, mod_ref, g_ref, *rest):
    if use_rope:
        cos_ref, sin_ref, w_ref = rest[:3]
        rest = rest[3:]
    else:
        w_ref = rest[0]
        rest = rest[1:]
    outs = rest[:n_out]
    a_scr = rest[n_out]
    x = x_ref[...]
    ms = jnp.mean(x * x, axis=-1, keepdims=True)
    a = x * lax.rsqrt(ms + EPS) * g_ref[...] * (1.0 + mod_ref[0, 1:2, :]) + mod_ref[0, 0:1, :]
    a_scr[...] = a.astype(BF16)
    for (oi, oc, wc, width, kind) in specs:
        acc = jnp.dot(a_scr[...], w_ref[:, wc:wc + width], preferred_element_type=F32)
        if kind in ("rope", "rope_scaled"):
            if kind == "rope_scaled":
                acc = acc * (GLA_HK ** -0.5)
            lane = lax.broadcasted_iota(jnp.int32, acc.shape, 1)
            rot = jnp.where((lane & 63) < 32,
                            pltpu.roll(acc, width - 32, axis=1),
                            pltpu.roll(acc, 32, axis=1))
            acc = acc * cos_ref[...] + rot * sin_ref[...]
        elif kind == "silu":
            acc = acc * jax.nn.sigmoid(acc)
        elif kind == "sigmoid":
            acc = jax.nn.sigmoid(acc)
        outs[oi][:, oc:oc + width] = acc.astype(outs[oi].dtype)


def _inproj(x2d, mod3, mod_row_fn, g, w_packed, specs, out_widths, tm, rope=None, name="inproj"):
    n, d = x2d.shape
    wcols = w_packed.shape[1]
    use_rope = rope is not None
    in_specs = [pl.BlockSpec((tm, d), lambda i: (i, 0)),
                pl.BlockSpec((1, 6, d), lambda i: (mod_row_fn(i), 0, 0)),
                pl.BlockSpec((1, d), lambda i: (0, 0))]
    args = [x2d, mod3, g.reshape(1, d)]
    if use_rope:
        cos, sin = rope
        nb = cos.shape[0] // tm
        in_specs += [pl.BlockSpec((tm, cos.shape[1]), lambda i: (i % nb, 0)),
                     pl.BlockSpec((tm, cos.shape[1]), lambda i: (i % nb, 0))]
        args += [cos, sin]
    in_specs.append(pl.BlockSpec((d, wcols), lambda i: (0, 0), pipeline_mode=pl.Buffered(1)))
    args.append(w_packed)
    out_specs = [pl.BlockSpec((tm, w), lambda i: (i, 0)) for w in out_widths]
    out_shape = [jax.ShapeDtypeStruct((n, w), BF16) for w in out_widths]
    return pl.pallas_call(
        functools.partial(_inproj_kernel, specs, use_rope, len(out_widths)),
        grid=(n // tm,),
        in_specs=in_specs,
        out_specs=out_specs,
        out_shape=out_shape,
        scratch_shapes=[pltpu.VMEM((tm, d), BF16)],
        compiler_params=_params(("arbitrary",)),
        name=name,
    )(*args)


def _na_kernel(rows_per_step, n_rows, q_ref, k_ref, v_ref, kc_ref, vc_ref, bias_ref, o_ref):
    j = pl.program_id(1)
    lane = lax.broadcasted_iota(jnp.int32, (GRID_W, LANES), 1)
    first_head = lane < NA_HEAD_DIM
    scale = NA_HEAD_DIM ** -0.5
    win_keys = NA_WIN_H * GRID_W

    def row_body(rr, carry):
        r = j * rows_per_step + rr
        rs = jnp.clip(r - NA_WIN_H // 2, 0, n_rows - NA_WIN_H)
        d = r - rs
        t0 = pl.multiple_of(rr * GRID_W, GRID_W)
        ks = pl.multiple_of(rs * GRID_W, GRID_W)
        pairs = range(NA_HEADS // 2)
        lanes = [slice(p * LANES, (p + 1) * LANES) for p in pairs]
        sws, scs = [], []
        for p in pairs:
            qp = q_ref[0, pl.ds(t0, GRID_W), lanes[p]]
            zero = jnp.zeros_like(qp)
            qs = jnp.concatenate([jnp.where(first_head, qp, zero), jnp.where(first_head, zero, qp)], axis=0)
            sws.append(lax.dot_general(qs, k_ref[0, pl.ds(ks, win_keys), lanes[p]], _NT,
                                       preferred_element_type=F32))
            scs.append(lax.dot_general(qs, kc_ref[0, :, lanes[p]], _NT, preferred_element_type=F32))
        pws, pcs, dens = [], [], []
        for p in pairs:
            sw = sws[p] * scale + bias_ref[d, p * LANES:(p + 1) * LANES, :]
            sc = scs[p] * scale
            m = jnp.maximum(jnp.max(sw, axis=-1, keepdims=True), jnp.max(sc, axis=-1, keepdims=True))
            pw = jnp.exp(sw - m)
            pc = jnp.exp(sc - m)
            dens.append(jnp.sum(pw, axis=-1, keepdims=True) + jnp.sum(pc, axis=-1, keepdims=True))
            pws.append(pw.astype(BF16))
            pcs.append(pc.astype(BF16))
        for p in pairs:
            o2 = (jnp.dot(pws[p], v_ref[0, pl.ds(ks, win_keys), lanes[p]], preferred_element_type=F32)
                  + jnp.dot(pcs[p], vc_ref[0, :, lanes[p]], preferred_element_type=F32)) / dens[p]
            o_ref[0, pl.ds(t0, GRID_W), lanes[p]] = jnp.where(first_head, o2[:GRID_W],
                                                             o2[GRID_W:]).astype(o_ref.dtype)
        return carry

    lax.fori_loop(0, rows_per_step, row_body, 0)


def _na_bias_table(rpb):
    c = np.arange(GRID_W)
    cstart = np.clip(c - NA_WIN_W // 2, 0, GRID_W - NA_WIN_W)
    kc = np.arange(GRID_W)
    valid = (kc[None, :] >= cstart[:, None]) & (kc[None, :] < cstart[:, None] + NA_WIN_W)
    col_idx = np.clip(kc[None, :] - c[:, None] + NA_WIN_W - 1, 0, 2 * NA_WIN_W - 2)
    t = jnp.where(jnp.asarray(valid)[None, None], rpb[:, :, col_idx].astype(F32), NEG_INF)
    dr = (NA_WIN_H - 1) - np.arange(NA_WIN_H)[:, None] + np.arange(NA_WIN_H)[None, :]
    full = t[:, dr]
    full = full.transpose(1, 0, 3, 2, 4)
    return full.reshape(NA_WIN_H, rpb.shape[0] * GRID_W, NA_WIN_H * GRID_W)


def _na_attention(q, k, v, kc, vc, bias, rows_per_step=4):
    b, s, w = q.shape
    n_rows = s // GRID_W
    ctx_len = kc.shape[1]
    blk = rows_per_step * GRID_W
    return pl.pallas_call(
        functools.partial(_na_kernel, rows_per_step, n_rows),
        grid=(b, n_rows // rows_per_step),
        in_specs=[pl.BlockSpec((1, blk, w), lambda bi, j: (bi, j, 0)),
                  pl.BlockSpec((1, s, w), lambda bi, j: (bi, 0, 0)),
                  pl.BlockSpec((1, s, w), lambda bi, j: (bi, 0, 0)),
                  pl.BlockSpec((1, ctx_len, w), lambda bi, j: (bi, 0, 0)),
                  pl.BlockSpec((1, ctx_len, w), lambda bi, j: (bi, 0, 0)),
                  pl.BlockSpec(bias.shape, lambda bi, j: (0, 0, 0), pipeline_mode=pl.Buffered(1))],
        out_specs=pl.BlockSpec((1, blk, w), lambda bi, j: (bi, j, 0)),
        out_shape=jax.ShapeDtypeStruct((b, s, w), BF16),
        compiler_params=_params(("arbitrary", "arbitrary")),
        name="na_attn",
    )(q, k, v, kc, vc, bias)


def _gla_kernel(n_lat, n_ctx, gq_ref, gk_ref, gv_ref, lr_ref, sog_ref, ck_ref, cv_ref, clr_ref,
                wa_ref, ba_ref, gn_ref, o_ref, st_scr, of_scr):
    c_len = GLA_CHUNK
    row = lax.broadcasted_iota(jnp.int32, (c_len, c_len), 0)
    col = lax.broadcasted_iota(jnp.int32, (c_len, c_len), 1)
    masks = (col <= row, col >= row)

    def chunk(t0, k_ref, v_ref, l_ref, direction, mode):
        mask = masks[direction]
        tri = mask.astype(BF16)
        z = jnp.dot(l_ref[0, pl.ds(t0, c_len), :], wa_ref[direction],
                    preferred_element_type=F32) + ba_ref[direction:direction + 1, :]
        g = (jnp.minimum(z, 0.0) - jnp.log(1.0 + jnp.exp(-jnp.abs(z)))) * (1.0 / GLA_GATE_NORM)
        g1 = g.astype(BF16)
        r1 = g - g1.astype(F32)
        g2 = r1.astype(BF16)
        g3 = (r1 - g2.astype(F32)).astype(BF16)
        b = (jnp.dot(tri, g1, preferred_element_type=F32)
             + jnp.dot(tri, g2, preferred_element_type=F32)
             + jnp.dot(tri, g3, preferred_element_type=F32))
        b_last = b[c_len - 1:c_len, :] if direction == 0 else b[0:1, :]
        kf = k_ref[0, pl.ds(t0, c_len), :].astype(F32)
        kd = (kf * jnp.exp(b_last - b)).astype(BF16)
        dec = jnp.exp(b_last)
        if mode != "state":
            qe = (gq_ref[0, pl.ds(t0, c_len), :].astype(F32) * jnp.exp(b)).astype(BF16)
            ke = (kf * jnp.exp(-b)).astype(BF16)
        for h in range(GLA_HEADS):
            kl = slice(h * GLA_HK, (h + 1) * GLA_HK)
            vl = slice(h * GLA_HV, (h + 1) * GLA_HV)
            v_h = v_ref[0, pl.ds(t0, c_len), vl]
            st = st_scr[h]
            if mode != "state":
                att = lax.dot_general(qe[:, kl], ke[:, kl], _NT, preferred_element_type=F32)
                att = jnp.where(mask, att, 0.0).astype(BF16)
                o_h = (jnp.dot(att, v_h, preferred_element_type=F32)
                       + lax.dot_general(qe[:, kl], st.astype(BF16), _NT, preferred_element_type=F32))
                if mode == "fwd":
                    of_scr[pl.ds(t0, c_len), vl] = o_h
                else:
                    o_t = of_scr[pl.ds(t0, c_len), vl] + o_h
                    ms = jnp.mean(o_t * o_t, axis=-1, keepdims=True)
                    o_n = o_t * lax.rsqrt(ms + EPS) * gn_ref[:, vl]
                    o_n = o_n * sog_ref[0, pl.ds(t0, c_len), vl].astype(F32)
                    o_ref[0, pl.ds(t0, c_len), vl] = o_n.astype(o_ref.dtype)
            upd = lax.dot_general(v_h, kd[:, kl], _TN, preferred_element_type=F32)
            st_scr[h] = st * dec[:, kl] + upd

    def scan(direction):
        st_scr[...] = jnp.zeros_like(st_scr)

        def ctx_body(i, carry):
            c = i if direction == 0 else n_ctx - 1 - i
            chunk(pl.multiple_of(c * c_len, c_len), ck_ref, cv_ref, clr_ref, direction, "state")
            return carry

        def lat_body(i, carry):
            c = i if direction == 0 else n_lat - 1 - i
            chunk(pl.multiple_of(c * c_len, c_len), gk_ref, gv_ref, lr_ref, direction,
                  "fwd" if direction == 0 else "bwd")
            return carry

        lax.fori_loop(0, n_ctx, ctx_body, 0)
        lax.fori_loop(0, n_lat, lat_body, 0)

    scan(0)
    scan(1)


def _gla(gq, gk, gv, lr, sog, ck, cv, clr, wa, ba, gn):
    b, s, _ = gq.shape
    ctx_len = ck.shape[1]
    n_lat = s // GLA_CHUNK
    n_ctx = ctx_len // GLA_CHUNK

    def per_batch(t, w):
        return pl.BlockSpec((1, t, w), lambda bi: (bi, 0, 0))

    return pl.pallas_call(
        functools.partial(_gla_kernel, n_lat, n_ctx),
        grid=(b,),
        in_specs=[per_batch(s, GLA_DK), per_batch(s, GLA_DK), per_batch(s, GLA_DV),
                  per_batch(s, LR_PAD), per_batch(s, GLA_DV),
                  per_batch(ctx_len, GLA_DK), per_batch(ctx_len, GLA_DV), per_batch(ctx_len, LR_PAD),
                  pl.BlockSpec(wa.shape, lambda bi: (0, 0, 0)),
                  pl.BlockSpec(ba.shape, lambda bi: (0, 0)),
                  pl.BlockSpec(gn.shape, lambda bi: (0, 0))],
        out_specs=per_batch(s, GLA_DV),
        out_shape=jax.ShapeDtypeStruct((b, s, GLA_DV), BF16),
        scratch_shapes=[pltpu.VMEM((GLA_HEADS, GLA_HV, GLA_HK), F32),
                        pltpu.VMEM((s, GLA_DV), F32)],
        compiler_params=_params(("arbitrary",)),
        name="gla",
    )(gq, gk, gv, lr, sog, ck, cv, clr, wa, ba, gn)


def _merge_kernel(ona_ref, ogla_ref, sgn_ref, sgg_ref, x_ref, mod_ref, gffn_ref,
                  wna_ref, wgla_ref, wout_ref, wr_ref, br_ref,
                  h_ref, f_ref, route_ref, cnt_ref):
    i = pl.program_id(0)
    tm = x_ref.shape[0]

    @pl.when(i == 0)
    def _():
        cnt_ref[...] = jnp.zeros_like(cnt_ref)

    y1 = (sgn_ref[...].astype(F32) * jnp.dot(ona_ref[...], wna_ref[...], preferred_element_type=F32)
          + sgg_ref[...].astype(F32) * jnp.dot(ogla_ref[...], wgla_ref[...], preferred_element_type=F32))
    y = jnp.dot(y1.astype(BF16), wout_ref[...], preferred_element_type=F32)
    h = x_ref[...] + mod_ref[0, 2:3, :] * y
    h_ref[...] = h
    ms = jnp.mean(h * h, axis=-1, keepdims=True)
    f = h * lax.rsqrt(ms + EPS) * gffn_ref[...] * (1.0 + mod_ref[0, 4:5, :]) + mod_ref[0, 3:4, :]
    f_ref[...] = f

    logits = jnp.dot(f, wr_ref[...], preferred_element_type=F32,
                     precision=lax.Precision.HIGHEST) + br_ref[...]
    lane = lax.broadcasted_iota(jnp.int32, (tm, ROUTE_LANES), 1)
    big = jnp.int32(ROUTE_LANES)

    def first_argmax(vals):
        m = jnp.max(vals, axis=-1, keepdims=True)
        idx = jnp.min(jnp.where(vals == m, lane, big), axis=-1, keepdims=True)
        return m, idx

    is_grp = lane < N_GROUPS
    lg = jnp.where(is_grp, logits, -jnp.inf)
    mg, grp = first_argmax(lg)
    p_grp = 1.0 / jnp.sum(jnp.where(is_grp, jnp.exp(lg - mg), 0.0), axis=-1, keepdims=True)
    lo = EXPERT_LANE0 + grp * EXPERTS_PER_GROUP
    in_grp = (lane >= lo) & (lane < lo + EXPERTS_PER_GROUP)
    le = jnp.where(in_grp, logits, -jnp.inf)
    m1, i1 = first_argmax(le)
    m2, i2 = first_argmax(jnp.where(lane == i1, -jnp.inf, le))
    t = jnp.exp(m2 - m1)
    w1 = p_grp / (1.0 + t)
    w2 = p_grp * t / (1.0 + t)

    hot1 = lane == i1
    hot2 = lane == i2
    onehot = (hot1 | hot2).astype(BF16)
    r_i = lax.broadcasted_iota(jnp.int32, (tm, tm), 0)
    c_i = lax.broadcasted_iota(jnp.int32, (tm, tm), 1)
    before = (c_i < r_i).astype(BF16)
    rank = jnp.dot(before, onehot, preferred_element_type=F32) + cnt_ref[...]
    rank1 = jnp.sum(jnp.where(hot1, rank, 0.0), axis=-1, keepdims=True)
    rank2 = jnp.sum(jnp.where(hot2, rank, 0.0), axis=-1, keepdims=True)
    cnt_ref[...] = cnt_ref[...] + jnp.sum(onehot.astype(F32), axis=0, keepdims=True)

    e1 = (i1 - EXPERT_LANE0).astype(F32)
    e2 = (i2 - EXPERT_LANE0).astype(F32)
    route = jnp.where(lane == 0, e1, 0.0)
    route = jnp.where(lane == 1, e2, route)
    route = jnp.where(lane == 2, w1, route)
    route = jnp.where(lane == 3, w2, route)
    route = jnp.where(lane == 4, rank1, route)
    route = jnp.where(lane == 5, rank2, route)
    route_ref[...] = route


def _merge_route(ona, ogla, sgn, sgg, x2d, mod3, gffn, wna, wgla, wout, wr, br, tokens_per_batch, tm):
    n, d = x2d.shape
    per_b = tokens_per_batch // tm

    def tile(w):
        return pl.BlockSpec((tm, w), lambda i: (i, 0))

    def whole(a):
        return pl.BlockSpec(a.shape, lambda i: (0,) * a.ndim)

    return pl.pallas_call(
        _merge_kernel,
        grid=(n // tm,),
        in_specs=[tile(NA_WIDTH), tile(GLA_DV), tile(d), tile(d), tile(d),
                  pl.BlockSpec((1, 6, d), lambda i: (i // per_b, 0, 0)),
                  whole(gffn), whole(wna), whole(wgla), whole(wout), whole(wr), whole(br)],
        out_specs=[tile(d), tile(d), tile(ROUTE_LANES),
                   pl.BlockSpec((1, ROUTE_LANES), lambda i: (0, 0))],
        out_shape=[jax.ShapeDtypeStruct((n, d), F32),
                   jax.ShapeDtypeStruct((n, d), F32),
                   jax.ShapeDtypeStruct((n, ROUTE_LANES), F32),
                   jax.ShapeDtypeStruct((1, ROUTE_LANES), F32)],
        compiler_params=_params(("arbitrary",)),
        name="merge_route",
    )(ona, ogla, sgn, sgg, x2d, mod3, gffn, wna, wgla, wout, wr, br)


def _row_copy(src_ref, src_row, dst_ref, dst_row, sem):
    return pltpu.make_async_copy(src_ref.at[pl.ds(src_row, 1), :], dst_ref.at[pl.ds(dst_row, 1), :], sem)


def _dispatch_kernel(dest_ref, f_ref, xb_in_ref, xb_ref, sem):
    del xb_in_ref
    tm = f_ref.shape[0]

    def start(t, carry):
        for k in range(2):
            _row_copy(f_ref, t, xb_ref, dest_ref[0, 0, k * tm + t], sem).start()
        return carry

    def wait(t, carry):
        for k in range(2):
            _row_copy(f_ref, t, xb_ref, dest_ref[0, 0, k * tm + t], sem).wait()
        return carry

    lax.fori_loop(0, tm, start, 0)
    lax.fori_loop(0, tm, wait, 0)


def _dispatch(dest3, f2d, n_pad, tm):
    n, d = f2d.shape
    xb0 = jnp.zeros((n_pad, d), F32)
    return pl.pallas_call(
        _dispatch_kernel,
        grid=(n // tm,),
        in_specs=[pl.BlockSpec((1, 1, 2 * tm), lambda i: (i, 0, 0), memory_space=pltpu.SMEM),
                  pl.BlockSpec((tm, d), lambda i: (i, 0)),
                  pl.BlockSpec(memory_space=pl.ANY)],
        out_specs=pl.BlockSpec(memory_space=pl.ANY),
        out_shape=jax.ShapeDtypeStruct((n_pad, d), F32),
        scratch_shapes=[pltpu.SemaphoreType.DMA],
        input_output_aliases={2: 0},
        compiler_params=_params(("arbitrary",)),
        name="dispatch",
    )(dest3, f2d, xb0)


def _expert_kernel(be_ref, nu_ref, x_ref, wg_ref, wu_ref, wd_ref, y_ref, wgu_scr, wd_scr):
    i = pl.program_id(0)
    prev = be_ref[jnp.maximum(i - 1, 0)]

    @pl.when((i == 0) | (be_ref[i] != prev))
    def _():
        wgu_scr[:, :D_EXPERT] = wg_ref[0].astype(BF16)
        wgu_scr[:, D_EXPERT:] = wu_ref[0].astype(BF16)
        wd_scr[...] = wd_ref[0].astype(BF16)

    @pl.when(i < nu_ref[0])
    def _():
        x = x_ref[...].astype(BF16)
        gu = jnp.dot(x, wgu_scr[...], preferred_element_type=F32)
        gate = gu[:, :D_EXPERT]
        hdn = gate * jax.nn.sigmoid(gate) * gu[:, D_EXPERT:]
        y_ref[...] = jnp.dot(hdn.astype(BF16), wd_scr[...], preferred_element_type=F32)

    @pl.when(i >= nu_ref[0])
    def _():
        y_ref[...] = jnp.zeros_like(y_ref)


def _experts(blk_expert, n_used, xb, wg, wu, wd):
    n_pad, d = xb.shape
    n_blk = n_pad // MOE_ROWS
    grid_spec = pltpu.PrefetchScalarGridSpec(
        num_scalar_prefetch=2,
        grid=(n_blk,),
        in_specs=[pl.BlockSpec((MOE_ROWS, d), lambda i, be, nu: (i, 0)),
                  pl.BlockSpec((1, d, D_EXPERT), lambda i, be, nu: (be[i], 0, 0)),
                  pl.BlockSpec((1, d, D_EXPERT), lambda i, be, nu: (be[i], 0, 0)),
                  pl.BlockSpec((1, D_EXPERT, d), lambda i, be, nu: (be[i], 0, 0))],
        out_specs=pl.BlockSpec((MOE_ROWS, d), lambda i, be, nu: (i, 0)),
        scratch_shapes=[pltpu.VMEM((d, 2 * D_EXPERT), BF16),
                        pltpu.VMEM((D_EXPERT, d), BF16)],
    )
    return pl.pallas_call(
        _expert_kernel,
        grid_spec=grid_spec,
        out_shape=jax.ShapeDtypeStruct((n_pad, d), F32),
        compiler_params=_params(("arbitrary",)),
        name="experts",
    )(blk_expert, n_used, xb, wg, wu, wd)


def _combine_kernel(dest_ref, h_ref, route_ref, mod_ref, fg_ref, yb_ref, o_ref, ybuf, sem):
    tm = h_ref.shape[0]

    def start(t, carry):
        for k in range(2):
            _row_copy(yb_ref, dest_ref[0, 0, k * tm + t], ybuf.at[k], t, sem).start()
        return carry

    def wait(t, carry):
        for k in range(2):
            _row_copy(yb_ref, dest_ref[0, 0, k * tm + t], ybuf.at[k], t, sem).wait()
        return carry

    lax.fori_loop(0, tm, start, 0)
    lax.fori_loop(0, tm, wait, 0)
    moe = route_ref[:, 2:3] * ybuf[0] + route_ref[:, 3:4] * ybuf[1]
    h = h_ref[...] + mod_ref[0, 5:6, :] * moe
    ms = jnp.mean(h * h, axis=-1, keepdims=True)
    o_ref[...] = h * lax.rsqrt(ms + EPS) * fg_ref[...]


def _combine(dest3, h2d, route, mod3, fg, yb, tokens_per_batch, tm):
    n, d = h2d.shape
    per_b = tokens_per_batch // tm
    return pl.pallas_call(
        _combine_kernel,
        grid=(n // tm,),
        in_specs=[pl.BlockSpec((1, 1, 2 * tm), lambda i: (i, 0, 0), memory_space=pltpu.SMEM),
                  pl.BlockSpec((tm, d), lambda i: (i, 0)),
                  pl.BlockSpec((tm, ROUTE_LANES), lambda i: (i, 0)),
                  pl.BlockSpec((1, 6, d), lambda i: (i // per_b, 0, 0)),
                  pl.BlockSpec((1, d), lambda i: (0, 0)),
                  pl.BlockSpec(memory_space=pl.ANY)],
        out_specs=pl.BlockSpec((tm, d), lambda i: (i, 0)),
        out_shape=jax.ShapeDtypeStruct((n, d), F32),
        scratch_shapes=[pltpu.VMEM((2, tm, d), F32), pltpu.SemaphoreType.DMA],
        compiler_params=_params(("arbitrary",)),
        name="combine",
    )(dest3, h2d, route, mod3, fg, yb)


def _rope_tables(s):
    t = np.arange(s)
    pos_r = jnp.asarray(t // GRID_W, F32)
    pos_c = jnp.asarray(t % GRID_W, F32)
    nf = GLA_HK // 4
    inv = ROPE_BASE ** (-jnp.arange(nf, dtype=F32) / nf)
    ang_r = pos_r[:, None] * inv
    ang_c = pos_c[:, None] * inv

    def half(ang):
        return (jnp.concatenate([jnp.cos(ang), jnp.cos(ang)], axis=-1),
                jnp.concatenate([-jnp.sin(ang), jnp.sin(ang)], axis=-1))

    cr, sr = half(ang_r)
    cc, sc = half(ang_c)
    cos = jnp.tile(jnp.concatenate([cr, cc], axis=-1), (1, GLA_HEADS))
    sin = jnp.tile(jnp.concatenate([sr, sc], axis=-1), (1, GLA_HEADS))
    return cos.astype(F32), sin.astype(F32)


def _dest_blocks(dest, tm):
    n = dest.shape[0]
    return dest.reshape(n // tm, tm, 2).transpose(0, 2, 1).reshape(n // tm, 1, 2 * tm)


def kernel(x, c, ctx, c_ctx, w_mod, b_mod, norm_attn_g, norm_ffn_g, w_in, w_gla_a2, b_gla_a2, gla_norm_g, na_rpb, w_na_o, w_gla_o, w_out, w_group, b_group, w_expert, b_expert, w_exp_gate, w_exp_up, w_exp_down, final_norm_g):
    b, s, d = x.shape
    ctx_len = ctx.shape[1]
    n = b * s
    assert w_mod.shape[0] == 1, "single-layer block"
    assert s % (GRID_W * NA_WIN_H) == 0 and ctx_len % GLA_CHUNK == 0

    mod_rows = -(-(b + 1) // 8) * 8
    cs = jnp.zeros((mod_rows, d), F32).at[:b].set(c).at[b].set(c_ctx)
    mod3 = _modulation(cs, w_mod[0], b_mod[0]).reshape(mod_rows, 6, d)

    w = w_in[0].astype(BF16)
    o_q, o_k, o_v = 0, NA_WIDTH, 2 * NA_WIDTH
    o_gq = 3 * NA_WIDTH
    o_gk = o_gq + GLA_DK
    o_gv = o_gk + GLA_DK
    o_og = o_gv + GLA_DV
    o_lr = o_og + GLA_DV
    o_mn = o_lr + 2 * GLA_GATE_RANK
    o_mg = o_mn + d
    w_lr = jnp.pad(w[:, o_lr:o_mn], ((0, 0), (0, LR_PAD - 2 * GLA_GATE_RANK)))
    w_lat = jnp.concatenate([w[:, :o_lr], w[:, o_mn:], w_lr], axis=1)
    cw = 512
    lat_specs = [(0, 0, o_q, cw, "plain"), (1, 0, o_k, cw, "plain"), (2, 0, o_v, cw, "plain"),
                 (3, 0, o_gq, cw, "rope_scaled"), (4, 0, o_gk, cw, "rope"),
                 (5, 0, o_gv, cw, "plain"), (5, cw, o_gv + cw, cw, "plain"),
                 (6, 0, o_og, cw, "silu"), (6, cw, o_og + cw, cw, "silu"),
                 (7, 0, o_lr, cw, "sigmoid"), (7, cw, o_lr + cw, cw, "sigmoid"),
                 (8, 0, o_lr + d, cw, "sigmoid"), (8, cw, o_lr + d + cw, cw, "sigmoid"),
                 (9, 0, o_lr + 2 * d, LR_PAD, "plain")]
    lat_widths = [NA_WIDTH, NA_WIDTH, NA_WIDTH, GLA_DK, GLA_DK, GLA_DV, GLA_DV, d, d, LR_PAD]
    tm = 512
    per_b = s // tm
    q_na, k_na, v_na, gq, gk, gv, sog, sgn, sgg, lr = _inproj(
        x.reshape(n, d), mod3, lambda i: i // per_b, norm_attn_g[0], w_lat, lat_specs, lat_widths, tm,
        rope=_rope_tables(s), name="inproj_lat")

    w_ctx = jnp.concatenate([w[:, o_k:o_gq], w[:, o_gk:o_og], w_lr], axis=1)
    ctx_specs = [(0, 0, 0, cw, "plain"), (1, 0, cw, cw, "plain"), (2, 0, 2 * cw, cw, "plain"),
                 (3, 0, 3 * cw, cw, "plain"), (3, cw, 4 * cw, cw, "plain"),
                 (4, 0, 5 * cw, LR_PAD, "plain")]
    ctx_widths = [NA_WIDTH, NA_WIDTH, GLA_DK, GLA_DV, LR_PAD]
    kc_na, vc_na, ck, cv, clr = _inproj(
        ctx.reshape(b * ctx_len, d), mod3, lambda i: b, norm_attn_g[0], w_ctx, ctx_specs, ctx_widths,
        ctx_len, name="inproj_ctx")

    def lat3(a):
        return a.reshape(b, s, a.shape[-1])

    def ctx3(a):
        return a.reshape(b, ctx_len, a.shape[-1])

    o_na = _na_attention(lat3(q_na), lat3(k_na), lat3(v_na), ctx3(kc_na), ctx3(vc_na),
                         _na_bias_table(na_rpb[0]))

    wa = jnp.zeros((2, LR_PAD, GLA_DK), F32)
    wa = wa.at[0, :GLA_GATE_RANK].set(w_gla_a2[0, 0]).at[1, GLA_GATE_RANK:2 * GLA_GATE_RANK].set(w_gla_a2[0, 1])
    gn = jnp.tile(gla_norm_g[0], GLA_HEADS).reshape(1, GLA_DV)
    o_gla = _gla(lat3(gq), lat3(gk), lat3(gv), lat3(lr), lat3(sog), ctx3(ck), ctx3(cv), ctx3(clr),
                 wa.astype(BF16), b_gla_a2[0], gn)

    wr = jnp.zeros((d, ROUTE_LANES), F32)
    wr = wr.at[:, :N_GROUPS].set(w_group[0]).at[:, EXPERT_LANE0:EXPERT_LANE0 + N_EXPERTS].set(w_expert[0])
    br = jnp.zeros((1, ROUTE_LANES), F32)
    br = br.at[0, :N_GROUPS].set(b_group[0]).at[0, EXPERT_LANE0:EXPERT_LANE0 + N_EXPERTS].set(b_expert[0])
    h, f_lat, route, counts = _merge_route(
        o_na.reshape(n, NA_WIDTH), o_gla.reshape(n, GLA_DV), sgn, sgg, x.reshape(n, d), mod3, norm_ffn_g[0].reshape(1, d),
        w_na_o[0].astype(BF16), w_gla_o[0].astype(BF16), w_out[0].astype(BF16), wr, br, s, tm)

    cnt = counts[0, EXPERT_LANE0:EXPERT_LANE0 + N_EXPERTS].astype(jnp.int32)
    padded = (cnt + MOE_ROWS - 1) // MOE_ROWS * MOE_ROWS
    pend = jnp.cumsum(padded)
    pstart = pend - padded
    n_pad = (2 * n + N_EXPERTS * (MOE_ROWS - 1) + MOE_ROWS - 1) // MOE_ROWS * MOE_ROWS
    n_blk = n_pad // MOE_ROWS
    eid = route[:, 0:2].astype(jnp.int32)
    rank = route[:, 4:6].astype(jnp.int32)
    onehot = eid[:, :, None] == jnp.arange(N_EXPERTS, dtype=jnp.int32)
    dest = rank + jnp.sum(jnp.where(onehot, pstart, 0), axis=-1)
    blk_start = jnp.arange(n_blk, dtype=jnp.int32) * MOE_ROWS
    blk_expert = jnp.minimum(jnp.sum(blk_start[:, None] >= pend[None, :], axis=-1), N_EXPERTS - 1)
    n_used = (pend[-1] // MOE_ROWS).reshape(1)

    tg = 256
    dest3 = _dest_blocks(dest.astype(jnp.int32), tg)
    xb = _dispatch(dest3, f_lat, n_pad, tg)
    yb = _experts(blk_expert.astype(jnp.int32), n_used.astype(jnp.int32), xb,
                  w_exp_gate[0], w_exp_up[0], w_exp_down[0])
    out = _combine(dest3, h, route, mod3, final_norm_g.reshape(1, d), yb, s, tg)
    return out.reshape(b, s, d)
```

```python
import functools

import jax
import jax.numpy as jnp
import numpy as np
from jax import lax
from jax.experimental import pallas as pl
from jax.experimental.pallas import tpu as pltpu

F32 = jnp.float32
BF16 = jnp.bfloat16

D_MODEL = 1024
GRID_W = 64
NA_HEADS = 8
NA_HEAD_DIM = 64
NA_WIDTH = NA_HEADS * NA_HEAD_DIM
NA_WIN_H = 8
NA_WIN_W = 16
GLA_HEADS = 4
GLA_DK = D_MODEL // 2
GLA_DV = D_MODEL
GLA_HK = GLA_DK // GLA_HEADS
GLA_HV = GLA_DV // GLA_HEADS
GLA_GATE_RANK = 16
GLA_GATE_NORM = 16.0
GLA_CHUNK = 64
GLA_GROUP = 256
ROPE_BASE = 10000.0
N_GROUPS = 4
EXPERTS_PER_GROUP = 8
N_EXPERTS = N_GROUPS * EXPERTS_PER_GROUP
D_EXPERT = D_MODEL // 2
EPS = 1e-6
NEG_INF = -1e30

LANES = 128
LR_PAD = LANES
ROUTE_LANES = LANES
EXPERT_LANE0 = N_GROUPS
MOE_ROWS = 256
SEG_ALIGN = 8
SEG_SIZES = (512, 256, 128, 64, 32, 16, 8)
TAIL_SIZES = (128, 64, 32, 16, 8)
VMEM_LIMIT = 56 * 1024 * 1024

_NT = (((1,), (1,)), ((), ()))
_TN = (((0,), (0,)), ((), ()))


def _params(sem, vmem=VMEM_LIMIT):
    return pltpu.CompilerParams(dimension_semantics=sem, vmem_limit_bytes=vmem)


def _mod_kernel(c_ref, w_ref, b_ref, o_ref):
    c = c_ref[...]
    s = c * jax.nn.sigmoid(c)
    o_ref[...] = jnp.dot(s.astype(BF16), w_ref[...].astype(BF16),
                         preferred_element_type=F32) + b_ref[...]


def _modulation(cs, w_mod, b_mod):
    rows, d = cs.shape
    n = w_mod.shape[1]
    bn = 1024
    return pl.pallas_call(
        _mod_kernel,
        grid=(n // bn,),
        in_specs=[pl.BlockSpec((rows, d), lambda j: (0, 0)),
                  pl.BlockSpec((d, bn), lambda j: (0, j)),
                  pl.BlockSpec((1, bn), lambda j: (0, j))],
        out_specs=pl.BlockSpec((rows, bn), lambda j: (0, j)),
        out_shape=jax.ShapeDtypeStruct((rows, n), F32),
        compiler_params=_params(("arbitrary",)),
        name="mod",
    )(cs, w_mod, b_mod.reshape(1, n))


def _inproj_kernel(specs, use_rope, n_out, x_ref, mod_ref, g_ref, *rest):
    if use_rope:
        cos_ref, sin_ref, w_ref = rest[:3]
        rest = rest[3:]
    else:
        w_ref = rest[0]
        rest = rest[1:]
    outs = rest[:n_out]
    a_scr = rest[n_out]
    x = x_ref[...]
    ms = jnp.mean(x * x, axis=-1, keepdims=True)
    a = x * lax.rsqrt(ms + EPS) * g_ref[...] * (1.0 + mod_ref[0, 1:2, :]) + mod_ref[0, 0:1, :]
    a_scr[...] = a.astype(BF16)
    for (oi, oc, wc, width, kind) in specs:
        acc = jnp.dot(a_scr[...], w_ref[:, wc:wc + width], preferred_element_type=F32)
        if kind in ("rope", "rope_scaled"):
            if kind == "rope_scaled":
                acc = acc * (GLA_HK ** -0.5)
            lane = lax.broadcasted_iota(jnp.int32, acc.shape, 1)
            rot = jnp.where((lane & 63) < 32,
                            pltpu.roll(acc, width - 32, axis=1),
                            pltpu.roll(acc, 32, axis=1))
            acc = acc * cos_ref[...] + rot * sin_ref[...]
        elif kind == "silu":
            acc = acc * jax.nn.sigmoid(acc)
        elif kind == "sigmoid":
            acc = jax.nn.sigmoid(acc)
        outs[oi][:, oc:oc + width] = acc.astype(outs[oi].dtype)


def _inproj(x2d, mod3, mod_row_fn, g, w_packed, specs, out_widths, tm, rope=None, name="inproj"):
    n, d = x2d.shape
    wcols = w_packed.shape[1]
    use_rope = rope is not None
    in_specs = [pl.BlockSpec((tm, d), lambda i: (i, 0)),
                pl.BlockSpec((1, 6, d), lambda i: (mod_row_fn(i), 0, 0)),
                pl.BlockSpec((1, d), lambda i: (0, 0))]
    args = [x2d, mod3, g.reshape(1, d)]
    if use_rope:
        cos, sin = rope
        nb = cos.shape[0] // tm
        in_specs += [pl.BlockSpec((tm, cos.shape[1]), lambda i: (i % nb, 0)),
                     pl.BlockSpec((tm, cos.shape[1]), lambda i: (i % nb, 0))]
        args += [cos, sin]
    in_specs.append(pl.BlockSpec((d, wcols), lambda i: (0, 0), pipeline_mode=pl.Buffered(1)))
    args.append(w_packed)
    out_specs = [pl.BlockSpec((tm, w), lambda i: (i, 0)) for w in out_widths]
    out_shape = [jax.ShapeDtypeStruct((n, w), BF16) for w in out_widths]
    return pl.pallas_call(
        functools.partial(_inproj_kernel, specs, use_rope, len(out_widths)),
        grid=(n // tm,),
        in_specs=in_specs,
        out_specs=out_specs,
        out_shape=out_shape,
        scratch_shapes=[pltpu.VMEM((tm, d), BF16)],
        compiler_params=_params(("arbitrary",)),
        name=name,
    )(*args)


def _na_kernel(rows_per_step, n_rows, q_ref, k_ref, v_ref, kc_ref, vc_ref, bias_ref, o_ref):
    j = pl.program_id(1)
    lane = lax.broadcasted_iota(jnp.int32, (GRID_W, LANES), 1)
    first_head = lane < NA_HEAD_DIM
    scale = NA_HEAD_DIM ** -0.5
    win_keys = NA_WIN_H * GRID_W

    def row_body(rr, carry):
        r = j * rows_per_step + rr
        rs = jnp.clip(r - NA_WIN_H // 2, 0, n_rows - NA_WIN_H)
        d = r - rs
        t0 = pl.multiple_of(rr * GRID_W, GRID_W)
        ks = pl.multiple_of(rs * GRID_W, GRID_W)
        pairs = range(NA_HEADS // 2)
        lanes = [slice(p * LANES, (p + 1) * LANES) for p in pairs]
        sws, scs = [], []
        for p in pairs:
            qp = q_ref[0, pl.ds(t0, GRID_W), lanes[p]]
            zero = jnp.zeros_like(qp)
            qs = jnp.concatenate([jnp.where(first_head, qp, zero), jnp.where(first_head, zero, qp)], axis=0)
            sws.append(lax.dot_general(qs, k_ref[0, pl.ds(ks, win_keys), lanes[p]], _NT,
                                       preferred_element_type=F32))
            scs.append(lax.dot_general(qs, kc_ref[0, :, lanes[p]], _NT, preferred_element_type=F32))
        pws, pcs, dens = [], [], []
        for p in pairs:
            sw = sws[p] * scale + bias_ref[d, p * LANES:(p + 1) * LANES, :]
            sc = scs[p] * scale
            m = jnp.maximum(jnp.max(sw, axis=-1, keepdims=True), jnp.max(sc, axis=-1, keepdims=True))
            pw = jnp.exp(sw - m)
            pc = jnp.exp(sc - m)
            dens.append(jnp.sum(pw, axis=-1, keepdims=True) + jnp.sum(pc, axis=-1, keepdims=True))
            pws.append(pw.astype(BF16))
            pcs.append(pc.astype(BF16))
        for p in pairs:
            o2 = (jnp.dot(pws[p], v_ref[0, pl.ds(ks, win_keys), lanes[p]], preferred_element_type=F32)
                  + jnp.dot(pcs[p], vc_ref[0, :, lanes[p]], preferred_element_type=F32)) / dens[p]
            o_ref[0, pl.ds(t0, GRID_W), lanes[p]] = jnp.where(first_head, o2[:GRID_W],
                                                             o2[GRID_W:]).astype(o_ref.dtype)
        return carry

    lax.fori_loop(0, rows_per_step, row_body, 0)


def _na_bias_table(rpb):
    c = np.arange(GRID_W)
    cstart = np.clip(c - NA_WIN_W // 2, 0, GRID_W - NA_WIN_W)
    kc = np.arange(GRID_W)
    valid = (kc[None, :] >= cstart[:, None]) & (kc[None, :] < cstart[:, None] + NA_WIN_W)
    col_idx = np.clip(kc[None, :] - c[:, None] + NA_WIN_W - 1, 0, 2 * NA_WIN_W - 2)
    t = jnp.where(jnp.asarray(valid)[None, None], rpb[:, :, col_idx].astype(F32), NEG_INF)
    dr = (NA_WIN_H - 1) - np.arange(NA_WIN_H)[:, None] + np.arange(NA_WIN_H)[None, :]
    full = t[:, dr]
    full = full.transpose(1, 0, 3, 2, 4)
    return full.reshape(NA_WIN_H, rpb.shape[0] * GRID_W, NA_WIN_H * GRID_W)


def _na_attention(q, k, v, kc, vc, bias, rows_per_step=4):
    b, s, w = q.shape
    n_rows = s // GRID_W
    ctx_len = kc.shape[1]
    blk = rows_per_step * GRID_W
    return pl.pallas_call(
        functools.partial(_na_kernel, rows_per_step, n_rows),
        grid=(b, n_rows // rows_per_step),
        in_specs=[pl.BlockSpec((1, blk, w), lambda bi, j: (bi, j, 0)),
                  pl.BlockSpec((1, s, w), lambda bi, j: (bi, 0, 0)),
                  pl.BlockSpec((1, s, w), lambda bi, j: (bi, 0, 0)),
                  pl.BlockSpec((1, ctx_len, w), lambda bi, j: (bi, 0, 0)),
                  pl.BlockSpec((1, ctx_len, w), lambda bi, j: (bi, 0, 0)),
                  pl.BlockSpec(bias.shape, lambda bi, j: (0, 0, 0), pipeline_mode=pl.Buffered(1))],
        out_specs=pl.BlockSpec((1, blk, w), lambda bi, j: (bi, j, 0)),
        out_shape=jax.ShapeDtypeStruct((b, s, w), BF16),
        compiler_params=_params(("arbitrary", "arbitrary")),
        name="na_attn",
    )(q, k, v, kc, vc, bias)


def _gla_kernel(n_groups, gq_ref, gk_ref, gv_ref, lr_ref, sog_ref, ck_ref, cv_ref, clr_ref,
                wa_ref, ba_ref, gn_ref, o_ref, st_scr, of_scr, qe_scr, kd_scr, ckd_scr, dec_scr, cdec_scr):
    c_len = GLA_CHUNK
    grp = GLA_GROUP
    cpg = grp // c_len
    row = lax.broadcasted_iota(jnp.int32, (grp, grp), 0)
    col = lax.broadcasted_iota(jnp.int32, (grp, grp), 1)
    same_chunk = (row // c_len) == (col // c_len)
    masks = (same_chunk & (col <= row), same_chunk & (col >= row))

    def gates(lr_rows, k_rows, direction):
        tri = masks[direction].astype(BF16)
        z = jnp.dot(lr_rows, wa_ref[direction], preferred_element_type=F32) + ba_ref[direction:direction + 1, :]
        g = (jnp.minimum(z, 0.0) - jnp.log(1.0 + jnp.exp(-jnp.abs(z)))) * (1.0 / GLA_GATE_NORM)
        g1 = g.astype(BF16)
        g2 = (g - g1.astype(F32)).astype(BF16)
        b = jnp.dot(tri, g1, preferred_element_type=F32) + jnp.dot(tri, g2, preferred_element_type=F32)
        edge = c_len - 1 if direction == 0 else 0
        lasts = [b[j * c_len + edge:j * c_len + edge + 1, :] for j in range(cpg)]
        b_last = jnp.concatenate([jnp.broadcast_to(v, (c_len, v.shape[1])) for v in lasts], axis=0)
        kd = (k_rows * jnp.exp(b_last - b)).astype(BF16)
        dec = jnp.exp(jnp.concatenate(lasts + [jnp.zeros_like(v) for v in lasts], axis=0))
        return b, kd, dec

    ck = ck_ref[0].astype(F32)
    for direction in range(2):
        _, kd, dec = gates(clr_ref[0], ck, direction)
        ckd_scr[direction] = kd
        cdec_scr[direction] = dec

    def group_body(gi, carry):
        r0 = pl.multiple_of(gi * grp, grp)
        rows = pl.ds(r0, grp)
        qf = gq_ref[0, rows, :].astype(F32)
        kf = gk_ref[0, rows, :].astype(F32)
        lr_rows = lr_ref[0, rows, :]
        qes, kes = [], []
        for direction in range(2):
            b, kd, dec = gates(lr_rows, kf, direction)
            qe = (qf * jnp.exp(b)).astype(BF16)
            qes.append(qe)
            kes.append((kf * jnp.exp(-b)).astype(BF16))
            qe_scr[direction, rows, :] = qe
            kd_scr[direction, rows, :] = kd
            dec_scr[direction, gi] = dec
        for h in range(GLA_HEADS):
            kl = slice(h * GLA_HK, (h + 1) * GLA_HK)
            vl = slice(h * GLA_HV, (h + 1) * GLA_HV)
            att = (jnp.where(masks[0], lax.dot_general(qes[0][:, kl], kes[0][:, kl], _NT,
                                                       preferred_element_type=F32), 0.0)
                   + jnp.where(masks[1], lax.dot_general(qes[1][:, kl], kes[1][:, kl], _NT,
                                                         preferred_element_type=F32), 0.0))
            of_scr[rows, vl] = jnp.dot(att.astype(BF16), gv_ref[0, rows, vl], preferred_element_type=F32)
        return carry

    lax.fori_loop(0, n_groups, group_body, 0)

    def state_step(h, v_rows, kd_rows, dec_row):
        kl = slice(h * GLA_HK, (h + 1) * GLA_HK)
        upd = lax.dot_general(v_rows, kd_rows[:, kl], _TN, preferred_element_type=F32)
        st_scr[h] = st_scr[h] * dec_row[:, kl] + upd

    def scan(direction):
        st_scr[...] = jnp.zeros_like(st_scr)
        order = range(cpg) if direction == 0 else range(cpg - 1, -1, -1)
        for j in order:
            rows = slice(j * c_len, (j + 1) * c_len)
            for h in range(GLA_HEADS):
                state_step(h, cv_ref[0, rows, h * GLA_HV:(h + 1) * GLA_HV],
                           ckd_scr[direction, rows, :], cdec_scr[direction, j:j + 1, :])

        def group_scan(i, carry):
            gi = i if direction == 0 else n_groups - 1 - i
            for j in order:
                rows = pl.ds(pl.multiple_of(gi * grp + j * c_len, c_len), c_len)
                qe_rows = qe_scr[direction, rows, :]
                kd_rows = kd_scr[direction, rows, :]
                dec_row = dec_scr[direction, gi, j:j + 1, :]
                for h in range(GLA_HEADS):
                    kl = slice(h * GLA_HK, (h + 1) * GLA_HK)
                    vl = slice(h * GLA_HV, (h + 1) * GLA_HV)
                    o_t = of_scr[rows, vl] + lax.dot_general(qe_rows[:, kl], st_scr[h].astype(BF16), _NT,
                                                             preferred_element_type=F32)
                    if direction == 0:
                        of_scr[rows, vl] = o_t
                    else:
                        ms = jnp.mean(o_t * o_t, axis=-1, keepdims=True)
                        o_n = o_t * lax.rsqrt(ms + EPS) * gn_ref[:, vl] * sog_ref[0, rows, vl].astype(F32)
                        o_ref[0, rows, vl] = o_n.astype(o_ref.dtype)
                    state_step(h, gv_ref[0, rows, vl], kd_rows, dec_row)
            return carry

        lax.fori_loop(0, n_groups, group_scan, 0)

    scan(0)
    scan(1)


def _gla(gq, gk, gv, lr, sog, ck, cv, clr, wa, ba, gn):
    b, s, _ = gq.shape
    ctx_len = ck.shape[1]
    assert ctx_len == GLA_GROUP and s % GLA_GROUP == 0
    n_groups = s // GLA_GROUP
    cpg = GLA_GROUP // GLA_CHUNK

    def per_batch(t, w):
        return pl.BlockSpec((1, t, w), lambda bi: (bi, 0, 0))

    return pl.pallas_call(
        functools.partial(_gla_kernel, n_groups),
        grid=(b,),
        in_specs=[per_batch(s, GLA_DK), per_batch(s, GLA_DK), per_batch(s, GLA_DV),
                  per_batch(s, LR_PAD), per_batch(s, GLA_DV),
                  per_batch(ctx_len, GLA_DK), per_batch(ctx_len, GLA_DV), per_batch(ctx_len, LR_PAD),
                  pl.BlockSpec(wa.shape, lambda bi: (0, 0, 0)),
                  pl.BlockSpec(ba.shape, lambda bi: (0, 0)),
                  pl.BlockSpec(gn.shape, lambda bi: (0, 0))],
        out_specs=per_batch(s, GLA_DV),
        out_shape=jax.ShapeDtypeStruct((b, s, GLA_DV), BF16),
        scratch_shapes=[pltpu.VMEM((GLA_HEADS, GLA_HV, GLA_HK), F32),
                        pltpu.VMEM((s, GLA_DV), F32),
                        pltpu.VMEM((2, s, GLA_DK), BF16),
                        pltpu.VMEM((2, s, GLA_DK), BF16),
                        pltpu.VMEM((2, ctx_len, GLA_DK), BF16),
                        pltpu.VMEM((2, n_groups, 2 * cpg, GLA_DK), F32),
                        pltpu.VMEM((2, 2 * cpg, GLA_DK), F32)],
        compiler_params=_params(("arbitrary",), vmem=60 * 1024 * 1024),
        name="gla",
    )(gq, gk, gv, lr, sog, ck, cv, clr, wa, ba, gn)


def _merge_kernel(ona_ref, ogla_ref, sgn_ref, sgg_ref, x_ref, mod_ref, gffn_ref,
                  wna_ref, wgla_ref, wout_ref, wr_ref, br_ref,
                  h_ref, f_ref, route_ref, cnt_ref):
    tm = x_ref.shape[0]
    y1 = (sgn_ref[...].astype(F32) * jnp.dot(ona_ref[...], wna_ref[...], preferred_element_type=F32)
          + sgg_ref[...].astype(F32) * jnp.dot(ogla_ref[...], wgla_ref[...], preferred_element_type=F32))
    y = jnp.dot(y1.astype(BF16), wout_ref[...], preferred_element_type=F32)
    h = x_ref[...] + mod_ref[0, 2:3, :] * y
    h_ref[...] = h
    ms = jnp.mean(h * h, axis=-1, keepdims=True)
    f = h * lax.rsqrt(ms + EPS) * gffn_ref[...] * (1.0 + mod_ref[0, 4:5, :]) + mod_ref[0, 3:4, :]
    f_ref[...] = f.astype(f_ref.dtype)

    f_hi = f.astype(BF16)
    f_lo = (f - f_hi.astype(F32)).astype(BF16)
    logits = (jnp.dot(f_hi, wr_ref[0], preferred_element_type=F32)
              + jnp.dot(f_hi, wr_ref[1], preferred_element_type=F32)
              + jnp.dot(f_lo, wr_ref[0], preferred_element_type=F32)) + br_ref[...]
    lane = lax.broadcasted_iota(jnp.int32, (tm, ROUTE_LANES), 1)
    big = jnp.int32(ROUTE_LANES)

    def first_argmax(vals):
        m = jnp.max(vals, axis=-1, keepdims=True)
        idx = jnp.min(jnp.where(vals == m, lane, big), axis=-1, keepdims=True)
        return m, idx

    is_grp = lane < N_GROUPS
    lg = jnp.where(is_grp, logits, -jnp.inf)
    mg, grp = first_argmax(lg)
    p_grp = 1.0 / jnp.sum(jnp.where(is_grp, jnp.exp(lg - mg), 0.0), axis=-1, keepdims=True)
    lo = EXPERT_LANE0 + grp * EXPERTS_PER_GROUP
    in_grp = (lane >= lo) & (lane < lo + EXPERTS_PER_GROUP)
    le = jnp.where(in_grp, logits, -jnp.inf)
    m1, i1 = first_argmax(le)
    m2, i2 = first_argmax(jnp.where(lane == i1, -jnp.inf, le))
    t = jnp.exp(m2 - m1)
    w1 = p_grp / (1.0 + t)
    w2 = p_grp * t / (1.0 + t)

    hot1 = lane == i1
    hot2 = lane == i2
    onehot = (hot1 | hot2).astype(BF16)
    r_i = lax.broadcasted_iota(jnp.int32, (tm, tm), 0)
    c_i = lax.broadcasted_iota(jnp.int32, (tm, tm), 1)
    before = (c_i < r_i).astype(BF16)
    rank = jnp.dot(before, onehot, preferred_element_type=F32)
    rank1 = jnp.sum(jnp.where(hot1, rank, 0.0), axis=-1, keepdims=True)
    rank2 = jnp.sum(jnp.where(hot2, rank, 0.0), axis=-1, keepdims=True)
    cnt_ref[0] = jnp.sum(onehot.astype(F32), axis=0, keepdims=True)

    e1 = (i1 - EXPERT_LANE0).astype(F32)
    e2 = (i2 - EXPERT_LANE0).astype(F32)
    route = jnp.where(lane == 0, e1, 0.0)
    route = jnp.where(lane == 1, e2, route)
    route = jnp.where(lane == 2, w1, route)
    route = jnp.where(lane == 3, w2, route)
    route = jnp.where(lane == 4, rank1, route)
    route = jnp.where(lane == 5, rank2, route)
    route_ref[...] = route


def _merge_route(ona, ogla, sgn, sgg, x2d, mod3, gffn, wna, wgla, wout, wr, br, tokens_per_batch, tm):
    n, d = x2d.shape
    per_b = tokens_per_batch // tm

    def tile(w):
        return pl.BlockSpec((tm, w), lambda i: (i, 0))

    def whole(a):
        return pl.BlockSpec(a.shape, lambda i: (0,) * a.ndim)

    return pl.pallas_call(
        _merge_kernel,
        grid=(n // tm,),
        in_specs=[tile(NA_WIDTH), tile(GLA_DV), tile(d), tile(d), tile(d),
                  pl.BlockSpec((1, 6, d), lambda i: (i // per_b, 0, 0)),
                  whole(gffn), whole(wna), whole(wgla), whole(wout), whole(wr), whole(br)],
        out_specs=[tile(d), tile(d), tile(ROUTE_LANES),
                   pl.BlockSpec((1, 1, ROUTE_LANES), lambda i: (i, 0, 0))],
        out_shape=[jax.ShapeDtypeStruct((n, d), F32),
                   jax.ShapeDtypeStruct((n, d), BF16),
                   jax.ShapeDtypeStruct((n, ROUTE_LANES), F32),
                   jax.ShapeDtypeStruct((n // tm, 1, ROUTE_LANES), F32)],
        compiler_params=_params(("arbitrary",)),
        name="merge_route",
    )(ona, ogla, sgn, sgg, x2d, mod3, gffn, wna, wgla, wout, wr, br)


HI_MASK = -65536


def _pack_rows(a):
    half = a.shape[1] // 2
    hi = lax.bitcast_convert_type(a[:, :half], jnp.int32)
    lo = lax.shift_right_logical(lax.bitcast_convert_type(a[:, half:], jnp.int32), 16)
    return hi | lo


def _unpack_rows(w):
    hi = lax.bitcast_convert_type(w & HI_MASK, F32).astype(BF16)
    lo = lax.bitcast_convert_type(lax.shift_left(w, 16), F32).astype(BF16)
    return hi, lo


def _segment_copies(seg_ref, make_copy, action):
    def body(e, carry):
        cnt = seg_ref[0, 0, e]
        local = seg_ref[0, 0, N_EXPERTS + e]
        glob = seg_ref[0, 0, 2 * N_EXPERTS + e]
        off = jnp.int32(0)
        for size in SEG_SIZES:
            @pl.when((cnt & size) != 0)
            def _():
                cp = make_copy(pl.multiple_of(local + off, SEG_ALIGN), pl.multiple_of(glob + off, SEG_ALIGN), size)
                getattr(cp, action)()
            off = off + (cnt & size)
        return carry

    lax.fori_loop(0, N_EXPERTS, body, 0)


def _dispatch_kernel(seg_ref, tail_ref, lpos_ref, f_ref, xs_ref, sbuf, zbuf, sem):
    i = pl.program_id(0)
    tm = f_ref.shape[0]
    rows_l = sbuf.shape[0]
    lp = lpos_ref[0]
    row = lax.broadcasted_iota(jnp.int32, (rows_l, tm), 0)
    perm = ((row == lp[0:1, :]) | (row == lp[1:2, :])).astype(BF16)
    sbuf[...] = _pack_rows(jnp.dot(perm, f_ref[...], preferred_element_type=F32))

    def copy(local, glob, size):
        return pltpu.make_async_copy(sbuf.at[pl.ds(local, size), :], xs_ref.at[pl.ds(glob, size), :], sem)

    _segment_copies(seg_ref, copy, "start")
    _segment_copies(seg_ref, copy, "wait")

    @pl.when(i == pl.num_programs(0) - 1)
    def _():
        zbuf[...] = jnp.zeros_like(zbuf)

        def tail(action):
            def body(e, carry):
                start = tail_ref[0, 0, e]
                length = tail_ref[0, 0, N_EXPERTS + e]
                off = jnp.int32(0)
                for size in TAIL_SIZES:
                    @pl.when((length & size) != 0)
                    def _():
                        cp = pltpu.make_async_copy(zbuf.at[pl.ds(0, size), :],
                                                   xs_ref.at[pl.ds(pl.multiple_of(start + off, SEG_ALIGN), size), :],
                                                   sem)
                        getattr(cp, action)()
                    off = off + (length & size)
                return carry
            lax.fori_loop(0, N_EXPERTS, body, 0)

        tail("start")
        tail("wait")

        def unused(action):
            def body(blk, carry):
                cp = pltpu.make_async_copy(zbuf, xs_ref.at[pl.ds(pl.multiple_of(blk * MOE_ROWS, MOE_ROWS),
                                                                  MOE_ROWS), :], sem)
                getattr(cp, action)()
                return carry
            lax.fori_loop(tail_ref[0, 0, 2 * N_EXPERTS], xs_ref.shape[0] // MOE_ROWS, body, 0)

        unused("start")
        unused("wait")


def _dispatch(seg, tail, lpos_rows, f2d, n_pad, rows_l, tm):
    n, d = f2d.shape
    return pl.pallas_call(
        _dispatch_kernel,
        grid=(n // tm,),
        in_specs=[pl.BlockSpec((1, 1, seg.shape[2]), lambda i: (i, 0, 0), memory_space=pltpu.SMEM),
                  pl.BlockSpec(tail.shape, lambda i: (0, 0, 0), memory_space=pltpu.SMEM),
                  pl.BlockSpec((1, 2, tm), lambda i: (i, 0, 0)),
                  pl.BlockSpec((tm, d), lambda i: (i, 0))],
        out_specs=pl.BlockSpec(memory_space=pl.ANY),
        out_shape=jax.ShapeDtypeStruct((n_pad, d // 2), jnp.int32),
        scratch_shapes=[pltpu.VMEM((rows_l, d // 2), jnp.int32),
                        pltpu.VMEM((MOE_ROWS, d // 2), jnp.int32),
                        pltpu.SemaphoreType.DMA],
        compiler_params=_params(("arbitrary",)),
        name="dispatch",
    )(seg, tail, lpos_rows, f2d)


def _expert_kernel(be_ref, nu_ref, x_ref, wg_ref, wu_ref, wd_ref, y_ref, wgu_scr, wd_scr):
    i = pl.program_id(0)
    prev = be_ref[jnp.maximum(i - 1, 0)]
    half = x_ref.shape[1]

    @pl.when((i == 0) | (be_ref[i] != prev))
    def _():
        wgu_scr[:, :D_EXPERT] = wg_ref[0].astype(BF16)
        wgu_scr[:, D_EXPERT:] = wu_ref[0].astype(BF16)
        wd_scr[...] = wd_ref[0].astype(BF16)

    @pl.when(i < nu_ref[0])
    def _():
        x_hi, x_lo = _unpack_rows(x_ref[...])
        gu = (jnp.dot(x_hi, wgu_scr[:half, :], preferred_element_type=F32)
              + jnp.dot(x_lo, wgu_scr[half:, :], preferred_element_type=F32))
        gate = gu[:, :D_EXPERT]
        hdn = gate * jax.nn.sigmoid(gate) * gu[:, D_EXPERT:]
        y = jnp.dot(hdn.astype(BF16), wd_scr[...], preferred_element_type=F32)
        y_ref[...] = _pack_rows(y.astype(BF16).astype(F32))

    @pl.when(i >= nu_ref[0])
    def _():
        y_ref[...] = jnp.zeros_like(y_ref)


def _experts(blk_expert, n_used, xs, wg, wu, wd):
    n_pad, half = xs.shape
    d = 2 * half
    n_blk = n_pad // MOE_ROWS

    def used_block(i, be, nu):
        return (jnp.minimum(i, nu[0] - 1), 0)

    grid_spec = pltpu.PrefetchScalarGridSpec(
        num_scalar_prefetch=2,
        grid=(n_blk,),
        in_specs=[pl.BlockSpec((MOE_ROWS, half), used_block),
                  pl.BlockSpec((1, d, D_EXPERT), lambda i, be, nu: (be[i], 0, 0)),
                  pl.BlockSpec((1, d, D_EXPERT), lambda i, be, nu: (be[i], 0, 0)),
                  pl.BlockSpec((1, D_EXPERT, d), lambda i, be, nu: (be[i], 0, 0))],
        out_specs=pl.BlockSpec((MOE_ROWS, half), lambda i, be, nu: (i, 0)),
        scratch_shapes=[pltpu.VMEM((d, 2 * D_EXPERT), BF16),
                        pltpu.VMEM((D_EXPERT, d), BF16)],
    )
    return pl.pallas_call(
        _expert_kernel,
        grid_spec=grid_spec,
        out_shape=jax.ShapeDtypeStruct((n_pad, half), jnp.int32),
        compiler_params=_params(("arbitrary",)),
        name="experts",
    )(blk_expert, n_used, xs, wg, wu, wd)


def _combine_kernel(seg_ref, col_ref, h_ref, mod_ref, fg_ref, ys_ref, o_ref, ybuf, sem):
    i = pl.program_id(0)
    tm = h_ref.shape[0]
    rows_l = ybuf.shape[0]

    @pl.when(i == 0)
    def _():
        ybuf[...] = jnp.zeros_like(ybuf)

    def copy(local, glob, size):
        return pltpu.make_async_copy(ys_ref.at[pl.ds(glob, size), :], ybuf.at[pl.ds(local, size), :], sem)

    _segment_copies(seg_ref, copy, "start")
    _segment_copies(seg_ref, copy, "wait")
    y_hi, y_lo = _unpack_rows(ybuf[...])
    y_all = jnp.concatenate([y_hi, y_lo], axis=1)
    col = lax.broadcasted_iota(jnp.int32, (tm, rows_l), 1)
    info = col_ref[...]
    y1 = jnp.dot((col == info[:, 0:1].astype(jnp.int32)).astype(BF16), y_all, preferred_element_type=F32)
    y2 = jnp.dot((col == info[:, 1:2].astype(jnp.int32)).astype(BF16), y_all, preferred_element_type=F32)
    moe = info[:, 2:3] * y1 + info[:, 3:4] * y2
    h = h_ref[...] + mod_ref[0, 5:6, :] * moe
    ms = jnp.mean(h * h, axis=-1, keepdims=True)
    o_ref[...] = h * lax.rsqrt(ms + EPS) * fg_ref[...]


def _combine(seg, colinfo, h2d, mod3, fg, ys, rows_l, tokens_per_batch, tm):
    n, d = h2d.shape
    per_b = tokens_per_batch // tm
    return pl.pallas_call(
        _combine_kernel,
        grid=(n // tm,),
        in_specs=[pl.BlockSpec((1, 1, seg.shape[2]), lambda i: (i, 0, 0), memory_space=pltpu.SMEM),
                  pl.BlockSpec((tm, colinfo.shape[1]), lambda i: (i, 0)),
                  pl.BlockSpec((tm, d), lambda i: (i, 0)),
                  pl.BlockSpec((1, 6, d), lambda i: (i // per_b, 0, 0)),
                  pl.BlockSpec((1, d), lambda i: (0, 0)),
                  pl.BlockSpec(memory_space=pl.ANY)],
        out_specs=pl.BlockSpec((tm, d), lambda i: (i, 0)),
        out_shape=jax.ShapeDtypeStruct((n, d), F32),
        scratch_shapes=[pltpu.VMEM((rows_l, d // 2), jnp.int32), pltpu.SemaphoreType.DMA],
        compiler_params=_params(("arbitrary",)),
        name="combine",
    )(seg, colinfo, h2d, mod3, fg, ys)


def _rope_tables(s):
    t = np.arange(s)
    pos_r = jnp.asarray(t // GRID_W, F32)
    pos_c = jnp.asarray(t % GRID_W, F32)
    nf = GLA_HK // 4
    inv = ROPE_BASE ** (-jnp.arange(nf, dtype=F32) / nf)
    ang_r = pos_r[:, None] * inv
    ang_c = pos_c[:, None] * inv

    def half(ang):
        return (jnp.concatenate([jnp.cos(ang), jnp.cos(ang)], axis=-1),
                jnp.concatenate([-jnp.sin(ang), jnp.sin(ang)], axis=-1))

    cr, sr = half(ang_r)
    cc, sc = half(ang_c)
    cos = jnp.tile(jnp.concatenate([cr, cc], axis=-1), (1, GLA_HEADS))
    sin = jnp.tile(jnp.concatenate([sr, sc], axis=-1), (1, GLA_HEADS))
    return cos.astype(F32), sin.astype(F32)


def kernel(x, c, ctx, c_ctx, w_mod, b_mod, norm_attn_g, norm_ffn_g, w_in, w_gla_a2, b_gla_a2, gla_norm_g, na_rpb, w_na_o, w_gla_o, w_out, w_group, b_group, w_expert, b_expert, w_exp_gate, w_exp_up, w_exp_down, final_norm_g):
    b, s, d = x.shape
    ctx_len = ctx.shape[1]
    n = b * s
    assert w_mod.shape[0] == 1, "single-layer block"
    assert s % (GRID_W * NA_WIN_H) == 0 and ctx_len % GLA_CHUNK == 0

    mod_rows = -(-(b + 1) // 8) * 8
    cs = jnp.zeros((mod_rows, d), F32).at[:b].set(c).at[b].set(c_ctx)
    mod3 = _modulation(cs, w_mod[0], b_mod[0]).reshape(mod_rows, 6, d)

    w = w_in[0].astype(BF16)
    o_q, o_k, o_v = 0, NA_WIDTH, 2 * NA_WIDTH
    o_gq = 3 * NA_WIDTH
    o_gk = o_gq + GLA_DK
    o_gv = o_gk + GLA_DK
    o_og = o_gv + GLA_DV
    o_lr = o_og + GLA_DV
    o_mn = o_lr + 2 * GLA_GATE_RANK
    w_lr = jnp.pad(w[:, o_lr:o_mn], ((0, 0), (0, LR_PAD - 2 * GLA_GATE_RANK)))
    w_lat = jnp.concatenate([w[:, :o_lr], w[:, o_mn:], w_lr], axis=1)
    cw = 512
    lat_specs = [(0, 0, o_q, cw, "plain"), (1, 0, o_k, cw, "plain"), (2, 0, o_v, cw, "plain"),
                 (3, 0, o_gq, cw, "rope_scaled"), (4, 0, o_gk, cw, "rope"),
                 (5, 0, o_gv, cw, "plain"), (5, cw, o_gv + cw, cw, "plain"),
                 (6, 0, o_og, cw, "silu"), (6, cw, o_og + cw, cw, "silu"),
                 (7, 0, o_lr, cw, "sigmoid"), (7, cw, o_lr + cw, cw, "sigmoid"),
                 (8, 0, o_lr + d, cw, "sigmoid"), (8, cw, o_lr + d + cw, cw, "sigmoid"),
                 (9, 0, o_lr + 2 * d, LR_PAD, "plain")]
    lat_widths = [NA_WIDTH, NA_WIDTH, NA_WIDTH, GLA_DK, GLA_DK, GLA_DV, GLA_DV, d, d, LR_PAD]
    tm = 512
    per_b = s // tm
    q_na, k_na, v_na, gq, gk, gv, sog, sgn, sgg, lr = _inproj(
        x.reshape(n, d), mod3, lambda i: i // per_b, norm_attn_g[0], w_lat, lat_specs, lat_widths, tm,
        rope=_rope_tables(s), name="inproj_lat")

    w_ctx = jnp.concatenate([w[:, o_k:o_gq], w[:, o_gk:o_og], w_lr], axis=1)
    ctx_specs = [(0, 0, 0, cw, "plain"), (1, 0, cw, cw, "plain"), (2, 0, 2 * cw, cw, "plain"),
                 (3, 0, 3 * cw, cw, "plain"), (3, cw, 4 * cw, cw, "plain"),
                 (4, 0, 5 * cw, LR_PAD, "plain")]
    ctx_widths = [NA_WIDTH, NA_WIDTH, GLA_DK, GLA_DV, LR_PAD]
    kc_na, vc_na, ck, cv, clr = _inproj(
        ctx.reshape(b * ctx_len, d), mod3, lambda i: b, norm_attn_g[0], w_ctx, ctx_specs, ctx_widths,
        ctx_len, name="inproj_ctx")

    def lat3(a):
        return a.reshape(b, s, a.shape[-1])

    def ctx3(a):
        return a.reshape(b, ctx_len, a.shape[-1])

    o_na = _na_attention(lat3(q_na), lat3(k_na), lat3(v_na), ctx3(kc_na), ctx3(vc_na),
                         _na_bias_table(na_rpb[0]))

    wa = jnp.zeros((2, LR_PAD, GLA_DK), F32)
    wa = wa.at[0, :GLA_GATE_RANK].set(w_gla_a2[0, 0]).at[1, GLA_GATE_RANK:2 * GLA_GATE_RANK].set(w_gla_a2[0, 1])
    gn = jnp.tile(gla_norm_g[0], GLA_HEADS).reshape(1, GLA_DV)
    o_gla = _gla(lat3(gq), lat3(gk), lat3(gv), lat3(lr), lat3(sog), ctx3(ck), ctx3(cv), ctx3(clr),
                 wa.astype(BF16), b_gla_a2[0], gn)

    wr = jnp.zeros((d, ROUTE_LANES), F32)
    wr = wr.at[:, :N_GROUPS].set(w_group[0]).at[:, EXPERT_LANE0:EXPERT_LANE0 + N_EXPERTS].set(w_expert[0])
    wr_hi = wr.astype(BF16)
    br = jnp.zeros((1, ROUTE_LANES), F32)
    br = br.at[0, :N_GROUPS].set(b_group[0]).at[0, EXPERT_LANE0:EXPERT_LANE0 + N_EXPERTS].set(b_expert[0])
    h, f_lat, route, counts = _merge_route(
        o_na.reshape(n, NA_WIDTH), o_gla.reshape(n, GLA_DV), sgn, sgg, x.reshape(n, d), mod3, norm_ffn_g[0].reshape(1, d),
        w_na_o[0].astype(BF16), w_gla_o[0].astype(BF16), w_out[0].astype(BF16),
        jnp.stack([wr_hi, (wr - wr_hi.astype(F32)).astype(BF16)]), br, s, tm)

    n_tiles = n // tm
    cnt = counts[:, 0, EXPERT_LANE0:EXPERT_LANE0 + N_EXPERTS].astype(jnp.int32)
    cnt_al = (cnt + SEG_ALIGN - 1) // SEG_ALIGN * SEG_ALIGN
    lstart = jnp.cumsum(cnt_al, axis=1) - cnt_al
    total = jnp.sum(cnt_al, axis=0)
    padded = (total + MOE_ROWS - 1) // MOE_ROWS * MOE_ROWS
    pend = jnp.cumsum(padded)
    pstart = pend - padded
    seg_dst = pstart[None, :] + jnp.cumsum(cnt_al, axis=0) - cnt_al
    seg = jnp.concatenate([cnt_al, lstart, seg_dst, jnp.zeros_like(cnt)], axis=1).reshape(n_tiles, 1, 4 * N_EXPERTS)
    n_used = (pend[-1] // MOE_ROWS).reshape(1)
    tail = jnp.concatenate([pstart + total, padded - total, n_used, jnp.zeros((2 * N_EXPERTS - 1,), jnp.int32)])
    tail = tail.reshape(1, 1, 4 * N_EXPERTS)
    seg_pad = n_tiles * N_EXPERTS * (SEG_ALIGN - 1)
    n_pad = (2 * n + seg_pad + N_EXPERTS * (MOE_ROWS - SEG_ALIGN) + MOE_ROWS - 1) // MOE_ROWS * MOE_ROWS
    n_blk = n_pad // MOE_ROWS
    rows_l = (2 * tm + N_EXPERTS * (SEG_ALIGN - 1) + MOE_ROWS - 1) // MOE_ROWS * MOE_ROWS
    blk_start = jnp.arange(n_blk, dtype=jnp.int32) * MOE_ROWS
    blk_expert = jnp.minimum(jnp.sum(blk_start[:, None] >= pend[None, :], axis=-1), N_EXPERTS - 1)

    eid = route[:, 0:2].astype(jnp.int32)
    onehot = eid[:, :, None] == jnp.arange(N_EXPERTS, dtype=jnp.int32)
    lstart_tok = jnp.repeat(lstart, tm, axis=0)[:, None, :]
    lpos = route[:, 4:6].astype(jnp.int32) + jnp.sum(jnp.where(onehot, lstart_tok, 0), axis=-1)
    lpos_rows = lpos.reshape(n_tiles, tm, 2).transpose(0, 2, 1)
    colinfo = jnp.concatenate([lpos.astype(F32), route[:, 2:4], jnp.zeros((n, 4), F32)], axis=1)

    xs = _dispatch(seg, tail, lpos_rows, f_lat, n_pad, rows_l, tm)
    ys = _experts(blk_expert.astype(jnp.int32), n_used.astype(jnp.int32), xs,
                  w_exp_gate[0], w_exp_up[0], w_exp_down[0])
    out = _combine(seg, colinfo, h, mod3, final_norm_g.reshape(1, d), ys, rows_l, s, tm)
    return out.reshape(b, s, d)
```

```python
import functools

import jax
import jax.numpy as jnp
import numpy as np
from jax import lax
from jax.experimental import pallas as pl
from jax.experimental.pallas import tpu as pltpu

F32 = jnp.float32
BF16 = jnp.bfloat16

D_MODEL = 1024
GRID_W = 64
NA_HEADS = 8
NA_HEAD_DIM = 64
NA_WIDTH = NA_HEADS * NA_HEAD_DIM
NA_WIN_H = 8
NA_WIN_W = 16
GLA_HEADS = 4
GLA_DK = D_MODEL // 2
GLA_DV = D_MODEL
GLA_HK = GLA_DK // GLA_HEADS
GLA_HV = GLA_DV // GLA_HEADS
GLA_GATE_RANK = 16
GLA_GATE_NORM = 16.0
GLA_CHUNK = 64
GLA_GROUP = 256
ROPE_BASE = 10000.0
N_GROUPS = 4
EXPERTS_PER_GROUP = 8
N_EXPERTS = N_GROUPS * EXPERTS_PER_GROUP
D_EXPERT = D_MODEL // 2
EPS = 1e-6
NEG_INF = -1e30

LANES = 128
MXU_DIM = 256
LR_PAD = LANES
ROUTE_LANES = LANES
EXPERT_LANE0 = N_GROUPS
MOE_ROWS = 512
SEG_ALIGN = 8
SEG_SIZES = (512, 256, 128, 64, 32, 16, 8)
TAIL_SIZES = (256, 128, 64, 32, 16, 8)
VMEM_LIMIT = 56 * 1024 * 1024

_NT = (((1,), (1,)), ((), ()))
_TN = (((0,), (0,)), ((), ()))


def _params(sem, vmem=VMEM_LIMIT):
    return pltpu.CompilerParams(dimension_semantics=sem, vmem_limit_bytes=vmem)


def _mod_kernel(c_ref, w_ref, b_ref, o_ref):
    c = c_ref[...]
    s = c * jax.nn.sigmoid(c)
    o_ref[...] = jnp.dot(s.astype(BF16), w_ref[...].astype(BF16),
                         preferred_element_type=F32) + b_ref[...]


def _modulation(cs, w_mod, b_mod):
    rows, d = cs.shape
    n = w_mod.shape[1]
    bn = 1024
    return pl.pallas_call(
        _mod_kernel,
        grid=(n // bn,),
        in_specs=[pl.BlockSpec((rows, d), lambda j: (0, 0)),
                  pl.BlockSpec((d, bn), lambda j: (0, j)),
                  pl.BlockSpec((1, bn), lambda j: (0, j))],
        out_specs=pl.BlockSpec((rows, bn), lambda j: (0, j)),
        out_shape=jax.ShapeDtypeStruct((rows, n), F32),
        compiler_params=_params(("arbitrary",)),
        name="mod",
    )(cs, w_mod, b_mod.reshape(1, n))


def _inproj_kernel(specs, use_rope, n_out, x_ref, mod_ref, g_ref, *rest):
    if use_rope:
        cos_ref, sin_ref, w_ref = rest[:3]
        rest = rest[3:]
    else:
        w_ref = rest[0]
        rest = rest[1:]
    outs = rest[:n_out]
    a_scr = rest[n_out]
    x = x_ref[...]
    ms = jnp.mean(x * x, axis=-1, keepdims=True)
    a = x * lax.rsqrt(ms + EPS) * g_ref[...] * (1.0 + mod_ref[0, 1:2, :]) + mod_ref[0, 0:1, :]
    a_scr[...] = a.astype(BF16)
    for (oi, oc, wc, width, kind) in specs:
        acc = jnp.dot(a_scr[...], w_ref[:, wc:wc + width], preferred_element_type=F32)
        if kind in ("rope", "rope_scaled"):
            if kind == "rope_scaled":
                acc = acc * (GLA_HK ** -0.5)
            lane = lax.broadcasted_iota(jnp.int32, acc.shape, 1)
            rot = jnp.where((lane & 63) < 32,
                            pltpu.roll(acc, width - 32, axis=1),
                            pltpu.roll(acc, 32, axis=1))
            acc = acc * cos_ref[...] + rot * sin_ref[...]
        elif kind == "silu":
            acc = acc * jax.nn.sigmoid(acc)
        elif kind == "sigmoid":
            acc = jax.nn.sigmoid(acc)
        outs[oi][:, oc:oc + width] = acc.astype(outs[oi].dtype)


def _inproj(x2d, mod3, mod_row_fn, g, w_packed, specs, out_widths, tm, rope=None, name="inproj"):
    n, d = x2d.shape
    wcols = w_packed.shape[1]
    use_rope = rope is not None
    in_specs = [pl.BlockSpec((tm, d), lambda i: (i, 0)),
                pl.BlockSpec((1, 6, d), lambda i: (mod_row_fn(i), 0, 0)),
                pl.BlockSpec((1, d), lambda i: (0, 0))]
    args = [x2d, mod3, g.reshape(1, d)]
    if use_rope:
        cos, sin = rope
        nb = cos.shape[0] // tm
        in_specs += [pl.BlockSpec((tm, cos.shape[1]), lambda i: (i % nb, 0)),
                     pl.BlockSpec((tm, cos.shape[1]), lambda i: (i % nb, 0))]
        args += [cos, sin]
    in_specs.append(pl.BlockSpec((d, wcols), lambda i: (0, 0), pipeline_mode=pl.Buffered(1)))
    args.append(w_packed)
    out_specs = [pl.BlockSpec((tm, w), lambda i: (i, 0)) for w in out_widths]
    out_shape = [jax.ShapeDtypeStruct((n, w), BF16) for w in out_widths]
    return pl.pallas_call(
        functools.partial(_inproj_kernel, specs, use_rope, len(out_widths)),
        grid=(n // tm,),
        in_specs=in_specs,
        out_specs=out_specs,
        out_shape=out_shape,
        scratch_shapes=[pltpu.VMEM((tm, d), BF16)],
        compiler_params=_params(("arbitrary",)),
        name=name,
    )(*args)


def _na_kernel(rows_per_step, n_rows, q_ref, k_ref, v_ref, kc_ref, vc_ref, bias_ref, o_ref):
    j = pl.program_id(1)
    lane = lax.broadcasted_iota(jnp.int32, (GRID_W, LANES), 1)
    first_head = lane < NA_HEAD_DIM
    scale = NA_HEAD_DIM ** -0.5
    win_keys = NA_WIN_H * GRID_W

    def row_body(rr, carry):
        r = j * rows_per_step + rr
        rs = jnp.clip(r - NA_WIN_H // 2, 0, n_rows - NA_WIN_H)
        d = r - rs
        t0 = pl.multiple_of(rr * GRID_W, GRID_W)
        ks = pl.multiple_of(rs * GRID_W, GRID_W)
        pairs = range(NA_HEADS // 2)
        lanes = [slice(p * LANES, (p + 1) * LANES) for p in pairs]
        sws, scs = [], []
        for p in pairs:
            qp = q_ref[0, pl.ds(t0, GRID_W), lanes[p]]
            zero = jnp.zeros_like(qp)
            qs = jnp.concatenate([jnp.where(first_head, qp, zero), jnp.where(first_head, zero, qp)], axis=0)
            sws.append(lax.dot_general(qs, k_ref[0, pl.ds(ks, win_keys), lanes[p]], _NT,
                                       preferred_element_type=F32))
            scs.append(lax.dot_general(qs, kc_ref[0, :, lanes[p]], _NT, preferred_element_type=F32))
        pws, pcs, dens = [], [], []
        for p in pairs:
            bias = jnp.concatenate([bias_ref[(NA_WIN_H - 1) - d + 2 * i, p] for i in range(NA_WIN_H // 2)], axis=1)
            sw = sws[p] * scale + bias
            sc = scs[p] * scale
            m = jnp.maximum(jnp.max(sw, axis=-1, keepdims=True), jnp.max(sc, axis=-1, keepdims=True))
            pw = jnp.exp(sw - m)
            pc = jnp.exp(sc - m)
            dens.append(jnp.sum(pw, axis=-1, keepdims=True) + jnp.sum(pc, axis=-1, keepdims=True))
            pws.append(pw.astype(BF16))
            pcs.append(pc.astype(BF16))
        for p in pairs:
            o2 = (jnp.dot(pws[p], v_ref[0, pl.ds(ks, win_keys), lanes[p]], preferred_element_type=F32)
                  + jnp.dot(pcs[p], vc_ref[0, :, lanes[p]], preferred_element_type=F32)) / dens[p]
            o_ref[0, pl.ds(t0, GRID_W), lanes[p]] = jnp.where(first_head, o2[:GRID_W],
                                                             o2[GRID_W:]).astype(o_ref.dtype)
        return carry

    lax.fori_loop(0, rows_per_step, row_body, 0)


def _na_bias_table(rpb):
    n_heads, n_dr, n_dc = rpb.shape
    c = np.arange(GRID_W)
    cstart = np.clip(c - NA_WIN_W // 2, 0, GRID_W - NA_WIN_W)
    kc = np.arange(GRID_W)
    valid = (kc[None, :] >= cstart[:, None]) & (kc[None, :] < cstart[:, None] + NA_WIN_W)
    col_idx = np.clip(kc[None, :] - c[:, None] + NA_WIN_W - 1, 0, n_dc - 1)
    pick = np.zeros((n_dc, GRID_W * GRID_W), np.float32)
    pick[col_idx.reshape(-1), np.arange(GRID_W * GRID_W)] = 1.0
    t = jnp.dot(rpb.reshape(n_heads * n_dr, n_dc).astype(F32), jnp.asarray(pick),
                precision=lax.Precision.HIGHEST).reshape(n_heads, n_dr, GRID_W, GRID_W)
    t = jnp.where(jnp.asarray(valid)[None, None], t, NEG_INF)
    t2 = jnp.concatenate([t[:, :-1], t[:, 1:]], axis=-1)
    return t2.transpose(1, 0, 2, 3).reshape(n_dr - 1, n_heads // 2, 2 * GRID_W, 2 * GRID_W)


def _na_attention(q, k, v, kc, vc, bias, rows_per_step=8):
    b, s, w = q.shape
    n_rows = s // GRID_W
    ctx_len = kc.shape[1]
    blk = rows_per_step * GRID_W
    return pl.pallas_call(
        functools.partial(_na_kernel, rows_per_step, n_rows),
        grid=(b, n_rows // rows_per_step),
        in_specs=[pl.BlockSpec((1, blk, w), lambda bi, j: (bi, j, 0)),
                  pl.BlockSpec((1, s, w), lambda bi, j: (bi, 0, 0)),
                  pl.BlockSpec((1, s, w), lambda bi, j: (bi, 0, 0)),
                  pl.BlockSpec((1, ctx_len, w), lambda bi, j: (bi, 0, 0)),
                  pl.BlockSpec((1, ctx_len, w), lambda bi, j: (bi, 0, 0)),
                  pl.BlockSpec(bias.shape, lambda bi, j: (0, 0, 0, 0), pipeline_mode=pl.Buffered(1))],
        out_specs=pl.BlockSpec((1, blk, w), lambda bi, j: (bi, j, 0)),
        out_shape=jax.ShapeDtypeStruct((b, s, w), BF16),
        compiler_params=_params(("arbitrary", "arbitrary")),
        name="na_attn",
    )(q, k, v, kc, vc, bias)


def _gla_kernel(n_groups, gq_ref, gk_ref, gv_ref, lr_ref, sog_ref, ck_ref, cv_ref, clr_ref,
                wa_ref, ba_ref, gn_ref, o_ref, st_scr, of_scr, qe_scr, kd_scr, ckd_scr, dec_scr, cdec_scr):
    c_len = GLA_CHUNK
    grp = GLA_GROUP
    cpg = grp // c_len
    row = lax.broadcasted_iota(jnp.int32, (grp, grp), 0)
    col = lax.broadcasted_iota(jnp.int32, (grp, grp), 1)
    same_chunk = (row // c_len) == (col // c_len)
    masks = (same_chunk & (col <= row), same_chunk & (col >= row))

    def gates(lr_rows, k_rows, direction):
        tri = masks[direction].astype(BF16)
        z = jnp.dot(lr_rows, wa_ref[direction], preferred_element_type=F32) + ba_ref[direction:direction + 1, :]
        g = (jnp.minimum(z, 0.0) - jnp.log(1.0 + jnp.exp(-jnp.abs(z)))) * (1.0 / GLA_GATE_NORM)
        g1 = g.astype(BF16)
        g2 = (g - g1.astype(F32)).astype(BF16)
        b = jnp.dot(tri, g1, preferred_element_type=F32) + jnp.dot(tri, g2, preferred_element_type=F32)
        edge = c_len - 1 if direction == 0 else 0
        lasts = [b[j * c_len + edge:j * c_len + edge + 1, :] for j in range(cpg)]
        b_last = jnp.concatenate([jnp.broadcast_to(v, (c_len, v.shape[1])) for v in lasts], axis=0)
        kd = (k_rows * jnp.exp(b_last - b)).astype(BF16)
        dec = jnp.exp(jnp.concatenate(lasts + [jnp.zeros_like(v) for v in lasts], axis=0))
        return b, kd, dec

    ck = ck_ref[0].astype(F32)
    for direction in range(2):
        _, kd, dec = gates(clr_ref[0], ck, direction)
        ckd_scr[direction] = kd
        cdec_scr[direction] = dec

    def group_body(gi, carry):
        r0 = pl.multiple_of(gi * grp, grp)
        rows = pl.ds(r0, grp)
        qf = gq_ref[0, rows, :].astype(F32)
        kf = gk_ref[0, rows, :].astype(F32)
        lr_rows = lr_ref[0, rows, :]
        qes, kes = [], []
        for direction in range(2):
            b, kd, dec = gates(lr_rows, kf, direction)
            qe = (qf * jnp.exp(b)).astype(BF16)
            qes.append(qe)
            kes.append((kf * jnp.exp(-b)).astype(BF16))
            qe_scr[direction, rows, :] = qe
            kd_scr[direction, rows, :] = kd
            dec_scr[direction, gi] = dec
        for h in range(GLA_HEADS):
            kl = slice(h * GLA_HK, (h + 1) * GLA_HK)
            vl = slice(h * GLA_HV, (h + 1) * GLA_HV)
            att = (jnp.where(masks[0], lax.dot_general(qes[0][:, kl], kes[0][:, kl], _NT,
                                                       preferred_element_type=F32), 0.0)
                   + jnp.where(masks[1], lax.dot_general(qes[1][:, kl], kes[1][:, kl], _NT,
                                                         preferred_element_type=F32), 0.0))
            of_scr[rows, vl] = jnp.dot(att.astype(BF16), gv_ref[0, rows, vl], preferred_element_type=F32)
        return carry

    lax.fori_loop(0, n_groups, group_body, 0)

    def state_step(h, v_rows, kd_rows, dec_row):
        kl = slice(h * GLA_HK, (h + 1) * GLA_HK)
        upd = lax.dot_general(v_rows, kd_rows[:, kl], _TN, preferred_element_type=F32)
        st_scr[h] = st_scr[h] * dec_row[:, kl] + upd

    def scan(direction):
        st_scr[...] = jnp.zeros_like(st_scr)
        order = range(cpg) if direction == 0 else range(cpg - 1, -1, -1)
        for j in order:
            rows = slice(j * c_len, (j + 1) * c_len)
            for h in range(GLA_HEADS):
                state_step(h, cv_ref[0, rows, h * GLA_HV:(h + 1) * GLA_HV],
                           ckd_scr[direction, rows, :], cdec_scr[direction, j:j + 1, :])

        def group_scan(i, carry):
            gi = i if direction == 0 else n_groups - 1 - i
            for j in order:
                rows = pl.ds(pl.multiple_of(gi * grp + j * c_len, c_len), c_len)
                qe_rows = qe_scr[direction, rows, :]
                kd_rows = kd_scr[direction, rows, :]
                dec_row = dec_scr[direction, gi, j:j + 1, :]
                for h in range(GLA_HEADS):
                    kl = slice(h * GLA_HK, (h + 1) * GLA_HK)
                    vl = slice(h * GLA_HV, (h + 1) * GLA_HV)
                    o_t = of_scr[rows, vl] + lax.dot_general(qe_rows[:, kl], st_scr[h].astype(BF16), _NT,
                                                             preferred_element_type=F32)
                    if direction == 0:
                        of_scr[rows, vl] = o_t
                    else:
                        ms = jnp.mean(o_t * o_t, axis=-1, keepdims=True)
                        o_n = o_t * lax.rsqrt(ms + EPS) * gn_ref[:, vl] * sog_ref[0, rows, vl].astype(F32)
                        o_ref[0, rows, vl] = o_n.astype(o_ref.dtype)
                    state_step(h, gv_ref[0, rows, vl], kd_rows, dec_row)
            return carry

        lax.fori_loop(0, n_groups, group_scan, 0)

    scan(0)
    scan(1)


def _gla(gq, gk, gv, lr, sog, ck, cv, clr, wa, ba, gn):
    b, s, _ = gq.shape
    ctx_len = ck.shape[1]
    assert ctx_len == GLA_GROUP and s % GLA_GROUP == 0
    n_groups = s // GLA_GROUP
    cpg = GLA_GROUP // GLA_CHUNK

    def per_batch(t, w):
        return pl.BlockSpec((1, t, w), lambda bi: (bi, 0, 0))

    return pl.pallas_call(
        functools.partial(_gla_kernel, n_groups),
        grid=(b,),
        in_specs=[per_batch(s, GLA_DK), per_batch(s, GLA_DK), per_batch(s, GLA_DV),
                  per_batch(s, LR_PAD), per_batch(s, GLA_DV),
                  per_batch(ctx_len, GLA_DK), per_batch(ctx_len, GLA_DV), per_batch(ctx_len, LR_PAD),
                  pl.BlockSpec(wa.shape, lambda bi: (0, 0, 0)),
                  pl.BlockSpec(ba.shape, lambda bi: (0, 0)),
                  pl.BlockSpec(gn.shape, lambda bi: (0, 0))],
        out_specs=per_batch(s, GLA_DV),
        out_shape=jax.ShapeDtypeStruct((b, s, GLA_DV), BF16),
        scratch_shapes=[pltpu.VMEM((GLA_HEADS, GLA_HV, GLA_HK), F32),
                        pltpu.VMEM((s, GLA_DV), F32),
                        pltpu.VMEM((2, s, GLA_DK), BF16),
                        pltpu.VMEM((2, s, GLA_DK), BF16),
                        pltpu.VMEM((2, ctx_len, GLA_DK), BF16),
                        pltpu.VMEM((2, n_groups, 2 * cpg, GLA_DK), F32),
                        pltpu.VMEM((2, 2 * cpg, GLA_DK), F32)],
        compiler_params=_params(("arbitrary",), vmem=60 * 1024 * 1024),
        name="gla",
    )(gq, gk, gv, lr, sog, ck, cv, clr, wa, ba, gn)


def _merge_kernel(ona_ref, ogla_ref, sgn_ref, sgg_ref, x_ref, mod_ref, gffn_ref,
                  wna_ref, wgla_ref, wout_ref, wr_ref, br_ref,
                  h_ref, f_ref, route_ref, cnt_ref):
    tm = x_ref.shape[0]
    y1 = (sgn_ref[...].astype(F32) * jnp.dot(ona_ref[...], wna_ref[...], preferred_element_type=F32)
          + sgg_ref[...].astype(F32) * jnp.dot(ogla_ref[...], wgla_ref[...], preferred_element_type=F32))
    y = jnp.dot(y1.astype(BF16), wout_ref[...], preferred_element_type=F32)
    h = x_ref[...] + mod_ref[0, 2:3, :] * y
    h_ref[...] = h
    ms = jnp.mean(h * h, axis=-1, keepdims=True)
    f = h * lax.rsqrt(ms + EPS) * gffn_ref[...] * (1.0 + mod_ref[0, 4:5, :]) + mod_ref[0, 3:4, :]
    f_ref[...] = f.astype(f_ref.dtype)

    f_hi = f.astype(BF16)
    f_lo = (f - f_hi.astype(F32)).astype(BF16)
    logits = (jnp.dot(f_hi, wr_ref[0], preferred_element_type=F32)
              + jnp.dot(f_hi, wr_ref[1], preferred_element_type=F32)
              + jnp.dot(f_lo, wr_ref[0], preferred_element_type=F32)) + br_ref[...]
    lane = lax.broadcasted_iota(jnp.int32, (tm, ROUTE_LANES), 1)
    big = jnp.int32(ROUTE_LANES)

    def first_argmax(vals):
        m = jnp.max(vals, axis=-1, keepdims=True)
        idx = jnp.min(jnp.where(vals == m, lane, big), axis=-1, keepdims=True)
        return m, idx

    is_grp = lane < N_GROUPS
    lg = jnp.where(is_grp, logits, -jnp.inf)
    mg, grp = first_argmax(lg)
    p_grp = 1.0 / jnp.sum(jnp.where(is_grp, jnp.exp(lg - mg), 0.0), axis=-1, keepdims=True)
    lo = EXPERT_LANE0 + grp * EXPERTS_PER_GROUP
    in_grp = (lane >= lo) & (lane < lo + EXPERTS_PER_GROUP)
    le = jnp.where(in_grp, logits, -jnp.inf)
    m1, i1 = first_argmax(le)
    m2, i2 = first_argmax(jnp.where(lane == i1, -jnp.inf, le))
    t = jnp.exp(m2 - m1)
    w1 = p_grp / (1.0 + t)
    w2 = p_grp * t / (1.0 + t)

    hot1 = lane == i1
    hot2 = lane == i2
    onehot = (hot1 | hot2).astype(BF16)
    r_i = lax.broadcasted_iota(jnp.int32, (tm, tm), 0)
    c_i = lax.broadcasted_iota(jnp.int32, (tm, tm), 1)
    before = (c_i < r_i).astype(BF16)
    rank = jnp.dot(before, onehot, preferred_element_type=F32)
    rank1 = jnp.sum(jnp.where(hot1, rank, 0.0), axis=-1, keepdims=True)
    rank2 = jnp.sum(jnp.where(hot2, rank, 0.0), axis=-1, keepdims=True)
    cnt_ref[0] = jnp.sum(onehot.astype(F32), axis=0, keepdims=True)

    e1 = (i1 - EXPERT_LANE0).astype(F32)
    e2 = (i2 - EXPERT_LANE0).astype(F32)
    route = jnp.where(lane == 0, e1, 0.0)
    route = jnp.where(lane == 1, e2, route)
    route = jnp.where(lane == 2, w1, route)
    route = jnp.where(lane == 3, w2, route)
    route = jnp.where(lane == 4, rank1, route)
    route = jnp.where(lane == 5, rank2, route)
    route_ref[...] = route


def _merge_route(ona, ogla, sgn, sgg, x2d, mod3, gffn, wna, wgla, wout, wr, br, tokens_per_batch, tm):
    n, d = x2d.shape
    per_b = tokens_per_batch // tm

    def tile(w):
        return pl.BlockSpec((tm, w), lambda i: (i, 0))

    def whole(a):
        return pl.BlockSpec(a.shape, lambda i: (0,) * a.ndim)

    return pl.pallas_call(
        _merge_kernel,
        grid=(n // tm,),
        in_specs=[tile(NA_WIDTH), tile(GLA_DV), tile(d), tile(d), tile(d),
                  pl.BlockSpec((1, 6, d), lambda i: (i // per_b, 0, 0)),
                  whole(gffn), whole(wna), whole(wgla), whole(wout), whole(wr), whole(br)],
        out_specs=[tile(d), tile(d), tile(ROUTE_LANES),
                   pl.BlockSpec((1, 1, ROUTE_LANES), lambda i: (i, 0, 0))],
        out_shape=[jax.ShapeDtypeStruct((n, d), F32),
                   jax.ShapeDtypeStruct((n, d), BF16),
                   jax.ShapeDtypeStruct((n, ROUTE_LANES), F32),
                   jax.ShapeDtypeStruct((n // tm, 1, ROUTE_LANES), F32)],
        compiler_params=_params(("arbitrary",)),
        name="merge_route",
    )(ona, ogla, sgn, sgg, x2d, mod3, gffn, wna, wgla, wout, wr, br)


HI_MASK = -65536


def _pack_rows(a):
    half = a.shape[1] // 2
    hi = lax.bitcast_convert_type(a[:, :half], jnp.int32)
    lo = lax.shift_right_logical(lax.bitcast_convert_type(a[:, half:], jnp.int32), 16)
    return hi | lo


def _unpack_rows(w):
    hi = lax.bitcast_convert_type(w & HI_MASK, F32).astype(BF16)
    lo = lax.bitcast_convert_type(lax.shift_left(w, 16), F32).astype(BF16)
    return hi, lo


def _segment_copies(seg_ref, make_copy, action):
    def body(e, carry):
        cnt = seg_ref[0, 0, e]
        local = seg_ref[0, 0, N_EXPERTS + e]
        glob = seg_ref[0, 0, 2 * N_EXPERTS + e]
        off = jnp.int32(0)
        for size in SEG_SIZES:
            @pl.when((cnt & size) != 0)
            def _():
                cp = make_copy(pl.multiple_of(local + off, SEG_ALIGN), pl.multiple_of(glob + off, SEG_ALIGN), size)
                getattr(cp, action)()
            off = off + (cnt & size)
        return carry

    lax.fori_loop(0, N_EXPERTS, body, 0)


def _dispatch_kernel(seg_ref, seg_prev_ref, tail_ref, lpos_ref, f_ref, xs_ref, sbuf, zbuf, sems):
    i = pl.program_id(0)
    last = pl.num_programs(0) - 1
    tm = f_ref.shape[0]
    rows_l = sbuf.shape[1]
    slot = i % 2
    lp = lpos_ref[0]
    row = lax.broadcasted_iota(jnp.int32, (rows_l, tm), 0)
    perm = ((row == lp[0:1, :]) | (row == lp[1:2, :])).astype(BF16)
    sbuf[slot] = _pack_rows(jnp.dot(perm, f_ref[...], preferred_element_type=F32))

    def copier(s):
        def copy(local, glob, size):
            return pltpu.make_async_copy(sbuf.at[s, pl.ds(local, size), :], xs_ref.at[pl.ds(glob, size), :],
                                         sems.at[s])
        return copy

    _segment_copies(seg_ref, copier(slot), "start")

    @pl.when(i > 0)
    def _():
        _segment_copies(seg_prev_ref, copier(1 - slot), "wait")

    @pl.when(i == last)
    def _():
        _segment_copies(seg_ref, copier(slot), "wait")
        sem = sems.at[slot]
        zbuf[...] = jnp.zeros_like(zbuf)

        def tail(action):
            def body(e, carry):
                start = tail_ref[0, 0, e]
                length = tail_ref[0, 0, N_EXPERTS + e]
                off = jnp.int32(0)
                for size in TAIL_SIZES:
                    @pl.when((length & size) != 0)
                    def _():
                        cp = pltpu.make_async_copy(zbuf.at[pl.ds(0, size), :],
                                                   xs_ref.at[pl.ds(pl.multiple_of(start + off, SEG_ALIGN), size), :],
                                                   sem)
                        getattr(cp, action)()
                    off = off + (length & size)
                return carry
            lax.fori_loop(0, N_EXPERTS, body, 0)

        tail("start")
        tail("wait")

        def unused(action):
            def body(blk, carry):
                cp = pltpu.make_async_copy(zbuf, xs_ref.at[pl.ds(pl.multiple_of(blk * MOE_ROWS, MOE_ROWS),
                                                                  MOE_ROWS), :], sem)
                getattr(cp, action)()
                return carry
            lax.fori_loop(tail_ref[0, 0, 2 * N_EXPERTS], xs_ref.shape[0] // MOE_ROWS, body, 0)

        unused("start")
        unused("wait")


def _dispatch(seg, tail, lpos_rows, f2d, n_pad, rows_l, tm):
    n, d = f2d.shape
    return pl.pallas_call(
        _dispatch_kernel,
        grid=(n // tm,),
        in_specs=[pl.BlockSpec((1, 1, seg.shape[2]), lambda i: (i, 0, 0), memory_space=pltpu.SMEM),
                  pl.BlockSpec((1, 1, seg.shape[2]), lambda i: (jnp.maximum(i - 1, 0), 0, 0),
                               memory_space=pltpu.SMEM),
                  pl.BlockSpec(tail.shape, lambda i: (0, 0, 0), memory_space=pltpu.SMEM),
                  pl.BlockSpec((1, 2, tm), lambda i: (i, 0, 0)),
                  pl.BlockSpec((tm, d), lambda i: (i, 0))],
        out_specs=pl.BlockSpec(memory_space=pl.ANY),
        out_shape=jax.ShapeDtypeStruct((n_pad, d // 2), jnp.int32),
        scratch_shapes=[pltpu.VMEM((2, rows_l, d // 2), jnp.int32),
                        pltpu.VMEM((MOE_ROWS, d // 2), jnp.int32),
                        pltpu.SemaphoreType.DMA((2,))],
        compiler_params=_params(("arbitrary",)),
        name="dispatch",
    )(seg, seg, tail, lpos_rows, f2d)


def _expert_kernel(be_ref, nu_ref, x_ref, wg_ref, wu_ref, wd_ref, y_ref, wgu_scr, wd_scr):
    i = pl.program_id(0)
    prev = be_ref[jnp.maximum(i - 1, 0)]
    half = x_ref.shape[1]

    @pl.when((i == 0) | (be_ref[i] != prev))
    def _():
        wgu_scr[:, :D_EXPERT] = wg_ref[0].astype(BF16)
        wgu_scr[:, D_EXPERT:] = wu_ref[0].astype(BF16)
        wd_scr[...] = wd_ref[0].astype(BF16)

    @pl.when(i < nu_ref[0])
    def _():
        x_hi, x_lo = _unpack_rows(x_ref[...])
        gu = (jnp.dot(x_hi, wgu_scr[:half, :], preferred_element_type=F32)
              + jnp.dot(x_lo, wgu_scr[half:, :], preferred_element_type=F32))
        gate = gu[:, :D_EXPERT]
        hdn = gate * jax.nn.sigmoid(gate) * gu[:, D_EXPERT:]
        y = jnp.dot(hdn.astype(BF16), wd_scr[...], preferred_element_type=F32)
        y_ref[...] = _pack_rows(y.astype(BF16).astype(F32))

    @pl.when(i >= nu_ref[0])
    def _():
        y_ref[...] = jnp.zeros_like(y_ref)


def _experts(blk_expert, n_used, xs, wg, wu, wd):
    n_pad, half = xs.shape
    d = 2 * half
    n_blk = n_pad // MOE_ROWS

    def used_block(i, be, nu):
        return (jnp.maximum(jnp.minimum(i, nu[0] - 1), 0), 0)

    grid_spec = pltpu.PrefetchScalarGridSpec(
        num_scalar_prefetch=2,
        grid=(n_blk,),
        in_specs=[pl.BlockSpec((MOE_ROWS, half), used_block),
                  pl.BlockSpec((1, d, D_EXPERT), lambda i, be, nu: (be[i], 0, 0)),
                  pl.BlockSpec((1, d, D_EXPERT), lambda i, be, nu: (be[i], 0, 0)),
                  pl.BlockSpec((1, D_EXPERT, d), lambda i, be, nu: (be[i], 0, 0))],
        out_specs=pl.BlockSpec((MOE_ROWS, half), lambda i, be, nu: (i, 0)),
        scratch_shapes=[pltpu.VMEM((d, 2 * D_EXPERT), BF16),
                        pltpu.VMEM((D_EXPERT, d), BF16)],
    )
    return pl.pallas_call(
        _expert_kernel,
        grid_spec=grid_spec,
        out_shape=jax.ShapeDtypeStruct((n_pad, half), jnp.int32),
        compiler_params=_params(("arbitrary",)),
        name="experts",
    )(blk_expert, n_used, xs, wg, wu, wd)


def _combine_kernel(seg_ref, seg_next_ref, col_ref, h_ref, mod_ref, fg_ref, ys_ref, o_ref, ybuf, sems):
    i = pl.program_id(0)
    tm = h_ref.shape[0]
    rows_l = ybuf.shape[1]
    slot = i % 2

    def copier(s):
        def copy(local, glob, size):
            return pltpu.make_async_copy(ys_ref.at[pl.ds(glob, size), :], ybuf.at[s, pl.ds(local, size), :],
                                         sems.at[s])
        return copy

    @pl.when(i == 0)
    def _():
        ybuf[...] = jnp.zeros_like(ybuf)
        _segment_copies(seg_ref, copier(slot), "start")

    @pl.when(i + 1 < pl.num_programs(0))
    def _():
        _segment_copies(seg_next_ref, copier(1 - slot), "start")

    _segment_copies(seg_ref, copier(slot), "wait")
    y_hi, y_lo = _unpack_rows(ybuf[slot])
    y_all = jnp.concatenate([y_hi, y_lo], axis=1)
    col = lax.broadcasted_iota(jnp.int32, (tm, rows_l), 1)
    info = col_ref[...]
    y1 = jnp.dot((col == info[:, 0:1].astype(jnp.int32)).astype(BF16), y_all, preferred_element_type=F32)
    y2 = jnp.dot((col == info[:, 1:2].astype(jnp.int32)).astype(BF16), y_all, preferred_element_type=F32)
    moe = info[:, 2:3] * y1 + info[:, 3:4] * y2
    h = h_ref[...] + mod_ref[0, 5:6, :] * moe
    ms = jnp.mean(h * h, axis=-1, keepdims=True)
    o_ref[...] = h * lax.rsqrt(ms + EPS) * fg_ref[...]


def _combine(seg, colinfo, h2d, mod3, fg, ys, rows_l, tokens_per_batch, tm):
    n, d = h2d.shape
    per_b = tokens_per_batch // tm
    return pl.pallas_call(
        _combine_kernel,
        grid=(n // tm,),
        in_specs=[pl.BlockSpec((1, 1, seg.shape[2]), lambda i: (i, 0, 0), memory_space=pltpu.SMEM),
                  pl.BlockSpec((1, 1, seg.shape[2]), lambda i: (jnp.minimum(i + 1, n // tm - 1), 0, 0),
                               memory_space=pltpu.SMEM),
                  pl.BlockSpec((tm, colinfo.shape[1]), lambda i: (i, 0)),
                  pl.BlockSpec((tm, d), lambda i: (i, 0)),
                  pl.BlockSpec((1, 6, d), lambda i: (i // per_b, 0, 0)),
                  pl.BlockSpec((1, d), lambda i: (0, 0)),
                  pl.BlockSpec(memory_space=pl.ANY)],
        out_specs=pl.BlockSpec((tm, d), lambda i: (i, 0)),
        out_shape=jax.ShapeDtypeStruct((n, d), F32),
        scratch_shapes=[pltpu.VMEM((2, rows_l, d // 2), jnp.int32), pltpu.SemaphoreType.DMA((2,))],
        compiler_params=_params(("arbitrary",)),
        name="combine",
    )(seg, seg, colinfo, h2d, mod3, fg, ys)


def _rope_tables(s):
    t = np.arange(s)
    pos_r = (t // GRID_W).astype(np.float32)
    pos_c = (t % GRID_W).astype(np.float32)
    nf = GLA_HK // 4
    inv = (np.float32(ROPE_BASE) ** (-np.arange(nf, dtype=np.float32) / np.float32(nf))).astype(np.float32)
    ang_r = pos_r[:, None] * inv
    ang_c = pos_c[:, None] * inv

    def half(ang):
        return (np.concatenate([np.cos(ang), np.cos(ang)], axis=-1),
                np.concatenate([-np.sin(ang), np.sin(ang)], axis=-1))

    cr, sr = half(ang_r)
    cc, sc = half(ang_c)
    cos = np.tile(np.concatenate([cr, cc], axis=-1), (1, GLA_HEADS)).astype(np.float32)
    sin = np.tile(np.concatenate([sr, sc], axis=-1), (1, GLA_HEADS)).astype(np.float32)
    return jnp.asarray(cos), jnp.asarray(sin)


def kernel(x, c, ctx, c_ctx, w_mod, b_mod, norm_attn_g, norm_ffn_g, w_in, w_gla_a2, b_gla_a2, gla_norm_g, na_rpb, w_na_o, w_gla_o, w_out, w_group, b_group, w_expert, b_expert, w_exp_gate, w_exp_up, w_exp_down, final_norm_g):
    b, s, d = x.shape
    ctx_len = ctx.shape[1]
    n = b * s
    assert w_mod.shape[0] == 1, "single-layer block"
    assert s % (GRID_W * NA_WIN_H) == 0 and ctx_len % GLA_CHUNK == 0

    mod_rows = -(-(b + 1) // 8) * 8
    cs = jnp.zeros((mod_rows, d), F32).at[:b].set(c).at[b].set(c_ctx)
    mod3 = _modulation(cs, w_mod[0], b_mod[0]).reshape(mod_rows, 6, d)

    w = w_in[0].astype(BF16)
    o_q, o_k, o_v = 0, NA_WIDTH, 2 * NA_WIDTH
    o_gq = 3 * NA_WIDTH
    o_gk = o_gq + GLA_DK
    o_gv = o_gk + GLA_DK
    o_og = o_gv + GLA_DV
    o_lr = o_og + GLA_DV
    o_mn = o_lr + 2 * GLA_GATE_RANK
    w_lr = jnp.pad(w[:, o_lr:o_mn], ((0, 0), (0, LR_PAD - 2 * GLA_GATE_RANK)))
    w_lat = jnp.concatenate([w[:, :o_lr], w[:, o_mn:], w_lr], axis=1)
    cw = 512
    lat_specs = [(0, 0, o_q, cw, "plain"), (1, 0, o_k, cw, "plain"), (2, 0, o_v, cw, "plain"),
                 (3, 0, o_gq, cw, "rope_scaled"), (4, 0, o_gk, cw, "rope"),
                 (5, 0, o_gv, cw, "plain"), (5, cw, o_gv + cw, cw, "plain"),
                 (6, 0, o_og, cw, "silu"), (6, cw, o_og + cw, cw, "silu"),
                 (7, 0, o_lr, cw, "sigmoid"), (7, cw, o_lr + cw, cw, "sigmoid"),
                 (8, 0, o_lr + d, cw, "sigmoid"), (8, cw, o_lr + d + cw, cw, "sigmoid"),
                 (9, 0, o_lr + 2 * d, LR_PAD, "plain")]
    lat_widths = [NA_WIDTH, NA_WIDTH, NA_WIDTH, GLA_DK, GLA_DK, GLA_DV, GLA_DV, d, d, LR_PAD]
    tm = 512
    per_b = s // tm
    q_na, k_na, v_na, gq, gk, gv, sog, sgn, sgg, lr = _inproj(
        x.reshape(n, d), mod3, lambda i: i // per_b, norm_attn_g[0], w_lat, lat_specs, lat_widths, tm,
        rope=_rope_tables(s), name="inproj_lat")

    w_ctx = jnp.concatenate([w[:, o_k:o_gq], w[:, o_gk:o_og], w_lr], axis=1)
    ctx_specs = [(0, 0, 0, cw, "plain"), (1, 0, cw, cw, "plain"), (2, 0, 2 * cw, cw, "plain"),
                 (3, 0, 3 * cw, cw, "plain"), (3, cw, 4 * cw, cw, "plain"),
                 (4, 0, 5 * cw, LR_PAD, "plain")]
    ctx_widths = [NA_WIDTH, NA_WIDTH, GLA_DK, GLA_DV, LR_PAD]
    kc_na, vc_na, ck, cv, clr = _inproj(
        ctx.reshape(b * ctx_len, d), mod3, lambda i: b, norm_attn_g[0], w_ctx, ctx_specs, ctx_widths,
        ctx_len, name="inproj_ctx")

    def lat3(a):
        return a.reshape(b, s, a.shape[-1])

    def ctx3(a):
        return a.reshape(b, ctx_len, a.shape[-1])

    o_na = _na_attention(lat3(q_na), lat3(k_na), lat3(v_na), ctx3(kc_na), ctx3(vc_na),
                         _na_bias_table(na_rpb[0]))

    wa = jnp.zeros((2, LR_PAD, GLA_DK), F32)
    wa = wa.at[0, :GLA_GATE_RANK].set(w_gla_a2[0, 0]).at[1, GLA_GATE_RANK:2 * GLA_GATE_RANK].set(w_gla_a2[0, 1])
    gn = jnp.tile(gla_norm_g[0], GLA_HEADS).reshape(1, GLA_DV)
    o_gla = _gla(lat3(gq), lat3(gk), lat3(gv), lat3(lr), lat3(sog), ctx3(ck), ctx3(cv), ctx3(clr),
                 wa.astype(BF16), b_gla_a2[0], gn)

    wr = jnp.zeros((d, ROUTE_LANES), F32)
    wr = wr.at[:, :N_GROUPS].set(w_group[0]).at[:, EXPERT_LANE0:EXPERT_LANE0 + N_EXPERTS].set(w_expert[0])
    wr_hi = wr.astype(BF16)
    br = jnp.zeros((1, ROUTE_LANES), F32)
    br = br.at[0, :N_GROUPS].set(b_group[0]).at[0, EXPERT_LANE0:EXPERT_LANE0 + N_EXPERTS].set(b_expert[0])
    h, f_lat, route, counts = _merge_route(
        o_na.reshape(n, NA_WIDTH), o_gla.reshape(n, GLA_DV), sgn, sgg, x.reshape(n, d), mod3, norm_ffn_g[0].reshape(1, d),
        w_na_o[0].astype(BF16), w_gla_o[0].astype(BF16), w_out[0].astype(BF16),
        jnp.stack([wr_hi, (wr - wr_hi.astype(F32)).astype(BF16)]), br, s, tm)

    n_tiles = n // tm
    cnt = counts[:, 0, EXPERT_LANE0:EXPERT_LANE0 + N_EXPERTS].astype(jnp.int32)
    cnt_al = (cnt + SEG_ALIGN - 1) // SEG_ALIGN * SEG_ALIGN
    lstart = jnp.cumsum(cnt_al, axis=1) - cnt_al
    total = jnp.sum(cnt_al, axis=0)
    padded = (total + MOE_ROWS - 1) // MOE_ROWS * MOE_ROWS
    pend = jnp.cumsum(padded)
    pstart = pend - padded
    seg_dst = pstart[None, :] + jnp.cumsum(cnt_al, axis=0) - cnt_al
    seg = jnp.concatenate([cnt_al, lstart, seg_dst, jnp.zeros_like(cnt)], axis=1).reshape(n_tiles, 1, 4 * N_EXPERTS)
    n_used = (pend[-1] // MOE_ROWS).reshape(1)
    tail = jnp.concatenate([pstart + total, padded - total, n_used, jnp.zeros((2 * N_EXPERTS - 1,), jnp.int32)])
    tail = tail.reshape(1, 1, 4 * N_EXPERTS)
    seg_pad = n_tiles * N_EXPERTS * (SEG_ALIGN - 1)
    n_pad = (2 * n + seg_pad + N_EXPERTS * (MOE_ROWS - SEG_ALIGN) + MOE_ROWS - 1) // MOE_ROWS * MOE_ROWS
    n_blk = n_pad // MOE_ROWS
    rows_l = (2 * tm + N_EXPERTS * (SEG_ALIGN - 1) + MXU_DIM - 1) // MXU_DIM * MXU_DIM
    blk_start = jnp.arange(n_blk, dtype=jnp.int32) * MOE_ROWS
    blk_expert = jnp.minimum(jnp.sum(blk_start[:, None] >= pend[None, :], axis=-1), N_EXPERTS - 1)

    eid = route[:, 0:2].astype(jnp.int32)
    onehot = eid[:, :, None] == jnp.arange(N_EXPERTS, dtype=jnp.int32)
    lstart_tok = jnp.repeat(lstart, tm, axis=0)[:, None, :]
    lpos = route[:, 4:6].astype(jnp.int32) + jnp.sum(jnp.where(onehot, lstart_tok, 0), axis=-1)
    lpos_rows = lpos.reshape(n_tiles, tm, 2).transpose(0, 2, 1)
    colinfo = jnp.concatenate([lpos.astype(F32), route[:, 2:4], jnp.zeros((n, 4), F32)], axis=1)

    xs = _dispatch(seg, tail, lpos_rows, f_lat, n_pad, rows_l, tm)
    ys = _experts(blk_expert.astype(jnp.int32), n_used.astype(jnp.int32), xs,
                  w_exp_gate[0], w_exp_up[0], w_exp_down[0])
    out = _combine(seg, colinfo, h, mod3, final_norm_g.reshape(1, d), ys, rows_l, s, tm)
    return out.reshape(b, s, d)
```

```python
import functools

import jax
import jax.numpy as jnp
import numpy as np
from jax import lax
from jax.experimental import pallas as pl
from jax.experimental.pallas import tpu as pltpu

F32 = jnp.float32
BF16 = jnp.bfloat16

D_MODEL = 1024
GRID_W = 64
NA_HEADS = 8
NA_HEAD_DIM = 64
NA_WIDTH = NA_HEADS * NA_HEAD_DIM
NA_WIN_H = 8
NA_WIN_W = 16
GLA_HEADS = 4
GLA_DK = D_MODEL // 2
GLA_DV = D_MODEL
GLA_HK = GLA_DK // GLA_HEADS
GLA_HV = GLA_DV // GLA_HEADS
GLA_GATE_RANK = 16
GLA_GATE_NORM = 16.0
GLA_CHUNK = 64
GLA_GROUP = 256
ROPE_BASE = 10000.0
N_GROUPS = 4
EXPERTS_PER_GROUP = 8
N_EXPERTS = N_GROUPS * EXPERTS_PER_GROUP
D_EXPERT = D_MODEL // 2
EPS = 1e-6
NEG_INF = -1e30

LANES = 128
MXU_DIM = 256
LR_PAD = LANES
ROUTE_LANES = LANES
EXPERT_LANE0 = N_GROUPS
MOE_ROWS = 512
SEG_ALIGN = 8
SEG_SMALL = (8, 16, 32)
SEG_BIG = 64
WAIT_SIZES = (1024, 512, 256, 128, 64, 32, 16, 8)
TAIL_SIZES = (256, 128, 64, 32, 16, 8)
VMEM_LIMIT = 56 * 1024 * 1024

_NT = (((1,), (1,)), ((), ()))
_TN = (((0,), (0,)), ((), ()))


def _params(sem, vmem=VMEM_LIMIT):
    return pltpu.CompilerParams(dimension_semantics=sem, vmem_limit_bytes=vmem)


def _mod_kernel(c_ref, w_ref, b_ref, o_ref):
    c = c_ref[...]
    s = c * jax.nn.sigmoid(c)
    o_ref[...] = jnp.dot(s.astype(BF16), w_ref[...].astype(BF16),
                         preferred_element_type=F32) + b_ref[...]


def _modulation(cs, w_mod, b_mod):
    rows, d = cs.shape
    n = w_mod.shape[1]
    bn = 1024
    return pl.pallas_call(
        _mod_kernel,
        grid=(n // bn,),
        in_specs=[pl.BlockSpec((rows, d), lambda j: (0, 0)),
                  pl.BlockSpec((d, bn), lambda j: (0, j)),
                  pl.BlockSpec((1, bn), lambda j: (0, j))],
        out_specs=pl.BlockSpec((rows, bn), lambda j: (0, j)),
        out_shape=jax.ShapeDtypeStruct((rows, n), F32),
        compiler_params=_params(("arbitrary",)),
        name="mod",
    )(cs, w_mod, b_mod.reshape(1, n))


def _inproj_kernel(specs, use_rope, n_out, x_ref, mod_ref, g_ref, *rest):
    if use_rope:
        cos_ref, sin_ref, w_ref = rest[:3]
        rest = rest[3:]
    else:
        w_ref = rest[0]
        rest = rest[1:]
    outs = rest[:n_out]
    a_scr = rest[n_out]
    x = x_ref[...]
    ms = jnp.mean(x * x, axis=-1, keepdims=True)
    a = x * lax.rsqrt(ms + EPS) * g_ref[...] * (1.0 + mod_ref[0, 1:2, :]) + mod_ref[0, 0:1, :]
    a_scr[...] = a.astype(BF16)
    for (oi, oc, wc, width, kind) in specs:
        acc = jnp.dot(a_scr[...], w_ref[:, wc:wc + width], preferred_element_type=F32)
        if kind in ("rope", "rope_scaled"):
            if kind == "rope_scaled":
                acc = acc * (GLA_HK ** -0.5)
            lane = lax.broadcasted_iota(jnp.int32, acc.shape, 1)
            rot = jnp.where((lane & 63) < 32,
                            pltpu.roll(acc, width - 32, axis=1),
                            pltpu.roll(acc, 32, axis=1))
            acc = acc * cos_ref[...] + rot * sin_ref[...]
        elif kind == "silu":
            acc = acc * jax.nn.sigmoid(acc)
        elif kind == "sigmoid":
            acc = jax.nn.sigmoid(acc)
        outs[oi][:, oc:oc + width] = acc.astype(outs[oi].dtype)


def _inproj(x2d, mod3, mod_row_fn, g, w_packed, specs, out_widths, tm, rope=None, name="inproj"):
    n, d = x2d.shape
    wcols = w_packed.shape[1]
    use_rope = rope is not None
    in_specs = [pl.BlockSpec((tm, d), lambda i: (i, 0)),
                pl.BlockSpec((1, 6, d), lambda i: (mod_row_fn(i), 0, 0)),
                pl.BlockSpec((1, d), lambda i: (0, 0))]
    args = [x2d, mod3, g.reshape(1, d)]
    if use_rope:
        cos, sin = rope
        nb = cos.shape[0] // tm
        in_specs += [pl.BlockSpec((tm, cos.shape[1]), lambda i: (i % nb, 0)),
                     pl.BlockSpec((tm, cos.shape[1]), lambda i: (i % nb, 0))]
        args += [cos, sin]
    in_specs.append(pl.BlockSpec((d, wcols), lambda i: (0, 0), pipeline_mode=pl.Buffered(1)))
    args.append(w_packed)
    out_specs = [pl.BlockSpec((tm, w), lambda i: (i, 0)) for w in out_widths]
    out_shape = [jax.ShapeDtypeStruct((n, w), BF16) for w in out_widths]
    return pl.pallas_call(
        functools.partial(_inproj_kernel, specs, use_rope, len(out_widths)),
        grid=(n // tm,),
        in_specs=in_specs,
        out_specs=out_specs,
        out_shape=out_shape,
        scratch_shapes=[pltpu.VMEM((tm, d), BF16)],
        compiler_params=_params(("arbitrary",)),
        name=name,
    )(*args)


def _na_kernel(rows_per_step, n_rows, q_ref, k_ref, v_ref, kc_ref, vc_ref, bias_ref, o_ref):
    j = pl.program_id(1)
    lane = lax.broadcasted_iota(jnp.int32, (GRID_W, LANES), 1)
    first_head = lane < NA_HEAD_DIM
    scale = NA_HEAD_DIM ** -0.5
    win_keys = NA_WIN_H * GRID_W

    def row_body(rr, carry):
        r = j * rows_per_step + rr
        rs = jnp.clip(r - NA_WIN_H // 2, 0, n_rows - NA_WIN_H)
        d = r - rs
        t0 = pl.multiple_of(rr * GRID_W, GRID_W)
        ks = pl.multiple_of(rs * GRID_W, GRID_W)
        pairs = range(NA_HEADS // 2)
        lanes = [slice(p * LANES, (p + 1) * LANES) for p in pairs]
        sws, scs = [], []
        for p in pairs:
            qp = q_ref[0, pl.ds(t0, GRID_W), lanes[p]] * scale
            zero = jnp.zeros_like(qp)
            qs = jnp.concatenate([jnp.where(first_head, qp, zero), jnp.where(first_head, zero, qp)], axis=0)
            sws.append(lax.dot_general(qs, k_ref[0, pl.ds(ks, win_keys), lanes[p]], _NT,
                                       preferred_element_type=F32))
            scs.append(lax.dot_general(qs, kc_ref[0, :, lanes[p]], _NT, preferred_element_type=F32))
        pws, pcs, dens = [], [], []
        for p in pairs:
            bias = jnp.concatenate([bias_ref[(NA_WIN_H - 1) - d + 2 * i, p] for i in range(NA_WIN_H // 2)], axis=1)
            sw = sws[p] + bias
            sc = scs[p]
            m = jnp.maximum(jnp.max(sw, axis=-1, keepdims=True), jnp.max(sc, axis=-1, keepdims=True))
            pw = jnp.exp(sw - m)
            pc = jnp.exp(sc - m)
            dens.append(jnp.sum(pw, axis=-1, keepdims=True) + jnp.sum(pc, axis=-1, keepdims=True))
            pws.append(pw.astype(BF16))
            pcs.append(pc.astype(BF16))
        for p in pairs:
            o2 = (jnp.dot(pws[p], v_ref[0, pl.ds(ks, win_keys), lanes[p]], preferred_element_type=F32)
                  + jnp.dot(pcs[p], vc_ref[0, :, lanes[p]], preferred_element_type=F32)) / dens[p]
            o_ref[0, pl.ds(t0, GRID_W), lanes[p]] = jnp.where(first_head, o2[:GRID_W],
                                                             o2[GRID_W:]).astype(o_ref.dtype)
        return carry

    lax.fori_loop(0, rows_per_step, row_body, 0)


def _na_bias_table(rpb):
    n_heads, n_dr, n_dc = rpb.shape
    c = np.arange(GRID_W)
    cstart = np.clip(c - NA_WIN_W // 2, 0, GRID_W - NA_WIN_W)
    kc = np.arange(GRID_W)
    valid = (kc[None, :] >= cstart[:, None]) & (kc[None, :] < cstart[:, None] + NA_WIN_W)
    col_idx = np.clip(kc[None, :] - c[:, None] + NA_WIN_W - 1, 0, n_dc - 1)
    pick = np.zeros((n_dc, GRID_W * GRID_W), np.float32)
    pick[col_idx.reshape(-1), np.arange(GRID_W * GRID_W)] = 1.0
    t = jnp.dot(rpb.reshape(n_heads * n_dr, n_dc).astype(F32), jnp.asarray(pick),
                precision=lax.Precision.HIGHEST).reshape(n_heads, n_dr, GRID_W, GRID_W)
    t = jnp.where(jnp.asarray(valid)[None, None], t, NEG_INF)
    t2 = jnp.concatenate([t[:, :-1], t[:, 1:]], axis=-1)
    return t2.transpose(1, 0, 2, 3).reshape(n_dr - 1, n_heads // 2, 2 * GRID_W, 2 * GRID_W)


def _na_attention(q, k, v, kc, vc, bias, rows_per_step=8):
    b, s, w = q.shape
    n_rows = s // GRID_W
    ctx_len = kc.shape[1]
    blk = rows_per_step * GRID_W
    return pl.pallas_call(
        functools.partial(_na_kernel, rows_per_step, n_rows),
        grid=(b, n_rows // rows_per_step),
        in_specs=[pl.BlockSpec((1, blk, w), lambda bi, j: (bi, j, 0)),
                  pl.BlockSpec((1, s, w), lambda bi, j: (bi, 0, 0)),
                  pl.BlockSpec((1, s, w), lambda bi, j: (bi, 0, 0)),
                  pl.BlockSpec((1, ctx_len, w), lambda bi, j: (bi, 0, 0)),
                  pl.BlockSpec((1, ctx_len, w), lambda bi, j: (bi, 0, 0)),
                  pl.BlockSpec(bias.shape, lambda bi, j: (0, 0, 0, 0), pipeline_mode=pl.Buffered(1))],
        out_specs=pl.BlockSpec((1, blk, w), lambda bi, j: (bi, j, 0)),
        out_shape=jax.ShapeDtypeStruct((b, s, w), BF16),
        compiler_params=_params(("arbitrary", "arbitrary")),
        name="na_attn",
    )(q, k, v, kc, vc, bias)


def _gla_kernel(n_groups, gq_ref, gk_ref, gv_ref, lr_ref, sog_ref, ck_ref, cv_ref, clr_ref,
                wa_ref, ba_ref, gn_ref, o_ref, st_scr, of_scr, qe_scr, kd_scr, ckd_scr, dec_scr, cdec_scr):
    c_len = GLA_CHUNK
    grp = GLA_GROUP
    cpg = grp // c_len
    row = lax.broadcasted_iota(jnp.int32, (grp, grp), 0)
    col = lax.broadcasted_iota(jnp.int32, (grp, grp), 1)
    same_chunk = (row // c_len) == (col // c_len)
    masks = (same_chunk & (col <= row), same_chunk & (col >= row))

    def gates(lr_rows, k_rows, direction):
        tri = masks[direction].astype(BF16)
        z = jnp.dot(lr_rows, wa_ref[direction], preferred_element_type=F32) + ba_ref[direction:direction + 1, :]
        g = (jnp.minimum(z, 0.0) - jnp.log(1.0 + jnp.exp(-jnp.abs(z)))) * (1.0 / GLA_GATE_NORM)
        g1 = g.astype(BF16)
        g2 = (g - g1.astype(F32)).astype(BF16)
        b = jnp.dot(tri, g1, preferred_element_type=F32) + jnp.dot(tri, g2, preferred_element_type=F32)
        edge = c_len - 1 if direction == 0 else 0
        decs = [jnp.exp(b[j * c_len + edge:j * c_len + edge + 1, :]) for j in range(cpg)]
        ke = k_rows * jnp.exp(-b)
        kd = (ke * jnp.concatenate([jnp.broadcast_to(v, (c_len, v.shape[1])) for v in decs], axis=0)).astype(BF16)
        dec = jnp.concatenate(decs + [jnp.ones_like(v) for v in decs], axis=0)
        return b, ke.astype(BF16), kd, dec

    ck = ck_ref[0].astype(F32)
    for direction in range(2):
        _, _, kd, dec = gates(clr_ref[0], ck, direction)
        ckd_scr[direction] = kd
        cdec_scr[direction] = dec

    def group_body(gi, carry):
        r0 = pl.multiple_of(gi * grp, grp)
        rows = pl.ds(r0, grp)
        qf = gq_ref[0, rows, :].astype(F32)
        kf = gk_ref[0, rows, :].astype(F32)
        lr_rows = lr_ref[0, rows, :]
        qes, kes = [], []
        for direction in range(2):
            b, ke, kd, dec = gates(lr_rows, kf, direction)
            qe = (qf * jnp.exp(b)).astype(BF16)
            qes.append(qe)
            kes.append(ke)
            qe_scr[direction, rows, :] = qe
            kd_scr[direction, rows, :] = kd
            dec_scr[direction, gi] = dec
        for h in range(GLA_HEADS):
            kl = slice(h * GLA_HK, (h + 1) * GLA_HK)
            vl = slice(h * GLA_HV, (h + 1) * GLA_HV)
            att = (jnp.where(masks[0], lax.dot_general(qes[0][:, kl], kes[0][:, kl], _NT,
                                                       preferred_element_type=F32), 0.0)
                   + jnp.where(masks[1], lax.dot_general(qes[1][:, kl], kes[1][:, kl], _NT,
                                                         preferred_element_type=F32), 0.0))
            of_scr[rows, vl] = jnp.dot(att.astype(BF16), gv_ref[0, rows, vl], preferred_element_type=F32)
        return carry

    lax.fori_loop(0, n_groups, group_body, 0)

    def state_step(h, v_rows, kd_rows, dec_row):
        kl = slice(h * GLA_HK, (h + 1) * GLA_HK)
        upd = lax.dot_general(v_rows, kd_rows[:, kl], _TN, preferred_element_type=F32)
        st_scr[h] = st_scr[h] * dec_row[:, kl] + upd

    def scan(direction):
        st_scr[...] = jnp.zeros_like(st_scr)
        order = range(cpg) if direction == 0 else range(cpg - 1, -1, -1)
        for j in order:
            rows = slice(j * c_len, (j + 1) * c_len)
            for h in range(GLA_HEADS):
                state_step(h, cv_ref[0, rows, h * GLA_HV:(h + 1) * GLA_HV],
                           ckd_scr[direction, rows, :], cdec_scr[direction, j:j + 1, :])

        def group_scan(i, carry):
            gi = i if direction == 0 else n_groups - 1 - i
            for j in order:
                rows = pl.ds(pl.multiple_of(gi * grp + j * c_len, c_len), c_len)
                qe_rows = qe_scr[direction, rows, :]
                kd_rows = kd_scr[direction, rows, :]
                dec_row = dec_scr[direction, gi, j:j + 1, :]
                for h in range(GLA_HEADS):
                    kl = slice(h * GLA_HK, (h + 1) * GLA_HK)
                    vl = slice(h * GLA_HV, (h + 1) * GLA_HV)
                    o_t = of_scr[rows, vl] + lax.dot_general(qe_rows[:, kl], st_scr[h].astype(BF16), _NT,
                                                             preferred_element_type=F32)
                    if direction == 0:
                        of_scr[rows, vl] = o_t
                    else:
                        ms = jnp.mean(o_t * o_t, axis=-1, keepdims=True)
                        o_n = o_t * lax.rsqrt(ms + EPS) * gn_ref[:, vl] * sog_ref[0, rows, vl].astype(F32)
                        o_ref[0, rows, vl] = o_n.astype(o_ref.dtype)
                    state_step(h, gv_ref[0, rows, vl], kd_rows, dec_row)
            return carry

        lax.fori_loop(0, n_groups, group_scan, 0)

    scan(0)
    scan(1)


def _gla(gq, gk, gv, lr, sog, ck, cv, clr, wa, ba, gn):
    b, s, _ = gq.shape
    ctx_len = ck.shape[1]
    assert ctx_len == GLA_GROUP and s % GLA_GROUP == 0
    n_groups = s // GLA_GROUP
    cpg = GLA_GROUP // GLA_CHUNK

    def per_batch(t, w):
        return pl.BlockSpec((1, t, w), lambda bi: (bi, 0, 0))

    return pl.pallas_call(
        functools.partial(_gla_kernel, n_groups),
        grid=(b,),
        in_specs=[per_batch(s, GLA_DK), per_batch(s, GLA_DK), per_batch(s, GLA_DV),
                  per_batch(s, LR_PAD), per_batch(s, GLA_DV),
                  per_batch(ctx_len, GLA_DK), per_batch(ctx_len, GLA_DV), per_batch(ctx_len, LR_PAD),
                  pl.BlockSpec(wa.shape, lambda bi: (0, 0, 0)),
                  pl.BlockSpec(ba.shape, lambda bi: (0, 0)),
                  pl.BlockSpec(gn.shape, lambda bi: (0, 0))],
        out_specs=per_batch(s, GLA_DV),
        out_shape=jax.ShapeDtypeStruct((b, s, GLA_DV), BF16),
        scratch_shapes=[pltpu.VMEM((GLA_HEADS, GLA_HV, GLA_HK), F32),
                        pltpu.VMEM((s, GLA_DV), F32),
                        pltpu.VMEM((2, s, GLA_DK), BF16),
                        pltpu.VMEM((2, s, GLA_DK), BF16),
                        pltpu.VMEM((2, ctx_len, GLA_DK), BF16),
                        pltpu.VMEM((2, n_groups, 2 * cpg, GLA_DK), F32),
                        pltpu.VMEM((2, 2 * cpg, GLA_DK), F32)],
        compiler_params=_params(("arbitrary",), vmem=60 * 1024 * 1024),
        name="gla",
    )(gq, gk, gv, lr, sog, ck, cv, clr, wa, ba, gn)


def _merge_kernel(ona_ref, ogla_ref, sgn_ref, sgg_ref, x_ref, mod_ref, gffn_ref,
                  wna_ref, wgla_ref, wout_ref, wr_ref, br_ref,
                  h_ref, f_ref, route_ref, route_t_ref, cnt_ref):
    rows = x_ref.shape[0]
    lane = lax.broadcasted_iota(jnp.int32, (rows, ROUTE_LANES), 1)
    big = jnp.int32(ROUTE_LANES)
    r_i = lax.broadcasted_iota(jnp.int32, (rows, rows), 0)
    c_i = lax.broadcasted_iota(jnp.int32, (rows, rows), 1)
    before = (c_i < r_i).astype(BF16)
    is_grp = lane < N_GROUPS

    def first_argmax(vals):
        m = jnp.max(vals, axis=-1, keepdims=True)
        idx = jnp.min(jnp.where(vals == m, lane, big), axis=-1, keepdims=True)
        return m, idx

    for rs in (slice(0, rows),):
        y1 = (sgn_ref[rs, :].astype(F32) * jnp.dot(ona_ref[rs, :], wna_ref[...], preferred_element_type=F32)
              + sgg_ref[rs, :].astype(F32) * jnp.dot(ogla_ref[rs, :], wgla_ref[...], preferred_element_type=F32))
        y = jnp.dot(y1.astype(BF16), wout_ref[...], preferred_element_type=F32)
        h = x_ref[rs, :] + mod_ref[0, 2:3, :] * y
        h_ref[rs, :] = h
        ms = jnp.mean(h * h, axis=-1, keepdims=True)
        f = h * lax.rsqrt(ms + EPS) * gffn_ref[...] * (1.0 + mod_ref[0, 4:5, :]) + mod_ref[0, 3:4, :]
        f_ref[rs, :] = f.astype(f_ref.dtype)

        f_hi = f.astype(BF16)
        f_lo = (f - f_hi.astype(F32)).astype(BF16)
        both = jnp.dot(f_hi, wr_ref[...], preferred_element_type=F32)
        logits = (both[:, :ROUTE_LANES] + both[:, ROUTE_LANES:]
                  + jnp.dot(f_lo, wr_ref[:, :ROUTE_LANES], preferred_element_type=F32)) + br_ref[...]

        lg = jnp.where(is_grp, logits, -jnp.inf)
        mg, grp = first_argmax(lg)
        p_grp = 1.0 / jnp.sum(jnp.where(is_grp, jnp.exp(lg - mg), 0.0), axis=-1, keepdims=True)
        lo = EXPERT_LANE0 + grp * EXPERTS_PER_GROUP
        in_grp = (lane >= lo) & (lane < lo + EXPERTS_PER_GROUP)
        le = jnp.where(in_grp, logits, -jnp.inf)
        m1, i1 = first_argmax(le)
        m2, i2 = first_argmax(jnp.where(lane == i1, -jnp.inf, le))
        t = jnp.exp(m2 - m1)
        w1 = p_grp / (1.0 + t)
        w2 = p_grp * t / (1.0 + t)

        hot1 = lane == i1
        hot2 = lane == i2
        onehot = (hot1 | hot2).astype(BF16)
        rank = jnp.dot(before, onehot, preferred_element_type=F32)
        cnt = jnp.sum(onehot.astype(F32), axis=0, keepdims=True)
        cnt_ref[0] = cnt
        cnt_al = jnp.floor((cnt + (SEG_ALIGN - 1.0)) * (1.0 / SEG_ALIGN)) * SEG_ALIGN
        l_i = lax.broadcasted_iota(jnp.int32, (ROUTE_LANES, ROUTE_LANES), 0)
        l_j = lax.broadcasted_iota(jnp.int32, (ROUTE_LANES, ROUTE_LANES), 1)
        lstart = jnp.dot(jnp.broadcast_to(cnt_al, (8, ROUTE_LANES)).astype(BF16), (l_i < l_j).astype(BF16),
                         preferred_element_type=F32)[0:1, :]
        pos = rank + lstart
        pos1 = jnp.sum(jnp.where(hot1, pos, 0.0), axis=-1, keepdims=True)
        pos2 = jnp.sum(jnp.where(hot2, pos, 0.0), axis=-1, keepdims=True)

        e1 = (i1 - EXPERT_LANE0).astype(F32)
        e2 = (i2 - EXPERT_LANE0).astype(F32)
        route = jnp.where(lane == 0, e1, 0.0)
        route = jnp.where(lane == 1, e2, route)
        route = jnp.where(lane == 2, w1, route)
        route = jnp.where(lane == 3, w2, route)
        route = jnp.where(lane == 4, pos1, route)
        route = jnp.where(lane == 5, pos2, route)
        route_ref[rs, :] = route
        route_t_ref[0] = route.T[0:8, :]


def _merge_route(ona, ogla, sgn, sgg, x2d, mod3, gffn, wna, wgla, wout, wr, br, tokens_per_batch, tm):
    n, d = x2d.shape
    per_b = tokens_per_batch // tm

    def tile(w):
        return pl.BlockSpec((tm, w), lambda i: (i, 0))

    def whole(a):
        return pl.BlockSpec(a.shape, lambda i: (0,) * a.ndim)

    return pl.pallas_call(
        _merge_kernel,
        grid=(n // tm,),
        in_specs=[tile(NA_WIDTH), tile(GLA_DV), tile(d), tile(d), tile(d),
                  pl.BlockSpec((1, 6, d), lambda i: (i // per_b, 0, 0)),
                  whole(gffn), whole(wna), whole(wgla), whole(wout), whole(wr), whole(br)],
        out_specs=[tile(d), tile(d), tile(ROUTE_LANES),
                   pl.BlockSpec((1, 8, tm), lambda i: (i, 0, 0)),
                   pl.BlockSpec((1, 1, ROUTE_LANES), lambda i: (i, 0, 0))],
        out_shape=[jax.ShapeDtypeStruct((n, d), F32),
                   jax.ShapeDtypeStruct((n, d), BF16),
                   jax.ShapeDtypeStruct((n, ROUTE_LANES), F32),
                   jax.ShapeDtypeStruct((n // tm, 8, tm), F32),
                   jax.ShapeDtypeStruct((n // tm, 1, ROUTE_LANES), F32)],
        compiler_params=_params(("arbitrary",)),
        name="merge_route",
    )(ona, ogla, sgn, sgg, x2d, mod3, gffn, wna, wgla, wout, wr, br)


HI_MASK = -65536


def _pack_rows(a):
    half = a.shape[1] // 2
    hi = lax.bitcast_convert_type(a[:, :half], jnp.int32)
    lo = lax.shift_right_logical(lax.bitcast_convert_type(a[:, half:], jnp.int32), 16)
    return hi | lo


def _unpack_rows(w):
    hi = lax.bitcast_convert_type(w & HI_MASK, F32).astype(BF16)
    lo = lax.bitcast_convert_type(lax.shift_left(w, 16), F32).astype(BF16)
    return hi, lo


def _segment_copies(seg_ref, make_copy, action):
    if action == "wait":
        total = seg_ref[0, 0, 3 * N_EXPERTS]
        for size in WAIT_SIZES:
            @pl.when((total & size) != 0)
            def _():
                make_copy(0, 0, size).wait()
        return

    def body(e, carry):
        cnt = seg_ref[0, 0, e]
        local = seg_ref[0, 0, N_EXPERTS + e]
        glob = seg_ref[0, 0, 2 * N_EXPERTS + e]
        off = jnp.int32(0)
        for size in SEG_SMALL:
            @pl.when((cnt & size) != 0)
            def _():
                make_copy(pl.multiple_of(local + off, SEG_ALIGN), pl.multiple_of(glob + off, SEG_ALIGN), size).start()
            off = off + (cnt & size)

        def big(k, c):
            o = off + k * SEG_BIG
            make_copy(pl.multiple_of(local + o, SEG_ALIGN), pl.multiple_of(glob + o, SEG_ALIGN), SEG_BIG).start()
            return c

        lax.fori_loop(0, cnt // SEG_BIG, big, 0)
        return carry

    lax.fori_loop(0, N_EXPERTS, body, 0)


def _dispatch_kernel(seg_ref, seg_prev_ref, tail_ref, lpos_ref, f_ref, xs_ref, sbuf, zbuf, sems):
    i = pl.program_id(0)
    last = pl.num_programs(0) - 1
    tm = f_ref.shape[0]
    rows_l = sbuf.shape[1]
    slot = i % 2
    lp = lpos_ref[0].astype(jnp.int32)
    row = lax.broadcasted_iota(jnp.int32, (rows_l, tm), 0)
    perm = ((row == lp[4:5, :]) | (row == lp[5:6, :])).astype(BF16)
    sbuf[slot] = _pack_rows(jnp.dot(perm, f_ref[...], preferred_element_type=F32))

    def copier(s):
        def copy(local, glob, size):
            return pltpu.make_async_copy(sbuf.at[s, pl.ds(local, size), :], xs_ref.at[pl.ds(glob, size), :],
                                         sems.at[s])
        return copy

    _segment_copies(seg_ref, copier(slot), "start")

    @pl.when(i > 0)
    def _():
        _segment_copies(seg_prev_ref, copier(1 - slot), "wait")

    @pl.when(i == last)
    def _():
        _segment_copies(seg_ref, copier(slot), "wait")
        sem = sems.at[slot]
        zbuf[...] = jnp.zeros_like(zbuf)

        def tail(action):
            def body(e, carry):
                start = tail_ref[0, 0, e]
                length = tail_ref[0, 0, N_EXPERTS + e]
                off = jnp.int32(0)
                for size in TAIL_SIZES:
                    @pl.when((length & size) != 0)
                    def _():
                        cp = pltpu.make_async_copy(zbuf.at[pl.ds(0, size), :],
                                                   xs_ref.at[pl.ds(pl.multiple_of(start + off, SEG_ALIGN), size), :],
                                                   sem)
                        getattr(cp, action)()
                    off = off + (length & size)
                return carry
            lax.fori_loop(0, N_EXPERTS, body, 0)

        tail("start")
        tail("wait")

        def unused(action):
            def body(blk, carry):
                cp = pltpu.make_async_copy(zbuf, xs_ref.at[pl.ds(pl.multiple_of(blk * MOE_ROWS, MOE_ROWS),
                                                                  MOE_ROWS), :], sem)
                getattr(cp, action)()
                return carry
            lax.fori_loop(tail_ref[0, 0, 2 * N_EXPERTS], xs_ref.shape[0] // MOE_ROWS, body, 0)

        unused("start")
        unused("wait")


def _dispatch(seg, tail, lpos_rows, f2d, n_pad, rows_l, tm):
    n, d = f2d.shape
    return pl.pallas_call(
        _dispatch_kernel,
        grid=(n // tm,),
        in_specs=[pl.BlockSpec((1, 1, seg.shape[2]), lambda i: (i, 0, 0), memory_space=pltpu.SMEM),
                  pl.BlockSpec((1, 1, seg.shape[2]), lambda i: (jnp.maximum(i - 1, 0), 0, 0),
                               memory_space=pltpu.SMEM),
                  pl.BlockSpec(tail.shape, lambda i: (0, 0, 0), memory_space=pltpu.SMEM),
                  pl.BlockSpec((1, 8, tm), lambda i: (i, 0, 0)),
                  pl.BlockSpec((tm, d), lambda i: (i, 0))],
        out_specs=pl.BlockSpec(memory_space=pl.ANY),
        out_shape=jax.ShapeDtypeStruct((n_pad, d // 2), jnp.int32),
        scratch_shapes=[pltpu.VMEM((2, rows_l, d // 2), jnp.int32),
                        pltpu.VMEM((MOE_ROWS, d // 2), jnp.int32),
                        pltpu.SemaphoreType.DMA((2,))],
        compiler_params=_params(("arbitrary",)),
        name="dispatch",
    )(seg, seg, tail, lpos_rows, f2d)


def _expert_kernel(be_ref, nu_ref, x_ref, wg_ref, wu_ref, wd_ref, y_ref, wgu_scr, wd_scr):
    i = pl.program_id(0)
    prev = be_ref[jnp.maximum(i - 1, 0)]
    half = x_ref.shape[1]

    @pl.when((i == 0) | (be_ref[i] != prev))
    def _():
        wgu_scr[:, :D_EXPERT] = wg_ref[0].astype(BF16)
        wgu_scr[:, D_EXPERT:] = wu_ref[0].astype(BF16)
        wd_scr[...] = wd_ref[0].astype(BF16)

    @pl.when(i < nu_ref[0])
    def _():
        x_hi, x_lo = _unpack_rows(x_ref[...])
        gu = (jnp.dot(x_hi, wgu_scr[:half, :], preferred_element_type=F32)
              + jnp.dot(x_lo, wgu_scr[half:, :], preferred_element_type=F32))
        gate = gu[:, :D_EXPERT]
        hdn = gate * jax.nn.sigmoid(gate) * gu[:, D_EXPERT:]
        y = jnp.dot(hdn.astype(BF16), wd_scr[...], preferred_element_type=F32)
        y_ref[...] = _pack_rows(y.astype(BF16).astype(F32))

    @pl.when(i >= nu_ref[0])
    def _():
        y_ref[...] = jnp.zeros_like(y_ref)


def _experts(blk_expert, n_used, xs, wg, wu, wd):
    n_pad, half = xs.shape
    d = 2 * half
    n_blk = n_pad // MOE_ROWS

    def used_block(i, be, nu):
        return (jnp.maximum(jnp.minimum(i, nu[0] - 1), 0), 0)

    grid_spec = pltpu.PrefetchScalarGridSpec(
        num_scalar_prefetch=2,
        grid=(n_blk,),
        in_specs=[pl.BlockSpec((MOE_ROWS, half), used_block),
                  pl.BlockSpec((1, d, D_EXPERT), lambda i, be, nu: (be[i], 0, 0)),
                  pl.BlockSpec((1, d, D_EXPERT), lambda i, be, nu: (be[i], 0, 0)),
                  pl.BlockSpec((1, D_EXPERT, d), lambda i, be, nu: (be[i], 0, 0))],
        out_specs=pl.BlockSpec((MOE_ROWS, half), lambda i, be, nu: (i, 0)),
        scratch_shapes=[pltpu.VMEM((d, 2 * D_EXPERT), BF16),
                        pltpu.VMEM((D_EXPERT, d), BF16)],
    )
    return pl.pallas_call(
        _expert_kernel,
        grid_spec=grid_spec,
        out_shape=jax.ShapeDtypeStruct((n_pad, half), jnp.int32),
        compiler_params=_params(("arbitrary",)),
        name="experts",
    )(blk_expert, n_used, xs, wg, wu, wd)


def _combine_kernel(seg_ref, seg_next_ref, col_ref, h_ref, mod_ref, fg_ref, ys_ref, o_ref, ybuf, sems):
    i = pl.program_id(0)
    tm = h_ref.shape[0]
    rows_l = ybuf.shape[1]
    slot = i % 2

    def copier(s):
        def copy(local, glob, size):
            return pltpu.make_async_copy(ys_ref.at[pl.ds(glob, size), :], ybuf.at[s, pl.ds(local, size), :],
                                         sems.at[s])
        return copy

    @pl.when(i == 0)
    def _():
        ybuf[...] = jnp.zeros_like(ybuf)
        _segment_copies(seg_ref, copier(slot), "start")

    @pl.when(i + 1 < pl.num_programs(0))
    def _():
        _segment_copies(seg_next_ref, copier(1 - slot), "start")

    _segment_copies(seg_ref, copier(slot), "wait")
    y_hi, y_lo = _unpack_rows(ybuf[slot])
    y_all = jnp.concatenate([y_hi, y_lo], axis=1)
    col = lax.broadcasted_iota(jnp.int32, (tm, rows_l), 1)
    info = col_ref[...]
    y1 = jnp.dot((col == info[:, 4:5].astype(jnp.int32)).astype(BF16), y_all, preferred_element_type=F32)
    y2 = jnp.dot((col == info[:, 5:6].astype(jnp.int32)).astype(BF16), y_all, preferred_element_type=F32)
    moe = info[:, 2:3] * y1 + info[:, 3:4] * y2
    h = h_ref[...] + mod_ref[0, 5:6, :] * moe
    ms = jnp.mean(h * h, axis=-1, keepdims=True)
    o_ref[...] = h * lax.rsqrt(ms + EPS) * fg_ref[...]


def _combine(seg, colinfo, h2d, mod3, fg, ys, rows_l, tokens_per_batch, tm):
    n, d = h2d.shape
    per_b = tokens_per_batch // tm
    return pl.pallas_call(
        _combine_kernel,
        grid=(n // tm,),
        in_specs=[pl.BlockSpec((1, 1, seg.shape[2]), lambda i: (i, 0, 0), memory_space=pltpu.SMEM),
                  pl.BlockSpec((1, 1, seg.shape[2]), lambda i: (jnp.minimum(i + 1, n // tm - 1), 0, 0),
                               memory_space=pltpu.SMEM),
                  pl.BlockSpec((tm, colinfo.shape[1]), lambda i: (i, 0)),
                  pl.BlockSpec((tm, d), lambda i: (i, 0)),
                  pl.BlockSpec((1, 6, d), lambda i: (i // per_b, 0, 0)),
                  pl.BlockSpec((1, d), lambda i: (0, 0)),
                  pl.BlockSpec(memory_space=pl.ANY)],
        out_specs=pl.BlockSpec((tm, d), lambda i: (i, 0)),
        out_shape=jax.ShapeDtypeStruct((n, d), F32),
        scratch_shapes=[pltpu.VMEM((2, rows_l, d // 2), jnp.int32), pltpu.SemaphoreType.DMA((2,))],
        compiler_params=_params(("arbitrary",)),
        name="combine",
    )(seg, seg, colinfo, h2d, mod3, fg, ys)


def _rope_tables(s):
    t = np.arange(s)
    pos_r = (t // GRID_W).astype(np.float32)
    pos_c = (t % GRID_W).astype(np.float32)
    nf = GLA_HK // 4
    inv = (np.float32(ROPE_BASE) ** (-np.arange(nf, dtype=np.float32) / np.float32(nf))).astype(np.float32)
    ang_r = pos_r[:, None] * inv
    ang_c = pos_c[:, None] * inv

    def half(ang):
        return (np.concatenate([np.cos(ang), np.cos(ang)], axis=-1),
                np.concatenate([-np.sin(ang), np.sin(ang)], axis=-1))

    cr, sr = half(ang_r)
    cc, sc = half(ang_c)
    cos = np.tile(np.concatenate([cr, cc], axis=-1), (1, GLA_HEADS)).astype(np.float32)
    sin = np.tile(np.concatenate([sr, sc], axis=-1), (1, GLA_HEADS)).astype(np.float32)
    return jnp.asarray(cos), jnp.asarray(sin)


def kernel(x, c, ctx, c_ctx, w_mod, b_mod, norm_attn_g, norm_ffn_g, w_in, w_gla_a2, b_gla_a2, gla_norm_g, na_rpb, w_na_o, w_gla_o, w_out, w_group, b_group, w_expert, b_expert, w_exp_gate, w_exp_up, w_exp_down, final_norm_g):
    b, s, d = x.shape
    ctx_len = ctx.shape[1]
    n = b * s
    assert w_mod.shape[0] == 1, "single-layer block"
    assert s % (GRID_W * NA_WIN_H) == 0 and ctx_len % GLA_CHUNK == 0

    mod_rows = -(-(b + 1) // 8) * 8
    cs = jnp.zeros((mod_rows, d), F32).at[:b].set(c).at[b].set(c_ctx)
    mod3 = _modulation(cs, w_mod[0], b_mod[0]).reshape(mod_rows, 6, d)

    w = w_in[0].astype(BF16)
    o_q, o_k, o_v = 0, NA_WIDTH, 2 * NA_WIDTH
    o_gq = 3 * NA_WIDTH
    o_gk = o_gq + GLA_DK
    o_gv = o_gk + GLA_DK
    o_og = o_gv + GLA_DV
    o_lr = o_og + GLA_DV
    o_mn = o_lr + 2 * GLA_GATE_RANK
    w_lr = jnp.pad(w[:, o_lr:o_mn], ((0, 0), (0, LR_PAD - 2 * GLA_GATE_RANK)))
    w_lat = jnp.concatenate([w[:, :o_lr], w[:, o_mn:], w_lr], axis=1)
    cw = 512
    lat_specs = [(0, 0, o_q, cw, "plain"), (1, 0, o_k, cw, "plain"), (2, 0, o_v, cw, "plain"),
                 (3, 0, o_gq, cw, "rope_scaled"), (4, 0, o_gk, cw, "rope"),
                 (5, 0, o_gv, cw, "plain"), (5, cw, o_gv + cw, cw, "plain"),
                 (6, 0, o_og, cw, "silu"), (6, cw, o_og + cw, cw, "silu"),
                 (7, 0, o_lr, cw, "sigmoid"), (7, cw, o_lr + cw, cw, "sigmoid"),
                 (8, 0, o_lr + d, cw, "sigmoid"), (8, cw, o_lr + d + cw, cw, "sigmoid"),
                 (9, 0, o_lr + 2 * d, LR_PAD, "plain")]
    lat_widths = [NA_WIDTH, NA_WIDTH, NA_WIDTH, GLA_DK, GLA_DK, GLA_DV, GLA_DV, d, d, LR_PAD]
    tm = 512
    per_b = s // tm
    q_na, k_na, v_na, gq, gk, gv, sog, sgn, sgg, lr = _inproj(
        x.reshape(n, d), mod3, lambda i: i // per_b, norm_attn_g[0], w_lat, lat_specs, lat_widths, tm,
        rope=_rope_tables(s), name="inproj_lat")

    w_ctx = jnp.concatenate([w[:, o_k:o_gq], w[:, o_gk:o_og], w_lr], axis=1)
    ctx_specs = [(0, 0, 0, cw, "plain"), (1, 0, cw, cw, "plain"), (2, 0, 2 * cw, cw, "plain"),
                 (3, 0, 3 * cw, cw, "plain"), (3, cw, 4 * cw, cw, "plain"),
                 (4, 0, 5 * cw, LR_PAD, "plain")]
    ctx_widths = [NA_WIDTH, NA_WIDTH, GLA_DK, GLA_DV, LR_PAD]
    kc_na, vc_na, ck, cv, clr = _inproj(
        ctx.reshape(b * ctx_len, d), mod3, lambda i: b, norm_attn_g[0], w_ctx, ctx_specs, ctx_widths,
        ctx_len, name="inproj_ctx")

    def lat3(a):
        return a.reshape(b, s, a.shape[-1])

    def ctx3(a):
        return a.reshape(b, ctx_len, a.shape[-1])

    o_na = _na_attention(lat3(q_na), lat3(k_na), lat3(v_na), ctx3(kc_na), ctx3(vc_na),
                         _na_bias_table(na_rpb[0]))

    wa = jnp.zeros((2, LR_PAD, GLA_DK), F32)
    wa = wa.at[0, :GLA_GATE_RANK].set(w_gla_a2[0, 0]).at[1, GLA_GATE_RANK:2 * GLA_GATE_RANK].set(w_gla_a2[0, 1])
    gn = jnp.tile(gla_norm_g[0], GLA_HEADS).reshape(1, GLA_DV)
    o_gla = _gla(lat3(gq), lat3(gk), lat3(gv), lat3(lr), lat3(sog), ctx3(ck), ctx3(cv), ctx3(clr),
                 wa.astype(BF16), b_gla_a2[0], gn)

    wr = jnp.zeros((d, ROUTE_LANES), F32)
    wr = wr.at[:, :N_GROUPS].set(w_group[0]).at[:, EXPERT_LANE0:EXPERT_LANE0 + N_EXPERTS].set(w_expert[0])
    wr_hi = wr.astype(BF16)
    br = jnp.zeros((1, ROUTE_LANES), F32)
    br = br.at[0, :N_GROUPS].set(b_group[0]).at[0, EXPERT_LANE0:EXPERT_LANE0 + N_EXPERTS].set(b_expert[0])
    h, f_lat, route, route_t, counts = _merge_route(
        o_na.reshape(n, NA_WIDTH), o_gla.reshape(n, GLA_DV), sgn, sgg, x.reshape(n, d), mod3, norm_ffn_g[0].reshape(1, d),
        w_na_o[0].astype(BF16), w_gla_o[0].astype(BF16), w_out[0].astype(BF16),
        jnp.concatenate([wr_hi, (wr - wr_hi.astype(F32)).astype(BF16)], axis=1), br, s, tm)

    n_tiles = n // tm
    cnt = counts[:, 0, EXPERT_LANE0:EXPERT_LANE0 + N_EXPERTS].astype(jnp.int32)
    cnt_al = (cnt + SEG_ALIGN - 1) // SEG_ALIGN * SEG_ALIGN
    lstart = jnp.cumsum(cnt_al, axis=1) - cnt_al
    total = jnp.sum(cnt_al, axis=0)
    padded = (total + MOE_ROWS - 1) // MOE_ROWS * MOE_ROWS
    pend = jnp.cumsum(padded)
    pstart = pend - padded
    seg_dst = pstart[None, :] + jnp.cumsum(cnt_al, axis=0) - cnt_al
    tile_rows = jnp.sum(cnt_al, axis=1, keepdims=True)
    seg = jnp.concatenate([cnt_al, lstart, seg_dst, tile_rows, jnp.zeros_like(cnt)[:, 1:]], axis=1)
    seg = seg.reshape(n_tiles, 1, 4 * N_EXPERTS)
    n_used = (pend[-1] // MOE_ROWS).reshape(1)
    tail = jnp.concatenate([pstart + total, padded - total, n_used, jnp.zeros((2 * N_EXPERTS - 1,), jnp.int32)])
    tail = tail.reshape(1, 1, 4 * N_EXPERTS)
    seg_pad = n_tiles * N_EXPERTS * (SEG_ALIGN - 1)
    n_pad = (2 * n + seg_pad + N_EXPERTS * (MOE_ROWS - SEG_ALIGN) + MOE_ROWS - 1) // MOE_ROWS * MOE_ROWS
    n_blk = n_pad // MOE_ROWS
    rows_l = (2 * tm + N_EXPERTS * (SEG_ALIGN - 1) + MXU_DIM - 1) // MXU_DIM * MXU_DIM
    blk_start = jnp.arange(n_blk, dtype=jnp.int32) * MOE_ROWS
    blk_expert = jnp.minimum(jnp.sum(blk_start[:, None] >= pend[None, :], axis=-1), N_EXPERTS - 1)

    xs = _dispatch(seg, tail, route_t, f_lat, n_pad, rows_l, tm)
    ys = _experts(blk_expert.astype(jnp.int32), n_used.astype(jnp.int32), xs,
                  w_exp_gate[0], w_exp_up[0], w_exp_down[0])
    out = _combine(seg, route, h, mod3, final_norm_g.reshape(1, d), ys, rows_l, s, tm)
    return out.reshape(b, s, d)
```

```python
import functools

import jax
import jax.numpy as jnp
import numpy as np
from jax import lax
from jax.experimental import pallas as pl
from jax.experimental.pallas import tpu as pltpu

F32 = jnp.float32
BF16 = jnp.bfloat16

D_MODEL = 1024
GRID_W = 64
NA_HEADS = 8
NA_HEAD_DIM = 64
NA_WIDTH = NA_HEADS * NA_HEAD_DIM
NA_WIN_H = 8
NA_WIN_W = 16
GLA_HEADS = 4
GLA_DK = D_MODEL // 2
GLA_DV = D_MODEL
GLA_HK = GLA_DK // GLA_HEADS
GLA_HV = GLA_DV // GLA_HEADS
GLA_GATE_RANK = 16
GLA_GATE_NORM = 16.0
GLA_CHUNK = 64
GLA_GROUP = 256
ROPE_BASE = 10000.0
N_GROUPS = 4
EXPERTS_PER_GROUP = 8
N_EXPERTS = N_GROUPS * EXPERTS_PER_GROUP
D_EXPERT = D_MODEL // 2
EPS = 1e-6
NEG_INF = -1e30

LANES = 128
MXU_DIM = 256
LR_PAD = LANES
ROUTE_LANES = LANES
EXPERT_LANE0 = N_GROUPS
MOE_ROWS = 512
SEG_ALIGN = 8
SEG_SMALL = (8, 16, 32)
SEG_BIG = 64
WAIT_SIZES = (1024, 512, 256, 128, 64, 32, 16, 8)
TAIL_SIZES = (256, 128, 64, 32, 16, 8)
VMEM_LIMIT = 56 * 1024 * 1024

_NT = (((1,), (1,)), ((), ()))
_TN = (((0,), (0,)), ((), ()))


def _params(sem, vmem=VMEM_LIMIT):
    return pltpu.CompilerParams(dimension_semantics=sem, vmem_limit_bytes=vmem)


def _mod_kernel(c_ref, w_ref, b_ref, o_ref):
    c = c_ref[...]
    s = c * jax.nn.sigmoid(c)
    o_ref[...] = jnp.dot(s.astype(BF16), w_ref[...].astype(BF16),
                         preferred_element_type=F32) + b_ref[...]


def _modulation(cs, w_mod, b_mod):
    rows, d = cs.shape
    n = w_mod.shape[1]
    bn = 1024
    return pl.pallas_call(
        _mod_kernel,
        grid=(n // bn,),
        in_specs=[pl.BlockSpec((rows, d), lambda j: (0, 0)),
                  pl.BlockSpec((d, bn), lambda j: (0, j)),
                  pl.BlockSpec((1, bn), lambda j: (0, j))],
        out_specs=pl.BlockSpec((rows, bn), lambda j: (0, j)),
        out_shape=jax.ShapeDtypeStruct((rows, n), F32),
        compiler_params=_params(("arbitrary",)),
        name="mod",
    )(cs, w_mod, b_mod.reshape(1, n))


def _inproj_kernel(specs, use_rope, n_w, n_out, x_ref, mod_ref, g_ref, *rest):
    if use_rope:
        cos_ref, sin_ref = rest[:2]
        rest = rest[2:]
    w_refs = rest[:n_w]
    outs = rest[n_w:n_w + n_out]
    a_scr = rest[n_w + n_out]
    x = x_ref[...]
    ms = jnp.mean(x * x, axis=-1, keepdims=True)
    a = x * lax.rsqrt(ms + EPS) * g_ref[...] * (1.0 + mod_ref[0, 1:2, :]) + mod_ref[0, 0:1, :]
    a_scr[...] = a.astype(BF16)
    for (oi, oc, wi, wc, width, kind) in specs:
        acc = jnp.dot(a_scr[...], w_refs[wi][:, wc:wc + width], preferred_element_type=F32)
        if kind in ("rope", "rope_scaled"):
            if kind == "rope_scaled":
                acc = acc * (GLA_HK ** -0.5)
            lane = lax.broadcasted_iota(jnp.int32, acc.shape, 1)
            rot = jnp.where((lane & 63) < 32,
                            pltpu.roll(acc, width - 32, axis=1),
                            pltpu.roll(acc, 32, axis=1))
            acc = acc * cos_ref[...] + rot * sin_ref[...]
        elif kind == "silu":
            acc = acc * jax.nn.sigmoid(acc)
        elif kind == "sigmoid":
            acc = jax.nn.sigmoid(acc)
        outs[oi][:, oc:oc + width] = acc.astype(outs[oi].dtype)


def _inproj(x2d, mod3, mod_row_fn, g, weights, specs, out_widths, tm, rope=None, name="inproj"):
    n, d = x2d.shape
    use_rope = rope is not None
    in_specs = [pl.BlockSpec((tm, d), lambda i: (i, 0)),
                pl.BlockSpec((1, 6, d), lambda i: (mod_row_fn(i), 0, 0)),
                pl.BlockSpec((1, d), lambda i: (0, 0))]
    args = [x2d, mod3, g.reshape(1, d)]
    if use_rope:
        cos, sin = rope
        nb = cos.shape[0] // tm
        in_specs += [pl.BlockSpec((tm, cos.shape[1]), lambda i: (i % nb, 0)),
                     pl.BlockSpec((tm, cos.shape[1]), lambda i: (i % nb, 0))]
        args += [cos, sin]
    in_specs += [pl.BlockSpec(w.shape, lambda i: (0, 0), pipeline_mode=pl.Buffered(1)) for w in weights]
    args += list(weights)
    out_specs = [pl.BlockSpec((tm, w), lambda i: (i, 0)) for w in out_widths]
    out_shape = [jax.ShapeDtypeStruct((n, w), BF16) for w in out_widths]
    return pl.pallas_call(
        functools.partial(_inproj_kernel, specs, use_rope, len(weights), len(out_widths)),
        grid=(n // tm,),
        in_specs=in_specs,
        out_specs=out_specs,
        out_shape=out_shape,
        scratch_shapes=[pltpu.VMEM((tm, d), BF16)],
        compiler_params=_params(("arbitrary",)),
        name=name,
    )(*args)


def _na_kernel(rows_per_step, n_rows, q_ref, k_ref, v_ref, kc_ref, vc_ref, bias_ref, o_ref, sw_scr, sc_scr):
    j = pl.program_id(1)
    lane = lax.broadcasted_iota(jnp.int32, (GRID_W, LANES), 1)
    first_head = lane < NA_HEAD_DIM
    scale = NA_HEAD_DIM ** -0.5
    win_keys = NA_WIN_H * GRID_W
    pairs = range(NA_HEADS // 2)
    lanes = [slice(p * LANES, (p + 1) * LANES) for p in pairs]
    stack = 2 * GRID_W

    def window(rr):
        r = j * rows_per_step + rr
        rs = jnp.clip(r - NA_WIN_H // 2, 0, n_rows - NA_WIN_H)
        return r - rs, pl.multiple_of(rr * GRID_W, GRID_W), pl.multiple_of(rs * GRID_W, GRID_W)

    def scores(rr, slot):
        _, t0, ks = window(rr)
        for p in pairs:
            qp = q_ref[0, pl.ds(t0, GRID_W), lanes[p]] * scale
            zero = jnp.zeros_like(qp)
            qs = jnp.concatenate([jnp.where(first_head, qp, zero), jnp.where(first_head, zero, qp)], axis=0)
            sw_scr[slot, p * stack:(p + 1) * stack, :] = lax.dot_general(
                qs, k_ref[0, pl.ds(ks, win_keys), lanes[p]], _NT, preferred_element_type=F32)
            sc_scr[slot, p * stack:(p + 1) * stack, :] = lax.dot_general(
                qs, kc_ref[0, :, lanes[p]], _NT, preferred_element_type=F32)

    def attend(rr, slot):
        d, t0, ks = window(rr)
        pws, pcs, dens = [], [], []
        for p in pairs:
            bias = jnp.concatenate([bias_ref[(NA_WIN_H - 1) - d + 2 * i, p] for i in range(NA_WIN_H // 2)], axis=1)
            sw = sw_scr[slot, p * stack:(p + 1) * stack, :] + bias
            sc = sc_scr[slot, p * stack:(p + 1) * stack, :]
            m = jnp.maximum(jnp.max(sw, axis=-1, keepdims=True), jnp.max(sc, axis=-1, keepdims=True))
            pw = jnp.exp(sw - m)
            pc = jnp.exp(sc - m)
            dens.append(jnp.sum(pw, axis=-1, keepdims=True) + jnp.sum(pc, axis=-1, keepdims=True))
            pws.append(pw.astype(BF16))
            pcs.append(pc.astype(BF16))
        for p in pairs:
            o2 = (jnp.dot(pws[p], v_ref[0, pl.ds(ks, win_keys), lanes[p]], preferred_element_type=F32)
                  + jnp.dot(pcs[p], vc_ref[0, :, lanes[p]], preferred_element_type=F32)) / dens[p]
            o_ref[0, pl.ds(t0, GRID_W), lanes[p]] = jnp.where(first_head, o2[:GRID_W],
                                                             o2[GRID_W:]).astype(o_ref.dtype)

    scores(0, 0)

    def two_rows(k, carry):
        scores(2 * k + 1, 1)
        attend(2 * k, 0)
        scores(2 * k + 2, 0)
        attend(2 * k + 1, 1)
        return carry

    lax.fori_loop(0, rows_per_step // 2 - 1, two_rows, 0)
    scores(rows_per_step - 1, 1)
    attend(rows_per_step - 2, 0)
    attend(rows_per_step - 1, 1)


def _na_bias_table(rpb):
    n_heads, n_dr, n_dc = rpb.shape
    c = np.arange(GRID_W)
    cstart = np.clip(c - NA_WIN_W // 2, 0, GRID_W - NA_WIN_W)
    kc = np.arange(GRID_W)
    valid = (kc[None, :] >= cstart[:, None]) & (kc[None, :] < cstart[:, None] + NA_WIN_W)
    col_idx = np.clip(kc[None, :] - c[:, None] + NA_WIN_W - 1, 0, n_dc - 1)
    pick = np.zeros((n_dc, GRID_W * GRID_W), np.float32)
    pick[col_idx.reshape(-1), np.arange(GRID_W * GRID_W)] = 1.0
    t = jnp.dot(rpb.reshape(n_heads * n_dr, n_dc).astype(F32), jnp.asarray(pick),
                precision=lax.Precision.HIGHEST).reshape(n_heads, n_dr, GRID_W, GRID_W)
    t = jnp.where(jnp.asarray(valid)[None, None], t, NEG_INF)
    t2 = jnp.concatenate([t[:, :-1], t[:, 1:]], axis=-1)
    return t2.transpose(1, 0, 2, 3).reshape(n_dr - 1, n_heads // 2, 2 * GRID_W, 2 * GRID_W)


def _na_attention(q, k, v, kc, vc, bias, rows_per_step=16):
    b, s, w = q.shape
    n_rows = s // GRID_W
    ctx_len = kc.shape[1]
    blk = rows_per_step * GRID_W
    return pl.pallas_call(
        functools.partial(_na_kernel, rows_per_step, n_rows),
        grid=(b, n_rows // rows_per_step),
        in_specs=[pl.BlockSpec((1, blk, w), lambda bi, j: (bi, j, 0)),
                  pl.BlockSpec((1, s, w), lambda bi, j: (bi, 0, 0)),
                  pl.BlockSpec((1, s, w), lambda bi, j: (bi, 0, 0)),
                  pl.BlockSpec((1, ctx_len, w), lambda bi, j: (bi, 0, 0)),
                  pl.BlockSpec((1, ctx_len, w), lambda bi, j: (bi, 0, 0)),
                  pl.BlockSpec(bias.shape, lambda bi, j: (0, 0, 0, 0), pipeline_mode=pl.Buffered(1))],
        out_specs=pl.BlockSpec((1, blk, w), lambda bi, j: (bi, j, 0)),
        out_shape=jax.ShapeDtypeStruct((b, s, w), BF16),
        scratch_shapes=[pltpu.VMEM((2, NA_HEADS * GRID_W, NA_WIN_H * GRID_W), F32),
                        pltpu.VMEM((2, NA_HEADS * GRID_W, ctx_len), F32)],
        compiler_params=_params(("arbitrary", "arbitrary")),
        name="na_attn",
    )(q, k, v, kc, vc, bias)


def _gla_kernel(n_groups, gq_ref, gk_ref, gv_ref, lr_ref, sog_ref, ck_ref, cv_ref, clr_ref,
                wa_ref, ba_ref, gn_ref, o_ref, st_scr, of_scr, qe_scr, kd_scr, ckd_scr, dec_scr, cdec_scr,
                qk_a):
    c_len = GLA_CHUNK
    grp = GLA_GROUP
    cpg = grp // c_len
    row = lax.broadcasted_iota(jnp.int32, (grp, grp), 0)
    col = lax.broadcasted_iota(jnp.int32, (grp, grp), 1)
    same_chunk = (row // c_len) == (col // c_len)
    masks = (same_chunk & (col <= row), same_chunk & (col >= row))

    def gates(lr_rows, k_rows, direction):
        tri = masks[direction].astype(BF16)
        z = jnp.dot(lr_rows, wa_ref[direction], preferred_element_type=F32) + ba_ref[direction:direction + 1, :]
        g = (jnp.minimum(z, 0.0) - jnp.log(1.0 + jnp.exp(-jnp.abs(z)))) * (1.0 / GLA_GATE_NORM)
        g1 = g.astype(BF16)
        g2 = (g - g1.astype(F32)).astype(BF16)
        b = jnp.dot(tri, g1, preferred_element_type=F32) + jnp.dot(tri, g2, preferred_element_type=F32)
        edge = c_len - 1 if direction == 0 else 0
        decs = [jnp.exp(b[j * c_len + edge:j * c_len + edge + 1, :]) for j in range(cpg)]
        ke = k_rows * jnp.exp(-b)
        kd = (ke * jnp.concatenate([jnp.broadcast_to(v, (c_len, v.shape[1])) for v in decs], axis=0)).astype(BF16)
        dec = jnp.concatenate(decs + [jnp.ones_like(v) for v in decs], axis=0)
        return b, ke.astype(BF16), kd, dec

    ck = ck_ref[0].astype(F32)
    for direction in range(2):
        _, _, kd, dec = gates(clr_ref[0], ck, direction)
        ckd_scr[direction] = kd
        cdec_scr[direction] = dec

    def group_gates(gi, qk_buf):
        rows = pl.ds(pl.multiple_of(gi * grp, grp), grp)
        qf = gq_ref[0, rows, :].astype(F32)
        kf = gk_ref[0, rows, :].astype(F32)
        lr_rows = lr_ref[0, rows, :]
        for direction in range(2):
            b, ke, kd, dec = gates(lr_rows, kf, direction)
            qe = (qf * jnp.exp(b)).astype(BF16)
            qk_buf[direction] = qe
            qk_buf[2 + direction] = ke
            qe_scr[direction, rows, :] = qe
            kd_scr[direction, rows, :] = kd
            dec_scr[direction, gi] = dec

    def group_attend(gi, qk_buf):
        rows = pl.ds(pl.multiple_of(gi * grp, grp), grp)
        for h in range(GLA_HEADS):
            kl = slice(h * GLA_HK, (h + 1) * GLA_HK)
            vl = slice(h * GLA_HV, (h + 1) * GLA_HV)
            att = (jnp.where(masks[0], lax.dot_general(qk_buf[0, :, kl], qk_buf[2, :, kl], _NT,
                                                       preferred_element_type=F32), 0.0)
                   + jnp.where(masks[1], lax.dot_general(qk_buf[1, :, kl], qk_buf[3, :, kl], _NT,
                                                         preferred_element_type=F32), 0.0))
            of_scr[rows, vl] = jnp.dot(att.astype(BF16), gv_ref[0, rows, vl], preferred_element_type=F32)

    def group_body(gi, carry):
        group_gates(gi, qk_a)
        group_attend(gi, qk_a)
        return carry

    lax.fori_loop(0, n_groups, group_body, 0)

    def state_step(h, v_rows, kd_rows, dec_row):
        kl = slice(h * GLA_HK, (h + 1) * GLA_HK)
        upd = lax.dot_general(v_rows, kd_rows[:, kl], _TN, preferred_element_type=F32)
        st_scr[h] = st_scr[h] * dec_row[:, kl] + upd

    def scan(direction):
        st_scr[...] = jnp.zeros_like(st_scr)
        order = range(cpg) if direction == 0 else range(cpg - 1, -1, -1)
        for j in order:
            rows = slice(j * c_len, (j + 1) * c_len)
            for h in range(GLA_HEADS):
                state_step(h, cv_ref[0, rows, h * GLA_HV:(h + 1) * GLA_HV],
                           ckd_scr[direction, rows, :], cdec_scr[direction, j:j + 1, :])

        def group_scan(i, carry):
            gi = i if direction == 0 else n_groups - 1 - i
            for j in order:
                rows = pl.ds(pl.multiple_of(gi * grp + j * c_len, c_len), c_len)
                qe_rows = qe_scr[direction, rows, :]
                kd_rows = kd_scr[direction, rows, :]
                dec_row = dec_scr[direction, gi, j:j + 1, :]
                for h in range(GLA_HEADS):
                    kl = slice(h * GLA_HK, (h + 1) * GLA_HK)
                    vl = slice(h * GLA_HV, (h + 1) * GLA_HV)
                    o_t = of_scr[rows, vl] + lax.dot_general(qe_rows[:, kl], st_scr[h].astype(BF16), _NT,
                                                             preferred_element_type=F32)
                    if direction == 0:
                        of_scr[rows, vl] = o_t
                    else:
                        ms = jnp.mean(o_t * o_t, axis=-1, keepdims=True)
                        o_n = o_t * lax.rsqrt(ms + EPS) * gn_ref[:, vl] * sog_ref[0, rows, vl].astype(F32)
                        o_ref[0, rows, vl] = o_n.astype(o_ref.dtype)
                    state_step(h, gv_ref[0, rows, vl], kd_rows, dec_row)
            return carry

        lax.fori_loop(0, n_groups, group_scan, 0)

    scan(0)
    scan(1)


def _gla(gq, gk, gv, lr, sog, ck, cv, clr, wa, ba, gn):
    b, s, _ = gq.shape
    ctx_len = ck.shape[1]
    assert ctx_len == GLA_GROUP and s % GLA_GROUP == 0
    n_groups = s // GLA_GROUP
    cpg = GLA_GROUP // GLA_CHUNK

    def per_batch(t, w):
        return pl.BlockSpec((1, t, w), lambda bi: (bi, 0, 0))

    return pl.pallas_call(
        functools.partial(_gla_kernel, n_groups),
        grid=(b,),
        in_specs=[per_batch(s, GLA_DK), per_batch(s, GLA_DK), per_batch(s, GLA_DV),
                  per_batch(s, LR_PAD), per_batch(s, GLA_DV),
                  per_batch(ctx_len, GLA_DK), per_batch(ctx_len, GLA_DV), per_batch(ctx_len, LR_PAD),
                  pl.BlockSpec(wa.shape, lambda bi: (0, 0, 0)),
                  pl.BlockSpec(ba.shape, lambda bi: (0, 0)),
                  pl.BlockSpec(gn.shape, lambda bi: (0, 0))],
        out_specs=per_batch(s, GLA_DV),
        out_shape=jax.ShapeDtypeStruct((b, s, GLA_DV), BF16),
        scratch_shapes=[pltpu.VMEM((GLA_HEADS, GLA_HV, GLA_HK), F32),
                        pltpu.VMEM((s, GLA_DV), F32),
                        pltpu.VMEM((2, s, GLA_DK), BF16),
                        pltpu.VMEM((2, s, GLA_DK), BF16),
                        pltpu.VMEM((2, ctx_len, GLA_DK), BF16),
                        pltpu.VMEM((2, n_groups, 2 * cpg, GLA_DK), F32),
                        pltpu.VMEM((2, 2 * cpg, GLA_DK), F32),
                        pltpu.VMEM((4, GLA_GROUP, GLA_DK), BF16)],
        compiler_params=_params(("arbitrary",), vmem=60 * 1024 * 1024),
        name="gla",
    )(gq, gk, gv, lr, sog, ck, cv, clr, wa, ba, gn)


def _merge_kernel(ona_ref, ogla_ref, sgn_ref, sgg_ref, x_ref, mod_ref, gffn_ref,
                  wna_ref, wgla_ref, wout_ref, wr_ref, br_ref,
                  h_ref, f_ref, route_ref, route_t_ref, cnt_ref):
    rows = x_ref.shape[0]
    lane = lax.broadcasted_iota(jnp.int32, (rows, ROUTE_LANES), 1)
    big = jnp.int32(ROUTE_LANES)
    r_i = lax.broadcasted_iota(jnp.int32, (rows, rows), 0)
    c_i = lax.broadcasted_iota(jnp.int32, (rows, rows), 1)
    before = (c_i < r_i).astype(BF16)
    is_grp = lane < N_GROUPS

    def first_argmax(vals):
        m = jnp.max(vals, axis=-1, keepdims=True)
        idx = jnp.min(jnp.where(vals == m, lane, big), axis=-1, keepdims=True)
        return m, idx

    for rs in (slice(0, rows),):
        y1 = (sgn_ref[rs, :].astype(F32) * jnp.dot(ona_ref[rs, :], wna_ref[...], preferred_element_type=F32)
              + sgg_ref[rs, :].astype(F32) * jnp.dot(ogla_ref[rs, :], wgla_ref[...], preferred_element_type=F32))
        y = jnp.dot(y1.astype(BF16), wout_ref[...], preferred_element_type=F32)
        h = x_ref[rs, :] + mod_ref[0, 2:3, :] * y
        h_ref[rs, :] = h
        ms = jnp.mean(h * h, axis=-1, keepdims=True)
        f = h * lax.rsqrt(ms + EPS) * gffn_ref[...] * (1.0 + mod_ref[0, 4:5, :]) + mod_ref[0, 3:4, :]
        f_ref[rs, :] = f.astype(f_ref.dtype)

        f_hi = f.astype(BF16)
        f_lo = (f - f_hi.astype(F32)).astype(BF16)
        both = jnp.dot(f_hi, wr_ref[...], preferred_element_type=F32)
        logits = (both[:, :ROUTE_LANES] + both[:, ROUTE_LANES:]
                  + jnp.dot(f_lo, wr_ref[:, :ROUTE_LANES], preferred_element_type=F32)) + br_ref[...]

        lg = jnp.where(is_grp, logits, -jnp.inf)
        mg, grp = first_argmax(lg)
        p_grp = 1.0 / jnp.sum(jnp.where(is_grp, jnp.exp(lg - mg), 0.0), axis=-1, keepdims=True)
        lo = EXPERT_LANE0 + grp * EXPERTS_PER_GROUP
        in_grp = (lane >= lo) & (lane < lo + EXPERTS_PER_GROUP)
        le = jnp.where(in_grp, logits, -jnp.inf)
        m1, i1 = first_argmax(le)
        m2, i2 = first_argmax(jnp.where(lane == i1, -jnp.inf, le))
        t = jnp.exp(m2 - m1)
        w1 = p_grp / (1.0 + t)
        w2 = p_grp * t / (1.0 + t)

        hot1 = lane == i1
        hot2 = lane == i2
        onehot = (hot1 | hot2).astype(BF16)
        rank = jnp.dot(before, onehot, preferred_element_type=F32)
        cnt = jnp.sum(onehot.astype(F32), axis=0, keepdims=True)
        cnt_ref[0] = cnt
        cnt_al = jnp.floor((cnt + (SEG_ALIGN - 1.0)) * (1.0 / SEG_ALIGN)) * SEG_ALIGN
        l_i = lax.broadcasted_iota(jnp.int32, (ROUTE_LANES, ROUTE_LANES), 0)
        l_j = lax.broadcasted_iota(jnp.int32, (ROUTE_LANES, ROUTE_LANES), 1)
        lstart = jnp.dot(jnp.broadcast_to(cnt_al, (8, ROUTE_LANES)).astype(BF16), (l_i < l_j).astype(BF16),
                         preferred_element_type=F32)[0:1, :]
        pos = rank + lstart
        pos1 = jnp.sum(jnp.where(hot1, pos, 0.0), axis=-1, keepdims=True)
        pos2 = jnp.sum(jnp.where(hot2, pos, 0.0), axis=-1, keepdims=True)

        e1 = (i1 - EXPERT_LANE0).astype(F32)
        e2 = (i2 - EXPERT_LANE0).astype(F32)
        route = jnp.where(lane == 0, e1, 0.0)
        route = jnp.where(lane == 1, e2, route)
        route = jnp.where(lane == 2, w1, route)
        route = jnp.where(lane == 3, w2, route)
        route = jnp.where(lane == 4, pos1, route)
        route = jnp.where(lane == 5, pos2, route)
        route_ref[rs, :] = route
        route_t_ref[0] = route.T[0:8, :]


def _merge_route(ona, ogla, sgn, sgg, x2d, mod3, gffn, wna, wgla, wout, wr, br, tokens_per_batch, tm):
    n, d = x2d.shape
    per_b = tokens_per_batch // tm

    def tile(w):
        return pl.BlockSpec((tm, w), lambda i: (i, 0))

    def whole(a):
        return pl.BlockSpec(a.shape, lambda i: (0,) * a.ndim)

    return pl.pallas_call(
        _merge_kernel,
        grid=(n // tm,),
        in_specs=[tile(NA_WIDTH), tile(GLA_DV), tile(d), tile(d), tile(d),
                  pl.BlockSpec((1, 6, d), lambda i: (i // per_b, 0, 0)),
                  whole(gffn), whole(wna), whole(wgla), whole(wout), whole(wr), whole(br)],
        out_specs=[tile(d), tile(d), tile(ROUTE_LANES),
                   pl.BlockSpec((1, 8, tm), lambda i: (i, 0, 0)),
                   pl.BlockSpec((1, 1, ROUTE_LANES), lambda i: (i, 0, 0))],
        out_shape=[jax.ShapeDtypeStruct((n, d), F32),
                   jax.ShapeDtypeStruct((n, d), BF16),
                   jax.ShapeDtypeStruct((n, ROUTE_LANES), F32),
                   jax.ShapeDtypeStruct((n // tm, 8, tm), F32),
                   jax.ShapeDtypeStruct((n // tm, 1, ROUTE_LANES), F32)],
        compiler_params=_params(("arbitrary",)),
        name="merge_route",
    )(ona, ogla, sgn, sgg, x2d, mod3, gffn, wna, wgla, wout, wr, br)


HI_MASK = -65536


def _pack_rows(a):
    half = a.shape[1] // 2
    hi = lax.bitcast_convert_type(a[:, :half], jnp.int32)
    lo = lax.shift_right_logical(lax.bitcast_convert_type(a[:, half:], jnp.int32), 16)
    return hi | lo


def _unpack_rows(w):
    hi = lax.bitcast_convert_type(w & HI_MASK, F32).astype(BF16)
    lo = lax.bitcast_convert_type(lax.shift_left(w, 16), F32).astype(BF16)
    return hi, lo


def _segment_copies(seg_ref, make_copy, action):
    if action == "wait":
        total = seg_ref[0, 0, 3 * N_EXPERTS]
        for size in WAIT_SIZES:
            @pl.when((total & size) != 0)
            def _():
                make_copy(0, 0, size).wait()
        return

    def body(e, carry):
        cnt = seg_ref[0, 0, e]
        local = seg_ref[0, 0, N_EXPERTS + e]
        glob = seg_ref[0, 0, 2 * N_EXPERTS + e]
        off = jnp.int32(0)
        for size in SEG_SMALL:
            @pl.when((cnt & size) != 0)
            def _():
                make_copy(pl.multiple_of(local + off, SEG_ALIGN), pl.multiple_of(glob + off, SEG_ALIGN), size).start()
            off = off + (cnt & size)

        def big(k, c):
            o = off + k * SEG_BIG
            make_copy(pl.multiple_of(local + o, SEG_ALIGN), pl.multiple_of(glob + o, SEG_ALIGN), SEG_BIG).start()
            return c

        lax.fori_loop(0, cnt // SEG_BIG, big, 0)
        return carry

    lax.fori_loop(0, N_EXPERTS, body, 0)


def _dispatch_kernel(seg_ref, seg_prev_ref, tail_ref, lpos_ref, f_ref, xs_ref, sbuf, zbuf, sems):
    i = pl.program_id(0)
    last = pl.num_programs(0) - 1
    tm = f_ref.shape[0]
    rows_l = sbuf.shape[1]
    slot = i % 2
    lp = lpos_ref[0].astype(jnp.int32)
    row = lax.broadcasted_iota(jnp.int32, (rows_l, tm), 0)
    perm = ((row == lp[4:5, :]) | (row == lp[5:6, :])).astype(BF16)
    sbuf[slot] = _pack_rows(jnp.dot(perm, f_ref[...], preferred_element_type=F32))

    def copier(s):
        def copy(local, glob, size):
            return pltpu.make_async_copy(sbuf.at[s, pl.ds(local, size), :], xs_ref.at[pl.ds(glob, size), :],
                                         sems.at[s])
        return copy

    _segment_copies(seg_ref, copier(slot), "start")

    @pl.when(i > 0)
    def _():
        _segment_copies(seg_prev_ref, copier(1 - slot), "wait")

    @pl.when(i == last)
    def _():
        _segment_copies(seg_ref, copier(slot), "wait")
        sem = sems.at[slot]
        zbuf[...] = jnp.zeros_like(zbuf)

        def tail(action):
            def body(e, carry):
                start = tail_ref[0, 0, e]
                length = tail_ref[0, 0, N_EXPERTS + e]
                off = jnp.int32(0)
                for size in TAIL_SIZES:
                    @pl.when((length & size) != 0)
                    def _():
                        cp = pltpu.make_async_copy(zbuf.at[pl.ds(0, size), :],
                                                   xs_ref.at[pl.ds(pl.multiple_of(start + off, SEG_ALIGN), size), :],
                                                   sem)
                        getattr(cp, action)()
                    off = off + (length & size)
                return carry
            lax.fori_loop(0, N_EXPERTS, body, 0)

        tail("start")
        tail("wait")

        def unused(action):
            def body(blk, carry):
                cp = pltpu.make_async_copy(zbuf, xs_ref.at[pl.ds(pl.multiple_of(blk * MOE_ROWS, MOE_ROWS),
                                                                  MOE_ROWS), :], sem)
                getattr(cp, action)()
                return carry
            lax.fori_loop(tail_ref[0, 0, 2 * N_EXPERTS], xs_ref.shape[0] // MOE_ROWS, body, 0)

        unused("start")
        unused("wait")


def _dispatch(seg, tail, lpos_rows, f2d, n_pad, rows_l, tm):
    n, d = f2d.shape
    return pl.pallas_call(
        _dispatch_kernel,
        grid=(n // tm,),
        in_specs=[pl.BlockSpec((1, 1, seg.shape[2]), lambda i: (i, 0, 0), memory_space=pltpu.SMEM),
                  pl.BlockSpec((1, 1, seg.shape[2]), lambda i: (jnp.maximum(i - 1, 0), 0, 0),
                               memory_space=pltpu.SMEM),
                  pl.BlockSpec(tail.shape, lambda i: (0, 0, 0), memory_space=pltpu.SMEM),
                  pl.BlockSpec((1, 8, tm), lambda i: (i, 0, 0)),
                  pl.BlockSpec((tm, d), lambda i: (i, 0))],
        out_specs=pl.BlockSpec(memory_space=pl.ANY),
        out_shape=jax.ShapeDtypeStruct((n_pad, d // 2), jnp.int32),
        scratch_shapes=[pltpu.VMEM((2, rows_l, d // 2), jnp.int32),
                        pltpu.VMEM((MOE_ROWS, d // 2), jnp.int32),
                        pltpu.SemaphoreType.DMA((2,))],
        compiler_params=_params(("arbitrary",)),
        name="dispatch",
    )(seg, seg, tail, lpos_rows, f2d)


def _expert_kernel(be_ref, nu_ref, x_ref, wg_ref, wu_ref, wd_ref, y_ref, wgu_scr, wd_scr):
    i = pl.program_id(0)
    prev = be_ref[jnp.maximum(i - 1, 0)]
    half = x_ref.shape[1]

    @pl.when((i == 0) | (be_ref[i] != prev))
    def _():
        wgu_scr[:, :D_EXPERT] = wg_ref[0].astype(BF16)
        wgu_scr[:, D_EXPERT:] = wu_ref[0].astype(BF16)
        wd_scr[...] = wd_ref[0].astype(BF16)

    @pl.when(i < nu_ref[0])
    def _():
        x_hi, x_lo = _unpack_rows(x_ref[...])
        gu = (jnp.dot(x_hi, wgu_scr[:half, :], preferred_element_type=F32)
              + jnp.dot(x_lo, wgu_scr[half:, :], preferred_element_type=F32))
        gate = gu[:, :D_EXPERT]
        hdn = gate * jax.nn.sigmoid(gate) * gu[:, D_EXPERT:]
        y = jnp.dot(hdn.astype(BF16), wd_scr[...], preferred_element_type=F32)
        y_ref[...] = _pack_rows(y.astype(BF16).astype(F32))

    @pl.when(i >= nu_ref[0])
    def _():
        y_ref[...] = jnp.zeros_like(y_ref)


def _experts(blk_expert, n_used, xs, wg, wu, wd):
    n_pad, half = xs.shape
    d = 2 * half
    n_blk = n_pad // MOE_ROWS

    def used_block(i, be, nu):
        return (jnp.maximum(jnp.minimum(i, nu[0] - 1), 0), 0)

    grid_spec = pltpu.PrefetchScalarGridSpec(
        num_scalar_prefetch=2,
        grid=(n_blk,),
        in_specs=[pl.BlockSpec((MOE_ROWS, half), used_block),
                  pl.BlockSpec((1, d, D_EXPERT), lambda i, be, nu: (be[i], 0, 0)),
                  pl.BlockSpec((1, d, D_EXPERT), lambda i, be, nu: (be[i], 0, 0)),
                  pl.BlockSpec((1, D_EXPERT, d), lambda i, be, nu: (be[i], 0, 0))],
        out_specs=pl.BlockSpec((MOE_ROWS, half), lambda i, be, nu: (i, 0)),
        scratch_shapes=[pltpu.VMEM((d, 2 * D_EXPERT), BF16),
                        pltpu.VMEM((D_EXPERT, d), BF16)],
    )
    return pl.pallas_call(
        _expert_kernel,
        grid_spec=grid_spec,
        out_shape=jax.ShapeDtypeStruct((n_pad, half), jnp.int32),
        compiler_params=_params(("arbitrary",)),
        name="experts",
    )(blk_expert, n_used, xs, wg, wu, wd)


def _combine_kernel(seg_ref, seg_next_ref, col_ref, h_ref, mod_ref, fg_ref, ys_ref, o_ref, ybuf, sems):
    i = pl.program_id(0)
    tm = h_ref.shape[0]
    rows_l = ybuf.shape[1]
    slot = i % 2

    def copier(s):
        def copy(local, glob, size):
            return pltpu.make_async_copy(ys_ref.at[pl.ds(glob, size), :], ybuf.at[s, pl.ds(local, size), :],
                                         sems.at[s])
        return copy

    @pl.when(i == 0)
    def _():
        ybuf[...] = jnp.zeros_like(ybuf)
        _segment_copies(seg_ref, copier(slot), "start")

    @pl.when(i + 1 < pl.num_programs(0))
    def _():
        _segment_copies(seg_next_ref, copier(1 - slot), "start")

    _segment_copies(seg_ref, copier(slot), "wait")
    y_hi, y_lo = _unpack_rows(ybuf[slot])
    y_all = jnp.concatenate([y_hi, y_lo], axis=1)
    col = lax.broadcasted_iota(jnp.int32, (tm, rows_l), 1)
    info = col_ref[...]
    y1 = jnp.dot((col == info[:, 4:5].astype(jnp.int32)).astype(BF16), y_all, preferred_element_type=F32)
    y2 = jnp.dot((col == info[:, 5:6].astype(jnp.int32)).astype(BF16), y_all, preferred_element_type=F32)
    moe = info[:, 2:3] * y1 + info[:, 3:4] * y2
    h = h_ref[...] + mod_ref[0, 5:6, :] * moe
    ms = jnp.mean(h * h, axis=-1, keepdims=True)
    o_ref[...] = h * lax.rsqrt(ms + EPS) * fg_ref[...]


def _combine(seg, colinfo, h2d, mod3, fg, ys, rows_l, tokens_per_batch, tm):
    n, d = h2d.shape
    per_b = tokens_per_batch // tm
    return pl.pallas_call(
        _combine_kernel,
        grid=(n // tm,),
        in_specs=[pl.BlockSpec((1, 1, seg.shape[2]), lambda i: (i, 0, 0), memory_space=pltpu.SMEM),
                  pl.BlockSpec((1, 1, seg.shape[2]), lambda i: (jnp.minimum(i + 1, n // tm - 1), 0, 0),
                               memory_space=pltpu.SMEM),
                  pl.BlockSpec((tm, colinfo.shape[1]), lambda i: (i, 0)),
                  pl.BlockSpec((tm, d), lambda i: (i, 0)),
                  pl.BlockSpec((1, 6, d), lambda i: (i // per_b, 0, 0)),
                  pl.BlockSpec((1, d), lambda i: (0, 0)),
                  pl.BlockSpec(memory_space=pl.ANY)],
        out_specs=pl.BlockSpec((tm, d), lambda i: (i, 0)),
        out_shape=jax.ShapeDtypeStruct((n, d), F32),
        scratch_shapes=[pltpu.VMEM((2, rows_l, d // 2), jnp.int32), pltpu.SemaphoreType.DMA((2,))],
        compiler_params=_params(("arbitrary",)),
        name="combine",
    )(seg, seg, colinfo, h2d, mod3, fg, ys)


def _rope_tables(s):
    t = np.arange(s)
    pos_r = (t // GRID_W).astype(np.float32)
    pos_c = (t % GRID_W).astype(np.float32)
    nf = GLA_HK // 4
    inv = (np.float32(ROPE_BASE) ** (-np.arange(nf, dtype=np.float32) / np.float32(nf))).astype(np.float32)
    ang_r = pos_r[:, None] * inv
    ang_c = pos_c[:, None] * inv

    def half(ang):
        return (np.concatenate([np.cos(ang), np.cos(ang)], axis=-1),
                np.concatenate([-np.sin(ang), np.sin(ang)], axis=-1))

    cr, sr = half(ang_r)
    cc, sc = half(ang_c)
    cos = np.tile(np.concatenate([cr, cc], axis=-1), (1, GLA_HEADS)).astype(np.float32)
    sin = np.tile(np.concatenate([sr, sc], axis=-1), (1, GLA_HEADS)).astype(np.float32)
    return jnp.asarray(cos), jnp.asarray(sin)


def kernel(x, c, ctx, c_ctx, w_mod, b_mod, norm_attn_g, norm_ffn_g, w_in, w_gla_a2, b_gla_a2, gla_norm_g, na_rpb, w_na_o, w_gla_o, w_out, w_group, b_group, w_expert, b_expert, w_exp_gate, w_exp_up, w_exp_down, final_norm_g):
    b, s, d = x.shape
    ctx_len = ctx.shape[1]
    n = b * s
    assert w_mod.shape[0] == 1, "single-layer block"
    assert s % (GRID_W * NA_WIN_H) == 0 and ctx_len % GLA_CHUNK == 0

    mod_rows = -(-(b + 1) // 8) * 8
    cs = jnp.zeros((mod_rows, d), F32).at[:b].set(c).at[b].set(c_ctx)
    mod3 = _modulation(cs, w_mod[0], b_mod[0]).reshape(mod_rows, 6, d)

    o_q, o_k, o_v = 0, NA_WIDTH, 2 * NA_WIDTH
    o_gq = 3 * NA_WIDTH
    o_gk = o_gq + GLA_DK
    o_gv = o_gk + GLA_DK
    o_og = o_gv + GLA_DV
    o_lr = o_og + GLA_DV
    o_mn = o_lr + 2 * GLA_GATE_RANK
    w_a = w_in[0][:, :o_lr].astype(BF16)
    w_g = w_in[0][:, o_mn:].astype(BF16)
    w_lr = jnp.pad(w_in[0][:, o_lr:o_mn].astype(BF16), ((0, 0), (0, LR_PAD - 2 * GLA_GATE_RANK)))
    cw = 512
    lat_specs = [(0, 0, 0, o_q, cw, "plain"), (1, 0, 0, o_k, cw, "plain"), (2, 0, 0, o_v, cw, "plain"),
                 (3, 0, 0, o_gq, cw, "rope_scaled"), (4, 0, 0, o_gk, cw, "rope"),
                 (5, 0, 0, o_gv, cw, "plain"), (5, cw, 0, o_gv + cw, cw, "plain"),
                 (6, 0, 0, o_og, cw, "silu"), (6, cw, 0, o_og + cw, cw, "silu"),
                 (7, 0, 1, 0, cw, "sigmoid"), (7, cw, 1, cw, cw, "sigmoid"),
                 (8, 0, 1, d, cw, "sigmoid"), (8, cw, 1, d + cw, cw, "sigmoid"),
                 (9, 0, 2, 0, LR_PAD, "plain")]
    lat_widths = [NA_WIDTH, NA_WIDTH, NA_WIDTH, GLA_DK, GLA_DK, GLA_DV, GLA_DV, d, d, LR_PAD]
    tm = 512
    per_b = s // tm
    q_na, k_na, v_na, gq, gk, gv, sog, sgn, sgg, lr = _inproj(
        x.reshape(n, d), mod3, lambda i: i // per_b, norm_attn_g[0], [w_a, w_g, w_lr], lat_specs, lat_widths, tm,
        rope=_rope_tables(s), name="inproj_lat")

    ctx_specs = [(0, 0, 0, o_k, cw, "plain"), (1, 0, 0, o_v, cw, "plain"), (2, 0, 0, o_gk, cw, "plain"),
                 (3, 0, 0, o_gv, cw, "plain"), (3, cw, 0, o_gv + cw, cw, "plain"),
                 (4, 0, 1, 0, LR_PAD, "plain")]
    ctx_widths = [NA_WIDTH, NA_WIDTH, GLA_DK, GLA_DV, LR_PAD]
    kc_na, vc_na, ck, cv, clr = _inproj(
        ctx.reshape(b * ctx_len, d), mod3, lambda i: b, norm_attn_g[0], [w_a, w_lr], ctx_specs, ctx_widths,
        ctx_len, name="inproj_ctx")

    def lat3(a):
        return a.reshape(b, s, a.shape[-1])

    def ctx3(a):
        return a.reshape(b, ctx_len, a.shape[-1])

    o_na = _na_attention(lat3(q_na), lat3(k_na), lat3(v_na), ctx3(kc_na), ctx3(vc_na),
                         _na_bias_table(na_rpb[0]))

    wa = jnp.zeros((2, LR_PAD, GLA_DK), F32)
    wa = wa.at[0, :GLA_GATE_RANK].set(w_gla_a2[0, 0]).at[1, GLA_GATE_RANK:2 * GLA_GATE_RANK].set(w_gla_a2[0, 1])
    gn = jnp.tile(gla_norm_g[0], GLA_HEADS).reshape(1, GLA_DV)
    o_gla = _gla(lat3(gq), lat3(gk), lat3(gv), lat3(lr), lat3(sog), ctx3(ck), ctx3(cv), ctx3(clr),
                 wa.astype(BF16), b_gla_a2[0], gn)

    wr = jnp.zeros((d, ROUTE_LANES), F32)
    wr = wr.at[:, :N_GROUPS].set(w_group[0]).at[:, EXPERT_LANE0:EXPERT_LANE0 + N_EXPERTS].set(w_expert[0])
    wr_hi = wr.astype(BF16)
    br = jnp.zeros((1, ROUTE_LANES), F32)
    br = br.at[0, :N_GROUPS].set(b_group[0]).at[0, EXPERT_LANE0:EXPERT_LANE0 + N_EXPERTS].set(b_expert[0])
    h, f_lat, route, route_t, counts = _merge_route(
        o_na.reshape(n, NA_WIDTH), o_gla.reshape(n, GLA_DV), sgn, sgg, x.reshape(n, d), mod3, norm_ffn_g[0].reshape(1, d),
        w_na_o[0].astype(BF16), w_gla_o[0].astype(BF16), w_out[0].astype(BF16),
        jnp.concatenate([wr_hi, (wr - wr_hi.astype(F32)).astype(BF16)], axis=1), br, s, tm)

    n_tiles = n // tm
    cnt = counts[:, 0, EXPERT_LANE0:EXPERT_LANE0 + N_EXPERTS].astype(jnp.int32)
    cnt_al = (cnt + SEG_ALIGN - 1) // SEG_ALIGN * SEG_ALIGN
    lstart = jnp.cumsum(cnt_al, axis=1) - cnt_al
    total = jnp.sum(cnt_al, axis=0)
    padded = (total + MOE_ROWS - 1) // MOE_ROWS * MOE_ROWS
    pend = jnp.cumsum(padded)
    pstart = pend - padded
    seg_dst = pstart[None, :] + jnp.cumsum(cnt_al, axis=0) - cnt_al
    tile_rows = jnp.sum(cnt_al, axis=1, keepdims=True)
    seg = jnp.concatenate([cnt_al, lstart, seg_dst, tile_rows, jnp.zeros_like(cnt)[:, 1:]], axis=1)
    seg = seg.reshape(n_tiles, 1, 4 * N_EXPERTS)
    n_used = (pend[-1] // MOE_ROWS).reshape(1)
    tail = jnp.concatenate([pstart + total, padded - total, n_used, jnp.zeros((2 * N_EXPERTS - 1,), jnp.int32)])
    tail = tail.reshape(1, 1, 4 * N_EXPERTS)
    seg_pad = n_tiles * N_EXPERTS * (SEG_ALIGN - 1)
    n_pad = (2 * n + seg_pad + N_EXPERTS * (MOE_ROWS - SEG_ALIGN) + MOE_ROWS - 1) // MOE_ROWS * MOE_ROWS
    n_blk = n_pad // MOE_ROWS
    rows_l = (2 * tm + N_EXPERTS * (SEG_ALIGN - 1) + MXU_DIM - 1) // MXU_DIM * MXU_DIM
    blk_start = jnp.arange(n_blk, dtype=jnp.int32) * MOE_ROWS
    blk_expert = jnp.minimum(jnp.sum(blk_start[:, None] >= pend[None, :], axis=-1), N_EXPERTS - 1)

    xs = _dispatch(seg, tail, route_t, f_lat, n_pad, rows_l, tm)
    ys = _experts(blk_expert.astype(jnp.int32), n_used.astype(jnp.int32), xs,
                  w_exp_gate[0], w_exp_up[0], w_exp_down[0])
    out = _combine(seg, route, h, mod3, final_norm_g.reshape(1, d), ys, rows_l, s, tm)
    return out.reshape(b, s, d)
```

```python
import functools

import jax
import jax.numpy as jnp
import numpy as np
from jax import lax
from jax.experimental import pallas as pl
from jax.experimental.pallas import tpu as pltpu

F32 = jnp.float32
BF16 = jnp.bfloat16

D_MODEL = 1024
GRID_W = 64
NA_HEADS = 8
NA_HEAD_DIM = 64
NA_WIDTH = NA_HEADS * NA_HEAD_DIM
NA_WIN_H = 8
NA_WIN_W = 16
GLA_HEADS = 4
GLA_DK = D_MODEL // 2
GLA_DV = D_MODEL
GLA_HK = GLA_DK // GLA_HEADS
GLA_HV = GLA_DV // GLA_HEADS
GLA_GATE_RANK = 16
GLA_GATE_NORM = 16.0
GLA_CHUNK = 64
GLA_GROUP = 256
ROPE_BASE = 10000.0
N_GROUPS = 4
EXPERTS_PER_GROUP = 8
N_EXPERTS = N_GROUPS * EXPERTS_PER_GROUP
D_EXPERT = D_MODEL // 2
EPS = 1e-6
NEG_INF = -1e30
LOG2E = 1.4426950408889634

LANES = 128
MXU_DIM = 256
LR_PAD = LANES
ROUTE_LANES = LANES
EXPERT_LANE0 = N_GROUPS
MOE_ROWS = 512
SEG_ALIGN = 8
SEG_SMALL = (8, 16, 32)
SEG_BIG = 64
WAIT_SIZES = (1024, 512, 256, 128, 64, 32, 16, 8)
TAIL_SIZES = (256, 128, 64, 32, 16, 8)
VMEM_LIMIT = 56 * 1024 * 1024

_NT = (((1,), (1,)), ((), ()))
_TN = (((0,), (0,)), ((), ()))


def _params(sem, vmem=VMEM_LIMIT):
    return pltpu.CompilerParams(dimension_semantics=sem, vmem_limit_bytes=vmem)


def _mod_kernel(c_ref, w_ref, b_ref, o_ref):
    c = c_ref[...]
    s = c * jax.nn.sigmoid(c)
    o_ref[...] = jnp.dot(s.astype(BF16), w_ref[...].astype(BF16),
                         preferred_element_type=F32) + b_ref[...]


def _modulation(cs, w_mod, b_mod):
    rows, d = cs.shape
    n = w_mod.shape[1]
    bn = 1024
    return pl.pallas_call(
        _mod_kernel,
        grid=(n // bn,),
        in_specs=[pl.BlockSpec((rows, d), lambda j: (0, 0)),
                  pl.BlockSpec((d, bn), lambda j: (0, j)),
                  pl.BlockSpec((1, bn), lambda j: (0, j))],
        out_specs=pl.BlockSpec((rows, bn), lambda j: (0, j)),
        out_shape=jax.ShapeDtypeStruct((rows, n), F32),
        compiler_params=_params(("arbitrary",)),
        name="mod",
    )(cs, w_mod, b_mod.reshape(1, n))


def _inproj_kernel(specs, use_rope, n_w, n_out, x_ref, mod_ref, g_ref, *rest):
    if use_rope:
        cos_ref, sin_ref = rest[:2]
        rest = rest[2:]
    w_refs = rest[:n_w]
    outs = rest[n_w:n_w + n_out]
    a_scr = rest[n_w + n_out]
    x = x_ref[...]
    ms = jnp.mean(x * x, axis=-1, keepdims=True)
    a = x * lax.rsqrt(ms + EPS) * g_ref[...] * (1.0 + mod_ref[0, 1:2, :]) + mod_ref[0, 0:1, :]
    a_scr[...] = a.astype(BF16)
    for (oi, oc, wi, wc, width, kind) in specs:
        acc = jnp.dot(a_scr[...], w_refs[wi][:, wc:wc + width], preferred_element_type=F32)
        if kind in ("rope", "rope_scaled"):
            if kind == "rope_scaled":
                acc = acc * (GLA_HK ** -0.5)
            lane = lax.broadcasted_iota(jnp.int32, acc.shape, 1)
            rot = jnp.where((lane & 63) < 32,
                            pltpu.roll(acc, width - 32, axis=1),
                            pltpu.roll(acc, 32, axis=1))
            acc = acc * cos_ref[...] + rot * sin_ref[...]
        elif kind == "silu":
            acc = acc * jax.nn.sigmoid(acc)
        elif kind == "sigmoid":
            acc = jax.nn.sigmoid(acc)
        outs[oi][:, oc:oc + width] = acc.astype(outs[oi].dtype)


def _inproj(x2d, mod3, mod_row_fn, g, weights, specs, out_widths, tm, rope=None, name="inproj"):
    n, d = x2d.shape
    use_rope = rope is not None
    in_specs = [pl.BlockSpec((tm, d), lambda i: (i, 0)),
                pl.BlockSpec((1, 6, d), lambda i: (mod_row_fn(i), 0, 0)),
                pl.BlockSpec((1, d), lambda i: (0, 0))]
    args = [x2d, mod3, g.reshape(1, d)]
    if use_rope:
        cos, sin = rope
        nb = cos.shape[0] // tm
        in_specs += [pl.BlockSpec((tm, cos.shape[1]), lambda i: (i % nb, 0)),
                     pl.BlockSpec((tm, cos.shape[1]), lambda i: (i % nb, 0))]
        args += [cos, sin]
    in_specs += [pl.BlockSpec(w.shape, lambda i: (0, 0), pipeline_mode=pl.Buffered(1)) for w in weights]
    args += list(weights)
    out_specs = [pl.BlockSpec((tm, w), lambda i: (i, 0)) for w in out_widths]
    out_shape = [jax.ShapeDtypeStruct((n, w), BF16) for w in out_widths]
    return pl.pallas_call(
        functools.partial(_inproj_kernel, specs, use_rope, len(weights), len(out_widths)),
        grid=(n // tm,),
        in_specs=in_specs,
        out_specs=out_specs,
        out_shape=out_shape,
        scratch_shapes=[pltpu.VMEM((tm, d), BF16)],
        compiler_params=_params(("arbitrary",)),
        name=name,
    )(*args)


def _na_kernel(rows_per_step, n_rows, q_ref, k_ref, v_ref, kc_ref, vc_ref, bias_ref, o_ref, sw_scr, sc_scr):
    j = pl.program_id(1)
    lane = lax.broadcasted_iota(jnp.int32, (GRID_W, LANES), 1)
    first_head = lane < NA_HEAD_DIM
    scale = NA_HEAD_DIM ** -0.5
    win_keys = NA_WIN_H * GRID_W
    pairs = range(NA_HEADS // 2)
    lanes = [slice(p * LANES, (p + 1) * LANES) for p in pairs]
    stack = 2 * GRID_W

    def window(rr):
        r = j * rows_per_step + rr
        rs = jnp.clip(r - NA_WIN_H // 2, 0, n_rows - NA_WIN_H)
        return r - rs, pl.multiple_of(rr * GRID_W, GRID_W), pl.multiple_of(rs * GRID_W, GRID_W)

    def scores(rr, slot):
        _, t0, ks = window(rr)
        for p in pairs:
            qp = q_ref[0, pl.ds(t0, GRID_W), lanes[p]] * scale
            zero = jnp.zeros_like(qp)
            qs = jnp.concatenate([jnp.where(first_head, qp, zero), jnp.where(first_head, zero, qp)], axis=0)
            sw_scr[slot, p * stack:(p + 1) * stack, :] = lax.dot_general(
                qs, k_ref[0, pl.ds(ks, win_keys), lanes[p]], _NT, preferred_element_type=F32)
            sc_scr[slot, p * stack:(p + 1) * stack, :] = lax.dot_general(
                qs, kc_ref[0, :, lanes[p]], _NT, preferred_element_type=F32)

    def attend(rr, slot):
        d, t0, ks = window(rr)
        pws, pcs, dens = [], [], []
        for p in pairs:
            bias = jnp.concatenate([bias_ref[(NA_WIN_H - 1) - d + 2 * i, p] for i in range(NA_WIN_H // 2)], axis=1)
            sw = sw_scr[slot, p * stack:(p + 1) * stack, :] * LOG2E + bias
            sc = sc_scr[slot, p * stack:(p + 1) * stack, :] * LOG2E
            m = jnp.maximum(jnp.max(sw, axis=-1, keepdims=True), jnp.max(sc, axis=-1, keepdims=True))
            pw = jnp.exp2(sw - m)
            pc = jnp.exp2(sc - m)
            dens.append(jnp.sum(pw, axis=-1, keepdims=True) + jnp.sum(pc, axis=-1, keepdims=True))
            pws.append(pw.astype(BF16))
            pcs.append(pc.astype(BF16))
        for p in pairs:
            o2 = (jnp.dot(pws[p], v_ref[0, pl.ds(ks, win_keys), lanes[p]], preferred_element_type=F32)
                  + jnp.dot(pcs[p], vc_ref[0, :, lanes[p]], preferred_element_type=F32)) / dens[p]
            o_ref[0, pl.ds(t0, GRID_W), lanes[p]] = jnp.where(first_head, o2[:GRID_W],
                                                             o2[GRID_W:]).astype(o_ref.dtype)

    scores(0, 0)

    def two_rows(k, carry):
        scores(2 * k + 1, 1)
        attend(2 * k, 0)
        scores(2 * k + 2, 0)
        attend(2 * k + 1, 1)
        return carry

    lax.fori_loop(0, rows_per_step // 2 - 1, two_rows, 0)
    scores(rows_per_step - 1, 1)
    attend(rows_per_step - 2, 0)
    attend(rows_per_step - 1, 1)


def _na_bias_table(rpb):
    n_heads, n_dr, n_dc = rpb.shape
    c = np.arange(GRID_W)
    cstart = np.clip(c - NA_WIN_W // 2, 0, GRID_W - NA_WIN_W)
    kc = np.arange(GRID_W)
    valid = (kc[None, :] >= cstart[:, None]) & (kc[None, :] < cstart[:, None] + NA_WIN_W)
    col_idx = np.clip(kc[None, :] - c[:, None] + NA_WIN_W - 1, 0, n_dc - 1)
    pick = np.zeros((n_dc, GRID_W * GRID_W), np.float32)
    pick[col_idx.reshape(-1), np.arange(GRID_W * GRID_W)] = 1.0
    t = jnp.dot(rpb.reshape(n_heads * n_dr, n_dc).astype(F32), jnp.asarray(pick),
                precision=lax.Precision.HIGHEST).reshape(n_heads, n_dr, GRID_W, GRID_W)
    t = jnp.where(jnp.asarray(valid)[None, None], t * LOG2E, NEG_INF)
    t2 = jnp.concatenate([t[:, :-1], t[:, 1:]], axis=-1)
    return t2.transpose(1, 0, 2, 3).reshape(n_dr - 1, n_heads // 2, 2 * GRID_W, 2 * GRID_W)


def _na_attention(q, k, v, kc, vc, bias, rows_per_step=16):
    b, s, w = q.shape
    n_rows = s // GRID_W
    ctx_len = kc.shape[1]
    blk = rows_per_step * GRID_W
    return pl.pallas_call(
        functools.partial(_na_kernel, rows_per_step, n_rows),
        grid=(b, n_rows // rows_per_step),
        in_specs=[pl.BlockSpec((1, blk, w), lambda bi, j: (bi, j, 0)),
                  pl.BlockSpec((1, s, w), lambda bi, j: (bi, 0, 0)),
                  pl.BlockSpec((1, s, w), lambda bi, j: (bi, 0, 0)),
                  pl.BlockSpec((1, ctx_len, w), lambda bi, j: (bi, 0, 0)),
                  pl.BlockSpec((1, ctx_len, w), lambda bi, j: (bi, 0, 0)),
                  pl.BlockSpec(bias.shape, lambda bi, j: (0, 0, 0, 0), pipeline_mode=pl.Buffered(1))],
        out_specs=pl.BlockSpec((1, blk, w), lambda bi, j: (bi, j, 0)),
        out_shape=jax.ShapeDtypeStruct((b, s, w), BF16),
        scratch_shapes=[pltpu.VMEM((2, NA_HEADS * GRID_W, NA_WIN_H * GRID_W), F32),
                        pltpu.VMEM((2, NA_HEADS * GRID_W, ctx_len), F32)],
        compiler_params=_params(("arbitrary", "arbitrary")),
        name="na_attn",
    )(q, k, v, kc, vc, bias)


def _gla_kernel(n_groups, gq_ref, gk_ref, gv_ref, lr_ref, sog_ref, ck_ref, cv_ref, clr_ref,
                wa_ref, ba_ref, gn_ref, o_ref, st_scr, of_scr, qe_scr, kd_scr, ckd_scr, dec_scr, cdec_scr,
                qk_a):
    c_len = GLA_CHUNK
    grp = GLA_GROUP
    cpg = grp // c_len
    row = lax.broadcasted_iota(jnp.int32, (grp, grp), 0)
    col = lax.broadcasted_iota(jnp.int32, (grp, grp), 1)
    same_chunk = (row // c_len) == (col // c_len)
    masks = (same_chunk & (col <= row), same_chunk & (col >= row))

    dirs = range(2)

    def gates(lr_rows, k_rows):
        zs = [jnp.dot(lr_rows, wa_ref[t], preferred_element_type=F32) + ba_ref[t:t + 1, :] for t in dirs]
        gs = [(jnp.minimum(z, 0.0) * LOG2E - jnp.log2(1.0 + jnp.exp2(jnp.abs(z) * (-LOG2E)))) * (1.0 / GLA_GATE_NORM)
              for z in zs]
        g1s = [g.astype(BF16) for g in gs]
        g2s = [(g - g1.astype(F32)).astype(BF16) for g, g1 in zip(gs, g1s)]
        tris = [masks[t].astype(BF16) for t in dirs]
        bs = [jnp.dot(tris[t], g1s[t], preferred_element_type=F32)
              + jnp.dot(tris[t], g2s[t], preferred_element_type=F32) for t in dirs]
        out = []
        for t in dirs:
            edge = c_len - 1 if t == 0 else 0
            decs = [jnp.exp2(bs[t][j * c_len + edge:j * c_len + edge + 1, :]) for j in range(cpg)]
            ke = k_rows * jnp.exp2(-bs[t])
            kd = (ke * jnp.concatenate([jnp.broadcast_to(v, (c_len, v.shape[1])) for v in decs],
                                       axis=0)).astype(BF16)
            dec = jnp.concatenate(decs + [jnp.ones_like(v) for v in decs], axis=0)
            out.append((bs[t], ke.astype(BF16), kd, dec))
        return out

    for t, (_, _, kd, dec) in enumerate(gates(clr_ref[0], ck_ref[0].astype(F32))):
        ckd_scr[t] = kd
        cdec_scr[t] = dec

    def group_gates(gi, qk_buf):
        rows = pl.ds(pl.multiple_of(gi * grp, grp), grp)
        qf = gq_ref[0, rows, :].astype(F32)
        for t, (b, ke, kd, dec) in enumerate(gates(lr_ref[0, rows, :], gk_ref[0, rows, :].astype(F32))):
            qe = (qf * jnp.exp2(b)).astype(BF16)
            qk_buf[t] = qe
            qk_buf[2 + t] = ke
            qe_scr[t, rows, :] = qe
            kd_scr[t, rows, :] = kd
            dec_scr[t, gi] = dec

    def group_attend(gi, qk_buf):
        rows = pl.ds(pl.multiple_of(gi * grp, grp), grp)
        heads = range(GLA_HEADS)
        kls = [slice(h * GLA_HK, (h + 1) * GLA_HK) for h in heads]
        raw = [[lax.dot_general(qk_buf[t, :, kls[h]], qk_buf[2 + t, :, kls[h]], _NT, preferred_element_type=F32)
                for t in dirs] for h in heads]
        atts = [(jnp.where(masks[0], raw[h][0], 0.0) + jnp.where(masks[1], raw[h][1], 0.0)).astype(BF16)
                for h in heads]
        for h in heads:
            vl = slice(h * GLA_HV, (h + 1) * GLA_HV)
            of_scr[rows, vl] = jnp.dot(atts[h], gv_ref[0, rows, vl], preferred_element_type=F32)

    def group_body(gi, carry):
        group_gates(gi, qk_a)
        group_attend(gi, qk_a)
        return carry

    lax.fori_loop(0, n_groups, group_body, 0)

    def state_step(h, v_rows, kd_rows, dec_row):
        kl = slice(h * GLA_HK, (h + 1) * GLA_HK)
        upd = lax.dot_general(v_rows, kd_rows[:, kl], _TN, preferred_element_type=F32)
        st_scr[h] = st_scr[h] * dec_row[:, kl] + upd

    def scan(direction):
        st_scr[...] = jnp.zeros_like(st_scr)
        order = range(cpg) if direction == 0 else range(cpg - 1, -1, -1)
        for j in order:
            rows = slice(j * c_len, (j + 1) * c_len)
            for h in range(GLA_HEADS):
                state_step(h, cv_ref[0, rows, h * GLA_HV:(h + 1) * GLA_HV],
                           ckd_scr[direction, rows, :], cdec_scr[direction, j:j + 1, :])

        def group_scan(i, carry):
            gi = i if direction == 0 else n_groups - 1 - i
            for j in order:
                rows = pl.ds(pl.multiple_of(gi * grp + j * c_len, c_len), c_len)
                qe_rows = qe_scr[direction, rows, :]
                kd_rows = kd_scr[direction, rows, :]
                dec_row = dec_scr[direction, gi, j:j + 1, :]
                for h in range(GLA_HEADS):
                    kl = slice(h * GLA_HK, (h + 1) * GLA_HK)
                    vl = slice(h * GLA_HV, (h + 1) * GLA_HV)
                    o_t = of_scr[rows, vl] + lax.dot_general(qe_rows[:, kl], st_scr[h].astype(BF16), _NT,
                                                             preferred_element_type=F32)
                    if direction == 0:
                        of_scr[rows, vl] = o_t
                    else:
                        ms = jnp.mean(o_t * o_t, axis=-1, keepdims=True)
                        o_n = o_t * lax.rsqrt(ms + EPS) * gn_ref[:, vl] * sog_ref[0, rows, vl].astype(F32)
                        o_ref[0, rows, vl] = o_n.astype(o_ref.dtype)
                    state_step(h, gv_ref[0, rows, vl], kd_rows, dec_row)
            return carry

        lax.fori_loop(0, n_groups, group_scan, 0)

    scan(0)
    scan(1)


def _gla(gq, gk, gv, lr, sog, ck, cv, clr, wa, ba, gn):
    b, s, _ = gq.shape
    ctx_len = ck.shape[1]
    assert ctx_len == GLA_GROUP and s % GLA_GROUP == 0
    n_groups = s // GLA_GROUP
    cpg = GLA_GROUP // GLA_CHUNK

    def per_batch(t, w):
        return pl.BlockSpec((1, t, w), lambda bi: (bi, 0, 0))

    return pl.pallas_call(
        functools.partial(_gla_kernel, n_groups),
        grid=(b,),
        in_specs=[per_batch(s, GLA_DK), per_batch(s, GLA_DK), per_batch(s, GLA_DV),
                  per_batch(s, LR_PAD), per_batch(s, GLA_DV),
                  per_batch(ctx_len, GLA_DK), per_batch(ctx_len, GLA_DV), per_batch(ctx_len, LR_PAD),
                  pl.BlockSpec(wa.shape, lambda bi: (0, 0, 0)),
                  pl.BlockSpec(ba.shape, lambda bi: (0, 0)),
                  pl.BlockSpec(gn.shape, lambda bi: (0, 0))],
        out_specs=per_batch(s, GLA_DV),
        out_shape=jax.ShapeDtypeStruct((b, s, GLA_DV), BF16),
        scratch_shapes=[pltpu.VMEM((GLA_HEADS, GLA_HV, GLA_HK), F32),
                        pltpu.VMEM((s, GLA_DV), F32),
                        pltpu.VMEM((2, s, GLA_DK), BF16),
                        pltpu.VMEM((2, s, GLA_DK), BF16),
                        pltpu.VMEM((2, ctx_len, GLA_DK), BF16),
                        pltpu.VMEM((2, n_groups, 2 * cpg, GLA_DK), F32),
                        pltpu.VMEM((2, 2 * cpg, GLA_DK), F32),
                        pltpu.VMEM((4, GLA_GROUP, GLA_DK), BF16)],
        compiler_params=_params(("arbitrary",), vmem=60 * 1024 * 1024),
        name="gla",
    )(gq, gk, gv, lr, sog, ck, cv, clr, wa, ba, gn)


def _merge_kernel(ona_ref, ogla_ref, sgn_ref, sgg_ref, x_ref, mod_ref, gffn_ref,
                  wna_ref, wgla_ref, wout_ref, wr_ref, br_ref,
                  h_ref, f_ref, route_ref, route_t_ref, cnt_ref):
    rows = x_ref.shape[0]
    lane = lax.broadcasted_iota(jnp.int32, (rows, ROUTE_LANES), 1)
    big = jnp.int32(ROUTE_LANES)
    r_i = lax.broadcasted_iota(jnp.int32, (rows, rows), 0)
    c_i = lax.broadcasted_iota(jnp.int32, (rows, rows), 1)
    before = (c_i < r_i).astype(BF16)
    is_grp = lane < N_GROUPS

    def first_argmax(vals):
        m = jnp.max(vals, axis=-1, keepdims=True)
        idx = jnp.min(jnp.where(vals == m, lane, big), axis=-1, keepdims=True)
        return m, idx

    for rs in (slice(0, rows),):
        y1 = (sgn_ref[rs, :].astype(F32) * jnp.dot(ona_ref[rs, :], wna_ref[...], preferred_element_type=F32)
              + sgg_ref[rs, :].astype(F32) * jnp.dot(ogla_ref[rs, :], wgla_ref[...], preferred_element_type=F32))
        y = jnp.dot(y1.astype(BF16), wout_ref[...], preferred_element_type=F32)
        h = x_ref[rs, :] + mod_ref[0, 2:3, :] * y
        h_ref[rs, :] = h
        ms = jnp.mean(h * h, axis=-1, keepdims=True)
        f = h * lax.rsqrt(ms + EPS) * gffn_ref[...] * (1.0 + mod_ref[0, 4:5, :]) + mod_ref[0, 3:4, :]
        f_ref[rs, :] = f.astype(f_ref.dtype)

        f_hi = f.astype(BF16)
        f_lo = (f - f_hi.astype(F32)).astype(BF16)
        both = jnp.dot(f_hi, wr_ref[...], preferred_element_type=F32)
        logits = (both[:, :ROUTE_LANES] + both[:, ROUTE_LANES:]
                  + jnp.dot(f_lo, wr_ref[:, :ROUTE_LANES], preferred_element_type=F32)) + br_ref[...]

        lg = jnp.where(is_grp, logits, -jnp.inf)
        mg, grp = first_argmax(lg)
        p_grp = 1.0 / jnp.sum(jnp.where(is_grp, jnp.exp(lg - mg), 0.0), axis=-1, keepdims=True)
        lo = EXPERT_LANE0 + grp * EXPERTS_PER_GROUP
        in_grp = (lane >= lo) & (lane < lo + EXPERTS_PER_GROUP)
        le = jnp.where(in_grp, logits, -jnp.inf)
        m1, i1 = first_argmax(le)
        m2, i2 = first_argmax(jnp.where(lane == i1, -jnp.inf, le))
        t = jnp.exp(m2 - m1)
        w1 = p_grp / (1.0 + t)
        w2 = p_grp * t / (1.0 + t)

        hot1 = lane == i1
        hot2 = lane == i2
        onehot = (hot1 | hot2).astype(BF16)
        rank = jnp.dot(before, onehot, preferred_element_type=F32)
        cnt = jnp.sum(onehot.astype(F32), axis=0, keepdims=True)
        cnt_ref[0] = cnt
        cnt_al = jnp.floor((cnt + (SEG_ALIGN - 1.0)) * (1.0 / SEG_ALIGN)) * SEG_ALIGN
        l_i = lax.broadcasted_iota(jnp.int32, (ROUTE_LANES, ROUTE_LANES), 0)
        l_j = lax.broadcasted_iota(jnp.int32, (ROUTE_LANES, ROUTE_LANES), 1)
        lstart = jnp.dot(jnp.broadcast_to(cnt_al, (8, ROUTE_LANES)).astype(BF16), (l_i < l_j).astype(BF16),
                         preferred_element_type=F32)[0:1, :]
        pos = rank + lstart
        pos1 = jnp.sum(jnp.where(hot1, pos, 0.0), axis=-1, keepdims=True)
        pos2 = jnp.sum(jnp.where(hot2, pos, 0.0), axis=-1, keepdims=True)

        e1 = (i1 - EXPERT_LANE0).astype(F32)
        e2 = (i2 - EXPERT_LANE0).astype(F32)
        route = jnp.where(lane == 0, e1, 0.0)
        route = jnp.where(lane == 1, e2, route)
        route = jnp.where(lane == 2, w1, route)
        route = jnp.where(lane == 3, w2, route)
        route = jnp.where(lane == 4, pos1, route)
        route = jnp.where(lane == 5, pos2, route)
        route_ref[rs, :] = route
        route_t_ref[0] = route.T[0:8, :]


def _merge_route(ona, ogla, sgn, sgg, x2d, mod3, gffn, wna, wgla, wout, wr, br, tokens_per_batch, tm):
    n, d = x2d.shape
    per_b = tokens_per_batch // tm

    def tile(w):
        return pl.BlockSpec((tm, w), lambda i: (i, 0))

    def whole(a):
        return pl.BlockSpec(a.shape, lambda i: (0,) * a.ndim)

    return pl.pallas_call(
        _merge_kernel,
        grid=(n // tm,),
        in_specs=[tile(NA_WIDTH), tile(GLA_DV), tile(d), tile(d), tile(d),
                  pl.BlockSpec((1, 6, d), lambda i: (i // per_b, 0, 0)),
                  whole(gffn), whole(wna), whole(wgla), whole(wout), whole(wr), whole(br)],
        out_specs=[tile(d), tile(d), tile(ROUTE_LANES),
                   pl.BlockSpec((1, 8, tm), lambda i: (i, 0, 0)),
                   pl.BlockSpec((1, 1, ROUTE_LANES), lambda i: (i, 0, 0))],
        out_shape=[jax.ShapeDtypeStruct((n, d), F32),
                   jax.ShapeDtypeStruct((n, d), BF16),
                   jax.ShapeDtypeStruct((n, ROUTE_LANES), F32),
                   jax.ShapeDtypeStruct((n // tm, 8, tm), F32),
                   jax.ShapeDtypeStruct((n // tm, 1, ROUTE_LANES), F32)],
        compiler_params=_params(("arbitrary",)),
        name="merge_route",
    )(ona, ogla, sgn, sgg, x2d, mod3, gffn, wna, wgla, wout, wr, br)


HI_MASK = -65536


def _pack_rows(a):
    half = a.shape[1] // 2
    hi = lax.bitcast_convert_type(a[:, :half], jnp.int32)
    lo = lax.shift_right_logical(lax.bitcast_convert_type(a[:, half:], jnp.int32), 16)
    return hi | lo


def _unpack_rows(w):
    hi = lax.bitcast_convert_type(w & HI_MASK, F32).astype(BF16)
    lo = lax.bitcast_convert_type(lax.shift_left(w, 16), F32).astype(BF16)
    return hi, lo


def _segment_copies(seg_ref, make_copy, action):
    if action == "wait":
        total = seg_ref[0, 0, 3 * N_EXPERTS]
        for size in WAIT_SIZES:
            @pl.when((total & size) != 0)
            def _():
                make_copy(0, 0, size).wait()
        return

    def body(e, carry):
        cnt = seg_ref[0, 0, e]
        local = seg_ref[0, 0, N_EXPERTS + e]
        glob = seg_ref[0, 0, 2 * N_EXPERTS + e]
        off = jnp.int32(0)
        for size in SEG_SMALL:
            @pl.when((cnt & size) != 0)
            def _():
                make_copy(pl.multiple_of(local + off, SEG_ALIGN), pl.multiple_of(glob + off, SEG_ALIGN), size).start()
            off = off + (cnt & size)

        def big(k, c):
            o = off + k * SEG_BIG
            make_copy(pl.multiple_of(local + o, SEG_ALIGN), pl.multiple_of(glob + o, SEG_ALIGN), SEG_BIG).start()
            return c

        lax.fori_loop(0, cnt // SEG_BIG, big, 0)
        return carry

    lax.fori_loop(0, N_EXPERTS, body, 0)


def _dispatch_kernel(seg_ref, seg_prev_ref, tail_ref, lpos_ref, f_ref, xs_ref, sbuf, zbuf, sems):
    i = pl.program_id(0)
    last = pl.num_programs(0) - 1
    tm = f_ref.shape[0]
    rows_l = sbuf.shape[1]
    slot = i % 2
    lp = lpos_ref[0].astype(jnp.int32)
    row = lax.broadcasted_iota(jnp.int32, (rows_l, tm), 0)
    perm = ((row == lp[4:5, :]) | (row == lp[5:6, :])).astype(BF16)
    sbuf[slot] = _pack_rows(jnp.dot(perm, f_ref[...], preferred_element_type=F32))

    def copier(s):
        def copy(local, glob, size):
            return pltpu.make_async_copy(sbuf.at[s, pl.ds(local, size), :], xs_ref.at[pl.ds(glob, size), :],
                                         sems.at[s])
        return copy

    _segment_copies(seg_ref, copier(slot), "start")

    @pl.when(i > 0)
    def _():
        _segment_copies(seg_prev_ref, copier(1 - slot), "wait")

    @pl.when(i == last)
    def _():
        _segment_copies(seg_ref, copier(slot), "wait")
        sem = sems.at[slot]
        zbuf[...] = jnp.zeros_like(zbuf)

        def tail(action):
            def body(e, carry):
                start = tail_ref[0, 0, e]
                length = tail_ref[0, 0, N_EXPERTS + e]
                off = jnp.int32(0)
                for size in TAIL_SIZES:
                    @pl.when((length & size) != 0)
                    def _():
                        cp = pltpu.make_async_copy(zbuf.at[pl.ds(0, size), :],
                                                   xs_ref.at[pl.ds(pl.multiple_of(start + off, SEG_ALIGN), size), :],
                                                   sem)
                        getattr(cp, action)()
                    off = off + (length & size)
                return carry
            lax.fori_loop(0, N_EXPERTS, body, 0)

        tail("start")
        tail("wait")

        def unused(action):
            def body(blk, carry):
                cp = pltpu.make_async_copy(zbuf, xs_ref.at[pl.ds(pl.multiple_of(blk * MOE_ROWS, MOE_ROWS),
                                                                  MOE_ROWS), :], sem)
                getattr(cp, action)()
                return carry
            lax.fori_loop(tail_ref[0, 0, 2 * N_EXPERTS], xs_ref.shape[0] // MOE_ROWS, body, 0)

        unused("start")
        unused("wait")


def _dispatch(seg, tail, lpos_rows, f2d, n_pad, rows_l, tm):
    n, d = f2d.shape
    return pl.pallas_call(
        _dispatch_kernel,
        grid=(n // tm,),
        in_specs=[pl.BlockSpec((1, 1, seg.shape[2]), lambda i: (i, 0, 0), memory_space=pltpu.SMEM),
                  pl.BlockSpec((1, 1, seg.shape[2]), lambda i: (jnp.maximum(i - 1, 0), 0, 0),
                               memory_space=pltpu.SMEM),
                  pl.BlockSpec(tail.shape, lambda i: (0, 0, 0), memory_space=pltpu.SMEM),
                  pl.BlockSpec((1, 8, tm), lambda i: (i, 0, 0)),
                  pl.BlockSpec((tm, d), lambda i: (i, 0))],
        out_specs=pl.BlockSpec(memory_space=pl.ANY),
        out_shape=jax.ShapeDtypeStruct((n_pad, d // 2), jnp.int32),
        scratch_shapes=[pltpu.VMEM((2, rows_l, d // 2), jnp.int32),
                        pltpu.VMEM((MOE_ROWS, d // 2), jnp.int32),
                        pltpu.SemaphoreType.DMA((2,))],
        compiler_params=_params(("arbitrary",)),
        name="dispatch",
    )(seg, seg, tail, lpos_rows, f2d)


def _expert_kernel(be_ref, nu_ref, x_ref, wg_ref, wu_ref, wd_ref, y_ref, wgu_scr, wd_scr):
    i = pl.program_id(0)
    prev = be_ref[jnp.maximum(i - 1, 0)]

    @pl.when((i == 0) | (be_ref[i] != prev))
    def _():
        wgu_scr[:, :D_EXPERT] = wg_ref[0].astype(BF16)
        wgu_scr[:, D_EXPERT:] = wu_ref[0].astype(BF16)
        wd_scr[...] = wd_ref[0].astype(BF16)

    @pl.when(i < nu_ref[0])
    def _():
        x = jnp.concatenate(_unpack_rows(x_ref[...]), axis=1)
        gu = jnp.dot(x, wgu_scr[...], preferred_element_type=F32)
        gate = gu[:, :D_EXPERT]
        hdn = gate * jax.nn.sigmoid(gate) * gu[:, D_EXPERT:]
        y = jnp.dot(hdn.astype(BF16), wd_scr[...], preferred_element_type=F32)
        y_ref[...] = _pack_rows(y.astype(BF16).astype(F32))

    @pl.when(i >= nu_ref[0])
    def _():
        y_ref[...] = jnp.zeros_like(y_ref)


def _experts(blk_expert, n_used, xs, wg, wu, wd):
    n_pad, half = xs.shape
    d = 2 * half
    n_blk = n_pad // MOE_ROWS

    def used_block(i, be, nu):
        return (jnp.maximum(jnp.minimum(i, nu[0] - 1), 0), 0)

    grid_spec = pltpu.PrefetchScalarGridSpec(
        num_scalar_prefetch=2,
        grid=(n_blk,),
        in_specs=[pl.BlockSpec((MOE_ROWS, half), used_block),
                  pl.BlockSpec((1, d, D_EXPERT), lambda i, be, nu: (be[i], 0, 0)),
                  pl.BlockSpec((1, d, D_EXPERT), lambda i, be, nu: (be[i], 0, 0)),
                  pl.BlockSpec((1, D_EXPERT, d), lambda i, be, nu: (be[i], 0, 0))],
        out_specs=pl.BlockSpec((MOE_ROWS, half), lambda i, be, nu: (i, 0)),
        scratch_shapes=[pltpu.VMEM((d, 2 * D_EXPERT), BF16),
                        pltpu.VMEM((D_EXPERT, d), BF16)],
    )
    return pl.pallas_call(
        _expert_kernel,
        grid_spec=grid_spec,
        out_shape=jax.ShapeDtypeStruct((n_pad, half), jnp.int32),
        compiler_params=_params(("arbitrary",)),
        name="experts",
    )(blk_expert, n_used, xs, wg, wu, wd)


def _combine_kernel(seg_ref, seg_next_ref, col_ref, h_ref, mod_ref, fg_ref, ys_ref, o_ref, ybuf, sems):
    i = pl.program_id(0)
    tm = h_ref.shape[0]
    rows_l = ybuf.shape[1]
    slot = i % 2

    def copier(s):
        def copy(local, glob, size):
            return pltpu.make_async_copy(ys_ref.at[pl.ds(glob, size), :], ybuf.at[s, pl.ds(local, size), :],
                                         sems.at[s])
        return copy

    @pl.when(i == 0)
    def _():
        ybuf[...] = jnp.zeros_like(ybuf)
        _segment_copies(seg_ref, copier(slot), "start")

    @pl.when(i + 1 < pl.num_programs(0))
    def _():
        _segment_copies(seg_next_ref, copier(1 - slot), "start")

    _segment_copies(seg_ref, copier(slot), "wait")
    y_hi, y_lo = _unpack_rows(ybuf[slot])
    y_all = jnp.concatenate([y_hi, y_lo], axis=1)
    col = lax.broadcasted_iota(jnp.int32, (tm, rows_l), 1)
    info = col_ref[...]
    y1 = jnp.dot((col == info[:, 4:5].astype(jnp.int32)).astype(BF16), y_all, preferred_element_type=F32)
    y2 = jnp.dot((col == info[:, 5:6].astype(jnp.int32)).astype(BF16), y_all, preferred_element_type=F32)
    moe = info[:, 2:3] * y1 + info[:, 3:4] * y2
    h = h_ref[...] + mod_ref[0, 5:6, :] * moe
    ms = jnp.mean(h * h, axis=-1, keepdims=True)
    o_ref[...] = h * lax.rsqrt(ms + EPS) * fg_ref[...]


def _combine(seg, colinfo, h2d, mod3, fg, ys, rows_l, tokens_per_batch, tm):
    n, d = h2d.shape
    per_b = tokens_per_batch // tm
    return pl.pallas_call(
        _combine_kernel,
        grid=(n // tm,),
        in_specs=[pl.BlockSpec((1, 1, seg.shape[2]), lambda i: (i, 0, 0), memory_space=pltpu.SMEM),
                  pl.BlockSpec((1, 1, seg.shape[2]), lambda i: (jnp.minimum(i + 1, n // tm - 1), 0, 0),
                               memory_space=pltpu.SMEM),
                  pl.BlockSpec((tm, colinfo.shape[1]), lambda i: (i, 0)),
                  pl.BlockSpec((tm, d), lambda i: (i, 0)),
                  pl.BlockSpec((1, 6, d), lambda i: (i // per_b, 0, 0)),
                  pl.BlockSpec((1, d), lambda i: (0, 0)),
                  pl.BlockSpec(memory_space=pl.ANY)],
        out_specs=pl.BlockSpec((tm, d), lambda i: (i, 0)),
        out_shape=jax.ShapeDtypeStruct((n, d), F32),
        scratch_shapes=[pltpu.VMEM((2, rows_l, d // 2), jnp.int32), pltpu.SemaphoreType.DMA((2,))],
        compiler_params=_params(("arbitrary",)),
        name="combine",
    )(seg, seg, colinfo, h2d, mod3, fg, ys)


def _rope_tables(s):
    t = np.arange(s)
    pos_r = (t // GRID_W).astype(np.float32)
    pos_c = (t % GRID_W).astype(np.float32)
    nf = GLA_HK // 4
    inv = (np.float32(ROPE_BASE) ** (-np.arange(nf, dtype=np.float32) / np.float32(nf))).astype(np.float32)
    ang_r = pos_r[:, None] * inv
    ang_c = pos_c[:, None] * inv

    def half(ang):
        return (np.concatenate([np.cos(ang), np.cos(ang)], axis=-1),
                np.concatenate([-np.sin(ang), np.sin(ang)], axis=-1))

    cr, sr = half(ang_r)
    cc, sc = half(ang_c)
    cos = np.tile(np.concatenate([cr, cc], axis=-1), (1, GLA_HEADS)).astype(np.float32)
    sin = np.tile(np.concatenate([sr, sc], axis=-1), (1, GLA_HEADS)).astype(np.float32)
    return jnp.asarray(cos), jnp.asarray(sin)


def kernel(x, c, ctx, c_ctx, w_mod, b_mod, norm_attn_g, norm_ffn_g, w_in, w_gla_a2, b_gla_a2, gla_norm_g, na_rpb, w_na_o, w_gla_o, w_out, w_group, b_group, w_expert, b_expert, w_exp_gate, w_exp_up, w_exp_down, final_norm_g):
    b, s, d = x.shape
    ctx_len = ctx.shape[1]
    n = b * s
    assert w_mod.shape[0] == 1, "single-layer block"
    assert s % (GRID_W * NA_WIN_H) == 0 and ctx_len % GLA_CHUNK == 0

    mod_rows = -(-(b + 1) // 8) * 8
    cs = jnp.zeros((mod_rows, d), F32).at[:b].set(c).at[b].set(c_ctx)
    mod3 = _modulation(cs, w_mod[0], b_mod[0]).reshape(mod_rows, 6, d)

    o_q, o_k, o_v = 0, NA_WIDTH, 2 * NA_WIDTH
    o_gq = 3 * NA_WIDTH
    o_gk = o_gq + GLA_DK
    o_gv = o_gk + GLA_DK
    o_og = o_gv + GLA_DV
    o_lr = o_og + GLA_DV
    o_mn = o_lr + 2 * GLA_GATE_RANK
    w_a = w_in[0][:, :o_lr].astype(BF16)
    w_g = w_in[0][:, o_mn:].astype(BF16)
    w_lr = jnp.pad(w_in[0][:, o_lr:o_mn].astype(BF16), ((0, 0), (0, LR_PAD - 2 * GLA_GATE_RANK)))
    cw = 512
    lat_specs = [(0, 0, 0, o_q, cw, "plain"), (1, 0, 0, o_k, cw, "plain"), (2, 0, 0, o_v, cw, "plain"),
                 (3, 0, 0, o_gq, cw, "rope_scaled"), (4, 0, 0, o_gk, cw, "rope"),
                 (5, 0, 0, o_gv, cw, "plain"), (5, cw, 0, o_gv + cw, cw, "plain"),
                 (6, 0, 0, o_og, cw, "silu"), (6, cw, 0, o_og + cw, cw, "silu"),
                 (7, 0, 1, 0, cw, "sigmoid"), (7, cw, 1, cw, cw, "sigmoid"),
                 (8, 0, 1, d, cw, "sigmoid"), (8, cw, 1, d + cw, cw, "sigmoid"),
                 (9, 0, 2, 0, LR_PAD, "plain")]
    lat_widths = [NA_WIDTH, NA_WIDTH, NA_WIDTH, GLA_DK, GLA_DK, GLA_DV, GLA_DV, d, d, LR_PAD]
    tm = 512
    per_b = s // tm
    q_na, k_na, v_na, gq, gk, gv, sog, sgn, sgg, lr = _inproj(
        x.reshape(n, d), mod3, lambda i: i // per_b, norm_attn_g[0], [w_a, w_g, w_lr], lat_specs, lat_widths, tm,
        rope=_rope_tables(s), name="inproj_lat")

    ctx_specs = [(0, 0, 0, o_k, cw, "plain"), (1, 0, 0, o_v, cw, "plain"), (2, 0, 0, o_gk, cw, "plain"),
                 (3, 0, 0, o_gv, cw, "plain"), (3, cw, 0, o_gv + cw, cw, "plain"),
                 (4, 0, 1, 0, LR_PAD, "plain")]
    ctx_widths = [NA_WIDTH, NA_WIDTH, GLA_DK, GLA_DV, LR_PAD]
    kc_na, vc_na, ck, cv, clr = _inproj(
        ctx.reshape(b * ctx_len, d), mod3, lambda i: b, norm_attn_g[0], [w_a, w_lr], ctx_specs, ctx_widths,
        ctx_len, name="inproj_ctx")

    def lat3(a):
        return a.reshape(b, s, a.shape[-1])

    def ctx3(a):
        return a.reshape(b, ctx_len, a.shape[-1])

    o_na = _na_attention(lat3(q_na), lat3(k_na), lat3(v_na), ctx3(kc_na), ctx3(vc_na),
                         _na_bias_table(na_rpb[0]))

    wa = jnp.zeros((2, LR_PAD, GLA_DK), F32)
    wa = wa.at[0, :GLA_GATE_RANK].set(w_gla_a2[0, 0]).at[1, GLA_GATE_RANK:2 * GLA_GATE_RANK].set(w_gla_a2[0, 1])
    gn = jnp.tile(gla_norm_g[0], GLA_HEADS).reshape(1, GLA_DV)
    o_gla = _gla(lat3(gq), lat3(gk), lat3(gv), lat3(lr), lat3(sog), ctx3(ck), ctx3(cv), ctx3(clr),
                 wa.astype(BF16), b_gla_a2[0], gn)

    wr = jnp.zeros((d, ROUTE_LANES), F32)
    wr = wr.at[:, :N_GROUPS].set(w_group[0]).at[:, EXPERT_LANE0:EXPERT_LANE0 + N_EXPERTS].set(w_expert[0])
    wr_hi = wr.astype(BF16)
    br = jnp.zeros((1, ROUTE_LANES), F32)
    br = br.at[0, :N_GROUPS].set(b_group[0]).at[0, EXPERT_LANE0:EXPERT_LANE0 + N_EXPERTS].set(b_expert[0])
    h, f_lat, route, route_t, counts = _merge_route(
        o_na.reshape(n, NA_WIDTH), o_gla.reshape(n, GLA_DV), sgn, sgg, x.reshape(n, d), mod3, norm_ffn_g[0].reshape(1, d),
        w_na_o[0].astype(BF16), w_gla_o[0].astype(BF16), w_out[0].astype(BF16),
        jnp.concatenate([wr_hi, (wr - wr_hi.astype(F32)).astype(BF16)], axis=1), br, s, tm)

    n_tiles = n // tm
    cnt = counts[:, 0, EXPERT_LANE0:EXPERT_LANE0 + N_EXPERTS].astype(jnp.int32)
    cnt_al = (cnt + SEG_ALIGN - 1) // SEG_ALIGN * SEG_ALIGN
    lstart = jnp.cumsum(cnt_al, axis=1) - cnt_al
    total = jnp.sum(cnt_al, axis=0)
    padded = (total + MOE_ROWS - 1) // MOE_ROWS * MOE_ROWS
    pend = jnp.cumsum(padded)
    pstart = pend - padded
    seg_dst = pstart[None, :] + jnp.cumsum(cnt_al, axis=0) - cnt_al
    tile_rows = jnp.sum(cnt_al, axis=1, keepdims=True)
    seg = jnp.concatenate([cnt_al, lstart, seg_dst, tile_rows, jnp.zeros_like(cnt)[:, 1:]], axis=1)
    seg = seg.reshape(n_tiles, 1, 4 * N_EXPERTS)
    n_used = (pend[-1] // MOE_ROWS).reshape(1)
    tail = jnp.concatenate([pstart + total, padded - total, n_used, jnp.zeros((2 * N_EXPERTS - 1,), jnp.int32)])
    tail = tail.reshape(1, 1, 4 * N_EXPERTS)
    seg_pad = n_tiles * N_EXPERTS * (SEG_ALIGN - 1)
    n_pad = (2 * n + seg_pad + N_EXPERTS * (MOE_ROWS - SEG_ALIGN) + MOE_ROWS - 1) // MOE_ROWS * MOE_ROWS
    n_blk = n_pad // MOE_ROWS
    rows_l = (2 * tm + N_EXPERTS * (SEG_ALIGN - 1) + MXU_DIM - 1) // MXU_DIM * MXU_DIM
    blk_start = jnp.arange(n_blk, dtype=jnp.int32) * MOE_ROWS
    blk_expert = jnp.minimum(jnp.sum(blk_start[:, None] >= pend[None, :], axis=-1), N_EXPERTS - 1)

    xs = _dispatch(seg, tail, route_t, f_lat, n_pad, rows_l, tm)
    ys = _experts(blk_expert.astype(jnp.int32), n_used.astype(jnp.int32), xs,
                  w_exp_gate[0], w_exp_up[0], w_exp_down[0])
    out = _combine(seg, route, h, mod3, final_norm_g.reshape(1, d), ys, rows_l, s, tm)
    return out.reshape(b, s, d)
```

```python
import functools

import jax
import jax.numpy as jnp
import numpy as np
from jax import lax
from jax.experimental import pallas as pl
from jax.experimental.pallas import tpu as pltpu

F32 = jnp.float32
BF16 = jnp.bfloat16

D_MODEL = 1024
GRID_W = 64
NA_HEADS = 8
NA_HEAD_DIM = 64
NA_WIDTH = NA_HEADS * NA_HEAD_DIM
NA_WIN_H = 8
NA_WIN_W = 16
GLA_HEADS = 4
GLA_DK = D_MODEL // 2
GLA_DV = D_MODEL
GLA_HK = GLA_DK // GLA_HEADS
GLA_HV = GLA_DV // GLA_HEADS
GLA_GATE_RANK = 16
GLA_GATE_NORM = 16.0
GLA_CHUNK = 64
GLA_GROUP = 256
ROPE_BASE = 10000.0
N_GROUPS = 4
EXPERTS_PER_GROUP = 8
N_EXPERTS = N_GROUPS * EXPERTS_PER_GROUP
D_EXPERT = D_MODEL // 2
EPS = 1e-6
NEG_INF = -1e30
LOG2E = 1.4426950408889634

LANES = 128
MXU_DIM = 256
LR_PAD = LANES
ROUTE_LANES = LANES
EXPERT_LANE0 = N_GROUPS
ROUTE_ROWS = 48
MOE_ROWS = 512
SEG_ALIGN = 8
SEG_SMALL = (8, 16, 32)
SEG_BIG = 64
WAIT_SIZES = (1024, 512, 256, 128, 64, 32, 16, 8)
TAIL_SIZES = (256, 128, 64, 32, 16, 8)
VMEM_LIMIT = 56 * 1024 * 1024

_NT = (((1,), (1,)), ((), ()))
_TN = (((0,), (0,)), ((), ()))


def _params(sem, vmem=VMEM_LIMIT):
    return pltpu.CompilerParams(dimension_semantics=sem, vmem_limit_bytes=vmem)


def _mod_kernel(c_ref, w_ref, b_ref, o_ref):
    c = c_ref[...]
    s = c * jax.nn.sigmoid(c)
    o_ref[...] = jnp.dot(s.astype(BF16), w_ref[...].astype(BF16),
                         preferred_element_type=F32) + b_ref[...]


def _modulation(cs, w_mod, b_mod):
    rows, d = cs.shape
    n = w_mod.shape[1]
    bn = 1024
    return pl.pallas_call(
        _mod_kernel,
        grid=(n // bn,),
        in_specs=[pl.BlockSpec((rows, d), lambda j: (0, 0)),
                  pl.BlockSpec((d, bn), lambda j: (0, j)),
                  pl.BlockSpec((1, bn), lambda j: (0, j))],
        out_specs=pl.BlockSpec((rows, bn), lambda j: (0, j)),
        out_shape=jax.ShapeDtypeStruct((rows, n), F32),
        compiler_params=_params(("arbitrary",)),
        name="mod",
    )(cs, w_mod, b_mod.reshape(1, n))


def _inproj_kernel(specs, use_rope, n_w, n_out, x_ref, mod_ref, g_ref, *rest):
    if use_rope:
        cos_ref, sin_ref = rest[:2]
        rest = rest[2:]
    w_refs = rest[:n_w]
    outs = rest[n_w:n_w + n_out]
    a_scr = rest[n_w + n_out]
    x = x_ref[...]
    ms = jnp.mean(x * x, axis=-1, keepdims=True)
    a = x * lax.rsqrt(ms + EPS) * g_ref[...] * (1.0 + mod_ref[0, 1:2, :]) + mod_ref[0, 0:1, :]
    a_scr[...] = a.astype(BF16)
    for (oi, oc, wi, wc, width, kind) in specs:
        acc = jnp.dot(a_scr[...], w_refs[wi][:, wc:wc + width], preferred_element_type=F32)
        if kind in ("rope", "rope_scaled"):
            if kind == "rope_scaled":
                acc = acc * (GLA_HK ** -0.5)
            lane = lax.broadcasted_iota(jnp.int32, acc.shape, 1)
            rot = jnp.where((lane & 63) < 32,
                            pltpu.roll(acc, width - 32, axis=1),
                            pltpu.roll(acc, 32, axis=1))
            acc = acc * cos_ref[...] + rot * sin_ref[...]
        elif kind == "silu":
            acc = acc * jax.nn.sigmoid(acc)
        elif kind == "sigmoid":
            acc = jax.nn.sigmoid(acc)
        outs[oi][:, oc:oc + width] = acc.astype(outs[oi].dtype)


def _inproj(x2d, mod3, mod_row_fn, g, weights, specs, out_widths, tm, rope=None, name="inproj"):
    n, d = x2d.shape
    use_rope = rope is not None
    in_specs = [pl.BlockSpec((tm, d), lambda i: (i, 0)),
                pl.BlockSpec((1, 6, d), lambda i: (mod_row_fn(i), 0, 0)),
                pl.BlockSpec((1, d), lambda i: (0, 0))]
    args = [x2d, mod3, g.reshape(1, d)]
    if use_rope:
        cos, sin = rope
        nb = cos.shape[0] // tm
        in_specs += [pl.BlockSpec((tm, cos.shape[1]), lambda i: (i % nb, 0)),
                     pl.BlockSpec((tm, cos.shape[1]), lambda i: (i % nb, 0))]
        args += [cos, sin]
    in_specs += [pl.BlockSpec(w.shape, lambda i: (0, 0), pipeline_mode=pl.Buffered(1)) for w in weights]
    args += list(weights)
    out_specs = [pl.BlockSpec((tm, w), lambda i: (i, 0)) for w in out_widths]
    out_shape = [jax.ShapeDtypeStruct((n, w), BF16) for w in out_widths]
    return pl.pallas_call(
        functools.partial(_inproj_kernel, specs, use_rope, len(weights), len(out_widths)),
        grid=(n // tm,),
        in_specs=in_specs,
        out_specs=out_specs,
        out_shape=out_shape,
        scratch_shapes=[pltpu.VMEM((tm, d), BF16)],
        compiler_params=_params(("arbitrary",)),
        name=name,
    )(*args)


def _na_kernel(rows_per_step, n_rows, q_ref, k_ref, v_ref, kc_ref, vc_ref, bias_ref, o_ref, sw_scr, sc_scr):
    j = pl.program_id(1)
    lane = lax.broadcasted_iota(jnp.int32, (GRID_W, LANES), 1)
    first_head = lane < NA_HEAD_DIM
    scale = NA_HEAD_DIM ** -0.5
    win_keys = NA_WIN_H * GRID_W
    pairs = range(NA_HEADS // 2)
    lanes = [slice(p * LANES, (p + 1) * LANES) for p in pairs]
    stack = 2 * GRID_W

    def window(rr):
        r = j * rows_per_step + rr
        rs = jnp.clip(r - NA_WIN_H // 2, 0, n_rows - NA_WIN_H)
        return r - rs, pl.multiple_of(rr * GRID_W, GRID_W), pl.multiple_of(rs * GRID_W, GRID_W)

    def scores(rr, slot):
        _, t0, ks = window(rr)
        for p in pairs:
            qp = q_ref[0, pl.ds(t0, GRID_W), lanes[p]] * scale
            zero = jnp.zeros_like(qp)
            qs = jnp.concatenate([jnp.where(first_head, qp, zero), jnp.where(first_head, zero, qp)], axis=0)
            sw_scr[slot, p * stack:(p + 1) * stack, :] = lax.dot_general(
                qs, k_ref[0, pl.ds(ks, win_keys), lanes[p]], _NT, preferred_element_type=F32)
            sc_scr[slot, p * stack:(p + 1) * stack, :] = lax.dot_general(
                qs, kc_ref[0, :, lanes[p]], _NT, preferred_element_type=F32)

    def attend(rr, slot):
        d, t0, ks = window(rr)
        pws, pcs, dens = [], [], []
        for p in pairs:
            bias = jnp.concatenate([bias_ref[(NA_WIN_H - 1) - d + 2 * i, p] for i in range(NA_WIN_H // 2)], axis=1)
            sw = sw_scr[slot, p * stack:(p + 1) * stack, :] * LOG2E + bias
            sc = sc_scr[slot, p * stack:(p + 1) * stack, :] * LOG2E
            m = jnp.maximum(jnp.max(sw, axis=-1, keepdims=True), jnp.max(sc, axis=-1, keepdims=True))
            pw = jnp.exp2(sw - m)
            pc = jnp.exp2(sc - m)
            dens.append(jnp.sum(pw, axis=-1, keepdims=True) + jnp.sum(pc, axis=-1, keepdims=True))
            pws.append(pw.astype(BF16))
            pcs.append(pc.astype(BF16))
        for p in pairs:
            o2 = (jnp.dot(pws[p], v_ref[0, pl.ds(ks, win_keys), lanes[p]], preferred_element_type=F32)
                  + jnp.dot(pcs[p], vc_ref[0, :, lanes[p]], preferred_element_type=F32)) / dens[p]
            o_ref[0, pl.ds(t0, GRID_W), lanes[p]] = jnp.where(first_head, o2[:GRID_W],
                                                             o2[GRID_W:]).astype(o_ref.dtype)

    scores(0, 0)

    def two_rows(k, carry):
        scores(2 * k + 1, 1)
        attend(2 * k, 0)
        scores(2 * k + 2, 0)
        attend(2 * k + 1, 1)
        return carry

    lax.fori_loop(0, rows_per_step // 2 - 1, two_rows, 0)
    scores(rows_per_step - 1, 1)
    attend(rows_per_step - 2, 0)
    attend(rows_per_step - 1, 1)


def _na_bias_table(rpb):
    n_heads, n_dr, n_dc = rpb.shape
    c = np.arange(GRID_W)
    cstart = np.clip(c - NA_WIN_W // 2, 0, GRID_W - NA_WIN_W)
    kc = np.arange(GRID_W)
    valid = (kc[None, :] >= cstart[:, None]) & (kc[None, :] < cstart[:, None] + NA_WIN_W)
    col_idx = np.clip(kc[None, :] - c[:, None] + NA_WIN_W - 1, 0, n_dc - 1)
    pick = np.zeros((n_dc, GRID_W * GRID_W), np.float32)
    pick[col_idx.reshape(-1), np.arange(GRID_W * GRID_W)] = 1.0
    t = jnp.dot(rpb.reshape(n_heads * n_dr, n_dc).astype(F32), jnp.asarray(pick),
                precision=lax.Precision.HIGHEST).reshape(n_heads, n_dr, GRID_W, GRID_W)
    t = jnp.where(jnp.asarray(valid)[None, None], t * LOG2E, NEG_INF)
    t2 = jnp.concatenate([t[:, :-1], t[:, 1:]], axis=-1)
    return t2.transpose(1, 0, 2, 3).reshape(n_dr - 1, n_heads // 2, 2 * GRID_W, 2 * GRID_W)


def _na_attention(q, k, v, kc, vc, bias, rows_per_step=16):
    b, s, w = q.shape
    n_rows = s // GRID_W
    ctx_len = kc.shape[1]
    blk = rows_per_step * GRID_W
    return pl.pallas_call(
        functools.partial(_na_kernel, rows_per_step, n_rows),
        grid=(b, n_rows // rows_per_step),
        in_specs=[pl.BlockSpec((1, blk, w), lambda bi, j: (bi, j, 0)),
                  pl.BlockSpec((1, s, w), lambda bi, j: (bi, 0, 0)),
                  pl.BlockSpec((1, s, w), lambda bi, j: (bi, 0, 0)),
                  pl.BlockSpec((1, ctx_len, w), lambda bi, j: (bi, 0, 0)),
                  pl.BlockSpec((1, ctx_len, w), lambda bi, j: (bi, 0, 0)),
                  pl.BlockSpec(bias.shape, lambda bi, j: (0, 0, 0, 0), pipeline_mode=pl.Buffered(1))],
        out_specs=pl.BlockSpec((1, blk, w), lambda bi, j: (bi, j, 0)),
        out_shape=jax.ShapeDtypeStruct((b, s, w), BF16),
        scratch_shapes=[pltpu.VMEM((2, NA_HEADS * GRID_W, NA_WIN_H * GRID_W), F32),
                        pltpu.VMEM((2, NA_HEADS * GRID_W, ctx_len), F32)],
        compiler_params=_params(("arbitrary", "arbitrary")),
        name="na_attn",
    )(q, k, v, kc, vc, bias)


def _gla_kernel(n_groups, gq_ref, gk_ref, gv_ref, lr_ref, sog_ref, ck_ref, cv_ref, clr_ref,
                wa_ref, ba_ref, gn_ref, o_ref, st_scr, of_scr, qe_scr, kd_scr, ckd_scr, dec_scr, cdec_scr,
                qk_a):
    c_len = GLA_CHUNK
    grp = GLA_GROUP
    cpg = grp // c_len
    row = lax.broadcasted_iota(jnp.int32, (grp, grp), 0)
    col = lax.broadcasted_iota(jnp.int32, (grp, grp), 1)
    same_chunk = (row // c_len) == (col // c_len)
    masks = (same_chunk & (col <= row), same_chunk & (col >= row))

    dirs = range(2)

    def gates(lr_rows, k_rows):
        zs = [jnp.dot(lr_rows, wa_ref[t], preferred_element_type=F32) + ba_ref[t:t + 1, :] for t in dirs]
        gs = [(jnp.minimum(z, 0.0) * LOG2E - jnp.log2(1.0 + jnp.exp2(jnp.abs(z) * (-LOG2E)))) * (1.0 / GLA_GATE_NORM)
              for z in zs]
        g1s = [g.astype(BF16) for g in gs]
        g2s = [(g - g1.astype(F32)).astype(BF16) for g, g1 in zip(gs, g1s)]
        tris = [masks[t].astype(BF16) for t in dirs]
        bs = [jnp.dot(tris[t], g1s[t], preferred_element_type=F32)
              + jnp.dot(tris[t], g2s[t], preferred_element_type=F32) for t in dirs]
        out = []
        for t in dirs:
            edge = c_len - 1 if t == 0 else 0
            decs = [jnp.exp2(bs[t][j * c_len + edge:j * c_len + edge + 1, :]) for j in range(cpg)]
            ke = k_rows * jnp.exp2(-bs[t])
            kd = (ke * jnp.concatenate([jnp.broadcast_to(v, (c_len, v.shape[1])) for v in decs],
                                       axis=0)).astype(BF16)
            dec = jnp.concatenate(decs + [jnp.ones_like(v) for v in decs], axis=0)
            out.append((bs[t], ke.astype(BF16), kd, dec))
        return out

    for t, (_, _, kd, dec) in enumerate(gates(clr_ref[0], ck_ref[0].astype(F32))):
        ckd_scr[t] = kd
        cdec_scr[t] = dec

    def group_gates(gi, qk_buf):
        rows = pl.ds(pl.multiple_of(gi * grp, grp), grp)
        qf = gq_ref[0, rows, :].astype(F32)
        for t, (b, ke, kd, dec) in enumerate(gates(lr_ref[0, rows, :], gk_ref[0, rows, :].astype(F32))):
            qe = (qf * jnp.exp2(b)).astype(BF16)
            qk_buf[t] = qe
            qk_buf[2 + t] = ke
            qe_scr[t, rows, :] = qe
            kd_scr[t, rows, :] = kd
            dec_scr[t, gi] = dec

    def group_attend(gi, qk_buf):
        rows = pl.ds(pl.multiple_of(gi * grp, grp), grp)
        heads = range(GLA_HEADS)
        kls = [slice(h * GLA_HK, (h + 1) * GLA_HK) for h in heads]
        raw = [[lax.dot_general(qk_buf[t, :, kls[h]], qk_buf[2 + t, :, kls[h]], _NT, preferred_element_type=F32)
                for t in dirs] for h in heads]
        atts = [(jnp.where(masks[0], raw[h][0], 0.0) + jnp.where(masks[1], raw[h][1], 0.0)).astype(BF16)
                for h in heads]
        for h in heads:
            vl = slice(h * GLA_HV, (h + 1) * GLA_HV)
            of_scr[rows, vl] = jnp.dot(atts[h], gv_ref[0, rows, vl], preferred_element_type=F32)

    def group_body(gi, carry):
        group_gates(gi, qk_a)
        group_attend(gi, qk_a)
        return carry

    lax.fori_loop(0, n_groups, group_body, 0)

    def state_step(h, v_rows, kd_rows, dec_row):
        kl = slice(h * GLA_HK, (h + 1) * GLA_HK)
        upd = lax.dot_general(v_rows, kd_rows[:, kl], _TN, preferred_element_type=F32)
        st_scr[h] = st_scr[h] * dec_row[:, kl] + upd

    def scan(direction):
        st_scr[...] = jnp.zeros_like(st_scr)
        order = range(cpg) if direction == 0 else range(cpg - 1, -1, -1)
        for j in order:
            rows = slice(j * c_len, (j + 1) * c_len)
            for h in range(GLA_HEADS):
                state_step(h, cv_ref[0, rows, h * GLA_HV:(h + 1) * GLA_HV],
                           ckd_scr[direction, rows, :], cdec_scr[direction, j:j + 1, :])

        def group_scan(i, carry):
            gi = i if direction == 0 else n_groups - 1 - i
            for j in order:
                rows = pl.ds(pl.multiple_of(gi * grp + j * c_len, c_len), c_len)
                qe_rows = qe_scr[direction, rows, :]
                kd_rows = kd_scr[direction, rows, :]
                dec_row = dec_scr[direction, gi, j:j + 1, :]
                for h in range(GLA_HEADS):
                    kl = slice(h * GLA_HK, (h + 1) * GLA_HK)
                    vl = slice(h * GLA_HV, (h + 1) * GLA_HV)
                    o_t = of_scr[rows, vl] + lax.dot_general(qe_rows[:, kl], st_scr[h].astype(BF16), _NT,
                                                             preferred_element_type=F32)
                    if direction == 0:
                        of_scr[rows, vl] = o_t
                    else:
                        ms = jnp.mean(o_t * o_t, axis=-1, keepdims=True)
                        o_n = o_t * lax.rsqrt(ms + EPS) * gn_ref[:, vl] * sog_ref[0, rows, vl].astype(F32)
                        o_ref[0, rows, vl] = o_n.astype(o_ref.dtype)
                    state_step(h, gv_ref[0, rows, vl], kd_rows, dec_row)
            return carry

        lax.fori_loop(0, n_groups, group_scan, 0)

    scan(0)
    scan(1)


def _gla(gq, gk, gv, lr, sog, ck, cv, clr, wa, ba, gn):
    b, s, _ = gq.shape
    ctx_len = ck.shape[1]
    assert ctx_len == GLA_GROUP and s % GLA_GROUP == 0
    n_groups = s // GLA_GROUP
    cpg = GLA_GROUP // GLA_CHUNK

    def per_batch(t, w):
        return pl.BlockSpec((1, t, w), lambda bi: (bi, 0, 0))

    return pl.pallas_call(
        functools.partial(_gla_kernel, n_groups),
        grid=(b,),
        in_specs=[per_batch(s, GLA_DK), per_batch(s, GLA_DK), per_batch(s, GLA_DV),
                  per_batch(s, LR_PAD), per_batch(s, GLA_DV),
                  per_batch(ctx_len, GLA_DK), per_batch(ctx_len, GLA_DV), per_batch(ctx_len, LR_PAD),
                  pl.BlockSpec(wa.shape, lambda bi: (0, 0, 0)),
                  pl.BlockSpec(ba.shape, lambda bi: (0, 0)),
                  pl.BlockSpec(gn.shape, lambda bi: (0, 0))],
        out_specs=per_batch(s, GLA_DV),
        out_shape=jax.ShapeDtypeStruct((b, s, GLA_DV), BF16),
        scratch_shapes=[pltpu.VMEM((GLA_HEADS, GLA_HV, GLA_HK), F32),
                        pltpu.VMEM((s, GLA_DV), F32),
                        pltpu.VMEM((2, s, GLA_DK), BF16),
                        pltpu.VMEM((2, s, GLA_DK), BF16),
                        pltpu.VMEM((2, ctx_len, GLA_DK), BF16),
                        pltpu.VMEM((2, n_groups, 2 * cpg, GLA_DK), F32),
                        pltpu.VMEM((2, 2 * cpg, GLA_DK), F32),
                        pltpu.VMEM((4, GLA_GROUP, GLA_DK), BF16)],
        compiler_params=_params(("arbitrary",), vmem=60 * 1024 * 1024),
        name="gla",
    )(gq, gk, gv, lr, sog, ck, cv, clr, wa, ba, gn)


def _merge_kernel(ona_ref, ogla_ref, sgn_ref, sgg_ref, x_ref, mod_ref, gffn_ref,
                  wna_ref, wgla_ref, wout_ref, wr_ref, br_ref,
                  h_ref, f_ref, route_ref, route_t_ref, cnt_ref):
    rows = x_ref.shape[0]
    y1 = (sgn_ref[...].astype(F32) * jnp.dot(ona_ref[...], wna_ref[...], preferred_element_type=F32)
          + sgg_ref[...].astype(F32) * jnp.dot(ogla_ref[...], wgla_ref[...], preferred_element_type=F32))
    y = jnp.dot(y1.astype(BF16), wout_ref[...], preferred_element_type=F32)
    h = x_ref[...] + mod_ref[0, 2:3, :] * y
    h_ref[...] = h
    ms = jnp.mean(h * h, axis=-1, keepdims=True)
    f = h * lax.rsqrt(ms + EPS) * gffn_ref[...] * (1.0 + mod_ref[0, 4:5, :]) + mod_ref[0, 3:4, :]
    f_ref[...] = f.astype(f_ref.dtype)

    f_hi = f.astype(BF16)
    f_lo = (f - f_hi.astype(F32)).astype(BF16)
    both = jnp.dot(f_hi, wr_ref[...], preferred_element_type=F32)
    logits = (both[:, :ROUTE_LANES] + both[:, ROUTE_LANES:]
              + jnp.dot(f_lo, wr_ref[:, :ROUTE_LANES], preferred_element_type=F32)) + br_ref[...]

    lt = logits.T[0:ROUTE_ROWS, :]
    er = lax.broadcasted_iota(jnp.int32, (ROUTE_ROWS, rows), 0)
    big = jnp.int32(ROUTE_ROWS)

    def first_argmax(vals):
        m = jnp.max(vals, axis=0, keepdims=True)
        idx = jnp.min(jnp.where(vals == m, er, big), axis=0, keepdims=True)
        return m, idx

    is_grp = er < N_GROUPS
    lg = jnp.where(is_grp, lt, -jnp.inf)
    mg, grp = first_argmax(lg)
    p_grp = 1.0 / jnp.sum(jnp.where(is_grp, jnp.exp(lg - mg), 0.0), axis=0, keepdims=True)
    lo = EXPERT_LANE0 + grp * EXPERTS_PER_GROUP
    in_grp = (er >= lo) & (er < lo + EXPERTS_PER_GROUP)
    le = jnp.where(in_grp, lt, -jnp.inf)
    m1, i1 = first_argmax(le)
    m2, i2 = first_argmax(jnp.where(er == i1, -jnp.inf, le))
    t = jnp.exp(m2 - m1)
    w1 = p_grp / (1.0 + t)
    w2 = p_grp * t / (1.0 + t)

    hot1 = er == i1
    hot2 = er == i2
    onehot = (hot1 | hot2).astype(BF16)
    s_i = lax.broadcasted_iota(jnp.int32, (rows, rows), 0)
    t_i = lax.broadcasted_iota(jnp.int32, (rows, rows), 1)
    rank = jnp.dot(onehot, (s_i < t_i).astype(BF16), preferred_element_type=F32)
    cnt = jnp.sum(onehot.astype(F32), axis=1, keepdims=True)
    cnt_al = jnp.floor((cnt + (SEG_ALIGN - 1.0)) * (1.0 / SEG_ALIGN)) * SEG_ALIGN
    e_i = lax.broadcasted_iota(jnp.int32, (ROUTE_ROWS, ROUTE_ROWS), 0)
    e_j = lax.broadcasted_iota(jnp.int32, (ROUTE_ROWS, ROUTE_ROWS), 1)
    lstart = jnp.dot((e_j < e_i).astype(BF16), jnp.broadcast_to(cnt_al, (ROUTE_ROWS, LANES)).astype(BF16),
                     preferred_element_type=F32)[:, 0:1]
    pos = rank + lstart
    pos1 = jnp.sum(jnp.where(hot1, pos, 0.0), axis=0, keepdims=True)
    pos2 = jnp.sum(jnp.where(hot2, pos, 0.0), axis=0, keepdims=True)

    e1 = (i1 - EXPERT_LANE0).astype(F32)
    e2 = (i2 - EXPERT_LANE0).astype(F32)
    fields = jnp.concatenate([e1, e2, w1, w2, pos1, pos2, jnp.zeros((2, rows), F32)], axis=0)
    route_t_ref[0] = fields
    route_ref[...] = jnp.concatenate([fields, jnp.zeros((ROUTE_LANES - 8, rows), F32)], axis=0).T
    cnt_rows = jnp.concatenate([jnp.broadcast_to(cnt, (ROUTE_ROWS, ROUTE_LANES)),
                                jnp.zeros((ROUTE_LANES - ROUTE_ROWS, ROUTE_LANES), F32)], axis=0)
    cnt_ref[0] = cnt_rows.T[0:1, :]


def _merge_route(ona, ogla, sgn, sgg, x2d, mod3, gffn, wna, wgla, wout, wr, br, tokens_per_batch, tm):
    n, d = x2d.shape
    per_b = tokens_per_batch // tm

    def tile(w):
        return pl.BlockSpec((tm, w), lambda i: (i, 0))

    def whole(a):
        return pl.BlockSpec(a.shape, lambda i: (0,) * a.ndim)

    return pl.pallas_call(
        _merge_kernel,
        grid=(n // tm,),
        in_specs=[tile(NA_WIDTH), tile(GLA_DV), tile(d), tile(d), tile(d),
                  pl.BlockSpec((1, 6, d), lambda i: (i // per_b, 0, 0)),
                  whole(gffn), whole(wna), whole(wgla), whole(wout), whole(wr), whole(br)],
        out_specs=[tile(d), tile(d), tile(ROUTE_LANES),
                   pl.BlockSpec((1, 8, tm), lambda i: (i, 0, 0)),
                   pl.BlockSpec((1, 1, ROUTE_LANES), lambda i: (i, 0, 0))],
        out_shape=[jax.ShapeDtypeStruct((n, d), F32),
                   jax.ShapeDtypeStruct((n, d), BF16),
                   jax.ShapeDtypeStruct((n, ROUTE_LANES), F32),
                   jax.ShapeDtypeStruct((n // tm, 8, tm), F32),
                   jax.ShapeDtypeStruct((n // tm, 1, ROUTE_LANES), F32)],
        compiler_params=_params(("arbitrary",)),
        name="merge_route",
    )(ona, ogla, sgn, sgg, x2d, mod3, gffn, wna, wgla, wout, wr, br)


HI_MASK = -65536


def _pack_rows(a):
    half = a.shape[1] // 2
    hi = lax.bitcast_convert_type(a[:, :half], jnp.int32)
    lo = lax.shift_right_logical(lax.bitcast_convert_type(a[:, half:], jnp.int32), 16)
    return hi | lo


def _unpack_rows(w):
    hi = lax.bitcast_convert_type(w & HI_MASK, F32).astype(BF16)
    lo = lax.bitcast_convert_type(lax.shift_left(w, 16), F32).astype(BF16)
    return hi, lo


def _segment_copies(seg_ref, make_copy, action):
    if action == "wait":
        total = seg_ref[0, 0, 3 * N_EXPERTS]
        for size in WAIT_SIZES:
            @pl.when((total & size) != 0)
            def _():
                make_copy(0, 0, size).wait()
        return

    def body(e, carry):
        cnt = seg_ref[0, 0, e]
        local = seg_ref[0, 0, N_EXPERTS + e]
        glob = seg_ref[0, 0, 2 * N_EXPERTS + e]
        off = jnp.int32(0)
        for size in SEG_SMALL:
            @pl.when((cnt & size) != 0)
            def _():
                make_copy(pl.multiple_of(local + off, SEG_ALIGN), pl.multiple_of(glob + off, SEG_ALIGN), size).start()
            off = off + (cnt & size)

        def big(k, c):
            o = off + k * SEG_BIG
            make_copy(pl.multiple_of(local + o, SEG_ALIGN), pl.multiple_of(glob + o, SEG_ALIGN), SEG_BIG).start()
            return c

        lax.fori_loop(0, cnt // SEG_BIG, big, 0)
        return carry

    lax.fori_loop(0, N_EXPERTS, body, 0)


def _dispatch_kernel(seg_ref, seg_prev_ref, tail_ref, lpos_ref, f_ref, xs_ref, sbuf, zbuf, sems):
    i = pl.program_id(0)
    last = pl.num_programs(0) - 1
    tm = f_ref.shape[0]
    rows_l = sbuf.shape[1]
    slot = i % 2
    lp = lpos_ref[0].astype(jnp.int32)
    row = lax.broadcasted_iota(jnp.int32, (rows_l, tm), 0)
    perm = ((row == lp[4:5, :]) | (row == lp[5:6, :])).astype(BF16)
    sbuf[slot] = _pack_rows(jnp.dot(perm, f_ref[...], preferred_element_type=F32))

    def copier(s):
        def copy(local, glob, size):
            return pltpu.make_async_copy(sbuf.at[s, pl.ds(local, size), :], xs_ref.at[pl.ds(glob, size), :],
                                         sems.at[s])
        return copy

    _segment_copies(seg_ref, copier(slot), "start")

    @pl.when(i > 0)
    def _():
        _segment_copies(seg_prev_ref, copier(1 - slot), "wait")

    @pl.when(i == last)
    def _():
        _segment_copies(seg_ref, copier(slot), "wait")
        sem = sems.at[slot]
        zbuf[...] = jnp.zeros_like(zbuf)

        def tail(action):
            def body(e, carry):
                start = tail_ref[0, 0, e]
                length = tail_ref[0, 0, N_EXPERTS + e]
                off = jnp.int32(0)
                for size in TAIL_SIZES:
                    @pl.when((length & size) != 0)
                    def _():
                        cp = pltpu.make_async_copy(zbuf.at[pl.ds(0, size), :],
                                                   xs_ref.at[pl.ds(pl.multiple_of(start + off, SEG_ALIGN), size), :],
                                                   sem)
                        getattr(cp, action)()
                    off = off + (length & size)
                return carry
            lax.fori_loop(0, N_EXPERTS, body, 0)

        tail("start")
        tail("wait")

        def unused(action):
            def body(blk, carry):
                cp = pltpu.make_async_copy(zbuf, xs_ref.at[pl.ds(pl.multiple_of(blk * MOE_ROWS, MOE_ROWS),
                                                                  MOE_ROWS), :], sem)
                getattr(cp, action)()
                return carry
            lax.fori_loop(tail_ref[0, 0, 2 * N_EXPERTS], xs_ref.shape[0] // MOE_ROWS, body, 0)

        unused("start")
        unused("wait")


def _dispatch(seg, tail, lpos_rows, f2d, n_pad, rows_l, tm):
    n, d = f2d.shape
    return pl.pallas_call(
        _dispatch_kernel,
        grid=(n // tm,),
        in_specs=[pl.BlockSpec((1, 1, seg.shape[2]), lambda i: (i, 0, 0), memory_space=pltpu.SMEM),
                  pl.BlockSpec((1, 1, seg.shape[2]), lambda i: (jnp.maximum(i - 1, 0), 0, 0),
                               memory_space=pltpu.SMEM),
                  pl.BlockSpec(tail.shape, lambda i: (0, 0, 0), memory_space=pltpu.SMEM),
                  pl.BlockSpec((1, 8, tm), lambda i: (i, 0, 0)),
                  pl.BlockSpec((tm, d), lambda i: (i, 0))],
        out_specs=pl.BlockSpec(memory_space=pl.ANY),
        out_shape=jax.ShapeDtypeStruct((n_pad, d // 2), jnp.int32),
        scratch_shapes=[pltpu.VMEM((2, rows_l, d // 2), jnp.int32),
                        pltpu.VMEM((MOE_ROWS, d // 2), jnp.int32),
                        pltpu.SemaphoreType.DMA((2,))],
        compiler_params=_params(("arbitrary",)),
        name="dispatch",
    )(seg, seg, tail, lpos_rows, f2d)


def _expert_kernel(be_ref, nu_ref, x_ref, wg_ref, wu_ref, wd_ref, y_ref, wgu_scr, wd_scr):
    i = pl.program_id(0)
    prev = be_ref[jnp.maximum(i - 1, 0)]

    @pl.when((i == 0) | (be_ref[i] != prev))
    def _():
        wgu_scr[:, :D_EXPERT] = wg_ref[0].astype(BF16)
        wgu_scr[:, D_EXPERT:] = wu_ref[0].astype(BF16)
        wd_scr[...] = wd_ref[0].astype(BF16)

    @pl.when(i < nu_ref[0])
    def _():
        x = jnp.concatenate(_unpack_rows(x_ref[...]), axis=1)
        gu = jnp.dot(x, wgu_scr[...], preferred_element_type=F32)
        gate = gu[:, :D_EXPERT]
        hdn = gate * jax.nn.sigmoid(gate) * gu[:, D_EXPERT:]
        y = jnp.dot(hdn.astype(BF16), wd_scr[...], preferred_element_type=F32)
        y_ref[...] = _pack_rows(y.astype(BF16).astype(F32))

    @pl.when(i >= nu_ref[0])
    def _():
        y_ref[...] = jnp.zeros_like(y_ref)


def _experts(blk_expert, n_used, xs, wg, wu, wd):
    n_pad, half = xs.shape
    d = 2 * half
    n_blk = n_pad // MOE_ROWS

    def used_block(i, be, nu):
        return (jnp.maximum(jnp.minimum(i, nu[0] - 1), 0), 0)

    grid_spec = pltpu.PrefetchScalarGridSpec(
        num_scalar_prefetch=2,
        grid=(n_blk,),
        in_specs=[pl.BlockSpec((MOE_ROWS, half), used_block),
                  pl.BlockSpec((1, d, D_EXPERT), lambda i, be, nu: (be[i], 0, 0)),
                  pl.BlockSpec((1, d, D_EXPERT), lambda i, be, nu: (be[i], 0, 0)),
                  pl.BlockSpec((1, D_EXPERT, d), lambda i, be, nu: (be[i], 0, 0))],
        out_specs=pl.BlockSpec((MOE_ROWS, half), lambda i, be, nu: (i, 0)),
        scratch_shapes=[pltpu.VMEM((d, 2 * D_EXPERT), BF16),
                        pltpu.VMEM((D_EXPERT, d), BF16)],
    )
    return pl.pallas_call(
        _expert_kernel,
        grid_spec=grid_spec,
        out_shape=jax.ShapeDtypeStruct((n_pad, half), jnp.int32),
        compiler_params=_params(("arbitrary",)),
        name="experts",
    )(blk_expert, n_used, xs, wg, wu, wd)


def _combine_kernel(seg_ref, seg_next_ref, col_ref, h_ref, mod_ref, fg_ref, ys_ref, o_ref, ybuf, sems):
    i = pl.program_id(0)
    tm = h_ref.shape[0]
    rows_l = ybuf.shape[1]
    slot = i % 2

    def copier(s):
        def copy(local, glob, size):
            return pltpu.make_async_copy(ys_ref.at[pl.ds(glob, size), :], ybuf.at[s, pl.ds(local, size), :],
                                         sems.at[s])
        return copy

    @pl.when(i == 0)
    def _():
        ybuf[...] = jnp.zeros_like(ybuf)
        _segment_copies(seg_ref, copier(slot), "start")

    @pl.when(i + 1 < pl.num_programs(0))
    def _():
        _segment_copies(seg_next_ref, copier(1 - slot), "start")

    _segment_copies(seg_ref, copier(slot), "wait")
    y_hi, y_lo = _unpack_rows(ybuf[slot])
    y_all = jnp.concatenate([y_hi, y_lo], axis=1)
    col = lax.broadcasted_iota(jnp.int32, (tm, rows_l), 1)
    info = col_ref[...]
    y1 = jnp.dot((col == info[:, 4:5].astype(jnp.int32)).astype(BF16), y_all, preferred_element_type=F32)
    y2 = jnp.dot((col == info[:, 5:6].astype(jnp.int32)).astype(BF16), y_all, preferred_element_type=F32)
    moe = info[:, 2:3] * y1 + info[:, 3:4] * y2
    h = h_ref[...] + mod_ref[0, 5:6, :] * moe
    ms = jnp.mean(h * h, axis=-1, keepdims=True)
    o_ref[...] = h * lax.rsqrt(ms + EPS) * fg_ref[...]


def _combine(seg, colinfo, h2d, mod3, fg, ys, rows_l, tokens_per_batch, tm):
    n, d = h2d.shape
    per_b = tokens_per_batch // tm
    return pl.pallas_call(
        _combine_kernel,
        grid=(n // tm,),
        in_specs=[pl.BlockSpec((1, 1, seg.shape[2]), lambda i: (i, 0, 0), memory_space=pltpu.SMEM),
                  pl.BlockSpec((1, 1, seg.shape[2]), lambda i: (jnp.minimum(i + 1, n // tm - 1), 0, 0),
                               memory_space=pltpu.SMEM),
                  pl.BlockSpec((tm, colinfo.shape[1]), lambda i: (i, 0)),
                  pl.BlockSpec((tm, d), lambda i: (i, 0)),
                  pl.BlockSpec((1, 6, d), lambda i: (i // per_b, 0, 0)),
                  pl.BlockSpec((1, d), lambda i: (0, 0)),
                  pl.BlockSpec(memory_space=pl.ANY)],
        out_specs=pl.BlockSpec((tm, d), lambda i: (i, 0)),
        out_shape=jax.ShapeDtypeStruct((n, d), F32),
        scratch_shapes=[pltpu.VMEM((2, rows_l, d // 2), jnp.int32), pltpu.SemaphoreType.DMA((2,))],
        compiler_params=_params(("arbitrary",)),
        name="combine",
    )(seg, seg, colinfo, h2d, mod3, fg, ys)


def _rope_tables(s):
    t = np.arange(s)
    pos_r = (t // GRID_W).astype(np.float32)
    pos_c = (t % GRID_W).astype(np.float32)
    nf = GLA_HK // 4
    inv = (np.float32(ROPE_BASE) ** (-np.arange(nf, dtype=np.float32) / np.float32(nf))).astype(np.float32)
    ang_r = pos_r[:, None] * inv
    ang_c = pos_c[:, None] * inv

    def half(ang):
        return (np.concatenate([np.cos(ang), np.cos(ang)], axis=-1),
                np.concatenate([-np.sin(ang), np.sin(ang)], axis=-1))

    cr, sr = half(ang_r)
    cc, sc = half(ang_c)
    cos = np.tile(np.concatenate([cr, cc], axis=-1), (1, GLA_HEADS)).astype(np.float32)
    sin = np.tile(np.concatenate([sr, sc], axis=-1), (1, GLA_HEADS)).astype(np.float32)
    return jnp.asarray(cos), jnp.asarray(sin)


def kernel(x, c, ctx, c_ctx, w_mod, b_mod, norm_attn_g, norm_ffn_g, w_in, w_gla_a2, b_gla_a2, gla_norm_g, na_rpb, w_na_o, w_gla_o, w_out, w_group, b_group, w_expert, b_expert, w_exp_gate, w_exp_up, w_exp_down, final_norm_g):
    b, s, d = x.shape
    ctx_len = ctx.shape[1]
    n = b * s
    assert w_mod.shape[0] == 1, "single-layer block"
    assert s % (GRID_W * NA_WIN_H) == 0 and ctx_len % GLA_CHUNK == 0

    mod_rows = -(-(b + 1) // 8) * 8
    cs = jnp.zeros((mod_rows, d), F32).at[:b].set(c).at[b].set(c_ctx)
    mod3 = _modulation(cs, w_mod[0], b_mod[0]).reshape(mod_rows, 6, d)

    o_q, o_k, o_v = 0, NA_WIDTH, 2 * NA_WIDTH
    o_gq = 3 * NA_WIDTH
    o_gk = o_gq + GLA_DK
    o_gv = o_gk + GLA_DK
    o_og = o_gv + GLA_DV
    o_lr = o_og + GLA_DV
    o_mn = o_lr + 2 * GLA_GATE_RANK
    w_a = w_in[0][:, :o_lr].astype(BF16)
    w_g = w_in[0][:, o_mn:].astype(BF16)
    w_lr = jnp.pad(w_in[0][:, o_lr:o_mn].astype(BF16), ((0, 0), (0, LR_PAD - 2 * GLA_GATE_RANK)))
    cw = 512
    lat_specs = [(0, 0, 0, o_q, cw, "plain"), (1, 0, 0, o_k, cw, "plain"), (2, 0, 0, o_v, cw, "plain"),
                 (3, 0, 0, o_gq, cw, "rope_scaled"), (4, 0, 0, o_gk, cw, "rope"),
                 (5, 0, 0, o_gv, cw, "plain"), (5, cw, 0, o_gv + cw, cw, "plain"),
                 (6, 0, 0, o_og, cw, "silu"), (6, cw, 0, o_og + cw, cw, "silu"),
                 (7, 0, 1, 0, cw, "sigmoid"), (7, cw, 1, cw, cw, "sigmoid"),
                 (8, 0, 1, d, cw, "sigmoid"), (8, cw, 1, d + cw, cw, "sigmoid"),
                 (9, 0, 2, 0, LR_PAD, "plain")]
    lat_widths = [NA_WIDTH, NA_WIDTH, NA_WIDTH, GLA_DK, GLA_DK, GLA_DV, GLA_DV, d, d, LR_PAD]
    tm = 512
    per_b = s // tm
    q_na, k_na, v_na, gq, gk, gv, sog, sgn, sgg, lr = _inproj(
        x.reshape(n, d), mod3, lambda i: i // per_b, norm_attn_g[0], [w_a, w_g, w_lr], lat_specs, lat_widths, tm,
        rope=_rope_tables(s), name="inproj_lat")

    ctx_specs = [(0, 0, 0, o_k, cw, "plain"), (1, 0, 0, o_v, cw, "plain"), (2, 0, 0, o_gk, cw, "plain"),
                 (3, 0, 0, o_gv, cw, "plain"), (3, cw, 0, o_gv + cw, cw, "plain"),
                 (4, 0, 1, 0, LR_PAD, "plain")]
    ctx_widths = [NA_WIDTH, NA_WIDTH, GLA_DK, GLA_DV, LR_PAD]
    kc_na, vc_na, ck, cv, clr = _inproj(
        ctx.reshape(b * ctx_len, d), mod3, lambda i: b, norm_attn_g[0], [w_a, w_lr], ctx_specs, ctx_widths,
        ctx_len, name="inproj_ctx")

    def lat3(a):
        return a.reshape(b, s, a.shape[-1])

    def ctx3(a):
        return a.reshape(b, ctx_len, a.shape[-1])

    o_na = _na_attention(lat3(q_na), lat3(k_na), lat3(v_na), ctx3(kc_na), ctx3(vc_na),
                         _na_bias_table(na_rpb[0]))

    wa = jnp.zeros((2, LR_PAD, GLA_DK), F32)
    wa = wa.at[0, :GLA_GATE_RANK].set(w_gla_a2[0, 0]).at[1, GLA_GATE_RANK:2 * GLA_GATE_RANK].set(w_gla_a2[0, 1])
    gn = jnp.tile(gla_norm_g[0], GLA_HEADS).reshape(1, GLA_DV)
    o_gla = _gla(lat3(gq), lat3(gk), lat3(gv), lat3(lr), lat3(sog), ctx3(ck), ctx3(cv), ctx3(clr),
                 wa.astype(BF16), b_gla_a2[0], gn)

    wr = jnp.zeros((d, ROUTE_LANES), F32)
    wr = wr.at[:, :N_GROUPS].set(w_group[0]).at[:, EXPERT_LANE0:EXPERT_LANE0 + N_EXPERTS].set(w_expert[0])
    wr_hi = wr.astype(BF16)
    br = jnp.zeros((1, ROUTE_LANES), F32)
    br = br.at[0, :N_GROUPS].set(b_group[0]).at[0, EXPERT_LANE0:EXPERT_LANE0 + N_EXPERTS].set(b_expert[0])
    h, f_lat, route, route_t, counts = _merge_route(
        o_na.reshape(n, NA_WIDTH), o_gla.reshape(n, GLA_DV), sgn, sgg, x.reshape(n, d), mod3, norm_ffn_g[0].reshape(1, d),
        w_na_o[0].astype(BF16), w_gla_o[0].astype(BF16), w_out[0].astype(BF16),
        jnp.concatenate([wr_hi, (wr - wr_hi.astype(F32)).astype(BF16)], axis=1), br, s, tm)

    n_tiles = n // tm
    cnt = counts[:, 0, EXPERT_LANE0:EXPERT_LANE0 + N_EXPERTS].astype(jnp.int32)
    cnt_al = (cnt + SEG_ALIGN - 1) // SEG_ALIGN * SEG_ALIGN
    lstart = jnp.cumsum(cnt_al, axis=1) - cnt_al
    total = jnp.sum(cnt_al, axis=0)
    padded = (total + MOE_ROWS - 1) // MOE_ROWS * MOE_ROWS
    pend = jnp.cumsum(padded)
    pstart = pend - padded
    seg_dst = pstart[None, :] + jnp.cumsum(cnt_al, axis=0) - cnt_al
    tile_rows = jnp.sum(cnt_al, axis=1, keepdims=True)
    seg = jnp.concatenate([cnt_al, lstart, seg_dst, tile_rows, jnp.zeros_like(cnt)[:, 1:]], axis=1)
    seg = seg.reshape(n_tiles, 1, 4 * N_EXPERTS)
    n_used = (pend[-1] // MOE_ROWS).reshape(1)
    tail = jnp.concatenate([pstart + total, padded - total, n_used, jnp.zeros((2 * N_EXPERTS - 1,), jnp.int32)])
    tail = tail.reshape(1, 1, 4 * N_EXPERTS)
    seg_pad = n_tiles * N_EXPERTS * (SEG_ALIGN - 1)
    n_pad = (2 * n + seg_pad + N_EXPERTS * (MOE_ROWS - SEG_ALIGN) + MOE_ROWS - 1) // MOE_ROWS * MOE_ROWS
    n_blk = n_pad // MOE_ROWS
    rows_l = (2 * tm + N_EXPERTS * (SEG_ALIGN - 1) + MXU_DIM - 1) // MXU_DIM * MXU_DIM
    blk_start = jnp.arange(n_blk, dtype=jnp.int32) * MOE_ROWS
    blk_expert = jnp.minimum(jnp.sum(blk_start[:, None] >= pend[None, :], axis=-1), N_EXPERTS - 1)

    xs = _dispatch(seg, tail, route_t, f_lat, n_pad, rows_l, tm)
    ys = _experts(blk_expert.astype(jnp.int32), n_used.astype(jnp.int32), xs,
                  w_exp_gate[0], w_exp_up[0], w_exp_down[0])
    out = _combine(seg, route, h, mod3, final_norm_g.reshape(1, d), ys, rows_l, s, tm)
    return out.reshape(b, s, d)
```

```python
import functools

import jax
import jax.numpy as jnp
import numpy as np
from jax import lax
from jax.experimental import pallas as pl
from jax.experimental.pallas import tpu as pltpu

F32 = jnp.float32
BF16 = jnp.bfloat16

D_MODEL = 1024
GRID_W = 64
NA_HEADS = 8
NA_HEAD_DIM = 64
NA_WIDTH = NA_HEADS * NA_HEAD_DIM
NA_WIN_H = 8
NA_WIN_W = 16
GLA_HEADS = 4
GLA_DK = D_MODEL // 2
GLA_DV = D_MODEL
GLA_HK = GLA_DK // GLA_HEADS
GLA_HV = GLA_DV // GLA_HEADS
GLA_GATE_RANK = 16
GLA_GATE_NORM = 16.0
GLA_CHUNK = 64
GLA_GROUP = 256
ROPE_BASE = 10000.0
N_GROUPS = 4
EXPERTS_PER_GROUP = 8
N_EXPERTS = N_GROUPS * EXPERTS_PER_GROUP
D_EXPERT = D_MODEL // 2
EPS = 1e-6
NEG_INF = -1e30
LOG2E = 1.4426950408889634

LANES = 128
MXU_DIM = 256
LR_PAD = LANES
ROUTE_LANES = LANES
EXPERT_LANE0 = N_GROUPS
ROUTE_ROWS = 48
MOE_ROWS = 512
SEG_ALIGN = 8
SEG_SMALL = (8, 16, 32)
SEG_BIG = 64
WAIT_SIZES = (1024, 512, 256, 128, 64, 32, 16, 8)
TAIL_SIZES = (256, 128, 64, 32, 16, 8)
VMEM_LIMIT = 56 * 1024 * 1024

_NT = (((1,), (1,)), ((), ()))
_TN = (((0,), (0,)), ((), ()))


def _params(sem, vmem=VMEM_LIMIT):
    return pltpu.CompilerParams(dimension_semantics=sem, vmem_limit_bytes=vmem)


def _mod_kernel(c_ref, w_ref, b_ref, o_ref):
    c = c_ref[...]
    s = c * jax.nn.sigmoid(c)
    o_ref[...] = jnp.dot(s.astype(BF16), w_ref[...].astype(BF16),
                         preferred_element_type=F32) + b_ref[...]


def _modulation(cs, w_mod, b_mod):
    rows, d = cs.shape
    n = w_mod.shape[1]
    bn = 1024
    return pl.pallas_call(
        _mod_kernel,
        grid=(n // bn,),
        in_specs=[pl.BlockSpec((rows, d), lambda j: (0, 0)),
                  pl.BlockSpec((d, bn), lambda j: (0, j)),
                  pl.BlockSpec((1, bn), lambda j: (0, j))],
        out_specs=pl.BlockSpec((rows, bn), lambda j: (0, j)),
        out_shape=jax.ShapeDtypeStruct((rows, n), F32),
        compiler_params=_params(("arbitrary",)),
        name="mod",
    )(cs, w_mod, b_mod.reshape(1, n))


def _inproj_kernel(specs, use_rope, n_w, n_out, x_ref, mod_ref, g_ref, *rest):
    if use_rope:
        cos_ref, sin_ref = rest[:2]
        rest = rest[2:]
    w_refs = rest[:n_w]
    outs = rest[n_w:n_w + n_out]
    a_scr = rest[n_w + n_out]
    x = x_ref[...]
    ms = jnp.mean(x * x, axis=-1, keepdims=True)
    a = x * lax.rsqrt(ms + EPS) * g_ref[...] * (1.0 + mod_ref[0, 1:2, :]) + mod_ref[0, 0:1, :]
    a_scr[...] = a.astype(BF16)
    for (oi, oc, wi, wc, width, kind) in specs:
        acc = jnp.dot(a_scr[...], w_refs[wi][:, wc:wc + width], preferred_element_type=F32)
        if kind in ("rope", "rope_scaled"):
            if kind == "rope_scaled":
                acc = acc * (GLA_HK ** -0.5)
            lane = lax.broadcasted_iota(jnp.int32, acc.shape, 1)
            rot = jnp.where((lane & 63) < 32,
                            pltpu.roll(acc, width - 32, axis=1),
                            pltpu.roll(acc, 32, axis=1))
            acc = acc * cos_ref[...] + rot * sin_ref[...]
        elif kind == "silu":
            acc = acc * jax.nn.sigmoid(acc)
        elif kind == "sigmoid":
            acc = jax.nn.sigmoid(acc)
        outs[oi][:, oc:oc + width] = acc.astype(outs[oi].dtype)


def _inproj(x2d, mod3, mod_row_fn, g, weights, specs, out_widths, tm, rope=None, name="inproj"):
    n, d = x2d.shape
    use_rope = rope is not None
    in_specs = [pl.BlockSpec((tm, d), lambda i: (i, 0)),
                pl.BlockSpec((1, 6, d), lambda i: (mod_row_fn(i), 0, 0)),
                pl.BlockSpec((1, d), lambda i: (0, 0))]
    args = [x2d, mod3, g.reshape(1, d)]
    if use_rope:
        cos, sin = rope
        nb = cos.shape[0] // tm
        in_specs += [pl.BlockSpec((tm, cos.shape[1]), lambda i: (i % nb, 0)),
                     pl.BlockSpec((tm, cos.shape[1]), lambda i: (i % nb, 0))]
        args += [cos, sin]
    in_specs += [pl.BlockSpec((d, cols), lambda i: (0, 0), pipeline_mode=pl.Buffered(1)) for _, cols in weights]
    args += [w for w, _ in weights]
    out_specs = [pl.BlockSpec((tm, w), lambda i: (i, 0)) for w in out_widths]
    out_shape = [jax.ShapeDtypeStruct((n, w), BF16) for w in out_widths]
    return pl.pallas_call(
        functools.partial(_inproj_kernel, specs, use_rope, len(weights), len(out_widths)),
        grid=(n // tm,),
        in_specs=in_specs,
        out_specs=out_specs,
        out_shape=out_shape,
        scratch_shapes=[pltpu.VMEM((tm, d), BF16)],
        compiler_params=_params(("arbitrary",)),
        name=name,
    )(*args)


def _na_kernel(rows_per_step, n_rows, q_ref, k_ref, v_ref, kc_ref, vc_ref, bias_ref, o_ref, sw_scr, sc_scr):
    j = pl.program_id(1)
    lane = lax.broadcasted_iota(jnp.int32, (GRID_W, LANES), 1)
    first_head = lane < NA_HEAD_DIM
    scale = NA_HEAD_DIM ** -0.5
    win_keys = NA_WIN_H * GRID_W
    pairs = range(NA_HEADS // 2)
    lanes = [slice(p * LANES, (p + 1) * LANES) for p in pairs]
    stack = 2 * GRID_W

    def window(rr):
        r = j * rows_per_step + rr
        rs = jnp.clip(r - NA_WIN_H // 2, 0, n_rows - NA_WIN_H)
        return r - rs, pl.multiple_of(rr * GRID_W, GRID_W), pl.multiple_of(rs * GRID_W, GRID_W)

    def scores(rr, slot):
        _, t0, ks = window(rr)
        for p in pairs:
            qp = q_ref[0, pl.ds(t0, GRID_W), lanes[p]] * scale
            zero = jnp.zeros_like(qp)
            qs = jnp.concatenate([jnp.where(first_head, qp, zero), jnp.where(first_head, zero, qp)], axis=0)
            sw_scr[slot, p * stack:(p + 1) * stack, :] = lax.dot_general(
                qs, k_ref[0, pl.ds(ks, win_keys), lanes[p]], _NT, preferred_element_type=F32)
            sc_scr[slot, p * stack:(p + 1) * stack, :] = lax.dot_general(
                qs, kc_ref[0, :, lanes[p]], _NT, preferred_element_type=F32)

    def attend(rr, slot):
        d, t0, ks = window(rr)
        pws, pcs, dens = [], [], []
        for p in pairs:
            bias = jnp.concatenate([bias_ref[(NA_WIN_H - 1) - d + 2 * i, p] for i in range(NA_WIN_H // 2)], axis=1)
            sw = sw_scr[slot, p * stack:(p + 1) * stack, :] * LOG2E + bias
            sc = sc_scr[slot, p * stack:(p + 1) * stack, :] * LOG2E
            m = jnp.maximum(jnp.max(sw, axis=-1, keepdims=True), jnp.max(sc, axis=-1, keepdims=True))
            pw = jnp.exp2(sw - m)
            pc = jnp.exp2(sc - m)
            dens.append(jnp.sum(pw, axis=-1, keepdims=True) + jnp.sum(pc, axis=-1, keepdims=True))
            pws.append(pw.astype(BF16))
            pcs.append(pc.astype(BF16))
        for p in pairs:
            o2 = (jnp.dot(pws[p], v_ref[0, pl.ds(ks, win_keys), lanes[p]], preferred_element_type=F32)
                  + jnp.dot(pcs[p], vc_ref[0, :, lanes[p]], preferred_element_type=F32)) / dens[p]
            o_ref[0, pl.ds(t0, GRID_W), lanes[p]] = jnp.where(first_head, o2[:GRID_W],
                                                             o2[GRID_W:]).astype(o_ref.dtype)

    scores(0, 0)

    def two_rows(k, carry):
        scores(2 * k + 1, 1)
        attend(2 * k, 0)
        scores(2 * k + 2, 0)
        attend(2 * k + 1, 1)
        return carry

    lax.fori_loop(0, rows_per_step // 2 - 1, two_rows, 0)
    scores(rows_per_step - 1, 1)
    attend(rows_per_step - 2, 0)
    attend(rows_per_step - 1, 1)


def _na_bias_table(rpb):
    n_heads, n_dr, n_dc = rpb.shape
    c = np.arange(GRID_W)
    cstart = np.clip(c - NA_WIN_W // 2, 0, GRID_W - NA_WIN_W)
    kc = np.arange(GRID_W)
    valid = (kc[None, :] >= cstart[:, None]) & (kc[None, :] < cstart[:, None] + NA_WIN_W)
    col_idx = np.clip(kc[None, :] - c[:, None] + NA_WIN_W - 1, 0, n_dc - 1)
    pick = np.zeros((n_dc, GRID_W * GRID_W), np.float32)
    pick[col_idx.reshape(-1), np.arange(GRID_W * GRID_W)] = 1.0
    t = jnp.dot(rpb.reshape(n_heads * n_dr, n_dc).astype(F32), jnp.asarray(pick),
                precision=lax.Precision.HIGHEST).reshape(n_heads, n_dr, GRID_W, GRID_W)
    t = jnp.where(jnp.asarray(valid)[None, None], t * LOG2E, NEG_INF)
    t2 = jnp.concatenate([t[:, :-1], t[:, 1:]], axis=-1)
    return t2.transpose(1, 0, 2, 3).reshape(n_dr - 1, n_heads // 2, 2 * GRID_W, 2 * GRID_W)


def _na_attention(q, k, v, kc, vc, bias, rows_per_step=16):
    b, s, w = q.shape
    n_rows = s // GRID_W
    ctx_len = kc.shape[1]
    blk = rows_per_step * GRID_W
    return pl.pallas_call(
        functools.partial(_na_kernel, rows_per_step, n_rows),
        grid=(b, n_rows // rows_per_step),
        in_specs=[pl.BlockSpec((1, blk, w), lambda bi, j: (bi, j, 0)),
                  pl.BlockSpec((1, s, w), lambda bi, j: (bi, 0, 0)),
                  pl.BlockSpec((1, s, w), lambda bi, j: (bi, 0, 0)),
                  pl.BlockSpec((1, ctx_len, w), lambda bi, j: (bi, 0, 0)),
                  pl.BlockSpec((1, ctx_len, w), lambda bi, j: (bi, 0, 0)),
                  pl.BlockSpec(bias.shape, lambda bi, j: (0, 0, 0, 0), pipeline_mode=pl.Buffered(1))],
        out_specs=pl.BlockSpec((1, blk, w), lambda bi, j: (bi, j, 0)),
        out_shape=jax.ShapeDtypeStruct((b, s, w), BF16),
        scratch_shapes=[pltpu.VMEM((2, NA_HEADS * GRID_W, NA_WIN_H * GRID_W), F32),
                        pltpu.VMEM((2, NA_HEADS * GRID_W, ctx_len), F32)],
        compiler_params=_params(("arbitrary", "arbitrary")),
        name="na_attn",
    )(q, k, v, kc, vc, bias)


def _gla_kernel(n_groups, gq_ref, gk_ref, gv_ref, lr_ref, sog_ref, ck_ref, cv_ref, clr_ref,
                wa_ref, ba_ref, gn_ref, o_ref, st_scr, of_scr, qe_scr, kd_scr, ckd_scr, dec_scr, cdec_scr,
                qk_a):
    c_len = GLA_CHUNK
    grp = GLA_GROUP
    cpg = grp // c_len
    row = lax.broadcasted_iota(jnp.int32, (grp, grp), 0)
    col = lax.broadcasted_iota(jnp.int32, (grp, grp), 1)
    same_chunk = (row // c_len) == (col // c_len)
    masks = (same_chunk & (col <= row), same_chunk & (col >= row))

    dirs = range(2)

    def gates(lr_rows, k_rows):
        zs = [jnp.dot(lr_rows, wa_ref[t], preferred_element_type=F32) + ba_ref[t:t + 1, :] for t in dirs]
        gs = [(jnp.minimum(z, 0.0) * LOG2E - jnp.log2(1.0 + jnp.exp2(jnp.abs(z) * (-LOG2E)))) * (1.0 / GLA_GATE_NORM)
              for z in zs]
        g1s = [g.astype(BF16) for g in gs]
        g2s = [(g - g1.astype(F32)).astype(BF16) for g, g1 in zip(gs, g1s)]
        tris = [masks[t].astype(BF16) for t in dirs]
        bs = [jnp.dot(tris[t], g1s[t], preferred_element_type=F32)
              + jnp.dot(tris[t], g2s[t], preferred_element_type=F32) for t in dirs]
        out = []
        for t in dirs:
            edge = c_len - 1 if t == 0 else 0
            decs = [jnp.exp2(bs[t][j * c_len + edge:j * c_len + edge + 1, :]) for j in range(cpg)]
            ke = k_rows * jnp.exp2(-bs[t])
            kd = (ke * jnp.concatenate([jnp.broadcast_to(v, (c_len, v.shape[1])) for v in decs],
                                       axis=0)).astype(BF16)
            dec = jnp.concatenate(decs + [jnp.ones_like(v) for v in decs], axis=0)
            out.append((bs[t], ke.astype(BF16), kd, dec))
        return out

    for t, (_, _, kd, dec) in enumerate(gates(clr_ref[0], ck_ref[0].astype(F32))):
        ckd_scr[t] = kd
        cdec_scr[t] = dec

    def group_gates(gi, qk_buf):
        rows = pl.ds(pl.multiple_of(gi * grp, grp), grp)
        qf = gq_ref[0, rows, :].astype(F32)
        for t, (b, ke, kd, dec) in enumerate(gates(lr_ref[0, rows, :], gk_ref[0, rows, :].astype(F32))):
            qe = (qf * jnp.exp2(b)).astype(BF16)
            qk_buf[t] = qe
            qk_buf[2 + t] = ke
            qe_scr[t, rows, :] = qe
            kd_scr[t, rows, :] = kd
            dec_scr[t, gi] = dec

    def group_attend(gi, qk_buf):
        rows = pl.ds(pl.multiple_of(gi * grp, grp), grp)
        heads = range(GLA_HEADS)
        kls = [slice(h * GLA_HK, (h + 1) * GLA_HK) for h in heads]
        raw = [[lax.dot_general(qk_buf[t, :, kls[h]], qk_buf[2 + t, :, kls[h]], _NT, preferred_element_type=F32)
                for t in dirs] for h in heads]
        atts = [(jnp.where(masks[0], raw[h][0], 0.0) + jnp.where(masks[1], raw[h][1], 0.0)).astype(BF16)
                for h in heads]
        for h in heads:
            vl = slice(h * GLA_HV, (h + 1) * GLA_HV)
            of_scr[rows, vl] = jnp.dot(atts[h], gv_ref[0, rows, vl], preferred_element_type=F32)

    def group_body(gi, carry):
        group_gates(gi, qk_a)
        group_attend(gi, qk_a)
        return carry

    lax.fori_loop(0, n_groups, group_body, 0)

    def state_step(h, v_rows, kd_rows, dec_row):
        kl = slice(h * GLA_HK, (h + 1) * GLA_HK)
        upd = lax.dot_general(v_rows, kd_rows[:, kl], _TN, preferred_element_type=F32)
        st_scr[h] = st_scr[h] * dec_row[:, kl] + upd

    def scan(direction):
        st_scr[...] = jnp.zeros_like(st_scr)
        order = range(cpg) if direction == 0 else range(cpg - 1, -1, -1)
        for j in order:
            rows = slice(j * c_len, (j + 1) * c_len)
            for h in range(GLA_HEADS):
                state_step(h, cv_ref[0, rows, h * GLA_HV:(h + 1) * GLA_HV],
                           ckd_scr[direction, rows, :], cdec_scr[direction, j:j + 1, :])

        def group_scan(i, carry):
            gi = i if direction == 0 else n_groups - 1 - i
            for j in order:
                rows = pl.ds(pl.multiple_of(gi * grp + j * c_len, c_len), c_len)
                qe_rows = qe_scr[direction, rows, :]
                kd_rows = kd_scr[direction, rows, :]
                dec_row = dec_scr[direction, gi, j:j + 1, :]
                for h in range(GLA_HEADS):
                    kl = slice(h * GLA_HK, (h + 1) * GLA_HK)
                    vl = slice(h * GLA_HV, (h + 1) * GLA_HV)
                    o_t = of_scr[rows, vl] + lax.dot_general(qe_rows[:, kl], st_scr[h].astype(BF16), _NT,
                                                             preferred_element_type=F32)
                    if direction == 0:
                        of_scr[rows, vl] = o_t
                    else:
                        ms = jnp.mean(o_t * o_t, axis=-1, keepdims=True)
                        o_n = o_t * lax.rsqrt(ms + EPS) * gn_ref[:, vl] * sog_ref[0, rows, vl].astype(F32)
                        o_ref[0, rows, vl] = o_n.astype(o_ref.dtype)
                    state_step(h, gv_ref[0, rows, vl], kd_rows, dec_row)
            return carry

        lax.fori_loop(0, n_groups, group_scan, 0)

    scan(0)
    scan(1)


def _gla(gq, gk, gv, lr, sog, ck, cv, clr, wa, ba, gn):
    b, s, _ = gq.shape
    ctx_len = ck.shape[1]
    assert ctx_len == GLA_GROUP and s % GLA_GROUP == 0
    n_groups = s // GLA_GROUP
    cpg = GLA_GROUP // GLA_CHUNK

    def per_batch(t, w):
        return pl.BlockSpec((1, t, w), lambda bi: (bi, 0, 0))

    return pl.pallas_call(
        functools.partial(_gla_kernel, n_groups),
        grid=(b,),
        in_specs=[per_batch(s, GLA_DK), per_batch(s, GLA_DK), per_batch(s, GLA_DV),
                  per_batch(s, LR_PAD), per_batch(s, GLA_DV),
                  per_batch(ctx_len, GLA_DK), per_batch(ctx_len, GLA_DV), per_batch(ctx_len, LR_PAD),
                  pl.BlockSpec(wa.shape, lambda bi: (0, 0, 0)),
                  pl.BlockSpec(ba.shape, lambda bi: (0, 0)),
                  pl.BlockSpec(gn.shape, lambda bi: (0, 0))],
        out_specs=per_batch(s, GLA_DV),
        out_shape=jax.ShapeDtypeStruct((b, s, GLA_DV), BF16),
        scratch_shapes=[pltpu.VMEM((GLA_HEADS, GLA_HV, GLA_HK), F32),
                        pltpu.VMEM((s, GLA_DV), F32),
                        pltpu.VMEM((2, s, GLA_DK), BF16),
                        pltpu.VMEM((2, s, GLA_DK), BF16),
                        pltpu.VMEM((2, ctx_len, GLA_DK), BF16),
                        pltpu.VMEM((2, n_groups, 2 * cpg, GLA_DK), F32),
                        pltpu.VMEM((2, 2 * cpg, GLA_DK), F32),
                        pltpu.VMEM((4, GLA_GROUP, GLA_DK), BF16)],
        compiler_params=_params(("arbitrary",), vmem=60 * 1024 * 1024),
        name="gla",
    )(gq, gk, gv, lr, sog, ck, cv, clr, wa, ba, gn)


def _merge_kernel(ona_ref, ogla_ref, sgn_ref, sgg_ref, x_ref, mod_ref, gffn_ref,
                  wna_ref, wgla_ref, wout_ref, wr_ref, br_ref,
                  h_ref, f_ref, route_ref, route_t_ref, cnt_ref):
    rows = x_ref.shape[0]
    y1 = (sgn_ref[...].astype(F32) * jnp.dot(ona_ref[...], wna_ref[...], preferred_element_type=F32)
          + sgg_ref[...].astype(F32) * jnp.dot(ogla_ref[...], wgla_ref[...], preferred_element_type=F32))
    y = jnp.dot(y1.astype(BF16), wout_ref[...], preferred_element_type=F32)
    h = x_ref[...] + mod_ref[0, 2:3, :] * y
    h_ref[...] = h
    ms = jnp.mean(h * h, axis=-1, keepdims=True)
    f = h * lax.rsqrt(ms + EPS) * gffn_ref[...] * (1.0 + mod_ref[0, 4:5, :]) + mod_ref[0, 3:4, :]
    f_ref[...] = f.astype(f_ref.dtype)

    f_hi = f.astype(BF16)
    f_lo = (f - f_hi.astype(F32)).astype(BF16)
    both = jnp.dot(f_hi, wr_ref[...], preferred_element_type=F32)
    logits = (both[:, :ROUTE_LANES] + both[:, ROUTE_LANES:]
              + jnp.dot(f_lo, wr_ref[:, :ROUTE_LANES], preferred_element_type=F32)) + br_ref[...]

    lt = logits.T[0:ROUTE_ROWS, :]
    er = lax.broadcasted_iota(jnp.int32, (ROUTE_ROWS, rows), 0)
    big = jnp.int32(ROUTE_ROWS)

    def first_argmax(vals):
        m = jnp.max(vals, axis=0, keepdims=True)
        idx = jnp.min(jnp.where(vals == m, er, big), axis=0, keepdims=True)
        return m, idx

    is_grp = er < N_GROUPS
    lg = jnp.where(is_grp, lt, -jnp.inf)
    mg, grp = first_argmax(lg)
    p_grp = 1.0 / jnp.sum(jnp.where(is_grp, jnp.exp(lg - mg), 0.0), axis=0, keepdims=True)
    lo = EXPERT_LANE0 + grp * EXPERTS_PER_GROUP
    in_grp = (er >= lo) & (er < lo + EXPERTS_PER_GROUP)
    le = jnp.where(in_grp, lt, -jnp.inf)
    m1, i1 = first_argmax(le)
    m2, i2 = first_argmax(jnp.where(er == i1, -jnp.inf, le))
    t = jnp.exp(m2 - m1)
    w1 = p_grp / (1.0 + t)
    w2 = p_grp * t / (1.0 + t)

    hot1 = er == i1
    hot2 = er == i2
    onehot = (hot1 | hot2).astype(BF16)
    s_i = lax.broadcasted_iota(jnp.int32, (rows, rows), 0)
    t_i = lax.broadcasted_iota(jnp.int32, (rows, rows), 1)
    rank = jnp.dot(onehot, (s_i < t_i).astype(BF16), preferred_element_type=F32)
    cnt = jnp.sum(onehot.astype(F32), axis=1, keepdims=True)
    cnt_al = jnp.floor((cnt + (SEG_ALIGN - 1.0)) * (1.0 / SEG_ALIGN)) * SEG_ALIGN
    e_i = lax.broadcasted_iota(jnp.int32, (ROUTE_ROWS, ROUTE_ROWS), 0)
    e_j = lax.broadcasted_iota(jnp.int32, (ROUTE_ROWS, ROUTE_ROWS), 1)
    lstart = jnp.dot((e_j < e_i).astype(BF16), jnp.broadcast_to(cnt_al, (ROUTE_ROWS, LANES)).astype(BF16),
                     preferred_element_type=F32)[:, 0:1]
    pos = rank + lstart
    pos1 = jnp.sum(jnp.where(hot1, pos, 0.0), axis=0, keepdims=True)
    pos2 = jnp.sum(jnp.where(hot2, pos, 0.0), axis=0, keepdims=True)

    e1 = (i1 - EXPERT_LANE0).astype(F32)
    e2 = (i2 - EXPERT_LANE0).astype(F32)
    fields = jnp.concatenate([e1, e2, w1, w2, pos1, pos2, jnp.zeros((2, rows), F32)], axis=0)
    route_t_ref[0] = fields
    route_ref[...] = jnp.concatenate([fields, jnp.zeros((ROUTE_LANES - 8, rows), F32)], axis=0).T
    cnt_rows = jnp.concatenate([jnp.broadcast_to(cnt, (ROUTE_ROWS, ROUTE_LANES)),
                                jnp.zeros((ROUTE_LANES - ROUTE_ROWS, ROUTE_LANES), F32)], axis=0)
    cnt_ref[0] = cnt_rows.T[0:1, :]


def _merge_route(ona, ogla, sgn, sgg, x2d, mod3, gffn, wna, wgla, wout, wr, br, tokens_per_batch, tm):
    n, d = x2d.shape
    per_b = tokens_per_batch // tm

    def tile(w):
        return pl.BlockSpec((tm, w), lambda i: (i, 0))

    def whole(a):
        return pl.BlockSpec(a.shape, lambda i: (0,) * a.ndim)

    return pl.pallas_call(
        _merge_kernel,
        grid=(n // tm,),
        in_specs=[tile(NA_WIDTH), tile(GLA_DV), tile(d), tile(d), tile(d),
                  pl.BlockSpec((1, 6, d), lambda i: (i // per_b, 0, 0)),
                  whole(gffn), whole(wna), whole(wgla), whole(wout), whole(wr), whole(br)],
        out_specs=[tile(d), tile(d), tile(ROUTE_LANES),
                   pl.BlockSpec((1, 8, tm), lambda i: (i, 0, 0)),
                   pl.BlockSpec((1, 1, ROUTE_LANES), lambda i: (i, 0, 0))],
        out_shape=[jax.ShapeDtypeStruct((n, d), F32),
                   jax.ShapeDtypeStruct((n, d), BF16),
                   jax.ShapeDtypeStruct((n, ROUTE_LANES), F32),
                   jax.ShapeDtypeStruct((n // tm, 8, tm), F32),
                   jax.ShapeDtypeStruct((n // tm, 1, ROUTE_LANES), F32)],
        compiler_params=_params(("arbitrary",)),
        name="merge_route",
    )(ona, ogla, sgn, sgg, x2d, mod3, gffn, wna, wgla, wout, wr, br)


HI_MASK = -65536


def _pack_rows(a):
    half = a.shape[1] // 2
    hi = lax.bitcast_convert_type(a[:, :half], jnp.int32)
    lo = lax.shift_right_logical(lax.bitcast_convert_type(a[:, half:], jnp.int32), 16)
    return hi | lo


def _unpack_rows(w):
    hi = lax.bitcast_convert_type(w & HI_MASK, F32).astype(BF16)
    lo = lax.bitcast_convert_type(lax.shift_left(w, 16), F32).astype(BF16)
    return hi, lo


def _segment_copies(seg_ref, make_copy, action):
    if action == "wait":
        total = seg_ref[0, 0, 3 * N_EXPERTS]
        for size in WAIT_SIZES:
            @pl.when((total & size) != 0)
            def _():
                make_copy(0, 0, size).wait()
        return

    def body(e, carry):
        cnt = seg_ref[0, 0, e]
        local = seg_ref[0, 0, N_EXPERTS + e]
        glob = seg_ref[0, 0, 2 * N_EXPERTS + e]
        off = jnp.int32(0)
        for size in SEG_SMALL:
            @pl.when((cnt & size) != 0)
            def _():
                make_copy(pl.multiple_of(local + off, SEG_ALIGN), pl.multiple_of(glob + off, SEG_ALIGN), size).start()
            off = off + (cnt & size)

        def big(k, c):
            o = off + k * SEG_BIG
            make_copy(pl.multiple_of(local + o, SEG_ALIGN), pl.multiple_of(glob + o, SEG_ALIGN), SEG_BIG).start()
            return c

        lax.fori_loop(0, cnt // SEG_BIG, big, 0)
        return carry

    lax.fori_loop(0, N_EXPERTS, body, 0)


def _dispatch_kernel(seg_ref, seg_prev_ref, tail_ref, lpos_ref, f_ref, xs_ref, sbuf, zbuf, sems):
    i = pl.program_id(0)
    last = pl.num_programs(0) - 1
    tm = f_ref.shape[0]
    rows_l = sbuf.shape[1]
    slot = i % 2
    fields = lpos_ref[0]
    lp = fields.astype(jnp.int32)
    half = f_ref.shape[1] // 2
    row = lax.broadcasted_iota(jnp.int32, (rows_l, tm), 0)
    first = row == lp[4:5, :]
    second = row == lp[5:6, :]
    sbuf[slot, :, :half] = _pack_rows(jnp.dot((first | second).astype(BF16), f_ref[...], preferred_element_type=F32))
    w_sorted = jnp.sum(jnp.where(first, fields[2:3, :], 0.0) + jnp.where(second, fields[3:4, :], 0.0),
                       axis=1, keepdims=True)
    sbuf[slot, :, half:] = lax.bitcast_convert_type(jnp.broadcast_to(w_sorted, (rows_l, LANES)), jnp.int32)

    def copier(s):
        def copy(local, glob, size):
            return pltpu.make_async_copy(sbuf.at[s, pl.ds(local, size), :], xs_ref.at[pl.ds(glob, size), :],
                                         sems.at[s])
        return copy

    _segment_copies(seg_ref, copier(slot), "start")

    @pl.when(i > 0)
    def _():
        _segment_copies(seg_prev_ref, copier(1 - slot), "wait")

    @pl.when(i == last)
    def _():
        _segment_copies(seg_ref, copier(slot), "wait")
        sem = sems.at[slot]
        zbuf[...] = jnp.zeros_like(zbuf)

        def tail(action):
            def body(e, carry):
                start = tail_ref[0, 0, e]
                length = tail_ref[0, 0, N_EXPERTS + e]
                off = jnp.int32(0)
                for size in TAIL_SIZES:
                    @pl.when((length & size) != 0)
                    def _():
                        cp = pltpu.make_async_copy(zbuf.at[pl.ds(0, size), :],
                                                   xs_ref.at[pl.ds(pl.multiple_of(start + off, SEG_ALIGN), size), :],
                                                   sem)
                        getattr(cp, action)()
                    off = off + (length & size)
                return carry
            lax.fori_loop(0, N_EXPERTS, body, 0)

        tail("start")
        tail("wait")

        def unused(action):
            def body(blk, carry):
                cp = pltpu.make_async_copy(zbuf, xs_ref.at[pl.ds(pl.multiple_of(blk * MOE_ROWS, MOE_ROWS),
                                                                  MOE_ROWS), :], sem)
                getattr(cp, action)()
                return carry
            lax.fori_loop(tail_ref[0, 0, 2 * N_EXPERTS], xs_ref.shape[0] // MOE_ROWS, body, 0)

        unused("start")
        unused("wait")


def _dispatch(seg, tail, lpos_rows, f2d, n_pad, rows_l, tm):
    n, d = f2d.shape
    return pl.pallas_call(
        _dispatch_kernel,
        grid=(n // tm,),
        in_specs=[pl.BlockSpec((1, 1, seg.shape[2]), lambda i: (i, 0, 0), memory_space=pltpu.SMEM),
                  pl.BlockSpec((1, 1, seg.shape[2]), lambda i: (jnp.maximum(i - 1, 0), 0, 0),
                               memory_space=pltpu.SMEM),
                  pl.BlockSpec(tail.shape, lambda i: (0, 0, 0), memory_space=pltpu.SMEM),
                  pl.BlockSpec((1, 8, tm), lambda i: (i, 0, 0)),
                  pl.BlockSpec((tm, d), lambda i: (i, 0))],
        out_specs=pl.BlockSpec(memory_space=pl.ANY),
        out_shape=jax.ShapeDtypeStruct((n_pad, d // 2 + LANES), jnp.int32),
        scratch_shapes=[pltpu.VMEM((2, rows_l, d // 2 + LANES), jnp.int32),
                        pltpu.VMEM((MOE_ROWS, d // 2 + LANES), jnp.int32),
                        pltpu.SemaphoreType.DMA((2,))],
        compiler_params=_params(("arbitrary",)),
        name="dispatch",
    )(seg, seg, tail, lpos_rows, f2d)


def _expert_kernel(be_ref, nu_ref, x_ref, wg_ref, wu_ref, wd_ref, y_ref, wgu_scr, wd_scr):
    i = pl.program_id(0)
    prev = be_ref[jnp.maximum(i - 1, 0)]

    @pl.when((i == 0) | (be_ref[i] != prev))
    def _():
        wgu_scr[:, :D_EXPERT] = wg_ref[0].astype(BF16)
        wgu_scr[:, D_EXPERT:] = wu_ref[0].astype(BF16)
        wd_scr[...] = wd_ref[0].astype(BF16)

    @pl.when(i < nu_ref[0])
    def _():
        half = y_ref.shape[1]
        x = jnp.concatenate(_unpack_rows(x_ref[:, :half]), axis=1)
        gu = jnp.dot(x, wgu_scr[...], preferred_element_type=F32)
        gate = gu[:, :D_EXPERT]
        hdn = gate * jax.nn.sigmoid(gate) * gu[:, D_EXPERT:]
        y = jnp.dot(hdn.astype(BF16), wd_scr[...], preferred_element_type=F32)
        y = y * lax.bitcast_convert_type(x_ref[:, half:half + 1], F32)
        y_ref[...] = _pack_rows(y.astype(BF16).astype(F32))

    @pl.when(i >= nu_ref[0])
    def _():
        y_ref[...] = jnp.zeros_like(y_ref)


def _experts(blk_expert, n_used, xs, wg, wu, wd):
    n_pad, row_words = xs.shape
    half = row_words - LANES
    d = 2 * half
    n_blk = n_pad // MOE_ROWS

    def used_block(i, be, nu):
        return (jnp.maximum(jnp.minimum(i, nu[0] - 1), 0), 0)

    grid_spec = pltpu.PrefetchScalarGridSpec(
        num_scalar_prefetch=2,
        grid=(n_blk,),
        in_specs=[pl.BlockSpec((MOE_ROWS, row_words), used_block),
                  pl.BlockSpec((1, d, D_EXPERT), lambda i, be, nu: (be[i], 0, 0)),
                  pl.BlockSpec((1, d, D_EXPERT), lambda i, be, nu: (be[i], 0, 0)),
                  pl.BlockSpec((1, D_EXPERT, d), lambda i, be, nu: (be[i], 0, 0))],
        out_specs=pl.BlockSpec((MOE_ROWS, half), lambda i, be, nu: (i, 0)),
        scratch_shapes=[pltpu.VMEM((d, 2 * D_EXPERT), BF16),
                        pltpu.VMEM((D_EXPERT, d), BF16)],
    )
    return pl.pallas_call(
        _expert_kernel,
        grid_spec=grid_spec,
        out_shape=jax.ShapeDtypeStruct((n_pad, half), jnp.int32),
        compiler_params=_params(("arbitrary",)),
        name="experts",
    )(blk_expert, n_used, xs, wg, wu, wd)


def _combine_kernel(seg_ref, seg_next_ref, col_ref, h_ref, mod_ref, fg_ref, ys_ref, o_ref, ybuf, sems):
    i = pl.program_id(0)
    tm = h_ref.shape[0]
    rows_l = ybuf.shape[1]
    slot = i % 2

    def copier(s):
        def copy(local, glob, size):
            return pltpu.make_async_copy(ys_ref.at[pl.ds(glob, size), :], ybuf.at[s, pl.ds(local, size), :],
                                         sems.at[s])
        return copy

    @pl.when(i == 0)
    def _():
        ybuf[...] = jnp.zeros_like(ybuf)
        _segment_copies(seg_ref, copier(slot), "start")

    @pl.when(i + 1 < pl.num_programs(0))
    def _():
        _segment_copies(seg_next_ref, copier(1 - slot), "start")

    _segment_copies(seg_ref, copier(slot), "wait")
    y_hi, y_lo = _unpack_rows(ybuf[slot])
    y_all = jnp.concatenate([y_hi, y_lo], axis=1)
    col = lax.broadcasted_iota(jnp.int32, (tm, rows_l), 1)
    info = col_ref[...]
    pick = (col == info[:, 4:5].astype(jnp.int32)) | (col == info[:, 5:6].astype(jnp.int32))
    moe = jnp.dot(pick.astype(BF16), y_all, preferred_element_type=F32)
    h = h_ref[...] + mod_ref[0, 5:6, :] * moe
    ms = jnp.mean(h * h, axis=-1, keepdims=True)
    o_ref[...] = h * lax.rsqrt(ms + EPS) * fg_ref[...]


def _combine(seg, colinfo, h2d, mod3, fg, ys, rows_l, tokens_per_batch, tm):
    n, d = h2d.shape
    per_b = tokens_per_batch // tm
    return pl.pallas_call(
        _combine_kernel,
        grid=(n // tm,),
        in_specs=[pl.BlockSpec((1, 1, seg.shape[2]), lambda i: (i, 0, 0), memory_space=pltpu.SMEM),
                  pl.BlockSpec((1, 1, seg.shape[2]), lambda i: (jnp.minimum(i + 1, n // tm - 1), 0, 0),
                               memory_space=pltpu.SMEM),
                  pl.BlockSpec((tm, colinfo.shape[1]), lambda i: (i, 0)),
                  pl.BlockSpec((tm, d), lambda i: (i, 0)),
                  pl.BlockSpec((1, 6, d), lambda i: (i // per_b, 0, 0)),
                  pl.BlockSpec((1, d), lambda i: (0, 0)),
                  pl.BlockSpec(memory_space=pl.ANY)],
        out_specs=pl.BlockSpec((tm, d), lambda i: (i, 0)),
        out_shape=jax.ShapeDtypeStruct((n, d), F32),
        scratch_shapes=[pltpu.VMEM((2, rows_l, d // 2), jnp.int32), pltpu.SemaphoreType.DMA((2,))],
        compiler_params=_params(("arbitrary",)),
        name="combine",
    )(seg, seg, colinfo, h2d, mod3, fg, ys)


def _rope_tables(s):
    t = np.arange(s)
    pos_r = (t // GRID_W).astype(np.float32)
    pos_c = (t % GRID_W).astype(np.float32)
    nf = GLA_HK // 4
    inv = (np.float32(ROPE_BASE) ** (-np.arange(nf, dtype=np.float32) / np.float32(nf))).astype(np.float32)
    ang_r = pos_r[:, None] * inv
    ang_c = pos_c[:, None] * inv

    def half(ang):
        return (np.concatenate([np.cos(ang), np.cos(ang)], axis=-1),
                np.concatenate([-np.sin(ang), np.sin(ang)], axis=-1))

    cr, sr = half(ang_r)
    cc, sc = half(ang_c)
    cos = np.tile(np.concatenate([cr, cc], axis=-1), (1, GLA_HEADS)).astype(np.float32)
    sin = np.tile(np.concatenate([sr, sc], axis=-1), (1, GLA_HEADS)).astype(np.float32)
    return jnp.asarray(cos), jnp.asarray(sin)


def kernel(x, c, ctx, c_ctx, w_mod, b_mod, norm_attn_g, norm_ffn_g, w_in, w_gla_a2, b_gla_a2, gla_norm_g, na_rpb, w_na_o, w_gla_o, w_out, w_group, b_group, w_expert, b_expert, w_exp_gate, w_exp_up, w_exp_down, final_norm_g):
    b, s, d = x.shape
    ctx_len = ctx.shape[1]
    n = b * s
    assert w_mod.shape[0] == 1, "single-layer block"
    assert s % (GRID_W * NA_WIN_H) == 0 and ctx_len % GLA_CHUNK == 0

    mod_rows = -(-(b + 1) // 8) * 8
    cs = jnp.zeros((mod_rows, d), F32).at[:b].set(c).at[b].set(c_ctx)
    mod3 = _modulation(cs, w_mod[0], b_mod[0]).reshape(mod_rows, 6, d)

    o_q, o_k, o_v = 0, NA_WIDTH, 2 * NA_WIDTH
    o_gq = 3 * NA_WIDTH
    o_gk = o_gq + GLA_DK
    o_gv = o_gk + GLA_DK
    o_og = o_gv + GLA_DV
    o_lr = o_og + GLA_DV
    o_mn = o_lr + 2 * GLA_GATE_RANK
    w_bf = w_in[0].astype(BF16)
    w_a = (w_bf, o_lr)
    w_g = (w_bf[:, o_mn:], 2 * d)
    w_lr = (jnp.pad(w_bf[:, o_lr:o_mn], ((0, 0), (0, LR_PAD - 2 * GLA_GATE_RANK))), LR_PAD)
    cw = 512
    lat_specs = [(0, 0, 0, o_q, cw, "plain"), (1, 0, 0, o_k, cw, "plain"), (2, 0, 0, o_v, cw, "plain"),
                 (3, 0, 0, o_gq, cw, "rope_scaled"), (4, 0, 0, o_gk, cw, "rope"),
                 (5, 0, 0, o_gv, cw, "plain"), (5, cw, 0, o_gv + cw, cw, "plain"),
                 (6, 0, 0, o_og, cw, "silu"), (6, cw, 0, o_og + cw, cw, "silu"),
                 (7, 0, 1, 0, cw, "sigmoid"), (7, cw, 1, cw, cw, "sigmoid"),
                 (8, 0, 1, d, cw, "sigmoid"), (8, cw, 1, d + cw, cw, "sigmoid"),
                 (9, 0, 2, 0, LR_PAD, "plain")]
    lat_widths = [NA_WIDTH, NA_WIDTH, NA_WIDTH, GLA_DK, GLA_DK, GLA_DV, GLA_DV, d, d, LR_PAD]
    tm = 512
    per_b = s // tm
    q_na, k_na, v_na, gq, gk, gv, sog, sgn, sgg, lr = _inproj(
        x.reshape(n, d), mod3, lambda i: i // per_b, norm_attn_g[0], [w_a, w_g, w_lr], lat_specs, lat_widths, tm,
        rope=_rope_tables(s), name="inproj_lat")

    ctx_specs = [(0, 0, 0, o_k, cw, "plain"), (1, 0, 0, o_v, cw, "plain"), (2, 0, 0, o_gk, cw, "plain"),
                 (3, 0, 0, o_gv, cw, "plain"), (3, cw, 0, o_gv + cw, cw, "plain"),
                 (4, 0, 1, 0, LR_PAD, "plain")]
    ctx_widths = [NA_WIDTH, NA_WIDTH, GLA_DK, GLA_DV, LR_PAD]
    kc_na, vc_na, ck, cv, clr = _inproj(
        ctx.reshape(b * ctx_len, d), mod3, lambda i: b, norm_attn_g[0], [w_a, w_lr], ctx_specs, ctx_widths,
        ctx_len, name="inproj_ctx")

    def lat3(a):
        return a.reshape(b, s, a.shape[-1])

    def ctx3(a):
        return a.reshape(b, ctx_len, a.shape[-1])

    o_na = _na_attention(lat3(q_na), lat3(k_na), lat3(v_na), ctx3(kc_na), ctx3(vc_na),
                         _na_bias_table(na_rpb[0]))

    wa = jnp.zeros((2, LR_PAD, GLA_DK), F32)
    wa = wa.at[0, :GLA_GATE_RANK].set(w_gla_a2[0, 0]).at[1, GLA_GATE_RANK:2 * GLA_GATE_RANK].set(w_gla_a2[0, 1])
    gn = jnp.tile(gla_norm_g[0], GLA_HEADS).reshape(1, GLA_DV)
    o_gla = _gla(lat3(gq), lat3(gk), lat3(gv), lat3(lr), lat3(sog), ctx3(ck), ctx3(cv), ctx3(clr),
                 wa.astype(BF16), b_gla_a2[0], gn)

    wr = jnp.zeros((d, ROUTE_LANES), F32)
    wr = wr.at[:, :N_GROUPS].set(w_group[0]).at[:, EXPERT_LANE0:EXPERT_LANE0 + N_EXPERTS].set(w_expert[0])
    wr_hi = wr.astype(BF16)
    br = jnp.zeros((1, ROUTE_LANES), F32)
    br = br.at[0, :N_GROUPS].set(b_group[0]).at[0, EXPERT_LANE0:EXPERT_LANE0 + N_EXPERTS].set(b_expert[0])
    h, f_lat, route, route_t, counts = _merge_route(
        o_na.reshape(n, NA_WIDTH), o_gla.reshape(n, GLA_DV), sgn, sgg, x.reshape(n, d), mod3, norm_ffn_g[0].reshape(1, d),
        w_na_o[0].astype(BF16), w_gla_o[0].astype(BF16), w_out[0].astype(BF16),
        jnp.concatenate([wr_hi, (wr - wr_hi.astype(F32)).astype(BF16)], axis=1), br, s, tm)

    n_tiles = n // tm
    cnt = counts[:, 0, EXPERT_LANE0:EXPERT_LANE0 + N_EXPERTS].astype(jnp.int32)
    cnt_al = (cnt + SEG_ALIGN - 1) // SEG_ALIGN * SEG_ALIGN
    lstart = jnp.cumsum(cnt_al, axis=1) - cnt_al
    total = jnp.sum(cnt_al, axis=0)
    padded = (total + MOE_ROWS - 1) // MOE_ROWS * MOE_ROWS
    pend = jnp.cumsum(padded)
    pstart = pend - padded
    seg_dst = pstart[None, :] + jnp.cumsum(cnt_al, axis=0) - cnt_al
    tile_rows = jnp.sum(cnt_al, axis=1, keepdims=True)
    seg = jnp.concatenate([cnt_al, lstart, seg_dst, tile_rows, jnp.zeros_like(cnt)[:, 1:]], axis=1)
    seg = seg.reshape(n_tiles, 1, 4 * N_EXPERTS)
    n_used = (pend[-1] // MOE_ROWS).reshape(1)
    tail = jnp.concatenate([pstart + total, padded - total, n_used, jnp.zeros((2 * N_EXPERTS - 1,), jnp.int32)])
    tail = tail.reshape(1, 1, 4 * N_EXPERTS)
    seg_pad = n_tiles * N_EXPERTS * (SEG_ALIGN - 1)
    n_pad = (2 * n + seg_pad + N_EXPERTS * (MOE_ROWS - SEG_ALIGN) + MOE_ROWS - 1) // MOE_ROWS * MOE_ROWS
    n_blk = n_pad // MOE_ROWS
    rows_l = (2 * tm + N_EXPERTS * (SEG_ALIGN - 1) + MXU_DIM - 1) // MXU_DIM * MXU_DIM
    blk_start = jnp.arange(n_blk, dtype=jnp.int32) * MOE_ROWS
    blk_expert = jnp.minimum(jnp.sum(blk_start[:, None] >= pend[None, :], axis=-1), N_EXPERTS - 1)

    xs = _dispatch(seg, tail, route_t, f_lat, n_pad, rows_l, tm)
    ys = _experts(blk_expert.astype(jnp.int32), n_used.astype(jnp.int32), xs,
                  w_exp_gate[0], w_exp_up[0], w_exp_down[0])
    out = _combine(seg, route, h, mod3, final_norm_g.reshape(1, d), ys, rows_l, s, tm)
    return out.reshape(b, s, d)
```

```python
import functools

import jax
import jax.numpy as jnp
import numpy as np
from jax import lax
from jax.experimental import pallas as pl
from jax.experimental.pallas import tpu as pltpu

F32 = jnp.float32
BF16 = jnp.bfloat16

D_MODEL = 1024
GRID_W = 64
NA_HEADS = 8
NA_HEAD_DIM = 64
NA_WIDTH = NA_HEADS * NA_HEAD_DIM
NA_WIN_H = 8
NA_WIN_W = 16
GLA_HEADS = 4
GLA_DK = D_MODEL // 2
GLA_DV = D_MODEL
GLA_HK = GLA_DK // GLA_HEADS
GLA_HV = GLA_DV // GLA_HEADS
GLA_GATE_RANK = 16
GLA_GATE_NORM = 16.0
GLA_CHUNK = 64
GLA_GROUP = 256
ROPE_BASE = 10000.0
N_GROUPS = 4
EXPERTS_PER_GROUP = 8
N_EXPERTS = N_GROUPS * EXPERTS_PER_GROUP
D_EXPERT = D_MODEL // 2
EPS = 1e-6
NEG_INF = -1e30
LOG2E = 1.4426950408889634

LANES = 128
MXU_DIM = 256
LR_PAD = LANES
ROUTE_LANES = LANES
EXPERT_LANE0 = N_GROUPS
ROUTE_ROWS = 48
MOE_ROWS = 512
SEG_ALIGN = 8
PIECE_SIZES = (8, 16, 32, 64)
PIECE_SLOTS = (32, 32, 32, 24)
MAX_BIG_PER_RUN = 8
PIECE_HEAD = 8
PIECE_TABLE = 256
WAIT_SIZES = (1024, 512, 256, 128, 64, 32, 16, 8)
TAIL_SIZES = (256, 128, 64, 32, 16, 8)
VMEM_LIMIT = 56 * 1024 * 1024

_NT = (((1,), (1,)), ((), ()))
_TN = (((0,), (0,)), ((), ()))


def _params(sem, vmem=VMEM_LIMIT):
    return pltpu.CompilerParams(dimension_semantics=sem, vmem_limit_bytes=vmem)


def _mod_kernel(c_ref, w_ref, b_ref, o_ref):
    c = c_ref[...]
    s = c * jax.nn.sigmoid(c)
    o_ref[...] = jnp.dot(s.astype(BF16), w_ref[...].astype(BF16),
                         preferred_element_type=F32) + b_ref[...]


def _modulation(cs, w_mod, b_mod):
    rows, d = cs.shape
    n = w_mod.shape[1]
    bn = 1024
    return pl.pallas_call(
        _mod_kernel,
        grid=(n // bn,),
        in_specs=[pl.BlockSpec((rows, d), lambda j: (0, 0)),
                  pl.BlockSpec((d, bn), lambda j: (0, j)),
                  pl.BlockSpec((1, bn), lambda j: (0, j))],
        out_specs=pl.BlockSpec((rows, bn), lambda j: (0, j)),
        out_shape=jax.ShapeDtypeStruct((rows, n), F32),
        compiler_params=_params(("arbitrary",)),
        name="mod",
    )(cs, w_mod, b_mod.reshape(1, n))


def _inproj_kernel(specs, use_rope, n_w, n_out, x_ref, mod_ref, g_ref, *rest):
    if use_rope:
        cos_ref, sin_ref = rest[:2]
        rest = rest[2:]
    w_refs = rest[:n_w]
    outs = rest[n_w:n_w + n_out]
    a_scr = rest[n_w + n_out]
    x = x_ref[...]
    ms = jnp.mean(x * x, axis=-1, keepdims=True)
    a = x * lax.rsqrt(ms + EPS) * g_ref[...] * (1.0 + mod_ref[0, 1:2, :]) + mod_ref[0, 0:1, :]
    a_scr[...] = a.astype(BF16)
    for (oi, oc, wi, wc, width, kind) in specs:
        acc = jnp.dot(a_scr[...], w_refs[wi][:, wc:wc + width], preferred_element_type=F32)
        if kind in ("rope", "rope_scaled"):
            if kind == "rope_scaled":
                acc = acc * (GLA_HK ** -0.5)
            lane = lax.broadcasted_iota(jnp.int32, acc.shape, 1)
            rot = jnp.where((lane & 63) < 32,
                            pltpu.roll(acc, width - 32, axis=1),
                            pltpu.roll(acc, 32, axis=1))
            acc = acc * cos_ref[...] + rot * sin_ref[...]
        elif kind == "silu":
            acc = acc * jax.nn.sigmoid(acc)
        elif kind == "sigmoid":
            acc = jax.nn.sigmoid(acc)
        outs[oi][:, oc:oc + width] = acc.astype(outs[oi].dtype)


def _inproj(x2d, mod3, mod_row_fn, g, weights, specs, out_widths, tm, rope=None, name="inproj"):
    n, d = x2d.shape
    use_rope = rope is not None
    in_specs = [pl.BlockSpec((tm, d), lambda i: (i, 0)),
                pl.BlockSpec((1, 6, d), lambda i: (mod_row_fn(i), 0, 0)),
                pl.BlockSpec((1, d), lambda i: (0, 0))]
    args = [x2d, mod3, g.reshape(1, d)]
    if use_rope:
        cos, sin = rope
        nb = cos.shape[0] // tm
        in_specs += [pl.BlockSpec((tm, cos.shape[1]), lambda i: (i % nb, 0)),
                     pl.BlockSpec((tm, cos.shape[1]), lambda i: (i % nb, 0))]
        args += [cos, sin]
    in_specs += [pl.BlockSpec((d, cols), lambda i: (0, 0), pipeline_mode=pl.Buffered(1)) for _, cols in weights]
    args += [w for w, _ in weights]
    out_specs = [pl.BlockSpec((tm, w), lambda i: (i, 0)) for w in out_widths]
    out_shape = [jax.ShapeDtypeStruct((n, w), BF16) for w in out_widths]
    return pl.pallas_call(
        functools.partial(_inproj_kernel, specs, use_rope, len(weights), len(out_widths)),
        grid=(n // tm,),
        in_specs=in_specs,
        out_specs=out_specs,
        out_shape=out_shape,
        scratch_shapes=[pltpu.VMEM((tm, d), BF16)],
        compiler_params=_params(("arbitrary",)),
        name=name,
    )(*args)


def _na_kernel(rows_per_step, n_rows, q_ref, k_ref, v_ref, kc_ref, vc_ref, bias_ref, o_ref, sw_scr, sc_scr):
    j = pl.program_id(1)
    lane = lax.broadcasted_iota(jnp.int32, (GRID_W, LANES), 1)
    first_head = lane < NA_HEAD_DIM
    scale = NA_HEAD_DIM ** -0.5
    win_keys = NA_WIN_H * GRID_W
    pairs = range(NA_HEADS // 2)
    lanes = [slice(p * LANES, (p + 1) * LANES) for p in pairs]
    stack = 2 * GRID_W

    def window(rr):
        r = j * rows_per_step + rr
        rs = jnp.clip(r - NA_WIN_H // 2, 0, n_rows - NA_WIN_H)
        return r - rs, pl.multiple_of(rr * GRID_W, GRID_W), pl.multiple_of(rs * GRID_W, GRID_W)

    def scores(rr, slot):
        _, t0, ks = window(rr)
        for p in pairs:
            qp = q_ref[0, pl.ds(t0, GRID_W), lanes[p]] * scale
            zero = jnp.zeros_like(qp)
            qs = jnp.concatenate([jnp.where(first_head, qp, zero), jnp.where(first_head, zero, qp)], axis=0)
            sw_scr[slot, p * stack:(p + 1) * stack, :] = lax.dot_general(
                qs, k_ref[0, pl.ds(ks, win_keys), lanes[p]], _NT, preferred_element_type=F32)
            sc_scr[slot, p * stack:(p + 1) * stack, :] = lax.dot_general(
                qs, kc_ref[0, :, lanes[p]], _NT, preferred_element_type=F32)

    def attend(rr, slot):
        d, t0, ks = window(rr)
        pws, pcs, dens = [], [], []
        for p in pairs:
            bias = jnp.concatenate([bias_ref[(NA_WIN_H - 1) - d + 2 * i, p] for i in range(NA_WIN_H // 2)], axis=1)
            sw = sw_scr[slot, p * stack:(p + 1) * stack, :] * LOG2E + bias
            sc = sc_scr[slot, p * stack:(p + 1) * stack, :] * LOG2E
            m = jnp.maximum(jnp.max(sw, axis=-1, keepdims=True), jnp.max(sc, axis=-1, keepdims=True))
            pw = jnp.exp2(sw - m)
            pc = jnp.exp2(sc - m)
            dens.append(jnp.sum(pw, axis=-1, keepdims=True) + jnp.sum(pc, axis=-1, keepdims=True))
            pws.append(pw.astype(BF16))
            pcs.append(pc.astype(BF16))
        for p in pairs:
            o2 = (jnp.dot(pws[p], v_ref[0, pl.ds(ks, win_keys), lanes[p]], preferred_element_type=F32)
                  + jnp.dot(pcs[p], vc_ref[0, :, lanes[p]], preferred_element_type=F32)) / dens[p]
            o_ref[0, pl.ds(t0, GRID_W), lanes[p]] = jnp.where(first_head, o2[:GRID_W],
                                                             o2[GRID_W:]).astype(o_ref.dtype)

    scores(0, 0)

    def two_rows(k, carry):
        scores(2 * k + 1, 1)
        attend(2 * k, 0)
        scores(2 * k + 2, 0)
        attend(2 * k + 1, 1)
        return carry

    lax.fori_loop(0, rows_per_step // 2 - 1, two_rows, 0)
    scores(rows_per_step - 1, 1)
    attend(rows_per_step - 2, 0)
    attend(rows_per_step - 1, 1)


def _na_bias_table(rpb):
    n_heads, n_dr, n_dc = rpb.shape
    c = np.arange(GRID_W)
    cstart = np.clip(c - NA_WIN_W // 2, 0, GRID_W - NA_WIN_W)
    kc = np.arange(GRID_W)
    valid = (kc[None, :] >= cstart[:, None]) & (kc[None, :] < cstart[:, None] + NA_WIN_W)
    col_idx = np.clip(kc[None, :] - c[:, None] + NA_WIN_W - 1, 0, n_dc - 1)
    pick = np.zeros((n_dc, GRID_W * GRID_W), np.float32)
    pick[col_idx.reshape(-1), np.arange(GRID_W * GRID_W)] = 1.0
    t = jnp.dot(rpb.reshape(n_heads * n_dr, n_dc).astype(F32), jnp.asarray(pick),
                precision=lax.Precision.HIGHEST).reshape(n_heads, n_dr, GRID_W, GRID_W)
    t = jnp.where(jnp.asarray(valid)[None, None], t * LOG2E, NEG_INF)
    t2 = jnp.concatenate([t[:, :-1], t[:, 1:]], axis=-1)
    return t2.transpose(1, 0, 2, 3).reshape(n_dr - 1, n_heads // 2, 2 * GRID_W, 2 * GRID_W)


def _na_attention(q, k, v, kc, vc, bias, rows_per_step=16):
    b, s, w = q.shape
    n_rows = s // GRID_W
    ctx_len = kc.shape[1]
    blk = rows_per_step * GRID_W
    return pl.pallas_call(
        functools.partial(_na_kernel, rows_per_step, n_rows),
        grid=(b, n_rows // rows_per_step),
        in_specs=[pl.BlockSpec((1, blk, w), lambda bi, j: (bi, j, 0)),
                  pl.BlockSpec((1, s, w), lambda bi, j: (bi, 0, 0)),
                  pl.BlockSpec((1, s, w), lambda bi, j: (bi, 0, 0)),
                  pl.BlockSpec((1, ctx_len, w), lambda bi, j: (bi, 0, 0)),
                  pl.BlockSpec((1, ctx_len, w), lambda bi, j: (bi, 0, 0)),
                  pl.BlockSpec(bias.shape, lambda bi, j: (0, 0, 0, 0), pipeline_mode=pl.Buffered(1))],
        out_specs=pl.BlockSpec((1, blk, w), lambda bi, j: (bi, j, 0)),
        out_shape=jax.ShapeDtypeStruct((b, s, w), BF16),
        scratch_shapes=[pltpu.VMEM((2, NA_HEADS * GRID_W, NA_WIN_H * GRID_W), F32),
                        pltpu.VMEM((2, NA_HEADS * GRID_W, ctx_len), F32)],
        compiler_params=_params(("arbitrary", "arbitrary")),
        name="na_attn",
    )(q, k, v, kc, vc, bias)


def _gla_kernel(n_groups, gq_ref, gk_ref, gv_ref, lr_ref, sog_ref, ck_ref, cv_ref, clr_ref,
                wa_ref, ba_ref, gn_ref, o_ref, st_scr, of_scr, qe_scr, kd_scr, ckd_scr, dec_scr, cdec_scr,
                qk_a):
    c_len = GLA_CHUNK
    grp = GLA_GROUP
    cpg = grp // c_len
    row = lax.broadcasted_iota(jnp.int32, (grp, grp), 0)
    col = lax.broadcasted_iota(jnp.int32, (grp, grp), 1)
    same_chunk = (row // c_len) == (col // c_len)
    masks = (same_chunk & (col <= row), same_chunk & (col >= row))

    dirs = range(2)

    def gates(lr_rows, k_rows):
        zs = [jnp.dot(lr_rows, wa_ref[t], preferred_element_type=F32) + ba_ref[t:t + 1, :] for t in dirs]
        gs = [(jnp.minimum(z, 0.0) * LOG2E - jnp.log2(1.0 + jnp.exp2(jnp.abs(z) * (-LOG2E)))) * (1.0 / GLA_GATE_NORM)
              for z in zs]
        g1s = [g.astype(BF16) for g in gs]
        g2s = [(g - g1.astype(F32)).astype(BF16) for g, g1 in zip(gs, g1s)]
        tris = [masks[t].astype(BF16) for t in dirs]
        bs = [jnp.dot(tris[t], g1s[t], preferred_element_type=F32)
              + jnp.dot(tris[t], g2s[t], preferred_element_type=F32) for t in dirs]
        out = []
        for t in dirs:
            edge = c_len - 1 if t == 0 else 0
            decs = [jnp.exp2(bs[t][j * c_len + edge:j * c_len + edge + 1, :]) for j in range(cpg)]
            ke = k_rows * jnp.exp2(-bs[t])
            kd = (ke * jnp.concatenate([jnp.broadcast_to(v, (c_len, v.shape[1])) for v in decs],
                                       axis=0)).astype(BF16)
            dec = jnp.concatenate(decs + [jnp.ones_like(v) for v in decs], axis=0)
            out.append((bs[t], ke.astype(BF16), kd, dec))
        return out

    for t, (_, _, kd, dec) in enumerate(gates(clr_ref[0], ck_ref[0].astype(F32))):
        ckd_scr[t] = kd
        cdec_scr[t] = dec

    def group_gates(gi, qk_buf):
        rows = pl.ds(pl.multiple_of(gi * grp, grp), grp)
        qf = gq_ref[0, rows, :].astype(F32)
        for t, (b, ke, kd, dec) in enumerate(gates(lr_ref[0, rows, :], gk_ref[0, rows, :].astype(F32))):
            qe = (qf * jnp.exp2(b)).astype(BF16)
            qk_buf[t] = qe
            qk_buf[2 + t] = ke
            qe_scr[t, rows, :] = qe
            kd_scr[t, rows, :] = kd
            dec_scr[t, gi] = dec

    def group_attend(gi, qk_buf):
        rows = pl.ds(pl.multiple_of(gi * grp, grp), grp)
        heads = range(GLA_HEADS)
        kls = [slice(h * GLA_HK, (h + 1) * GLA_HK) for h in heads]
        raw = [[lax.dot_general(qk_buf[t, :, kls[h]], qk_buf[2 + t, :, kls[h]], _NT, preferred_element_type=F32)
                for t in dirs] for h in heads]
        atts = [(jnp.where(masks[0], raw[h][0], 0.0) + jnp.where(masks[1], raw[h][1], 0.0)).astype(BF16)
                for h in heads]
        for h in heads:
            vl = slice(h * GLA_HV, (h + 1) * GLA_HV)
            of_scr[rows, vl] = jnp.dot(atts[h], gv_ref[0, rows, vl], preferred_element_type=F32)

    def group_body(gi, carry):
        group_gates(gi, qk_a)
        group_attend(gi, qk_a)
        return carry

    lax.fori_loop(0, n_groups, group_body, 0)

    def state_step(h, v_rows, kd_rows, dec_row):
        kl = slice(h * GLA_HK, (h + 1) * GLA_HK)
        upd = lax.dot_general(v_rows, kd_rows[:, kl], _TN, preferred_element_type=F32)
        st_scr[h] = st_scr[h] * dec_row[:, kl] + upd

    def scan(direction):
        st_scr[...] = jnp.zeros_like(st_scr)
        order = range(cpg) if direction == 0 else range(cpg - 1, -1, -1)
        for j in order:
            rows = slice(j * c_len, (j + 1) * c_len)
            for h in range(GLA_HEADS):
                state_step(h, cv_ref[0, rows, h * GLA_HV:(h + 1) * GLA_HV],
                           ckd_scr[direction, rows, :], cdec_scr[direction, j:j + 1, :])

        def group_scan(i, carry):
            gi = i if direction == 0 else n_groups - 1 - i
            for j in order:
                rows = pl.ds(pl.multiple_of(gi * grp + j * c_len, c_len), c_len)
                qe_rows = qe_scr[direction, rows, :]
                kd_rows = kd_scr[direction, rows, :]
                dec_row = dec_scr[direction, gi, j:j + 1, :]
                for h in range(GLA_HEADS):
                    kl = slice(h * GLA_HK, (h + 1) * GLA_HK)
                    vl = slice(h * GLA_HV, (h + 1) * GLA_HV)
                    o_t = of_scr[rows, vl] + lax.dot_general(qe_rows[:, kl], st_scr[h].astype(BF16), _NT,
                                                             preferred_element_type=F32)
                    if direction == 0:
                        of_scr[rows, vl] = o_t
                    else:
                        ms = jnp.mean(o_t * o_t, axis=-1, keepdims=True)
                        o_n = o_t * lax.rsqrt(ms + EPS) * gn_ref[:, vl] * sog_ref[0, rows, vl].astype(F32)
                        o_ref[0, rows, vl] = o_n.astype(o_ref.dtype)
                    state_step(h, gv_ref[0, rows, vl], kd_rows, dec_row)
            return carry

        lax.fori_loop(0, n_groups, group_scan, 0)

    scan(0)
    scan(1)


def _gla(gq, gk, gv, lr, sog, ck, cv, clr, wa, ba, gn):
    b, s, _ = gq.shape
    ctx_len = ck.shape[1]
    assert ctx_len == GLA_GROUP and s % GLA_GROUP == 0
    n_groups = s // GLA_GROUP
    cpg = GLA_GROUP // GLA_CHUNK

    def per_batch(t, w):
        return pl.BlockSpec((1, t, w), lambda bi: (bi, 0, 0))

    return pl.pallas_call(
        functools.partial(_gla_kernel, n_groups),
        grid=(b,),
        in_specs=[per_batch(s, GLA_DK), per_batch(s, GLA_DK), per_batch(s, GLA_DV),
                  per_batch(s, LR_PAD), per_batch(s, GLA_DV),
                  per_batch(ctx_len, GLA_DK), per_batch(ctx_len, GLA_DV), per_batch(ctx_len, LR_PAD),
                  pl.BlockSpec(wa.shape, lambda bi: (0, 0, 0)),
                  pl.BlockSpec(ba.shape, lambda bi: (0, 0)),
                  pl.BlockSpec(gn.shape, lambda bi: (0, 0))],
        out_specs=per_batch(s, GLA_DV),
        out_shape=jax.ShapeDtypeStruct((b, s, GLA_DV), BF16),
        scratch_shapes=[pltpu.VMEM((GLA_HEADS, GLA_HV, GLA_HK), F32),
                        pltpu.VMEM((s, GLA_DV), F32),
                        pltpu.VMEM((2, s, GLA_DK), BF16),
                        pltpu.VMEM((2, s, GLA_DK), BF16),
                        pltpu.VMEM((2, ctx_len, GLA_DK), BF16),
                        pltpu.VMEM((2, n_groups, 2 * cpg, GLA_DK), F32),
                        pltpu.VMEM((2, 2 * cpg, GLA_DK), F32),
                        pltpu.VMEM((4, GLA_GROUP, GLA_DK), BF16)],
        compiler_params=_params(("arbitrary",), vmem=60 * 1024 * 1024),
        name="gla",
    )(gq, gk, gv, lr, sog, ck, cv, clr, wa, ba, gn)


def _merge_kernel(ona_ref, ogla_ref, sgn_ref, sgg_ref, x_ref, mod_ref, gffn_ref,
                  wna_ref, wgla_ref, wout_ref, wr_ref, br_ref,
                  h_ref, f_ref, route_ref, route_t_ref, cnt_ref):
    rows = x_ref.shape[0]
    y1 = (sgn_ref[...].astype(F32) * jnp.dot(ona_ref[...], wna_ref[...], preferred_element_type=F32)
          + sgg_ref[...].astype(F32) * jnp.dot(ogla_ref[...], wgla_ref[...], preferred_element_type=F32))
    y = jnp.dot(y1.astype(BF16), wout_ref[...], preferred_element_type=F32)
    h = x_ref[...] + mod_ref[0, 2:3, :] * y
    h_ref[...] = h
    ms = jnp.mean(h * h, axis=-1, keepdims=True)
    f = h * lax.rsqrt(ms + EPS) * gffn_ref[...] * (1.0 + mod_ref[0, 4:5, :]) + mod_ref[0, 3:4, :]
    f_ref[...] = f.astype(f_ref.dtype)

    f_hi = f.astype(BF16)
    f_lo = (f - f_hi.astype(F32)).astype(BF16)
    both = jnp.dot(f_hi, wr_ref[...], preferred_element_type=F32)
    logits = (both[:, :ROUTE_LANES] + both[:, ROUTE_LANES:]
              + jnp.dot(f_lo, wr_ref[:, :ROUTE_LANES], preferred_element_type=F32)) + br_ref[...]

    lt = logits.T[0:ROUTE_ROWS, :]
    er = lax.broadcasted_iota(jnp.int32, (ROUTE_ROWS, rows), 0)
    big = jnp.int32(ROUTE_ROWS)

    def first_argmax(vals):
        m = jnp.max(vals, axis=0, keepdims=True)
        idx = jnp.min(jnp.where(vals == m, er, big), axis=0, keepdims=True)
        return m, idx

    is_grp = er < N_GROUPS
    lg = jnp.where(is_grp, lt, -jnp.inf)
    mg, grp = first_argmax(lg)
    p_grp = 1.0 / jnp.sum(jnp.where(is_grp, jnp.exp(lg - mg), 0.0), axis=0, keepdims=True)
    lo = EXPERT_LANE0 + grp * EXPERTS_PER_GROUP
    in_grp = (er >= lo) & (er < lo + EXPERTS_PER_GROUP)
    le = jnp.where(in_grp, lt, -jnp.inf)
    m1, i1 = first_argmax(le)
    m2, i2 = first_argmax(jnp.where(er == i1, -jnp.inf, le))
    t = jnp.exp(m2 - m1)
    w1 = p_grp / (1.0 + t)
    w2 = p_grp * t / (1.0 + t)

    hot1 = er == i1
    hot2 = er == i2
    onehot = (hot1 | hot2).astype(BF16)
    s_i = lax.broadcasted_iota(jnp.int32, (rows, rows), 0)
    t_i = lax.broadcasted_iota(jnp.int32, (rows, rows), 1)
    rank = jnp.dot(onehot, (s_i < t_i).astype(BF16), preferred_element_type=F32)
    cnt = jnp.sum(onehot.astype(F32), axis=1, keepdims=True)
    cnt_al = jnp.floor((cnt + (SEG_ALIGN - 1.0)) * (1.0 / SEG_ALIGN)) * SEG_ALIGN
    e_i = lax.broadcasted_iota(jnp.int32, (ROUTE_ROWS, ROUTE_ROWS), 0)
    e_j = lax.broadcasted_iota(jnp.int32, (ROUTE_ROWS, ROUTE_ROWS), 1)
    lstart = jnp.dot((e_j < e_i).astype(BF16), jnp.broadcast_to(cnt_al, (ROUTE_ROWS, LANES)).astype(BF16),
                     preferred_element_type=F32)[:, 0:1]
    pos = rank + lstart
    pos1 = jnp.sum(jnp.where(hot1, pos, 0.0), axis=0, keepdims=True)
    pos2 = jnp.sum(jnp.where(hot2, pos, 0.0), axis=0, keepdims=True)

    e1 = (i1 - EXPERT_LANE0).astype(F32)
    e2 = (i2 - EXPERT_LANE0).astype(F32)
    fields = jnp.concatenate([e1, e2, w1, w2, pos1, pos2, jnp.zeros((2, rows), F32)], axis=0)
    route_t_ref[0] = fields
    route_ref[...] = jnp.concatenate([fields, jnp.zeros((ROUTE_LANES - 8, rows), F32)], axis=0).T
    cnt_rows = jnp.concatenate([jnp.broadcast_to(cnt, (ROUTE_ROWS, ROUTE_LANES)),
                                jnp.zeros((ROUTE_LANES - ROUTE_ROWS, ROUTE_LANES), F32)], axis=0)
    cnt_ref[0] = cnt_rows.T[0:1, :]


def _merge_route(ona, ogla, sgn, sgg, x2d, mod3, gffn, wna, wgla, wout, wr, br, tokens_per_batch, tm):
    n, d = x2d.shape
    per_b = tokens_per_batch // tm

    def tile(w):
        return pl.BlockSpec((tm, w), lambda i: (i, 0))

    def whole(a):
        return pl.BlockSpec(a.shape, lambda i: (0,) * a.ndim)

    return pl.pallas_call(
        _merge_kernel,
        grid=(n // tm,),
        in_specs=[tile(NA_WIDTH), tile(GLA_DV), tile(d), tile(d), tile(d),
                  pl.BlockSpec((1, 6, d), lambda i: (i // per_b, 0, 0)),
                  whole(gffn), whole(wna), whole(wgla), whole(wout), whole(wr), whole(br)],
        out_specs=[tile(d), tile(d), tile(ROUTE_LANES),
                   pl.BlockSpec((1, 8, tm), lambda i: (i, 0, 0)),
                   pl.BlockSpec((1, 1, ROUTE_LANES), lambda i: (i, 0, 0))],
        out_shape=[jax.ShapeDtypeStruct((n, d), F32),
                   jax.ShapeDtypeStruct((n, d), BF16),
                   jax.ShapeDtypeStruct((n, ROUTE_LANES), F32),
                   jax.ShapeDtypeStruct((n // tm, 8, tm), F32),
                   jax.ShapeDtypeStruct((n // tm, 1, ROUTE_LANES), F32)],
        compiler_params=_params(("arbitrary",)),
        name="merge_route",
    )(ona, ogla, sgn, sgg, x2d, mod3, gffn, wna, wgla, wout, wr, br)


HI_MASK = -65536


def _pack_rows(a):
    half = a.shape[1] // 2
    hi = lax.bitcast_convert_type(a[:, :half], jnp.int32)
    lo = lax.shift_right_logical(lax.bitcast_convert_type(a[:, half:], jnp.int32), 16)
    return hi | lo


def _unpack_rows(w):
    hi = lax.bitcast_convert_type(w & HI_MASK, F32).astype(BF16)
    lo = lax.bitcast_convert_type(lax.shift_left(w, 16), F32).astype(BF16)
    return hi, lo


def _piece_table(cnt_al, lstart, seg_dst):
    n_tiles = cnt_al.shape[0]

    def compact(mask, local, glob, slots):
        pos = jnp.cumsum(mask.astype(jnp.int32), axis=1) - 1
        hot = mask[:, :, None] & (pos[:, :, None] == jnp.arange(slots, dtype=jnp.int32))
        pick = lambda v: jnp.sum(jnp.where(hot, v[:, :, None], 0), axis=1)
        return jnp.sum(mask, axis=1).astype(jnp.int32), pick(local), pick(glob)

    counts, lists = [], []
    for size, slots in zip(PIECE_SIZES[:-1], PIECE_SLOTS[:-1]):
        off = cnt_al & (size - 1)
        n, lo, gl = compact((cnt_al & size) != 0, lstart + off, seg_dst + off, slots)
        counts.append(n)
        lists += [lo, gl]
    big = PIECE_SIZES[-1]
    k = jnp.arange(MAX_BIG_PER_RUN, dtype=jnp.int32)
    off = (cnt_al & (big - 1))[:, :, None] + big * k
    mask = k < (cnt_al // big)[:, :, None]
    flat = lambda a: a.reshape(n_tiles, -1)
    n, lo, gl = compact(flat(mask), flat(lstart[:, :, None] + off), flat(seg_dst[:, :, None] + off), PIECE_SLOTS[-1])
    counts.append(n)
    lists += [lo, gl]
    head = jnp.stack(counts + [jnp.sum(cnt_al, axis=1)], axis=1)
    head = jnp.pad(head, ((0, 0), (0, PIECE_HEAD - head.shape[1])))
    table = jnp.concatenate([head] + lists, axis=1)
    table = jnp.pad(table, ((0, 0), (0, PIECE_TABLE - table.shape[1])))
    return table.reshape(n_tiles, 1, PIECE_TABLE).astype(jnp.int32)


def _segment_copies(seg_ref, make_copy, action):
    if action == "wait":
        total = seg_ref[0, 0, len(PIECE_SIZES)]
        for size in WAIT_SIZES:
            @pl.when((total & size) != 0)
            def _():
                make_copy(0, 0, size).wait()
        return

    base = PIECE_HEAD
    for c, (size, slots) in enumerate(zip(PIECE_SIZES, PIECE_SLOTS)):
        def piece(k, carry, size=size, lo=base, gl=base + slots):
            make_copy(pl.multiple_of(seg_ref[0, 0, lo + k], SEG_ALIGN),
                      pl.multiple_of(seg_ref[0, 0, gl + k], SEG_ALIGN), size).start()
            return carry

        lax.fori_loop(0, seg_ref[0, 0, c], piece, 0)
        base += 2 * slots


def _dispatch_kernel(seg_ref, seg_prev_ref, tail_ref, lpos_ref, f_ref, xs_ref, sbuf, zbuf, sems):
    i = pl.program_id(0)
    last = pl.num_programs(0) - 1
    tm = f_ref.shape[0]
    rows_l = sbuf.shape[1]
    slot = i % 2
    fields = lpos_ref[0]
    lp = fields.astype(jnp.int32)
    half = f_ref.shape[1] // 2
    row = lax.broadcasted_iota(jnp.int32, (rows_l, tm), 0)
    first = row == lp[4:5, :]
    second = row == lp[5:6, :]
    sbuf[slot, :, :half] = _pack_rows(jnp.dot((first | second).astype(BF16), f_ref[...], preferred_element_type=F32))
    w_sorted = jnp.sum(jnp.where(first, fields[2:3, :], 0.0) + jnp.where(second, fields[3:4, :], 0.0),
                       axis=1, keepdims=True)
    sbuf[slot, :, half:] = lax.bitcast_convert_type(jnp.broadcast_to(w_sorted, (rows_l, LANES)), jnp.int32)

    def copier(s):
        def copy(local, glob, size):
            return pltpu.make_async_copy(sbuf.at[s, pl.ds(local, size), :], xs_ref.at[pl.ds(glob, size), :],
                                         sems.at[s])
        return copy

    _segment_copies(seg_ref, copier(slot), "start")

    @pl.when(i > 0)
    def _():
        _segment_copies(seg_prev_ref, copier(1 - slot), "wait")

    @pl.when(i == last)
    def _():
        _segment_copies(seg_ref, copier(slot), "wait")
        sem = sems.at[slot]
        zbuf[...] = jnp.zeros_like(zbuf)

        def tail(action):
            def body(e, carry):
                start = tail_ref[0, 0, e]
                length = tail_ref[0, 0, N_EXPERTS + e]
                off = jnp.int32(0)
                for size in TAIL_SIZES:
                    @pl.when((length & size) != 0)
                    def _():
                        cp = pltpu.make_async_copy(zbuf.at[pl.ds(0, size), :],
                                                   xs_ref.at[pl.ds(pl.multiple_of(start + off, SEG_ALIGN), size), :],
                                                   sem)
                        getattr(cp, action)()
                    off = off + (length & size)
                return carry
            lax.fori_loop(0, N_EXPERTS, body, 0)

        tail("start")
        tail("wait")

        def unused(action):
            def body(blk, carry):
                cp = pltpu.make_async_copy(zbuf, xs_ref.at[pl.ds(pl.multiple_of(blk * MOE_ROWS, MOE_ROWS),
                                                                  MOE_ROWS), :], sem)
                getattr(cp, action)()
                return carry
            lax.fori_loop(tail_ref[0, 0, 2 * N_EXPERTS], xs_ref.shape[0] // MOE_ROWS, body, 0)

        unused("start")
        unused("wait")


def _dispatch(seg, tail, lpos_rows, f2d, n_pad, rows_l, tm):
    n, d = f2d.shape
    return pl.pallas_call(
        _dispatch_kernel,
        grid=(n // tm,),
        in_specs=[pl.BlockSpec((1, 1, seg.shape[2]), lambda i: (i, 0, 0), memory_space=pltpu.SMEM),
                  pl.BlockSpec((1, 1, seg.shape[2]), lambda i: (jnp.maximum(i - 1, 0), 0, 0),
                               memory_space=pltpu.SMEM),
                  pl.BlockSpec(tail.shape, lambda i: (0, 0, 0), memory_space=pltpu.SMEM),
                  pl.BlockSpec((1, 8, tm), lambda i: (i, 0, 0)),
                  pl.BlockSpec((tm, d), lambda i: (i, 0))],
        out_specs=pl.BlockSpec(memory_space=pl.ANY),
        out_shape=jax.ShapeDtypeStruct((n_pad, d // 2 + LANES), jnp.int32),
        scratch_shapes=[pltpu.VMEM((2, rows_l, d // 2 + LANES), jnp.int32),
                        pltpu.VMEM((MOE_ROWS, d // 2 + LANES), jnp.int32),
                        pltpu.SemaphoreType.DMA((2,))],
        compiler_params=_params(("arbitrary",)),
        name="dispatch",
    )(seg, seg, tail, lpos_rows, f2d)


def _expert_kernel(be_ref, nu_ref, x_ref, wg_ref, wu_ref, wd_ref, y_ref, wgu_scr, wd_scr):
    i = pl.program_id(0)
    prev = be_ref[jnp.maximum(i - 1, 0)]

    @pl.when((i == 0) | (be_ref[i] != prev))
    def _():
        wgu_scr[:, :D_EXPERT] = wg_ref[0].astype(BF16)
        wgu_scr[:, D_EXPERT:] = wu_ref[0].astype(BF16)
        wd_scr[...] = wd_ref[0].astype(BF16)

    @pl.when(i < nu_ref[0])
    def _():
        half = y_ref.shape[1]
        x = jnp.concatenate(_unpack_rows(x_ref[:, :half]), axis=1)
        gu = jnp.dot(x, wgu_scr[...], preferred_element_type=F32)
        gate = gu[:, :D_EXPERT]
        hdn = gate * jax.nn.sigmoid(gate) * gu[:, D_EXPERT:]
        y = jnp.dot(hdn.astype(BF16), wd_scr[...], preferred_element_type=F32)
        y = y * lax.bitcast_convert_type(x_ref[:, half:half + 1], F32)
        y_ref[...] = _pack_rows(y.astype(BF16).astype(F32))

    @pl.when(i >= nu_ref[0])
    def _():
        y_ref[...] = jnp.zeros_like(y_ref)


def _experts(blk_expert, n_used, xs, wg, wu, wd):
    n_pad, row_words = xs.shape
    half = row_words - LANES
    d = 2 * half
    n_blk = n_pad // MOE_ROWS

    def used_block(i, be, nu):
        return (jnp.maximum(jnp.minimum(i, nu[0] - 1), 0), 0)

    grid_spec = pltpu.PrefetchScalarGridSpec(
        num_scalar_prefetch=2,
        grid=(n_blk,),
        in_specs=[pl.BlockSpec((MOE_ROWS, row_words), used_block),
                  pl.BlockSpec((1, d, D_EXPERT), lambda i, be, nu: (be[i], 0, 0)),
                  pl.BlockSpec((1, d, D_EXPERT), lambda i, be, nu: (be[i], 0, 0)),
                  pl.BlockSpec((1, D_EXPERT, d), lambda i, be, nu: (be[i], 0, 0))],
        out_specs=pl.BlockSpec((MOE_ROWS, half), lambda i, be, nu: (i, 0)),
        scratch_shapes=[pltpu.VMEM((d, 2 * D_EXPERT), BF16),
                        pltpu.VMEM((D_EXPERT, d), BF16)],
    )
    return pl.pallas_call(
        _expert_kernel,
        grid_spec=grid_spec,
        out_shape=jax.ShapeDtypeStruct((n_pad, half), jnp.int32),
        compiler_params=_params(("arbitrary",)),
        name="experts",
    )(blk_expert, n_used, xs, wg, wu, wd)


def _combine_kernel(seg_ref, seg_next_ref, col_ref, h_ref, mod_ref, fg_ref, ys_ref, o_ref, ybuf, sems):
    i = pl.program_id(0)
    tm = h_ref.shape[0]
    rows_l = ybuf.shape[1]
    slot = i % 2

    def copier(s):
        def copy(local, glob, size):
            return pltpu.make_async_copy(ys_ref.at[pl.ds(glob, size), :], ybuf.at[s, pl.ds(local, size), :],
                                         sems.at[s])
        return copy

    @pl.when(i == 0)
    def _():
        ybuf[...] = jnp.zeros_like(ybuf)
        _segment_copies(seg_ref, copier(slot), "start")

    @pl.when(i + 1 < pl.num_programs(0))
    def _():
        _segment_copies(seg_next_ref, copier(1 - slot), "start")

    _segment_copies(seg_ref, copier(slot), "wait")
    y_hi, y_lo = _unpack_rows(ybuf[slot])
    y_all = jnp.concatenate([y_hi, y_lo], axis=1)
    col = lax.broadcasted_iota(jnp.int32, (tm, rows_l), 1)
    info = col_ref[...]
    pick = (col == info[:, 4:5].astype(jnp.int32)) | (col == info[:, 5:6].astype(jnp.int32))
    moe = jnp.dot(pick.astype(BF16), y_all, preferred_element_type=F32)
    h = h_ref[...] + mod_ref[0, 5:6, :] * moe
    ms = jnp.mean(h * h, axis=-1, keepdims=True)
    o_ref[...] = h * lax.rsqrt(ms + EPS) * fg_ref[...]


def _combine(seg, colinfo, h2d, mod3, fg, ys, rows_l, tokens_per_batch, tm):
    n, d = h2d.shape
    per_b = tokens_per_batch // tm
    return pl.pallas_call(
        _combine_kernel,
        grid=(n // tm,),
        in_specs=[pl.BlockSpec((1, 1, seg.shape[2]), lambda i: (i, 0, 0), memory_space=pltpu.SMEM),
                  pl.BlockSpec((1, 1, seg.shape[2]), lambda i: (jnp.minimum(i + 1, n // tm - 1), 0, 0),
                               memory_space=pltpu.SMEM),
                  pl.BlockSpec((tm, colinfo.shape[1]), lambda i: (i, 0)),
                  pl.BlockSpec((tm, d), lambda i: (i, 0)),
                  pl.BlockSpec((1, 6, d), lambda i: (i // per_b, 0, 0)),
                  pl.BlockSpec((1, d), lambda i: (0, 0)),
                  pl.BlockSpec(memory_space=pl.ANY)],
        out_specs=pl.BlockSpec((tm, d), lambda i: (i, 0)),
        out_shape=jax.ShapeDtypeStruct((n, d), F32),
        scratch_shapes=[pltpu.VMEM((2, rows_l, d // 2), jnp.int32), pltpu.SemaphoreType.DMA((2,))],
        compiler_params=_params(("arbitrary",)),
        name="combine",
    )(seg, seg, colinfo, h2d, mod3, fg, ys)


def _rope_tables(s):
    t = np.arange(s)
    pos_r = (t // GRID_W).astype(np.float32)
    pos_c = (t % GRID_W).astype(np.float32)
    nf = GLA_HK // 4
    inv = (np.float32(ROPE_BASE) ** (-np.arange(nf, dtype=np.float32) / np.float32(nf))).astype(np.float32)
    ang_r = pos_r[:, None] * inv
    ang_c = pos_c[:, None] * inv

    def half(ang):
        return (np.concatenate([np.cos(ang), np.cos(ang)], axis=-1),
                np.concatenate([-np.sin(ang), np.sin(ang)], axis=-1))

    cr, sr = half(ang_r)
    cc, sc = half(ang_c)
    cos = np.tile(np.concatenate([cr, cc], axis=-1), (1, GLA_HEADS)).astype(np.float32)
    sin = np.tile(np.concatenate([sr, sc], axis=-1), (1, GLA_HEADS)).astype(np.float32)
    return jnp.asarray(cos), jnp.asarray(sin)


def kernel(x, c, ctx, c_ctx, w_mod, b_mod, norm_attn_g, norm_ffn_g, w_in, w_gla_a2, b_gla_a2, gla_norm_g, na_rpb, w_na_o, w_gla_o, w_out, w_group, b_group, w_expert, b_expert, w_exp_gate, w_exp_up, w_exp_down, final_norm_g):
    b, s, d = x.shape
    ctx_len = ctx.shape[1]
    n = b * s
    assert w_mod.shape[0] == 1, "single-layer block"
    assert s % (GRID_W * NA_WIN_H) == 0 and ctx_len % GLA_CHUNK == 0

    mod_rows = -(-(b + 1) // 8) * 8
    cs = jnp.zeros((mod_rows, d), F32).at[:b].set(c).at[b].set(c_ctx)
    mod3 = _modulation(cs, w_mod[0], b_mod[0]).reshape(mod_rows, 6, d)

    o_q, o_k, o_v = 0, NA_WIDTH, 2 * NA_WIDTH
    o_gq = 3 * NA_WIDTH
    o_gk = o_gq + GLA_DK
    o_gv = o_gk + GLA_DK
    o_og = o_gv + GLA_DV
    o_lr = o_og + GLA_DV
    o_mn = o_lr + 2 * GLA_GATE_RANK
    w_bf = w_in[0].astype(BF16)
    w_a = (w_bf, o_lr)
    w_g = (w_bf[:, o_mn:], 2 * d)
    w_lr = (jnp.pad(w_bf[:, o_lr:o_mn], ((0, 0), (0, LR_PAD - 2 * GLA_GATE_RANK))), LR_PAD)
    cw = 512
    lat_specs = [(0, 0, 0, o_q, cw, "plain"), (1, 0, 0, o_k, cw, "plain"), (2, 0, 0, o_v, cw, "plain"),
                 (3, 0, 0, o_gq, cw, "rope_scaled"), (4, 0, 0, o_gk, cw, "rope"),
                 (5, 0, 0, o_gv, cw, "plain"), (5, cw, 0, o_gv + cw, cw, "plain"),
                 (6, 0, 0, o_og, cw, "silu"), (6, cw, 0, o_og + cw, cw, "silu"),
                 (7, 0, 1, 0, cw, "sigmoid"), (7, cw, 1, cw, cw, "sigmoid"),
                 (8, 0, 1, d, cw, "sigmoid"), (8, cw, 1, d + cw, cw, "sigmoid"),
                 (9, 0, 2, 0, LR_PAD, "plain")]
    lat_widths = [NA_WIDTH, NA_WIDTH, NA_WIDTH, GLA_DK, GLA_DK, GLA_DV, GLA_DV, d, d, LR_PAD]
    tm = 512
    per_b = s // tm
    q_na, k_na, v_na, gq, gk, gv, sog, sgn, sgg, lr = _inproj(
        x.reshape(n, d), mod3, lambda i: i // per_b, norm_attn_g[0], [w_a, w_g, w_lr], lat_specs, lat_widths, tm,
        rope=_rope_tables(s), name="inproj_lat")

    ctx_specs = [(0, 0, 0, o_k, cw, "plain"), (1, 0, 0, o_v, cw, "plain"), (2, 0, 0, o_gk, cw, "plain"),
                 (3, 0, 0, o_gv, cw, "plain"), (3, cw, 0, o_gv + cw, cw, "plain"),
                 (4, 0, 1, 0, LR_PAD, "plain")]
    ctx_widths = [NA_WIDTH, NA_WIDTH, GLA_DK, GLA_DV, LR_PAD]
    kc_na, vc_na, ck, cv, clr = _inproj(
        ctx.reshape(b * ctx_len, d), mod3, lambda i: b, norm_attn_g[0], [w_a, w_lr], ctx_specs, ctx_widths,
        ctx_len, name="inproj_ctx")

    def lat3(a):
        return a.reshape(b, s, a.shape[-1])

    def ctx3(a):
        return a.reshape(b, ctx_len, a.shape[-1])

    o_na = _na_attention(lat3(q_na), lat3(k_na), lat3(v_na), ctx3(kc_na), ctx3(vc_na),
                         _na_bias_table(na_rpb[0]))

    wa = jnp.zeros((2, LR_PAD, GLA_DK), F32)
    wa = wa.at[0, :GLA_GATE_RANK].set(w_gla_a2[0, 0]).at[1, GLA_GATE_RANK:2 * GLA_GATE_RANK].set(w_gla_a2[0, 1])
    gn = jnp.tile(gla_norm_g[0], GLA_HEADS).reshape(1, GLA_DV)
    o_gla = _gla(lat3(gq), lat3(gk), lat3(gv), lat3(lr), lat3(sog), ctx3(ck), ctx3(cv), ctx3(clr),
                 wa.astype(BF16), b_gla_a2[0], gn)

    wr = jnp.zeros((d, ROUTE_LANES), F32)
    wr = wr.at[:, :N_GROUPS].set(w_group[0]).at[:, EXPERT_LANE0:EXPERT_LANE0 + N_EXPERTS].set(w_expert[0])
    wr_hi = wr.astype(BF16)
    br = jnp.zeros((1, ROUTE_LANES), F32)
    br = br.at[0, :N_GROUPS].set(b_group[0]).at[0, EXPERT_LANE0:EXPERT_LANE0 + N_EXPERTS].set(b_expert[0])
    h, f_lat, route, route_t, counts = _merge_route(
        o_na.reshape(n, NA_WIDTH), o_gla.reshape(n, GLA_DV), sgn, sgg, x.reshape(n, d), mod3, norm_ffn_g[0].reshape(1, d),
        w_na_o[0].astype(BF16), w_gla_o[0].astype(BF16), w_out[0].astype(BF16),
        jnp.concatenate([wr_hi, (wr - wr_hi.astype(F32)).astype(BF16)], axis=1), br, s, tm)

    n_tiles = n // tm
    cnt = counts[:, 0, EXPERT_LANE0:EXPERT_LANE0 + N_EXPERTS].astype(jnp.int32)
    cnt_al = (cnt + SEG_ALIGN - 1) // SEG_ALIGN * SEG_ALIGN
    lstart = jnp.cumsum(cnt_al, axis=1) - cnt_al
    total = jnp.sum(cnt_al, axis=0)
    padded = (total + MOE_ROWS - 1) // MOE_ROWS * MOE_ROWS
    pend = jnp.cumsum(padded)
    pstart = pend - padded
    seg_dst = pstart[None, :] + jnp.cumsum(cnt_al, axis=0) - cnt_al
    assert tm // PIECE_SIZES[-1] <= MAX_BIG_PER_RUN
    assert (2 * tm + N_EXPERTS * (SEG_ALIGN - 1)) // PIECE_SIZES[-1] <= PIECE_SLOTS[-1]
    seg = _piece_table(cnt_al, lstart, seg_dst)
    n_used = (pend[-1] // MOE_ROWS).reshape(1)
    tail = jnp.concatenate([pstart + total, padded - total, n_used, jnp.zeros((2 * N_EXPERTS - 1,), jnp.int32)])
    tail = tail.reshape(1, 1, 4 * N_EXPERTS)
    seg_pad = n_tiles * N_EXPERTS * (SEG_ALIGN - 1)
    n_pad = (2 * n + seg_pad + N_EXPERTS * (MOE_ROWS - SEG_ALIGN) + MOE_ROWS - 1) // MOE_ROWS * MOE_ROWS
    n_blk = n_pad // MOE_ROWS
    rows_l = (2 * tm + N_EXPERTS * (SEG_ALIGN - 1) + MXU_DIM - 1) // MXU_DIM * MXU_DIM
    blk_start = jnp.arange(n_blk, dtype=jnp.int32) * MOE_ROWS
    blk_expert = jnp.minimum(jnp.sum(blk_start[:, None] >= pend[None, :], axis=-1), N_EXPERTS - 1)

    xs = _dispatch(seg, tail, route_t, f_lat, n_pad, rows_l, tm)
    ys = _experts(blk_expert.astype(jnp.int32), n_used.astype(jnp.int32), xs,
                  w_exp_gate[0], w_exp_up[0], w_exp_down[0])
    out = _combine(seg, route, h, mod3, final_norm_g.reshape(1, d), ys, rows_l, s, tm)
    return out.reshape(b, s, d)
```

```python
import functools

import jax
import jax.numpy as jnp
import numpy as np
from jax import lax
from jax.experimental import pallas as pl
from jax.experimental.pallas import tpu as pltpu

F32 = jnp.float32
BF16 = jnp.bfloat16

D_MODEL = 1024
GRID_W = 64
NA_HEADS = 8
NA_HEAD_DIM = 64
NA_WIDTH = NA_HEADS * NA_HEAD_DIM
NA_WIN_H = 8
NA_WIN_W = 16
GLA_HEADS = 4
GLA_DK = D_MODEL // 2
GLA_DV = D_MODEL
GLA_HK = GLA_DK // GLA_HEADS
GLA_HV = GLA_DV // GLA_HEADS
GLA_GATE_RANK = 16
GLA_GATE_NORM = 16.0
GLA_CHUNK = 64
GLA_GROUP = 256
ROPE_BASE = 10000.0
N_GROUPS = 4
EXPERTS_PER_GROUP = 8
N_EXPERTS = N_GROUPS * EXPERTS_PER_GROUP
D_EXPERT = D_MODEL // 2
EPS = 1e-6
NEG_INF = -1e30
LOG2E = 1.4426950408889634

LANES = 128
MXU_DIM = 256
LR_PAD = LANES
ROUTE_LANES = LANES
EXPERT_LANE0 = N_GROUPS
ROUTE_ROWS = 48
MOE_ROWS = 512
SEG_ALIGN = 8
PIECE_SIZES = (8, 16, 24, 32, 40, 48, 56, 64)
PIECE_SLOTS = (32, 32, 32, 32, 32, 32, 32, 24)
MAX_BIG_PER_RUN = 8
PIECE_HEAD = 16
PIECE_TABLE = 512
WAIT_SIZES = (1024, 512, 256, 128, 64, 32, 16, 8)
TAIL_SIZES = (256, 128, 64, 32, 16, 8)
VMEM_LIMIT = 56 * 1024 * 1024

_NT = (((1,), (1,)), ((), ()))
_TN = (((0,), (0,)), ((), ()))


def _params(sem, vmem=VMEM_LIMIT):
    return pltpu.CompilerParams(dimension_semantics=sem, vmem_limit_bytes=vmem)


def _mod_kernel(c_ref, w_ref, b_ref, o_ref):
    c = c_ref[...]
    s = c * jax.nn.sigmoid(c)
    o_ref[...] = jnp.dot(s.astype(BF16), w_ref[...].astype(BF16),
                         preferred_element_type=F32) + b_ref[...]


def _modulation(cs, w_mod, b_mod):
    rows, d = cs.shape
    n = w_mod.shape[1]
    bn = 1024
    return pl.pallas_call(
        _mod_kernel,
        grid=(n // bn,),
        in_specs=[pl.BlockSpec((rows, d), lambda j: (0, 0)),
                  pl.BlockSpec((d, bn), lambda j: (0, j)),
                  pl.BlockSpec((1, bn), lambda j: (0, j))],
        out_specs=pl.BlockSpec((rows, bn), lambda j: (0, j)),
        out_shape=jax.ShapeDtypeStruct((rows, n), F32),
        compiler_params=_params(("arbitrary",)),
        name="mod",
    )(cs, w_mod, b_mod.reshape(1, n))


def _inproj_kernel(specs, use_rope, n_w, n_out, x_ref, mod_ref, g_ref, *rest):
    if use_rope:
        cos_ref, sin_ref = rest[:2]
        rest = rest[2:]
    w_refs = rest[:n_w]
    outs = rest[n_w:n_w + n_out]
    a_scr = rest[n_w + n_out]
    x = x_ref[...]
    ms = jnp.mean(x * x, axis=-1, keepdims=True)
    a = x * lax.rsqrt(ms + EPS) * g_ref[...] * (1.0 + mod_ref[0, 1:2, :]) + mod_ref[0, 0:1, :]
    a_scr[...] = a.astype(BF16)
    for (oi, oc, wi, wc, width, kind) in specs:
        acc = jnp.dot(a_scr[...], w_refs[wi][:, wc:wc + width], preferred_element_type=F32)
        if kind in ("rope", "rope_scaled"):
            if kind == "rope_scaled":
                acc = acc * (GLA_HK ** -0.5)
            lane = lax.broadcasted_iota(jnp.int32, acc.shape, 1)
            rot = jnp.where((lane & 63) < 32,
                            pltpu.roll(acc, width - 32, axis=1),
                            pltpu.roll(acc, 32, axis=1))
            acc = acc * cos_ref[...] + rot * sin_ref[...]
        elif kind == "silu":
            acc = acc * jax.nn.sigmoid(acc)
        elif kind == "sigmoid":
            acc = jax.nn.sigmoid(acc)
        outs[oi][:, oc:oc + width] = acc.astype(outs[oi].dtype)


def _inproj(x2d, mod3, mod_row_fn, g, weights, specs, out_widths, tm, rope=None, name="inproj"):
    n, d = x2d.shape
    use_rope = rope is not None
    in_specs = [pl.BlockSpec((tm, d), lambda i: (i, 0)),
                pl.BlockSpec((1, 6, d), lambda i: (mod_row_fn(i), 0, 0)),
                pl.BlockSpec((1, d), lambda i: (0, 0))]
    args = [x2d, mod3, g.reshape(1, d)]
    if use_rope:
        cos, sin = rope
        nb = cos.shape[0] // tm
        in_specs += [pl.BlockSpec((tm, cos.shape[1]), lambda i: (i % nb, 0)),
                     pl.BlockSpec((tm, cos.shape[1]), lambda i: (i % nb, 0))]
        args += [cos, sin]
    in_specs += [pl.BlockSpec((d, cols), lambda i: (0, 0), pipeline_mode=pl.Buffered(1)) for _, cols in weights]
    args += [w for w, _ in weights]
    out_specs = [pl.BlockSpec((tm, w), lambda i: (i, 0)) for w in out_widths]
    out_shape = [jax.ShapeDtypeStruct((n, w), BF16) for w in out_widths]
    return pl.pallas_call(
        functools.partial(_inproj_kernel, specs, use_rope, len(weights), len(out_widths)),
        grid=(n // tm,),
        in_specs=in_specs,
        out_specs=out_specs,
        out_shape=out_shape,
        scratch_shapes=[pltpu.VMEM((tm, d), BF16)],
        compiler_params=_params(("arbitrary",)),
        name=name,
    )(*args)


def _na_kernel(rows_per_step, n_rows, q_ref, k_ref, v_ref, kc_ref, vc_ref, bias_ref, o_ref, sw_scr, sc_scr):
    j = pl.program_id(1)
    lane = lax.broadcasted_iota(jnp.int32, (GRID_W, LANES), 1)
    first_head = lane < NA_HEAD_DIM
    scale = NA_HEAD_DIM ** -0.5
    win_keys = NA_WIN_H * GRID_W
    pairs = range(NA_HEADS // 2)
    lanes = [slice(p * LANES, (p + 1) * LANES) for p in pairs]
    stack = 2 * GRID_W

    def window(rr):
        r = j * rows_per_step + rr
        rs = jnp.clip(r - NA_WIN_H // 2, 0, n_rows - NA_WIN_H)
        return r - rs, pl.multiple_of(rr * GRID_W, GRID_W), pl.multiple_of(rs * GRID_W, GRID_W)

    def scores(rr, slot):
        _, t0, ks = window(rr)
        for p in pairs:
            qp = q_ref[0, pl.ds(t0, GRID_W), lanes[p]] * scale
            zero = jnp.zeros_like(qp)
            qs = jnp.concatenate([jnp.where(first_head, qp, zero), jnp.where(first_head, zero, qp)], axis=0)
            sw_scr[slot, p * stack:(p + 1) * stack, :] = lax.dot_general(
                qs, k_ref[0, pl.ds(ks, win_keys), lanes[p]], _NT, preferred_element_type=F32)
            sc_scr[slot, p * stack:(p + 1) * stack, :] = lax.dot_general(
                qs, kc_ref[0, :, lanes[p]], _NT, preferred_element_type=F32)

    def attend(rr, slot):
        d, t0, ks = window(rr)
        pws, pcs, dens = [], [], []
        for p in pairs:
            bias = jnp.concatenate([bias_ref[(NA_WIN_H - 1) - d + 2 * i, p] for i in range(NA_WIN_H // 2)], axis=1)
            sw = sw_scr[slot, p * stack:(p + 1) * stack, :] * LOG2E + bias
            sc = sc_scr[slot, p * stack:(p + 1) * stack, :] * LOG2E
            m = jnp.maximum(jnp.max(sw, axis=-1, keepdims=True), jnp.max(sc, axis=-1, keepdims=True))
            pw = jnp.exp2(sw - m)
            pc = jnp.exp2(sc - m)
            dens.append(jnp.sum(pw, axis=-1, keepdims=True) + jnp.sum(pc, axis=-1, keepdims=True))
            pws.append(pw.astype(BF16))
            pcs.append(pc.astype(BF16))
        for p in pairs:
            o2 = (jnp.dot(pws[p], v_ref[0, pl.ds(ks, win_keys), lanes[p]], preferred_element_type=F32)
                  + jnp.dot(pcs[p], vc_ref[0, :, lanes[p]], preferred_element_type=F32)) / dens[p]
            o_ref[0, pl.ds(t0, GRID_W), lanes[p]] = jnp.where(first_head, o2[:GRID_W],
                                                             o2[GRID_W:]).astype(o_ref.dtype)

    scores(0, 0)

    def two_rows(k, carry):
        scores(2 * k + 1, 1)
        attend(2 * k, 0)
        scores(2 * k + 2, 0)
        attend(2 * k + 1, 1)
        return carry

    lax.fori_loop(0, rows_per_step // 2 - 1, two_rows, 0)
    scores(rows_per_step - 1, 1)
    attend(rows_per_step - 2, 0)
    attend(rows_per_step - 1, 1)


def _na_bias_table(rpb):
    n_heads, n_dr, n_dc = rpb.shape
    c = np.arange(GRID_W)
    cstart = np.clip(c - NA_WIN_W // 2, 0, GRID_W - NA_WIN_W)
    kc = np.arange(GRID_W)
    valid = (kc[None, :] >= cstart[:, None]) & (kc[None, :] < cstart[:, None] + NA_WIN_W)
    col_idx = np.clip(kc[None, :] - c[:, None] + NA_WIN_W - 1, 0, n_dc - 1)
    pick = np.zeros((n_dc, GRID_W * GRID_W), np.float32)
    pick[col_idx.reshape(-1), np.arange(GRID_W * GRID_W)] = 1.0
    t = jnp.dot(rpb.reshape(n_heads * n_dr, n_dc).astype(F32), jnp.asarray(pick),
                precision=lax.Precision.HIGHEST).reshape(n_heads, n_dr, GRID_W, GRID_W)
    t = jnp.where(jnp.asarray(valid)[None, None], t * LOG2E, NEG_INF)
    t2 = jnp.concatenate([t[:, :-1], t[:, 1:]], axis=-1)
    return t2.transpose(1, 0, 2, 3).reshape(n_dr - 1, n_heads // 2, 2 * GRID_W, 2 * GRID_W)


def _na_attention(q, k, v, kc, vc, bias, rows_per_step=32):
    b, s, w = q.shape
    n_rows = s // GRID_W
    ctx_len = kc.shape[1]
    blk = rows_per_step * GRID_W
    return pl.pallas_call(
        functools.partial(_na_kernel, rows_per_step, n_rows),
        grid=(b, n_rows // rows_per_step),
        in_specs=[pl.BlockSpec((1, blk, w), lambda bi, j: (bi, j, 0)),
                  pl.BlockSpec((1, s, w), lambda bi, j: (bi, 0, 0)),
                  pl.BlockSpec((1, s, w), lambda bi, j: (bi, 0, 0)),
                  pl.BlockSpec((1, ctx_len, w), lambda bi, j: (bi, 0, 0)),
                  pl.BlockSpec((1, ctx_len, w), lambda bi, j: (bi, 0, 0)),
                  pl.BlockSpec(bias.shape, lambda bi, j: (0, 0, 0, 0), pipeline_mode=pl.Buffered(1))],
        out_specs=pl.BlockSpec((1, blk, w), lambda bi, j: (bi, j, 0)),
        out_shape=jax.ShapeDtypeStruct((b, s, w), BF16),
        scratch_shapes=[pltpu.VMEM((2, NA_HEADS * GRID_W, NA_WIN_H * GRID_W), F32),
                        pltpu.VMEM((2, NA_HEADS * GRID_W, ctx_len), F32)],
        compiler_params=_params(("arbitrary", "arbitrary")),
        name="na_attn",
    )(q, k, v, kc, vc, bias)


def _gla_kernel(n_groups, gq_ref, gk_ref, gv_ref, lr_ref, sog_ref, ck_ref, cv_ref, clr_ref,
                wa_ref, ba_ref, gn_ref, o_ref, st_scr, of_scr, qe_scr, kd_scr, ckd_scr, dec_scr, cdec_scr,
                qk_a):
    c_len = GLA_CHUNK
    grp = GLA_GROUP
    cpg = grp // c_len
    row = lax.broadcasted_iota(jnp.int32, (grp, grp), 0)
    col = lax.broadcasted_iota(jnp.int32, (grp, grp), 1)
    same_chunk = (row // c_len) == (col // c_len)
    masks = (same_chunk & (col <= row), same_chunk & (col >= row))

    dirs = range(2)

    def gates(lr_rows, k_rows):
        zs = [jnp.dot(lr_rows, wa_ref[t], preferred_element_type=F32) + ba_ref[t:t + 1, :] for t in dirs]
        gs = [(jnp.minimum(z, 0.0) * LOG2E - jnp.log2(1.0 + jnp.exp2(jnp.abs(z) * (-LOG2E)))) * (1.0 / GLA_GATE_NORM)
              for z in zs]
        g1s = [g.astype(BF16) for g in gs]
        g2s = [(g - g1.astype(F32)).astype(BF16) for g, g1 in zip(gs, g1s)]
        tris = [masks[t].astype(BF16) for t in dirs]
        bs = [jnp.dot(tris[t], g1s[t], preferred_element_type=F32)
              + jnp.dot(tris[t], g2s[t], preferred_element_type=F32) for t in dirs]
        out = []
        for t in dirs:
            edge = c_len - 1 if t == 0 else 0
            decs = [jnp.exp2(bs[t][j * c_len + edge:j * c_len + edge + 1, :]) for j in range(cpg)]
            ke = k_rows * jnp.exp2(-bs[t])
            kd = (ke * jnp.concatenate([jnp.broadcast_to(v, (c_len, v.shape[1])) for v in decs],
                                       axis=0)).astype(BF16)
            dec = jnp.concatenate(decs + [jnp.ones_like(v) for v in decs], axis=0)
            out.append((bs[t], ke.astype(BF16), kd, dec))
        return out

    for t, (_, _, kd, dec) in enumerate(gates(clr_ref[0], ck_ref[0].astype(F32))):
        ckd_scr[t] = kd
        cdec_scr[t] = dec

    def group_gates(gi, qk_buf):
        rows = pl.ds(pl.multiple_of(gi * grp, grp), grp)
        qf = gq_ref[0, rows, :].astype(F32)
        for t, (b, ke, kd, dec) in enumerate(gates(lr_ref[0, rows, :], gk_ref[0, rows, :].astype(F32))):
            qe = (qf * jnp.exp2(b)).astype(BF16)
            qk_buf[t] = qe
            qk_buf[2 + t] = ke
            qe_scr[t, rows, :] = qe
            kd_scr[t, rows, :] = kd
            dec_scr[t, gi] = dec

    def group_attend(gi, qk_buf):
        rows = pl.ds(pl.multiple_of(gi * grp, grp), grp)
        heads = range(GLA_HEADS)
        kls = [slice(h * GLA_HK, (h + 1) * GLA_HK) for h in heads]
        raw = [[lax.dot_general(qk_buf[t, :, kls[h]], qk_buf[2 + t, :, kls[h]], _NT, preferred_element_type=F32)
                for t in dirs] for h in heads]
        atts = [(jnp.where(masks[0], raw[h][0], 0.0) + jnp.where(masks[1], raw[h][1], 0.0)).astype(BF16)
                for h in heads]
        for h in heads:
            vl = slice(h * GLA_HV, (h + 1) * GLA_HV)
            of_scr[rows, vl] = jnp.dot(atts[h], gv_ref[0, rows, vl], preferred_element_type=F32)

    def group_body(gi, carry):
        group_gates(gi, qk_a)
        group_attend(gi, qk_a)
        return carry

    lax.fori_loop(0, n_groups, group_body, 0)

    def state_step(h, v_rows, kd_rows, dec_row):
        kl = slice(h * GLA_HK, (h + 1) * GLA_HK)
        upd = lax.dot_general(v_rows, kd_rows[:, kl], _TN, preferred_element_type=F32)
        st_scr[h] = st_scr[h] * dec_row[:, kl] + upd

    def scan(direction):
        st_scr[...] = jnp.zeros_like(st_scr)
        order = range(cpg) if direction == 0 else range(cpg - 1, -1, -1)
        for j in order:
            rows = slice(j * c_len, (j + 1) * c_len)
            for h in range(GLA_HEADS):
                state_step(h, cv_ref[0, rows, h * GLA_HV:(h + 1) * GLA_HV],
                           ckd_scr[direction, rows, :], cdec_scr[direction, j:j + 1, :])

        def group_scan(i, carry):
            gi = i if direction == 0 else n_groups - 1 - i
            for j in order:
                rows = pl.ds(pl.multiple_of(gi * grp + j * c_len, c_len), c_len)
                qe_rows = qe_scr[direction, rows, :]
                kd_rows = kd_scr[direction, rows, :]
                dec_row = dec_scr[direction, gi, j:j + 1, :]
                for h in range(GLA_HEADS):
                    kl = slice(h * GLA_HK, (h + 1) * GLA_HK)
                    vl = slice(h * GLA_HV, (h + 1) * GLA_HV)
                    o_t = of_scr[rows, vl] + lax.dot_general(qe_rows[:, kl], st_scr[h].astype(BF16), _NT,
                                                             preferred_element_type=F32)
                    if direction == 0:
                        of_scr[rows, vl] = o_t
                    else:
                        ms = jnp.mean(o_t * o_t, axis=-1, keepdims=True)
                        o_n = o_t * lax.rsqrt(ms + EPS) * gn_ref[:, vl] * sog_ref[0, rows, vl].astype(F32)
                        o_ref[0, rows, vl] = o_n.astype(o_ref.dtype)
                    state_step(h, gv_ref[0, rows, vl], kd_rows, dec_row)
            return carry

        lax.fori_loop(0, n_groups, group_scan, 0)

    scan(0)
    scan(1)


def _gla(gq, gk, gv, lr, sog, ck, cv, clr, wa, ba, gn):
    b, s, _ = gq.shape
    ctx_len = ck.shape[1]
    assert ctx_len == GLA_GROUP and s % GLA_GROUP == 0
    n_groups = s // GLA_GROUP
    cpg = GLA_GROUP // GLA_CHUNK

    def per_batch(t, w):
        return pl.BlockSpec((1, t, w), lambda bi: (bi, 0, 0))

    return pl.pallas_call(
        functools.partial(_gla_kernel, n_groups),
        grid=(b,),
        in_specs=[per_batch(s, GLA_DK), per_batch(s, GLA_DK), per_batch(s, GLA_DV),
                  per_batch(s, LR_PAD), per_batch(s, GLA_DV),
                  per_batch(ctx_len, GLA_DK), per_batch(ctx_len, GLA_DV), per_batch(ctx_len, LR_PAD),
                  pl.BlockSpec(wa.shape, lambda bi: (0, 0, 0)),
                  pl.BlockSpec(ba.shape, lambda bi: (0, 0)),
                  pl.BlockSpec(gn.shape, lambda bi: (0, 0))],
        out_specs=per_batch(s, GLA_DV),
        out_shape=jax.ShapeDtypeStruct((b, s, GLA_DV), BF16),
        scratch_shapes=[pltpu.VMEM((GLA_HEADS, GLA_HV, GLA_HK), F32),
                        pltpu.VMEM((s, GLA_DV), F32),
                        pltpu.VMEM((2, s, GLA_DK), BF16),
                        pltpu.VMEM((2, s, GLA_DK), BF16),
                        pltpu.VMEM((2, ctx_len, GLA_DK), BF16),
                        pltpu.VMEM((2, n_groups, 2 * cpg, GLA_DK), F32),
                        pltpu.VMEM((2, 2 * cpg, GLA_DK), F32),
                        pltpu.VMEM((4, GLA_GROUP, GLA_DK), BF16)],
        compiler_params=_params(("arbitrary",), vmem=60 * 1024 * 1024),
        name="gla",
    )(gq, gk, gv, lr, sog, ck, cv, clr, wa, ba, gn)


def _merge_kernel(ona_ref, ogla_ref, sgn_ref, sgg_ref, x_ref, mod_ref, gffn_ref,
                  wna_ref, wgla_ref, wout_ref, wr_ref, br_ref,
                  h_ref, f_ref, route_ref, route_t_ref, cnt_ref):
    rows = x_ref.shape[0]
    y1 = (sgn_ref[...].astype(F32) * jnp.dot(ona_ref[...], wna_ref[...], preferred_element_type=F32)
          + sgg_ref[...].astype(F32) * jnp.dot(ogla_ref[...], wgla_ref[...], preferred_element_type=F32))
    y = jnp.dot(y1.astype(BF16), wout_ref[...], preferred_element_type=F32)
    h = x_ref[...] + mod_ref[0, 2:3, :] * y
    h_ref[...] = h
    ms = jnp.mean(h * h, axis=-1, keepdims=True)
    f = h * lax.rsqrt(ms + EPS) * gffn_ref[...] * (1.0 + mod_ref[0, 4:5, :]) + mod_ref[0, 3:4, :]
    f_ref[...] = f.astype(f_ref.dtype)

    f_hi = f.astype(BF16)
    f_lo = (f - f_hi.astype(F32)).astype(BF16)
    both = jnp.dot(f_hi, wr_ref[...], preferred_element_type=F32)
    logits = (both[:, :ROUTE_LANES] + both[:, ROUTE_LANES:]
              + jnp.dot(f_lo, wr_ref[:, :ROUTE_LANES], preferred_element_type=F32)) + br_ref[...]

    lt = logits.T[0:ROUTE_ROWS, :]
    er = lax.broadcasted_iota(jnp.int32, (ROUTE_ROWS, rows), 0)
    big = jnp.int32(ROUTE_ROWS)

    def first_argmax(vals):
        m = jnp.max(vals, axis=0, keepdims=True)
        idx = jnp.min(jnp.where(vals == m, er, big), axis=0, keepdims=True)
        return m, idx

    is_grp = er < N_GROUPS
    lg = jnp.where(is_grp, lt, -jnp.inf)
    mg, grp = first_argmax(lg)
    p_grp = 1.0 / jnp.sum(jnp.where(is_grp, jnp.exp(lg - mg), 0.0), axis=0, keepdims=True)
    lo = EXPERT_LANE0 + grp * EXPERTS_PER_GROUP
    in_grp = (er >= lo) & (er < lo + EXPERTS_PER_GROUP)
    le = jnp.where(in_grp, lt, -jnp.inf)
    m1, i1 = first_argmax(le)
    m2, i2 = first_argmax(jnp.where(er == i1, -jnp.inf, le))
    t = jnp.exp(m2 - m1)
    w1 = p_grp / (1.0 + t)
    w2 = p_grp * t / (1.0 + t)

    hot1 = er == i1
    hot2 = er == i2
    onehot = (hot1 | hot2).astype(BF16)
    s_i = lax.broadcasted_iota(jnp.int32, (rows, rows), 0)
    t_i = lax.broadcasted_iota(jnp.int32, (rows, rows), 1)
    rank = jnp.dot(onehot, (s_i < t_i).astype(BF16), preferred_element_type=F32)
    cnt = jnp.sum(onehot.astype(F32), axis=1, keepdims=True)
    cnt_al = jnp.floor((cnt + (SEG_ALIGN - 1.0)) * (1.0 / SEG_ALIGN)) * SEG_ALIGN
    e_i = lax.broadcasted_iota(jnp.int32, (ROUTE_ROWS, ROUTE_ROWS), 0)
    e_j = lax.broadcasted_iota(jnp.int32, (ROUTE_ROWS, ROUTE_ROWS), 1)
    lstart = jnp.dot((e_j < e_i).astype(BF16), jnp.broadcast_to(cnt_al, (ROUTE_ROWS, LANES)).astype(BF16),
                     preferred_element_type=F32)[:, 0:1]
    pos = rank + lstart
    pos1 = jnp.sum(jnp.where(hot1, pos, 0.0), axis=0, keepdims=True)
    pos2 = jnp.sum(jnp.where(hot2, pos, 0.0), axis=0, keepdims=True)

    e1 = (i1 - EXPERT_LANE0).astype(F32)
    e2 = (i2 - EXPERT_LANE0).astype(F32)
    fields = jnp.concatenate([e1, e2, w1, w2, pos1, pos2, jnp.zeros((2, rows), F32)], axis=0)
    route_t_ref[0] = fields
    route_ref[...] = jnp.concatenate([fields, jnp.zeros((ROUTE_LANES - 8, rows), F32)], axis=0).T
    cnt_rows = jnp.concatenate([jnp.broadcast_to(cnt, (ROUTE_ROWS, ROUTE_LANES)),
                                jnp.zeros((ROUTE_LANES - ROUTE_ROWS, ROUTE_LANES), F32)], axis=0)
    cnt_ref[0] = cnt_rows.T[0:1, :]


def _merge_route(ona, ogla, sgn, sgg, x2d, mod3, gffn, wna, wgla, wout, wr, br, tokens_per_batch, tm):
    n, d = x2d.shape
    per_b = tokens_per_batch // tm

    def tile(w):
        return pl.BlockSpec((tm, w), lambda i: (i, 0))

    def whole(a):
        return pl.BlockSpec(a.shape, lambda i: (0,) * a.ndim)

    return pl.pallas_call(
        _merge_kernel,
        grid=(n // tm,),
        in_specs=[tile(NA_WIDTH), tile(GLA_DV), tile(d), tile(d), tile(d),
                  pl.BlockSpec((1, 6, d), lambda i: (i // per_b, 0, 0)),
                  whole(gffn), whole(wna), whole(wgla), whole(wout), whole(wr), whole(br)],
        out_specs=[tile(d), tile(d), tile(ROUTE_LANES),
                   pl.BlockSpec((1, 8, tm), lambda i: (i, 0, 0)),
                   pl.BlockSpec((1, 1, ROUTE_LANES), lambda i: (i, 0, 0))],
        out_shape=[jax.ShapeDtypeStruct((n, d), F32),
                   jax.ShapeDtypeStruct((n, d), BF16),
                   jax.ShapeDtypeStruct((n, ROUTE_LANES), F32),
                   jax.ShapeDtypeStruct((n // tm, 8, tm), F32),
                   jax.ShapeDtypeStruct((n // tm, 1, ROUTE_LANES), F32)],
        compiler_params=_params(("arbitrary",)),
        name="merge_route",
    )(ona, ogla, sgn, sgg, x2d, mod3, gffn, wna, wgla, wout, wr, br)


HI_MASK = -65536


def _pack_rows(a):
    half = a.shape[1] // 2
    hi = lax.bitcast_convert_type(a[:, :half], jnp.int32)
    lo = lax.shift_right_logical(lax.bitcast_convert_type(a[:, half:], jnp.int32), 16)
    return hi | lo


def _unpack_rows(w):
    hi = lax.bitcast_convert_type(w & HI_MASK, F32).astype(BF16)
    lo = lax.bitcast_convert_type(lax.shift_left(w, 16), F32).astype(BF16)
    return hi, lo


def _piece_table(cnt_al, lstart, seg_dst):
    n_tiles = cnt_al.shape[0]

    def compact(mask, local, glob, slots):
        pos = jnp.cumsum(mask.astype(jnp.int32), axis=1) - 1
        hot = mask[:, :, None] & (pos[:, :, None] == jnp.arange(slots, dtype=jnp.int32))
        pick = lambda v: jnp.sum(jnp.where(hot, v[:, :, None], 0), axis=1)
        return jnp.sum(mask, axis=1).astype(jnp.int32), pick(local), pick(glob)

    counts, lists = [], []
    big = PIECE_SIZES[-1]
    for size, slots in zip(PIECE_SIZES[:-1], PIECE_SLOTS[:-1]):
        n, lo, gl = compact((cnt_al & (big - 1)) == size, lstart, seg_dst, slots)
        counts.append(n)
        lists += [lo, gl]
    k = jnp.arange(MAX_BIG_PER_RUN, dtype=jnp.int32)
    off = (cnt_al & (big - 1))[:, :, None] + big * k
    mask = k < (cnt_al // big)[:, :, None]
    flat = lambda a: a.reshape(n_tiles, -1)
    n, lo, gl = compact(flat(mask), flat(lstart[:, :, None] + off), flat(seg_dst[:, :, None] + off), PIECE_SLOTS[-1])
    counts.append(n)
    lists += [lo, gl]
    head = jnp.stack(counts + [jnp.sum(cnt_al, axis=1)], axis=1)
    head = jnp.pad(head, ((0, 0), (0, PIECE_HEAD - head.shape[1])))
    table = jnp.concatenate([head] + lists, axis=1)
    table = jnp.pad(table, ((0, 0), (0, PIECE_TABLE - table.shape[1])))
    return table.reshape(n_tiles, 1, PIECE_TABLE).astype(jnp.int32)


def _segment_copies(seg_ref, make_copy, action):
    if action == "wait":
        total = seg_ref[0, 0, len(PIECE_SIZES)]
        for size in WAIT_SIZES:
            @pl.when((total & size) != 0)
            def _():
                make_copy(0, 0, size).wait()
        return

    base = PIECE_HEAD
    for c, (size, slots) in enumerate(zip(PIECE_SIZES, PIECE_SLOTS)):
        def piece(k, carry, size=size, lo=base, gl=base + slots):
            make_copy(pl.multiple_of(seg_ref[0, 0, lo + k], SEG_ALIGN),
                      pl.multiple_of(seg_ref[0, 0, gl + k], SEG_ALIGN), size).start()
            return carry

        lax.fori_loop(0, seg_ref[0, 0, c], piece, 0)
        base += 2 * slots


def _dispatch_kernel(seg_ref, seg_prev_ref, tail_ref, lpos_ref, f_ref, xs_ref, sbuf, zbuf, sems):
    i = pl.program_id(0)
    last = pl.num_programs(0) - 1
    tm = f_ref.shape[0]
    rows_l = sbuf.shape[1]
    slot = i % 2
    fields = lpos_ref[0]
    lp = fields.astype(jnp.int32)
    half = f_ref.shape[1] // 2
    row = lax.broadcasted_iota(jnp.int32, (rows_l, tm), 0)
    first = row == lp[4:5, :]
    second = row == lp[5:6, :]
    sbuf[slot, :, :half] = _pack_rows(jnp.dot((first | second).astype(BF16), f_ref[...], preferred_element_type=F32))
    w_sorted = jnp.sum(jnp.where(first, fields[2:3, :], 0.0) + jnp.where(second, fields[3:4, :], 0.0),
                       axis=1, keepdims=True)
    sbuf[slot, :, half:] = lax.bitcast_convert_type(jnp.broadcast_to(w_sorted, (rows_l, LANES)), jnp.int32)

    def copier(s):
        def copy(local, glob, size):
            return pltpu.make_async_copy(sbuf.at[s, pl.ds(local, size), :], xs_ref.at[pl.ds(glob, size), :],
                                         sems.at[s])
        return copy

    _segment_copies(seg_ref, copier(slot), "start")

    @pl.when(i > 0)
    def _():
        _segment_copies(seg_prev_ref, copier(1 - slot), "wait")

    @pl.when(i == last)
    def _():
        _segment_copies(seg_ref, copier(slot), "wait")
        sem = sems.at[slot]
        zbuf[...] = jnp.zeros_like(zbuf)

        def tail(action):
            def body(e, carry):
                start = tail_ref[0, 0, e]
                length = tail_ref[0, 0, N_EXPERTS + e]
                off = jnp.int32(0)
                for size in TAIL_SIZES:
                    @pl.when((length & size) != 0)
                    def _():
                        cp = pltpu.make_async_copy(zbuf.at[pl.ds(0, size), :],
                                                   xs_ref.at[pl.ds(pl.multiple_of(start + off, SEG_ALIGN), size), :],
                                                   sem)
                        getattr(cp, action)()
                    off = off + (length & size)
                return carry
            lax.fori_loop(0, N_EXPERTS, body, 0)

        tail("start")
        tail("wait")

        def unused(action):
            def body(blk, carry):
                cp = pltpu.make_async_copy(zbuf, xs_ref.at[pl.ds(pl.multiple_of(blk * MOE_ROWS, MOE_ROWS),
                                                                  MOE_ROWS), :], sem)
                getattr(cp, action)()
                return carry
            lax.fori_loop(tail_ref[0, 0, 2 * N_EXPERTS], xs_ref.shape[0] // MOE_ROWS, body, 0)

        unused("start")
        unused("wait")


def _dispatch(seg, tail, lpos_rows, f2d, n_pad, rows_l, tm):
    n, d = f2d.shape
    return pl.pallas_call(
        _dispatch_kernel,
        grid=(n // tm,),
        in_specs=[pl.BlockSpec((1, 1, seg.shape[2]), lambda i: (i, 0, 0), memory_space=pltpu.SMEM),
                  pl.BlockSpec((1, 1, seg.shape[2]), lambda i: (jnp.maximum(i - 1, 0), 0, 0),
                               memory_space=pltpu.SMEM),
                  pl.BlockSpec(tail.shape, lambda i: (0, 0, 0), memory_space=pltpu.SMEM),
                  pl.BlockSpec((1, 8, tm), lambda i: (i, 0, 0)),
                  pl.BlockSpec((tm, d), lambda i: (i, 0))],
        out_specs=pl.BlockSpec(memory_space=pl.ANY),
        out_shape=jax.ShapeDtypeStruct((n_pad, d // 2 + LANES), jnp.int32),
        scratch_shapes=[pltpu.VMEM((2, rows_l, d // 2 + LANES), jnp.int32),
                        pltpu.VMEM((MOE_ROWS, d // 2 + LANES), jnp.int32),
                        pltpu.SemaphoreType.DMA((2,))],
        compiler_params=_params(("arbitrary",)),
        name="dispatch",
    )(seg, seg, tail, lpos_rows, f2d)


def _expert_kernel(be_ref, nu_ref, x_ref, wg_ref, wu_ref, wd_ref, y_ref, wgu_scr, wd_scr):
    i = pl.program_id(0)
    prev = be_ref[jnp.maximum(i - 1, 0)]

    @pl.when((i == 0) | (be_ref[i] != prev))
    def _():
        wgu_scr[:, :D_EXPERT] = wg_ref[0].astype(BF16)
        wgu_scr[:, D_EXPERT:] = wu_ref[0].astype(BF16)
        wd_scr[...] = wd_ref[0].astype(BF16)

    @pl.when(i < nu_ref[0])
    def _():
        half = y_ref.shape[1]
        x = jnp.concatenate(_unpack_rows(x_ref[:, :half]), axis=1)
        gu = jnp.dot(x, wgu_scr[...], preferred_element_type=F32)
        gate = gu[:, :D_EXPERT]
        hdn = gate * jax.nn.sigmoid(gate) * gu[:, D_EXPERT:]
        y = jnp.dot(hdn.astype(BF16), wd_scr[...], preferred_element_type=F32)
        y = y * lax.bitcast_convert_type(x_ref[:, half:half + 1], F32)
        y_ref[...] = _pack_rows(y.astype(BF16).astype(F32))

    @pl.when(i >= nu_ref[0])
    def _():
        y_ref[...] = jnp.zeros_like(y_ref)


def _experts(blk_expert, n_used, xs, wg, wu, wd):
    n_pad, row_words = xs.shape
    half = row_words - LANES
    d = 2 * half
    n_blk = n_pad // MOE_ROWS

    def used_block(i, be, nu):
        return (jnp.maximum(jnp.minimum(i, nu[0] - 1), 0), 0)

    grid_spec = pltpu.PrefetchScalarGridSpec(
        num_scalar_prefetch=2,
        grid=(n_blk,),
        in_specs=[pl.BlockSpec((MOE_ROWS, row_words), used_block),
                  pl.BlockSpec((1, d, D_EXPERT), lambda i, be, nu: (be[i], 0, 0)),
                  pl.BlockSpec((1, d, D_EXPERT), lambda i, be, nu: (be[i], 0, 0)),
                  pl.BlockSpec((1, D_EXPERT, d), lambda i, be, nu: (be[i], 0, 0))],
        out_specs=pl.BlockSpec((MOE_ROWS, half), lambda i, be, nu: (i, 0)),
        scratch_shapes=[pltpu.VMEM((d, 2 * D_EXPERT), BF16),
                        pltpu.VMEM((D_EXPERT, d), BF16)],
    )
    return pl.pallas_call(
        _expert_kernel,
        grid_spec=grid_spec,
        out_shape=jax.ShapeDtypeStruct((n_pad, half), jnp.int32),
        compiler_params=_params(("arbitrary",)),
        name="experts",
    )(blk_expert, n_used, xs, wg, wu, wd)


def _combine_kernel(seg_ref, seg_next_ref, col_ref, h_ref, mod_ref, fg_ref, ys_ref, o_ref, ybuf, sems):
    i = pl.program_id(0)
    tm = h_ref.shape[0]
    rows_l = ybuf.shape[1]
    slot = i % 2

    def copier(s):
        def copy(local, glob, size):
            return pltpu.make_async_copy(ys_ref.at[pl.ds(glob, size), :], ybuf.at[s, pl.ds(local, size), :],
                                         sems.at[s])
        return copy

    @pl.when(i == 0)
    def _():
        ybuf[...] = jnp.zeros_like(ybuf)
        _segment_copies(seg_ref, copier(slot), "start")

    @pl.when(i + 1 < pl.num_programs(0))
    def _():
        _segment_copies(seg_next_ref, copier(1 - slot), "start")

    _segment_copies(seg_ref, copier(slot), "wait")
    y_hi, y_lo = _unpack_rows(ybuf[slot])
    y_all = jnp.concatenate([y_hi, y_lo], axis=1)
    col = lax.broadcasted_iota(jnp.int32, (tm, rows_l), 1)
    info = col_ref[...]
    pick = (col == info[:, 4:5].astype(jnp.int32)) | (col == info[:, 5:6].astype(jnp.int32))
    moe = jnp.dot(pick.astype(BF16), y_all, preferred_element_type=F32)
    h = h_ref[...] + mod_ref[0, 5:6, :] * moe
    ms = jnp.mean(h * h, axis=-1, keepdims=True)
    o_ref[...] = h * lax.rsqrt(ms + EPS) * fg_ref[...]


def _combine(seg, colinfo, h2d, mod3, fg, ys, rows_l, tokens_per_batch, tm):
    n, d = h2d.shape
    per_b = tokens_per_batch // tm
    return pl.pallas_call(
        _combine_kernel,
        grid=(n // tm,),
        in_specs=[pl.BlockSpec((1, 1, seg.shape[2]), lambda i: (i, 0, 0), memory_space=pltpu.SMEM),
                  pl.BlockSpec((1, 1, seg.shape[2]), lambda i: (jnp.minimum(i + 1, n // tm - 1), 0, 0),
                               memory_space=pltpu.SMEM),
                  pl.BlockSpec((tm, colinfo.shape[1]), lambda i: (i, 0)),
                  pl.BlockSpec((tm, d), lambda i: (i, 0)),
                  pl.BlockSpec((1, 6, d), lambda i: (i // per_b, 0, 0)),
                  pl.BlockSpec((1, d), lambda i: (0, 0)),
                  pl.BlockSpec(memory_space=pl.ANY)],
        out_specs=pl.BlockSpec((tm, d), lambda i: (i, 0)),
        out_shape=jax.ShapeDtypeStruct((n, d), F32),
        scratch_shapes=[pltpu.VMEM((2, rows_l, d // 2), jnp.int32), pltpu.SemaphoreType.DMA((2,))],
        compiler_params=_params(("arbitrary",)),
        name="combine",
    )(seg, seg, colinfo, h2d, mod3, fg, ys)


def _rope_tables(s):
    t = np.arange(s)
    pos_r = (t // GRID_W).astype(np.float32)
    pos_c = (t % GRID_W).astype(np.float32)
    nf = GLA_HK // 4
    inv = (np.float32(ROPE_BASE) ** (-np.arange(nf, dtype=np.float32) / np.float32(nf))).astype(np.float32)
    ang_r = pos_r[:, None] * inv
    ang_c = pos_c[:, None] * inv

    def half(ang):
        return (np.concatenate([np.cos(ang), np.cos(ang)], axis=-1),
                np.concatenate([-np.sin(ang), np.sin(ang)], axis=-1))

    cr, sr = half(ang_r)
    cc, sc = half(ang_c)
    cos = np.tile(np.concatenate([cr, cc], axis=-1), (1, GLA_HEADS)).astype(np.float32)
    sin = np.tile(np.concatenate([sr, sc], axis=-1), (1, GLA_HEADS)).astype(np.float32)
    return jnp.asarray(cos), jnp.asarray(sin)


def kernel(x, c, ctx, c_ctx, w_mod, b_mod, norm_attn_g, norm_ffn_g, w_in, w_gla_a2, b_gla_a2, gla_norm_g, na_rpb, w_na_o, w_gla_o, w_out, w_group, b_group, w_expert, b_expert, w_exp_gate, w_exp_up, w_exp_down, final_norm_g):
    b, s, d = x.shape
    ctx_len = ctx.shape[1]
    n = b * s
    assert w_mod.shape[0] == 1, "single-layer block"
    assert s % (GRID_W * NA_WIN_H) == 0 and ctx_len % GLA_CHUNK == 0

    mod_rows = -(-(b + 1) // 8) * 8
    cs = jnp.zeros((mod_rows, d), F32).at[:b].set(c).at[b].set(c_ctx)
    mod3 = _modulation(cs, w_mod[0], b_mod[0]).reshape(mod_rows, 6, d)

    o_q, o_k, o_v = 0, NA_WIDTH, 2 * NA_WIDTH
    o_gq = 3 * NA_WIDTH
    o_gk = o_gq + GLA_DK
    o_gv = o_gk + GLA_DK
    o_og = o_gv + GLA_DV
    o_lr = o_og + GLA_DV
    o_mn = o_lr + 2 * GLA_GATE_RANK
    w_bf = w_in[0].astype(BF16)
    w_a = (w_bf, o_lr)
    w_g = (w_bf[:, o_mn:], 2 * d)
    w_lr = (jnp.pad(w_bf[:, o_lr:o_mn], ((0, 0), (0, LR_PAD - 2 * GLA_GATE_RANK))), LR_PAD)
    cw = 512
    lat_specs = [(0, 0, 0, o_q, cw, "plain"), (1, 0, 0, o_k, cw, "plain"), (2, 0, 0, o_v, cw, "plain"),
                 (3, 0, 0, o_gq, cw, "rope_scaled"), (4, 0, 0, o_gk, cw, "rope"),
                 (5, 0, 0, o_gv, cw, "plain"), (5, cw, 0, o_gv + cw, cw, "plain"),
                 (6, 0, 0, o_og, cw, "silu"), (6, cw, 0, o_og + cw, cw, "silu"),
                 (7, 0, 1, 0, cw, "sigmoid"), (7, cw, 1, cw, cw, "sigmoid"),
                 (8, 0, 1, d, cw, "sigmoid"), (8, cw, 1, d + cw, cw, "sigmoid"),
                 (9, 0, 2, 0, LR_PAD, "plain")]
    lat_widths = [NA_WIDTH, NA_WIDTH, NA_WIDTH, GLA_DK, GLA_DK, GLA_DV, GLA_DV, d, d, LR_PAD]
    tm = 512
    per_b = s // tm
    q_na, k_na, v_na, gq, gk, gv, sog, sgn, sgg, lr = _inproj(
        x.reshape(n, d), mod3, lambda i: i // per_b, norm_attn_g[0], [w_a, w_g, w_lr], lat_specs, lat_widths, tm,
        rope=_rope_tables(s), name="inproj_lat")

    ctx_specs = [(0, 0, 0, o_k, cw, "plain"), (1, 0, 0, o_v, cw, "plain"), (2, 0, 0, o_gk, cw, "plain"),
                 (3, 0, 0, o_gv, cw, "plain"), (3, cw, 0, o_gv + cw, cw, "plain"),
                 (4, 0, 1, 0, LR_PAD, "plain")]
    ctx_widths = [NA_WIDTH, NA_WIDTH, GLA_DK, GLA_DV, LR_PAD]
    kc_na, vc_na, ck, cv, clr = _inproj(
        ctx.reshape(b * ctx_len, d), mod3, lambda i: b, norm_attn_g[0], [w_a, w_lr], ctx_specs, ctx_widths,
        tm if (b * ctx_len) % tm == 0 else ctx_len, name="inproj_ctx")

    def lat3(a):
        return a.reshape(b, s, a.shape[-1])

    def ctx3(a):
        return a.reshape(b, ctx_len, a.shape[-1])

    o_na = _na_attention(lat3(q_na), lat3(k_na), lat3(v_na), ctx3(kc_na), ctx3(vc_na),
                         _na_bias_table(na_rpb[0]))

    wa = jnp.zeros((2, LR_PAD, GLA_DK), F32)
    wa = wa.at[0, :GLA_GATE_RANK].set(w_gla_a2[0, 0]).at[1, GLA_GATE_RANK:2 * GLA_GATE_RANK].set(w_gla_a2[0, 1])
    gn = jnp.tile(gla_norm_g[0], GLA_HEADS).reshape(1, GLA_DV)
    o_gla = _gla(lat3(gq), lat3(gk), lat3(gv), lat3(lr), lat3(sog), ctx3(ck), ctx3(cv), ctx3(clr),
                 wa.astype(BF16), b_gla_a2[0], gn)

    wr = jnp.zeros((d, ROUTE_LANES), F32)
    wr = wr.at[:, :N_GROUPS].set(w_group[0]).at[:, EXPERT_LANE0:EXPERT_LANE0 + N_EXPERTS].set(w_expert[0])
    wr_hi = wr.astype(BF16)
    br = jnp.zeros((1, ROUTE_LANES), F32)
    br = br.at[0, :N_GROUPS].set(b_group[0]).at[0, EXPERT_LANE0:EXPERT_LANE0 + N_EXPERTS].set(b_expert[0])
    h, f_lat, route, route_t, counts = _merge_route(
        o_na.reshape(n, NA_WIDTH), o_gla.reshape(n, GLA_DV), sgn, sgg, x.reshape(n, d), mod3, norm_ffn_g[0].reshape(1, d),
        w_na_o[0].astype(BF16), w_gla_o[0].astype(BF16), w_out[0].astype(BF16),
        jnp.concatenate([wr_hi, (wr - wr_hi.astype(F32)).astype(BF16)], axis=1), br, s, tm)

    n_tiles = n // tm
    cnt = counts[:, 0, EXPERT_LANE0:EXPERT_LANE0 + N_EXPERTS].astype(jnp.int32)
    cnt_al = (cnt + SEG_ALIGN - 1) // SEG_ALIGN * SEG_ALIGN
    lstart = jnp.cumsum(cnt_al, axis=1) - cnt_al
    total = jnp.sum(cnt_al, axis=0)
    padded = (total + MOE_ROWS - 1) // MOE_ROWS * MOE_ROWS
    pend = jnp.cumsum(padded)
    pstart = pend - padded
    seg_dst = pstart[None, :] + jnp.cumsum(cnt_al, axis=0) - cnt_al
    assert tm // PIECE_SIZES[-1] <= MAX_BIG_PER_RUN
    assert (2 * tm + N_EXPERTS * (SEG_ALIGN - 1)) // PIECE_SIZES[-1] <= PIECE_SLOTS[-1]
    seg = _piece_table(cnt_al, lstart, seg_dst)
    n_used = (pend[-1] // MOE_ROWS).reshape(1)
    tail = jnp.concatenate([pstart + total, padded - total, n_used, jnp.zeros((2 * N_EXPERTS - 1,), jnp.int32)])
    tail = tail.reshape(1, 1, 4 * N_EXPERTS)
    seg_pad = n_tiles * N_EXPERTS * (SEG_ALIGN - 1)
    n_pad = (2 * n + seg_pad + N_EXPERTS * (MOE_ROWS - SEG_ALIGN) + MOE_ROWS - 1) // MOE_ROWS * MOE_ROWS
    n_blk = n_pad // MOE_ROWS
    rows_l = (2 * tm + N_EXPERTS * (SEG_ALIGN - 1) + MXU_DIM - 1) // MXU_DIM * MXU_DIM
    blk_start = jnp.arange(n_blk, dtype=jnp.int32) * MOE_ROWS
    blk_expert = jnp.minimum(jnp.sum(blk_start[:, None] >= pend[None, :], axis=-1), N_EXPERTS - 1)

    xs = _dispatch(seg, tail, route_t, f_lat, n_pad, rows_l, tm)
    ys = _experts(blk_expert.astype(jnp.int32), n_used.astype(jnp.int32), xs,
                  w_exp_gate[0], w_exp_up[0], w_exp_down[0])
    out = _combine(seg, route, h, mod3, final_norm_g.reshape(1, d), ys, rows_l, s, tm)
    return out.reshape(b, s, d)
```

```python
import functools

import jax
import jax.numpy as jnp
import numpy as np
from jax import lax
from jax.experimental import pallas as pl
from jax.experimental.pallas import tpu as pltpu

F32 = jnp.float32
BF16 = jnp.bfloat16

D_MODEL = 1024
GRID_W = 64
NA_HEADS = 8
NA_HEAD_DIM = 64
NA_WIDTH = NA_HEADS * NA_HEAD_DIM
NA_WIN_H = 8
NA_WIN_W = 16
GLA_HEADS = 4
GLA_DK = D_MODEL // 2
GLA_DV = D_MODEL
GLA_HK = GLA_DK // GLA_HEADS
GLA_HV = GLA_DV // GLA_HEADS
GLA_GATE_RANK = 16
GLA_GATE_NORM = 16.0
GLA_CHUNK = 64
GLA_GROUP = 256
ROPE_BASE = 10000.0
N_GROUPS = 4
EXPERTS_PER_GROUP = 8
N_EXPERTS = N_GROUPS * EXPERTS_PER_GROUP
D_EXPERT = D_MODEL // 2
EPS = 1e-6
NEG_INF = -1e30
LOG2E = 1.4426950408889634

LANES = 128
MXU_DIM = 256
LR_PAD = LANES
ROUTE_LANES = LANES
EXPERT_LANE0 = N_GROUPS
ROUTE_ROWS = 48
MOE_ROWS = 512
SEG_ALIGN = 8
PIECE_SIZES = (8, 16, 24, 32, 40, 48, 56, 64)
PIECE_SLOTS = (32, 32, 32, 32, 32, 32, 32, 24)
MAX_BIG_PER_RUN = 8
PIECE_HEAD = 16
PIECE_TABLE = 512
WAIT_SIZES = (1024, 512, 256, 128, 64, 32, 16, 8)
TAIL_SIZES = (256, 128, 64, 32, 16, 8)
VMEM_LIMIT = 56 * 1024 * 1024

_NT = (((1,), (1,)), ((), ()))
_TN = (((0,), (0,)), ((), ()))


def _params(sem, vmem=VMEM_LIMIT):
    return pltpu.CompilerParams(dimension_semantics=sem, vmem_limit_bytes=vmem)


def _mod_kernel(c_ref, w_ref, b_ref, o_ref):
    c = c_ref[...]
    s = c * jax.nn.sigmoid(c)
    o_ref[...] = jnp.dot(s.astype(BF16), w_ref[...].astype(BF16),
                         preferred_element_type=F32) + b_ref[...]


def _modulation(cs, w_mod, b_mod):
    rows, d = cs.shape
    n = w_mod.shape[1]
    bn = 1024
    return pl.pallas_call(
        _mod_kernel,
        grid=(n // bn,),
        in_specs=[pl.BlockSpec((rows, d), lambda j: (0, 0)),
                  pl.BlockSpec((d, bn), lambda j: (0, j)),
                  pl.BlockSpec((1, bn), lambda j: (0, j))],
        out_specs=pl.BlockSpec((rows, bn), lambda j: (0, j)),
        out_shape=jax.ShapeDtypeStruct((rows, n), F32),
        compiler_params=_params(("arbitrary",)),
        name="mod",
    )(cs, w_mod, b_mod.reshape(1, n))


def _inproj_kernel(specs, use_rope, n_w, n_out, x_ref, mod_ref, g_ref, *rest):
    if use_rope:
        cos_ref, sin_ref = rest[:2]
        rest = rest[2:]
    w_refs = rest[:n_w]
    outs = rest[n_w:n_w + n_out]
    a_scr = rest[n_w + n_out]
    x = x_ref[...]
    ms = jnp.mean(x * x, axis=-1, keepdims=True)
    a = x * lax.rsqrt(ms + EPS) * g_ref[...] * (1.0 + mod_ref[0, 1:2, :]) + mod_ref[0, 0:1, :]
    a_scr[...] = a.astype(BF16)
    for (oi, oc, wi, wc, width, kind) in specs:
        acc = jnp.dot(a_scr[...], w_refs[wi][:, wc:wc + width], preferred_element_type=F32)
        if kind in ("rope", "rope_scaled"):
            if kind == "rope_scaled":
                acc = acc * (GLA_HK ** -0.5)
            lane = lax.broadcasted_iota(jnp.int32, acc.shape, 1)
            rot = jnp.where((lane & 63) < 32,
                            pltpu.roll(acc, width - 32, axis=1),
                            pltpu.roll(acc, 32, axis=1))
            acc = acc * cos_ref[...] + rot * sin_ref[...]
        elif kind == "silu":
            acc = acc * jax.nn.sigmoid(acc)
        elif kind == "sigmoid":
            acc = jax.nn.sigmoid(acc)
        outs[oi][:, oc:oc + width] = acc.astype(outs[oi].dtype)


def _inproj(x2d, mod3, mod_row_fn, g, weights, specs, out_widths, tm, rope=None, name="inproj"):
    n, d = x2d.shape
    use_rope = rope is not None
    in_specs = [pl.BlockSpec((tm, d), lambda i: (i, 0)),
                pl.BlockSpec((1, 6, d), lambda i: (mod_row_fn(i), 0, 0)),
                pl.BlockSpec((1, d), lambda i: (0, 0))]
    args = [x2d, mod3, g.reshape(1, d)]
    if use_rope:
        cos, sin = rope
        nb = cos.shape[0] // tm
        in_specs += [pl.BlockSpec((tm, cos.shape[1]), lambda i: (i % nb, 0)),
                     pl.BlockSpec((tm, cos.shape[1]), lambda i: (i % nb, 0))]
        args += [cos, sin]
    in_specs += [pl.BlockSpec((d, cols), lambda i: (0, 0), pipeline_mode=pl.Buffered(1)) for _, cols in weights]
    args += [w for w, _ in weights]
    out_specs = [pl.BlockSpec((tm, w), lambda i: (i, 0)) for w in out_widths]
    out_shape = [jax.ShapeDtypeStruct((n, w), BF16) for w in out_widths]
    return pl.pallas_call(
        functools.partial(_inproj_kernel, specs, use_rope, len(weights), len(out_widths)),
        grid=(n // tm,),
        in_specs=in_specs,
        out_specs=out_specs,
        out_shape=out_shape,
        scratch_shapes=[pltpu.VMEM((tm, d), BF16)],
        compiler_params=_params(("arbitrary",)),
        name=name,
    )(*args)


def _na_kernel(rows_per_step, n_rows, q_ref, k_ref, v_ref, kc_ref, vc_ref, bias_ref, o_ref, sw_scr, sc_scr):
    j = pl.program_id(1)
    lane = lax.broadcasted_iota(jnp.int32, (GRID_W, LANES), 1)
    first_head = lane < NA_HEAD_DIM
    scale = NA_HEAD_DIM ** -0.5
    win_keys = NA_WIN_H * GRID_W
    pairs = range(NA_HEADS // 2)
    lanes = [slice(p * LANES, (p + 1) * LANES) for p in pairs]
    stack = 2 * GRID_W

    def window(rr):
        r = j * rows_per_step + rr
        rs = jnp.clip(r - NA_WIN_H // 2, 0, n_rows - NA_WIN_H)
        return r - rs, pl.multiple_of(rr * GRID_W, GRID_W), pl.multiple_of(rs * GRID_W, GRID_W)

    def scores(rr, slot):
        _, t0, ks = window(rr)
        for p in pairs:
            qp = q_ref[0, pl.ds(t0, GRID_W), lanes[p]] * scale
            zero = jnp.zeros_like(qp)
            qs = jnp.concatenate([jnp.where(first_head, qp, zero), jnp.where(first_head, zero, qp)], axis=0)
            sw_scr[slot, p * stack:(p + 1) * stack, :] = lax.dot_general(
                qs, k_ref[0, pl.ds(ks, win_keys), lanes[p]], _NT, preferred_element_type=F32)
            sc_scr[slot, p * stack:(p + 1) * stack, :] = lax.dot_general(
                qs, kc_ref[0, :, lanes[p]], _NT, preferred_element_type=F32)

    def attend(rr, slot):
        d, t0, ks = window(rr)
        pws, pcs, dens = [], [], []
        for p in pairs:
            bias = jnp.concatenate([bias_ref[(NA_WIN_H - 1) - d + 2 * i, p] for i in range(NA_WIN_H // 2)], axis=1)
            sw = sw_scr[slot, p * stack:(p + 1) * stack, :] * LOG2E + bias
            sc = sc_scr[slot, p * stack:(p + 1) * stack, :] * LOG2E
            m = jnp.maximum(jnp.max(sw, axis=-1, keepdims=True), jnp.max(sc, axis=-1, keepdims=True))
            pw = jnp.exp2(sw - m)
            pc = jnp.exp2(sc - m)
            dens.append(jnp.sum(pw, axis=-1, keepdims=True) + jnp.sum(pc, axis=-1, keepdims=True))
            pws.append(pw.astype(BF16))
            pcs.append(pc.astype(BF16))
        for p in pairs:
            o2 = (jnp.dot(pws[p], v_ref[0, pl.ds(ks, win_keys), lanes[p]], preferred_element_type=F32)
                  + jnp.dot(pcs[p], vc_ref[0, :, lanes[p]], preferred_element_type=F32)) / dens[p]
            o_ref[0, pl.ds(t0, GRID_W), lanes[p]] = jnp.where(first_head, o2[:GRID_W],
                                                             o2[GRID_W:]).astype(o_ref.dtype)

    scores(0, 0)

    def two_rows(k, carry):
        scores(2 * k + 1, 1)
        attend(2 * k, 0)
        scores(2 * k + 2, 0)
        attend(2 * k + 1, 1)
        return carry

    lax.fori_loop(0, rows_per_step // 2 - 1, two_rows, 0)
    scores(rows_per_step - 1, 1)
    attend(rows_per_step - 2, 0)
    attend(rows_per_step - 1, 1)


def _na_bias_table(rpb):
    n_heads, n_dr, n_dc = rpb.shape
    c = np.arange(GRID_W)
    cstart = np.clip(c - NA_WIN_W // 2, 0, GRID_W - NA_WIN_W)
    kc = np.arange(GRID_W)
    valid = (kc[None, :] >= cstart[:, None]) & (kc[None, :] < cstart[:, None] + NA_WIN_W)
    col_idx = np.clip(kc[None, :] - c[:, None] + NA_WIN_W - 1, 0, n_dc - 1)
    pick = np.zeros((n_dc, GRID_W * GRID_W), np.float32)
    pick[col_idx.reshape(-1), np.arange(GRID_W * GRID_W)] = 1.0
    t = jnp.dot(rpb.reshape(n_heads * n_dr, n_dc).astype(F32), jnp.asarray(pick),
                precision=lax.Precision.HIGHEST).reshape(n_heads, n_dr, GRID_W, GRID_W)
    t = jnp.where(jnp.asarray(valid)[None, None], t * LOG2E, NEG_INF)
    t2 = jnp.concatenate([t[:, :-1], t[:, 1:]], axis=-1)
    return t2.transpose(1, 0, 2, 3).reshape(n_dr - 1, n_heads // 2, 2 * GRID_W, 2 * GRID_W)


def _na_attention(q, k, v, kc, vc, bias, rows_per_step=32):
    b, s, w = q.shape
    n_rows = s // GRID_W
    ctx_len = kc.shape[1]
    blk = rows_per_step * GRID_W
    return pl.pallas_call(
        functools.partial(_na_kernel, rows_per_step, n_rows),
        grid=(b, n_rows // rows_per_step),
        in_specs=[pl.BlockSpec((1, blk, w), lambda bi, j: (bi, j, 0)),
                  pl.BlockSpec((1, s, w), lambda bi, j: (bi, 0, 0)),
                  pl.BlockSpec((1, s, w), lambda bi, j: (bi, 0, 0)),
                  pl.BlockSpec((1, ctx_len, w), lambda bi, j: (bi, 0, 0)),
                  pl.BlockSpec((1, ctx_len, w), lambda bi, j: (bi, 0, 0)),
                  pl.BlockSpec(bias.shape, lambda bi, j: (0, 0, 0, 0), pipeline_mode=pl.Buffered(1))],
        out_specs=pl.BlockSpec((1, blk, w), lambda bi, j: (bi, j, 0)),
        out_shape=jax.ShapeDtypeStruct((b, s, w), BF16),
        scratch_shapes=[pltpu.VMEM((2, NA_HEADS * GRID_W, NA_WIN_H * GRID_W), F32),
                        pltpu.VMEM((2, NA_HEADS * GRID_W, ctx_len), F32)],
        compiler_params=_params(("arbitrary", "arbitrary")),
        name="na_attn",
    )(q, k, v, kc, vc, bias)


def _gla_kernel(n_groups, gq_ref, gk_ref, gv_ref, lr_ref, sog_ref, ck_ref, cv_ref, clr_ref,
                wa_ref, ba_ref, gn_ref, o_ref, st_scr, of_scr, qe_scr, kd_scr, ckd_scr, dec_scr, cdec_scr,
                qk_a):
    c_len = GLA_CHUNK
    grp = GLA_GROUP
    cpg = grp // c_len
    row = lax.broadcasted_iota(jnp.int32, (grp, grp), 0)
    col = lax.broadcasted_iota(jnp.int32, (grp, grp), 1)
    same_chunk = (row // c_len) == (col // c_len)
    masks = (same_chunk & (col <= row), same_chunk & (col >= row))

    dirs = range(2)

    def gates(lr_rows, k_rows):
        zs = [jnp.dot(lr_rows, wa_ref[t], preferred_element_type=F32) + ba_ref[t:t + 1, :] for t in dirs]
        gs = [(jnp.minimum(z, 0.0) * LOG2E - jnp.log2(1.0 + jnp.exp2(jnp.abs(z) * (-LOG2E)))) * (1.0 / GLA_GATE_NORM)
              for z in zs]
        g1s = [g.astype(BF16) for g in gs]
        g2s = [(g - g1.astype(F32)).astype(BF16) for g, g1 in zip(gs, g1s)]
        tris = [masks[t].astype(BF16) for t in dirs]
        bs = [jnp.dot(tris[t], g1s[t], preferred_element_type=F32)
              + jnp.dot(tris[t], g2s[t], preferred_element_type=F32) for t in dirs]
        out = []
        for t in dirs:
            edge = c_len - 1 if t == 0 else 0
            decs = [jnp.exp2(bs[t][j * c_len + edge:j * c_len + edge + 1, :]) for j in range(cpg)]
            ke = k_rows * jnp.exp2(-bs[t])
            kd = (ke * jnp.concatenate([jnp.broadcast_to(v, (c_len, v.shape[1])) for v in decs],
                                       axis=0)).astype(BF16)
            dec = jnp.concatenate(decs + [jnp.ones_like(v) for v in decs], axis=0)
            out.append((bs[t], ke.astype(BF16), kd, dec))
        return out

    for t, (_, _, kd, dec) in enumerate(gates(clr_ref[0], ck_ref[0].astype(F32))):
        ckd_scr[t] = kd
        cdec_scr[t] = dec

    def group_gates(gi, qk_buf):
        rows = pl.ds(pl.multiple_of(gi * grp, grp), grp)
        qf = gq_ref[0, rows, :].astype(F32)
        for t, (b, ke, kd, dec) in enumerate(gates(lr_ref[0, rows, :], gk_ref[0, rows, :].astype(F32))):
            qe = (qf * jnp.exp2(b)).astype(BF16)
            qk_buf[t] = qe
            qk_buf[2 + t] = ke
            qe_scr[t, rows, :] = qe
            kd_scr[t, rows, :] = kd
            dec_scr[t, gi] = dec

    def group_attend(gi, qk_buf):
        rows = pl.ds(pl.multiple_of(gi * grp, grp), grp)
        heads = range(GLA_HEADS)
        kls = [slice(h * GLA_HK, (h + 1) * GLA_HK) for h in heads]
        raw = [[lax.dot_general(qk_buf[t, :, kls[h]], qk_buf[2 + t, :, kls[h]], _NT, preferred_element_type=F32)
                for t in dirs] for h in heads]
        atts = [(jnp.where(masks[0], raw[h][0], 0.0) + jnp.where(masks[1], raw[h][1], 0.0)).astype(BF16)
                for h in heads]
        for h in heads:
            vl = slice(h * GLA_HV, (h + 1) * GLA_HV)
            of_scr[rows, vl] = jnp.dot(atts[h], gv_ref[0, rows, vl], preferred_element_type=F32)

    def group_body(gi, carry):
        group_gates(gi, qk_a)
        group_attend(gi, qk_a)
        return carry

    lax.fori_loop(0, n_groups, group_body, 0)

    def state_step(t, h, v_rows, kd_rows, dec_row):
        kl = slice(h * GLA_HK, (h + 1) * GLA_HK)
        upd = lax.dot_general(v_rows, kd_rows[:, kl], _TN, preferred_element_type=F32)
        st_scr[t, h] = st_scr[t, h] * dec_row[:, kl] + upd

    def chunk_order(t):
        return range(cpg) if t == 0 else range(cpg - 1, -1, -1)

    st_scr[...] = jnp.zeros_like(st_scr)
    for t in dirs:
        for j in chunk_order(t):
            rows = slice(j * c_len, (j + 1) * c_len)
            for h in range(GLA_HEADS):
                state_step(t, h, cv_ref[0, rows, h * GLA_HV:(h + 1) * GLA_HV],
                           ckd_scr[t, rows, :], cdec_scr[t, j:j + 1, :])

    def chunk_step(t, gi, j, finish):
        rows = pl.ds(pl.multiple_of(gi * grp + j * c_len, c_len), c_len)
        qe_rows = qe_scr[t, rows, :]
        kd_rows = kd_scr[t, rows, :]
        dec_row = dec_scr[t, gi, j:j + 1, :]
        for h in range(GLA_HEADS):
            kl = slice(h * GLA_HK, (h + 1) * GLA_HK)
            vl = slice(h * GLA_HV, (h + 1) * GLA_HV)
            o_t = of_scr[rows, vl] + lax.dot_general(qe_rows[:, kl], st_scr[t, h].astype(BF16), _NT,
                                                     preferred_element_type=F32)
            if finish:
                ms = jnp.mean(o_t * o_t, axis=-1, keepdims=True)
                o_n = o_t * lax.rsqrt(ms + EPS) * gn_ref[:, vl] * sog_ref[0, rows, vl].astype(F32)
                o_ref[0, rows, vl] = o_n.astype(o_ref.dtype)
            else:
                of_scr[rows, vl] = o_t
            state_step(t, h, gv_ref[0, rows, vl], kd_rows, dec_row)

    def both_scans(finish):
        def body(i, carry):
            for jf, jb in zip(chunk_order(0), chunk_order(1)):
                chunk_step(0, i, jf, finish)
                chunk_step(1, n_groups - 1 - i, jb, finish)
            return carry
        return body

    lax.fori_loop(0, n_groups // 2, both_scans(False), 0)
    lax.fori_loop(n_groups // 2, n_groups, both_scans(True), 0)


def _gla(gq, gk, gv, lr, sog, ck, cv, clr, wa, ba, gn):
    b, s, _ = gq.shape
    ctx_len = ck.shape[1]
    assert ctx_len == GLA_GROUP and s % (2 * GLA_GROUP) == 0
    n_groups = s // GLA_GROUP
    cpg = GLA_GROUP // GLA_CHUNK

    def per_batch(t, w):
        return pl.BlockSpec((1, t, w), lambda bi: (bi, 0, 0))

    return pl.pallas_call(
        functools.partial(_gla_kernel, n_groups),
        grid=(b,),
        in_specs=[per_batch(s, GLA_DK), per_batch(s, GLA_DK), per_batch(s, GLA_DV),
                  per_batch(s, LR_PAD), per_batch(s, GLA_DV),
                  per_batch(ctx_len, GLA_DK), per_batch(ctx_len, GLA_DV), per_batch(ctx_len, LR_PAD),
                  pl.BlockSpec(wa.shape, lambda bi: (0, 0, 0)),
                  pl.BlockSpec(ba.shape, lambda bi: (0, 0)),
                  pl.BlockSpec(gn.shape, lambda bi: (0, 0))],
        out_specs=per_batch(s, GLA_DV),
        out_shape=jax.ShapeDtypeStruct((b, s, GLA_DV), BF16),
        scratch_shapes=[pltpu.VMEM((2, GLA_HEADS, GLA_HV, GLA_HK), F32),
                        pltpu.VMEM((s, GLA_DV), F32),
                        pltpu.VMEM((2, s, GLA_DK), BF16),
                        pltpu.VMEM((2, s, GLA_DK), BF16),
                        pltpu.VMEM((2, ctx_len, GLA_DK), BF16),
                        pltpu.VMEM((2, n_groups, 2 * cpg, GLA_DK), F32),
                        pltpu.VMEM((2, 2 * cpg, GLA_DK), F32),
                        pltpu.VMEM((4, GLA_GROUP, GLA_DK), BF16)],
        compiler_params=_params(("arbitrary",), vmem=60 * 1024 * 1024),
        name="gla",
    )(gq, gk, gv, lr, sog, ck, cv, clr, wa, ba, gn)


def _merge_kernel(ona_ref, ogla_ref, sgn_ref, sgg_ref, x_ref, mod_ref, gffn_ref,
                  wna_ref, wgla_ref, wout_ref, wr_ref, br_ref,
                  h_ref, f_ref, route_ref, route_t_ref, cnt_ref):
    rows = x_ref.shape[0]
    y1 = (sgn_ref[...].astype(F32) * jnp.dot(ona_ref[...], wna_ref[...], preferred_element_type=F32)
          + sgg_ref[...].astype(F32) * jnp.dot(ogla_ref[...], wgla_ref[...], preferred_element_type=F32))
    y = jnp.dot(y1.astype(BF16), wout_ref[...], preferred_element_type=F32)
    h = x_ref[...] + mod_ref[0, 2:3, :] * y
    h_ref[...] = h
    ms = jnp.mean(h * h, axis=-1, keepdims=True)
    f = h * lax.rsqrt(ms + EPS) * gffn_ref[...] * (1.0 + mod_ref[0, 4:5, :]) + mod_ref[0, 3:4, :]
    f_ref[...] = f.astype(f_ref.dtype)

    f_hi = f.astype(BF16)
    f_lo = (f - f_hi.astype(F32)).astype(BF16)
    both = jnp.dot(f_hi, wr_ref[...], preferred_element_type=F32)
    logits = (both[:, :ROUTE_LANES] + both[:, ROUTE_LANES:]
              + jnp.dot(f_lo, wr_ref[:, :ROUTE_LANES], preferred_element_type=F32)) + br_ref[...]

    lt = logits.T[0:ROUTE_ROWS, :]
    er = lax.broadcasted_iota(jnp.int32, (ROUTE_ROWS, rows), 0)
    big = jnp.int32(ROUTE_ROWS)

    def first_argmax(vals):
        m = jnp.max(vals, axis=0, keepdims=True)
        idx = jnp.min(jnp.where(vals == m, er, big), axis=0, keepdims=True)
        return m, idx

    is_grp = er < N_GROUPS
    lg = jnp.where(is_grp, lt, -jnp.inf)
    mg, grp = first_argmax(lg)
    p_grp = 1.0 / jnp.sum(jnp.where(is_grp, jnp.exp(lg - mg), 0.0), axis=0, keepdims=True)
    lo = EXPERT_LANE0 + grp * EXPERTS_PER_GROUP
    in_grp = (er >= lo) & (er < lo + EXPERTS_PER_GROUP)
    le = jnp.where(in_grp, lt, -jnp.inf)
    m1, i1 = first_argmax(le)
    m2, i2 = first_argmax(jnp.where(er == i1, -jnp.inf, le))
    t = jnp.exp(m2 - m1)
    w1 = p_grp / (1.0 + t)
    w2 = p_grp * t / (1.0 + t)

    hot1 = er == i1
    hot2 = er == i2
    onehot = (hot1 | hot2).astype(BF16)
    s_i = lax.broadcasted_iota(jnp.int32, (rows, rows), 0)
    t_i = lax.broadcasted_iota(jnp.int32, (rows, rows), 1)
    rank = jnp.dot(onehot, (s_i < t_i).astype(BF16), preferred_element_type=F32)
    cnt = jnp.sum(onehot.astype(F32), axis=1, keepdims=True)
    cnt_al = jnp.floor((cnt + (SEG_ALIGN - 1.0)) * (1.0 / SEG_ALIGN)) * SEG_ALIGN
    e_i = lax.broadcasted_iota(jnp.int32, (ROUTE_ROWS, ROUTE_ROWS), 0)
    e_j = lax.broadcasted_iota(jnp.int32, (ROUTE_ROWS, ROUTE_ROWS), 1)
    lstart = jnp.dot((e_j < e_i).astype(BF16), jnp.broadcast_to(cnt_al, (ROUTE_ROWS, LANES)).astype(BF16),
                     preferred_element_type=F32)[:, 0:1]
    pos = rank + lstart
    pos1 = jnp.sum(jnp.where(hot1, pos, 0.0), axis=0, keepdims=True)
    pos2 = jnp.sum(jnp.where(hot2, pos, 0.0), axis=0, keepdims=True)

    e1 = (i1 - EXPERT_LANE0).astype(F32)
    e2 = (i2 - EXPERT_LANE0).astype(F32)
    fields = jnp.concatenate([e1, e2, w1, w2, pos1, pos2, jnp.zeros((2, rows), F32)], axis=0)
    route_t_ref[0] = fields
    route_ref[...] = jnp.concatenate([fields, jnp.zeros((ROUTE_LANES - 8, rows), F32)], axis=0).T
    cnt_rows = jnp.concatenate([jnp.broadcast_to(cnt, (ROUTE_ROWS, ROUTE_LANES)),
                                jnp.zeros((ROUTE_LANES - ROUTE_ROWS, ROUTE_LANES), F32)], axis=0)
    cnt_ref[0] = cnt_rows.T[0:1, :]


def _merge_route(ona, ogla, sgn, sgg, x2d, mod3, gffn, wna, wgla, wout, wr, br, tokens_per_batch, tm):
    n, d = x2d.shape
    per_b = tokens_per_batch // tm

    def tile(w):
        return pl.BlockSpec((tm, w), lambda i: (i, 0))

    def whole(a):
        return pl.BlockSpec(a.shape, lambda i: (0,) * a.ndim)

    return pl.pallas_call(
        _merge_kernel,
        grid=(n // tm,),
        in_specs=[tile(NA_WIDTH), tile(GLA_DV), tile(d), tile(d), tile(d),
                  pl.BlockSpec((1, 6, d), lambda i: (i // per_b, 0, 0)),
                  whole(gffn), whole(wna), whole(wgla), whole(wout), whole(wr), whole(br)],
        out_specs=[tile(d), tile(d), tile(ROUTE_LANES),
                   pl.BlockSpec((1, 8, tm), lambda i: (i, 0, 0)),
                   pl.BlockSpec((1, 1, ROUTE_LANES), lambda i: (i, 0, 0))],
        out_shape=[jax.ShapeDtypeStruct((n, d), F32),
                   jax.ShapeDtypeStruct((n, d), BF16),
                   jax.ShapeDtypeStruct((n, ROUTE_LANES), F32),
                   jax.ShapeDtypeStruct((n // tm, 8, tm), F32),
                   jax.ShapeDtypeStruct((n // tm, 1, ROUTE_LANES), F32)],
        compiler_params=_params(("arbitrary",)),
        name="merge_route",
    )(ona, ogla, sgn, sgg, x2d, mod3, gffn, wna, wgla, wout, wr, br)


HI_MASK = -65536


def _pack_rows(a):
    half = a.shape[1] // 2
    hi = lax.bitcast_convert_type(a[:, :half], jnp.int32)
    lo = lax.shift_right_logical(lax.bitcast_convert_type(a[:, half:], jnp.int32), 16)
    return hi | lo


def _unpack_rows(w):
    hi = lax.bitcast_convert_type(w & HI_MASK, F32).astype(BF16)
    lo = lax.bitcast_convert_type(lax.shift_left(w, 16), F32).astype(BF16)
    return hi, lo


def _piece_table(cnt_al, lstart, seg_dst):
    n_tiles = cnt_al.shape[0]

    def compact(mask, local, glob, slots):
        pos = jnp.cumsum(mask.astype(jnp.int32), axis=1) - 1
        hot = mask[:, :, None] & (pos[:, :, None] == jnp.arange(slots, dtype=jnp.int32))
        pick = lambda v: jnp.sum(jnp.where(hot, v[:, :, None], 0), axis=1)
        return jnp.sum(mask, axis=1).astype(jnp.int32), pick(local), pick(glob)

    counts, lists = [], []
    big = PIECE_SIZES[-1]
    for size, slots in zip(PIECE_SIZES[:-1], PIECE_SLOTS[:-1]):
        n, lo, gl = compact((cnt_al & (big - 1)) == size, lstart, seg_dst, slots)
        counts.append(n)
        lists += [lo, gl]
    k = jnp.arange(MAX_BIG_PER_RUN, dtype=jnp.int32)
    off = (cnt_al & (big - 1))[:, :, None] + big * k
    mask = k < (cnt_al // big)[:, :, None]
    flat = lambda a: a.reshape(n_tiles, -1)
    n, lo, gl = compact(flat(mask), flat(lstart[:, :, None] + off), flat(seg_dst[:, :, None] + off), PIECE_SLOTS[-1])
    counts.append(n)
    lists += [lo, gl]
    head = jnp.stack(counts + [jnp.sum(cnt_al, axis=1)], axis=1)
    head = jnp.pad(head, ((0, 0), (0, PIECE_HEAD - head.shape[1])))
    table = jnp.concatenate([head] + lists, axis=1)
    table = jnp.pad(table, ((0, 0), (0, PIECE_TABLE - table.shape[1])))
    return table.reshape(n_tiles, 1, PIECE_TABLE).astype(jnp.int32)


def _segment_copies(seg_ref, make_copy, action):
    if action == "wait":
        total = seg_ref[0, 0, len(PIECE_SIZES)]
        for size in WAIT_SIZES:
            @pl.when((total & size) != 0)
            def _():
                make_copy(0, 0, size).wait()
        return

    base = PIECE_HEAD
    for c, (size, slots) in enumerate(zip(PIECE_SIZES, PIECE_SLOTS)):
        def piece(k, carry, size=size, lo=base, gl=base + slots):
            make_copy(pl.multiple_of(seg_ref[0, 0, lo + k], SEG_ALIGN),
                      pl.multiple_of(seg_ref[0, 0, gl + k], SEG_ALIGN), size).start()
            return carry

        lax.fori_loop(0, seg_ref[0, 0, c], piece, 0)
        base += 2 * slots


def _dispatch_kernel(seg_ref, seg_prev_ref, tail_ref, lpos_ref, f_ref, xs_ref, sbuf, zbuf, sems):
    i = pl.program_id(0)
    last = pl.num_programs(0) - 1
    tm = f_ref.shape[0]
    rows_l = sbuf.shape[1]
    slot = i % 2
    fields = lpos_ref[0]
    lp = fields.astype(jnp.int32)
    half = f_ref.shape[1] // 2
    row = lax.broadcasted_iota(jnp.int32, (rows_l, tm), 0)
    first = row == lp[4:5, :]
    second = row == lp[5:6, :]
    sbuf[slot, :, :half] = _pack_rows(jnp.dot((first | second).astype(BF16), f_ref[...], preferred_element_type=F32))
    w_sorted = jnp.sum(jnp.where(first, fields[2:3, :], 0.0) + jnp.where(second, fields[3:4, :], 0.0),
                       axis=1, keepdims=True)
    sbuf[slot, :, half:] = lax.bitcast_convert_type(jnp.broadcast_to(w_sorted, (rows_l, LANES)), jnp.int32)

    def copier(s):
        def copy(local, glob, size):
            return pltpu.make_async_copy(sbuf.at[s, pl.ds(local, size), :], xs_ref.at[pl.ds(glob, size), :],
                                         sems.at[s])
        return copy

    _segment_copies(seg_ref, copier(slot), "start")

    @pl.when(i > 0)
    def _():
        _segment_copies(seg_prev_ref, copier(1 - slot), "wait")

    @pl.when(i == last)
    def _():
        _segment_copies(seg_ref, copier(slot), "wait")
        sem = sems.at[slot]
        zbuf[...] = jnp.zeros_like(zbuf)

        def tail(action):
            def body(e, carry):
                start = tail_ref[0, 0, e]
                length = tail_ref[0, 0, N_EXPERTS + e]
                off = jnp.int32(0)
                for size in TAIL_SIZES:
                    @pl.when((length & size) != 0)
                    def _():
                        cp = pltpu.make_async_copy(zbuf.at[pl.ds(0, size), :],
                                                   xs_ref.at[pl.ds(pl.multiple_of(start + off, SEG_ALIGN), size), :],
                                                   sem)
                        getattr(cp, action)()
                    off = off + (length & size)
                return carry
            lax.fori_loop(0, N_EXPERTS, body, 0)

        tail("start")
        tail("wait")

        def unused(action):
            def body(blk, carry):
                cp = pltpu.make_async_copy(zbuf, xs_ref.at[pl.ds(pl.multiple_of(blk * MOE_ROWS, MOE_ROWS),
                                                                  MOE_ROWS), :], sem)
                getattr(cp, action)()
                return carry
            lax.fori_loop(tail_ref[0, 0, 2 * N_EXPERTS], xs_ref.shape[0] // MOE_ROWS, body, 0)

        unused("start")
        unused("wait")


def _dispatch(seg, tail, lpos_rows, f2d, n_pad, rows_l, tm):
    n, d = f2d.shape
    return pl.pallas_call(
        _dispatch_kernel,
        grid=(n // tm,),
        in_specs=[pl.BlockSpec((1, 1, seg.shape[2]), lambda i: (i, 0, 0), memory_space=pltpu.SMEM),
                  pl.BlockSpec((1, 1, seg.shape[2]), lambda i: (jnp.maximum(i - 1, 0), 0, 0),
                               memory_space=pltpu.SMEM),
                  pl.BlockSpec(tail.shape, lambda i: (0, 0, 0), memory_space=pltpu.SMEM),
                  pl.BlockSpec((1, 8, tm), lambda i: (i, 0, 0)),
                  pl.BlockSpec((tm, d), lambda i: (i, 0))],
        out_specs=pl.BlockSpec(memory_space=pl.ANY),
        out_shape=jax.ShapeDtypeStruct((n_pad, d // 2 + LANES), jnp.int32),
        scratch_shapes=[pltpu.VMEM((2, rows_l, d // 2 + LANES), jnp.int32),
                        pltpu.VMEM((MOE_ROWS, d // 2 + LANES), jnp.int32),
                        pltpu.SemaphoreType.DMA((2,))],
        compiler_params=_params(("arbitrary",)),
        name="dispatch",
    )(seg, seg, tail, lpos_rows, f2d)


def _expert_kernel(be_ref, nu_ref, x_ref, wg_ref, wu_ref, wd_ref, y_ref, wgu_scr, wd_scr):
    i = pl.program_id(0)
    prev = be_ref[jnp.maximum(i - 1, 0)]

    @pl.when((i == 0) | (be_ref[i] != prev))
    def _():
        wgu_scr[:, :D_EXPERT] = wg_ref[0].astype(BF16)
        wgu_scr[:, D_EXPERT:] = wu_ref[0].astype(BF16)
        wd_scr[...] = wd_ref[0].astype(BF16)

    @pl.when(i < nu_ref[0])
    def _():
        half = y_ref.shape[1]
        x = jnp.concatenate(_unpack_rows(x_ref[:, :half]), axis=1)
        gu = jnp.dot(x, wgu_scr[...], preferred_element_type=F32)
        gate = gu[:, :D_EXPERT]
        hdn = gate * jax.nn.sigmoid(gate) * gu[:, D_EXPERT:]
        y = jnp.dot(hdn.astype(BF16), wd_scr[...], preferred_element_type=F32)
        y = y * lax.bitcast_convert_type(x_ref[:, half:half + 1], F32)
        y_ref[...] = _pack_rows(y.astype(BF16).astype(F32))

    @pl.when(i >= nu_ref[0])
    def _():
        y_ref[...] = jnp.zeros_like(y_ref)


def _experts(blk_expert, n_used, xs, wg, wu, wd):
    n_pad, row_words = xs.shape
    half = row_words - LANES
    d = 2 * half
    n_blk = n_pad // MOE_ROWS

    def used_block(i, be, nu):
        return (jnp.maximum(jnp.minimum(i, nu[0] - 1), 0), 0)

    grid_spec = pltpu.PrefetchScalarGridSpec(
        num_scalar_prefetch=2,
        grid=(n_blk,),
        in_specs=[pl.BlockSpec((MOE_ROWS, row_words), used_block),
                  pl.BlockSpec((1, d, D_EXPERT), lambda i, be, nu: (be[i], 0, 0)),
                  pl.BlockSpec((1, d, D_EXPERT), lambda i, be, nu: (be[i], 0, 0)),
                  pl.BlockSpec((1, D_EXPERT, d), lambda i, be, nu: (be[i], 0, 0))],
        out_specs=pl.BlockSpec((MOE_ROWS, half), lambda i, be, nu: (i, 0)),
        scratch_shapes=[pltpu.VMEM((d, 2 * D_EXPERT), BF16),
                        pltpu.VMEM((D_EXPERT, d), BF16)],
    )
    return pl.pallas_call(
        _expert_kernel,
        grid_spec=grid_spec,
        out_shape=jax.ShapeDtypeStruct((n_pad, half), jnp.int32),
        compiler_params=_params(("arbitrary",)),
        name="experts",
    )(blk_expert, n_used, xs, wg, wu, wd)


def _combine_kernel(seg_ref, seg_next_ref, col_ref, h_ref, mod_ref, fg_ref, ys_ref, o_ref, ybuf, sems):
    i = pl.program_id(0)
    tm = h_ref.shape[0]
    rows_l = ybuf.shape[1]
    slot = i % 2

    def copier(s):
        def copy(local, glob, size):
            return pltpu.make_async_copy(ys_ref.at[pl.ds(glob, size), :], ybuf.at[s, pl.ds(local, size), :],
                                         sems.at[s])
        return copy

    @pl.when(i == 0)
    def _():
        ybuf[...] = jnp.zeros_like(ybuf)
        _segment_copies(seg_ref, copier(slot), "start")

    @pl.when(i + 1 < pl.num_programs(0))
    def _():
        _segment_copies(seg_next_ref, copier(1 - slot), "start")

    _segment_copies(seg_ref, copier(slot), "wait")
    y_hi, y_lo = _unpack_rows(ybuf[slot])
    y_all = jnp.concatenate([y_hi, y_lo], axis=1)
    col = lax.broadcasted_iota(jnp.int32, (tm, rows_l), 1)
    info = col_ref[...]
    pick = (col == info[:, 4:5].astype(jnp.int32)) | (col == info[:, 5:6].astype(jnp.int32))
    moe = jnp.dot(pick.astype(BF16), y_all, preferred_element_type=F32)
    h = h_ref[...] + mod_ref[0, 5:6, :] * moe
    ms = jnp.mean(h * h, axis=-1, keepdims=True)
    o_ref[...] = h * lax.rsqrt(ms + EPS) * fg_ref[...]


def _combine(seg, colinfo, h2d, mod3, fg, ys, rows_l, tokens_per_batch, tm):
    n, d = h2d.shape
    per_b = tokens_per_batch // tm
    return pl.pallas_call(
        _combine_kernel,
        grid=(n // tm,),
        in_specs=[pl.BlockSpec((1, 1, seg.shape[2]), lambda i: (i, 0, 0), memory_space=pltpu.SMEM),
                  pl.BlockSpec((1, 1, seg.shape[2]), lambda i: (jnp.minimum(i + 1, n // tm - 1), 0, 0),
                               memory_space=pltpu.SMEM),
                  pl.BlockSpec((tm, colinfo.shape[1]), lambda i: (i, 0)),
                  pl.BlockSpec((tm, d), lambda i: (i, 0)),
                  pl.BlockSpec((1, 6, d), lambda i: (i // per_b, 0, 0)),
                  pl.BlockSpec((1, d), lambda i: (0, 0)),
                  pl.BlockSpec(memory_space=pl.ANY)],
        out_specs=pl.BlockSpec((tm, d), lambda i: (i, 0)),
        out_shape=jax.ShapeDtypeStruct((n, d), F32),
        scratch_shapes=[pltpu.VMEM((2, rows_l, d // 2), jnp.int32), pltpu.SemaphoreType.DMA((2,))],
        compiler_params=_params(("arbitrary",)),
        name="combine",
    )(seg, seg, colinfo, h2d, mod3, fg, ys)


def _rope_tables(s):
    t = np.arange(s)
    pos_r = (t // GRID_W).astype(np.float32)
    pos_c = (t % GRID_W).astype(np.float32)
    nf = GLA_HK // 4
    inv = (np.float32(ROPE_BASE) ** (-np.arange(nf, dtype=np.float32) / np.float32(nf))).astype(np.float32)
    ang_r = pos_r[:, None] * inv
    ang_c = pos_c[:, None] * inv

    def half(ang):
        return (np.concatenate([np.cos(ang), np.cos(ang)], axis=-1),
                np.concatenate([-np.sin(ang), np.sin(ang)], axis=-1))

    cr, sr = half(ang_r)
    cc, sc = half(ang_c)
    cos = np.tile(np.concatenate([cr, cc], axis=-1), (1, GLA_HEADS)).astype(np.float32)
    sin = np.tile(np.concatenate([sr, sc], axis=-1), (1, GLA_HEADS)).astype(np.float32)
    return jnp.asarray(cos), jnp.asarray(sin)


def kernel(x, c, ctx, c_ctx, w_mod, b_mod, norm_attn_g, norm_ffn_g, w_in, w_gla_a2, b_gla_a2, gla_norm_g, na_rpb, w_na_o, w_gla_o, w_out, w_group, b_group, w_expert, b_expert, w_exp_gate, w_exp_up, w_exp_down, final_norm_g):
    b, s, d = x.shape
    ctx_len = ctx.shape[1]
    n = b * s
    assert w_mod.shape[0] == 1, "single-layer block"
    assert s % (GRID_W * NA_WIN_H) == 0 and ctx_len % GLA_CHUNK == 0

    mod_rows = -(-(b + 1) // 8) * 8
    cs = jnp.zeros((mod_rows, d), F32).at[:b].set(c).at[b].set(c_ctx)
    mod3 = _modulation(cs, w_mod[0], b_mod[0]).reshape(mod_rows, 6, d)

    o_q, o_k, o_v = 0, NA_WIDTH, 2 * NA_WIDTH
    o_gq = 3 * NA_WIDTH
    o_gk = o_gq + GLA_DK
    o_gv = o_gk + GLA_DK
    o_og = o_gv + GLA_DV
    o_lr = o_og + GLA_DV
    o_mn = o_lr + 2 * GLA_GATE_RANK
    w_bf = w_in[0].astype(BF16)
    w_a = (w_bf, o_lr)
    w_g = (w_bf[:, o_mn:], 2 * d)
    w_lr = (jnp.pad(w_bf[:, o_lr:o_mn], ((0, 0), (0, LR_PAD - 2 * GLA_GATE_RANK))), LR_PAD)
    cw = 512
    lat_specs = [(0, 0, 0, o_q, cw, "plain"), (1, 0, 0, o_k, cw, "plain"), (2, 0, 0, o_v, cw, "plain"),
                 (3, 0, 0, o_gq, cw, "rope_scaled"), (4, 0, 0, o_gk, cw, "rope"),
                 (5, 0, 0, o_gv, cw, "plain"), (5, cw, 0, o_gv + cw, cw, "plain"),
                 (6, 0, 0, o_og, cw, "silu"), (6, cw, 0, o_og + cw, cw, "silu"),
                 (7, 0, 1, 0, cw, "sigmoid"), (7, cw, 1, cw, cw, "sigmoid"),
                 (8, 0, 1, d, cw, "sigmoid"), (8, cw, 1, d + cw, cw, "sigmoid"),
                 (9, 0, 2, 0, LR_PAD, "plain")]
    lat_widths = [NA_WIDTH, NA_WIDTH, NA_WIDTH, GLA_DK, GLA_DK, GLA_DV, GLA_DV, d, d, LR_PAD]
    tm = 512
    per_b = s // tm
    q_na, k_na, v_na, gq, gk, gv, sog, sgn, sgg, lr = _inproj(
        x.reshape(n, d), mod3, lambda i: i // per_b, norm_attn_g[0], [w_a, w_g, w_lr], lat_specs, lat_widths, tm,
        rope=_rope_tables(s), name="inproj_lat")

    ctx_specs = [(0, 0, 0, o_k, cw, "plain"), (1, 0, 0, o_v, cw, "plain"), (2, 0, 0, o_gk, cw, "plain"),
                 (3, 0, 0, o_gv, cw, "plain"), (3, cw, 0, o_gv + cw, cw, "plain"),
                 (4, 0, 1, 0, LR_PAD, "plain")]
    ctx_widths = [NA_WIDTH, NA_WIDTH, GLA_DK, GLA_DV, LR_PAD]
    kc_na, vc_na, ck, cv, clr = _inproj(
        ctx.reshape(b * ctx_len, d), mod3, lambda i: b, norm_attn_g[0], [w_a, w_lr], ctx_specs, ctx_widths,
        tm if (b * ctx_len) % tm == 0 else ctx_len, name="inproj_ctx")

    def lat3(a):
        return a.reshape(b, s, a.shape[-1])

    def ctx3(a):
        return a.reshape(b, ctx_len, a.shape[-1])

    o_na = _na_attention(lat3(q_na), lat3(k_na), lat3(v_na), ctx3(kc_na), ctx3(vc_na),
                         _na_bias_table(na_rpb[0]))

    wa = jnp.zeros((2, LR_PAD, GLA_DK), F32)
    wa = wa.at[0, :GLA_GATE_RANK].set(w_gla_a2[0, 0]).at[1, GLA_GATE_RANK:2 * GLA_GATE_RANK].set(w_gla_a2[0, 1])
    gn = jnp.tile(gla_norm_g[0], GLA_HEADS).reshape(1, GLA_DV)
    o_gla = _gla(lat3(gq), lat3(gk), lat3(gv), lat3(lr), lat3(sog), ctx3(ck), ctx3(cv), ctx3(clr),
                 wa.astype(BF16), b_gla_a2[0], gn)

    wr = jnp.zeros((d, ROUTE_LANES), F32)
    wr = wr.at[:, :N_GROUPS].set(w_group[0]).at[:, EXPERT_LANE0:EXPERT_LANE0 + N_EXPERTS].set(w_expert[0])
    wr_hi = wr.astype(BF16)
    br = jnp.zeros((1, ROUTE_LANES), F32)
    br = br.at[0, :N_GROUPS].set(b_group[0]).at[0, EXPERT_LANE0:EXPERT_LANE0 + N_EXPERTS].set(b_expert[0])
    h, f_lat, route, route_t, counts = _merge_route(
        o_na.reshape(n, NA_WIDTH), o_gla.reshape(n, GLA_DV), sgn, sgg, x.reshape(n, d), mod3, norm_ffn_g[0].reshape(1, d),
        w_na_o[0].astype(BF16), w_gla_o[0].astype(BF16), w_out[0].astype(BF16),
        jnp.concatenate([wr_hi, (wr - wr_hi.astype(F32)).astype(BF16)], axis=1), br, s, tm)

    n_tiles = n // tm
    cnt = counts[:, 0, EXPERT_LANE0:EXPERT_LANE0 + N_EXPERTS].astype(jnp.int32)
    cnt_al = (cnt + SEG_ALIGN - 1) // SEG_ALIGN * SEG_ALIGN
    lstart = jnp.cumsum(cnt_al, axis=1) - cnt_al
    total = jnp.sum(cnt_al, axis=0)
    padded = (total + MOE_ROWS - 1) // MOE_ROWS * MOE_ROWS
    pend = jnp.cumsum(padded)
    pstart = pend - padded
    seg_dst = pstart[None, :] + jnp.cumsum(cnt_al, axis=0) - cnt_al
    assert tm // PIECE_SIZES[-1] <= MAX_BIG_PER_RUN
    assert (2 * tm + N_EXPERTS * (SEG_ALIGN - 1)) // PIECE_SIZES[-1] <= PIECE_SLOTS[-1]
    seg = _piece_table(cnt_al, lstart, seg_dst)
    n_used = (pend[-1] // MOE_ROWS).reshape(1)
    tail = jnp.concatenate([pstart + total, padded - total, n_used, jnp.zeros((2 * N_EXPERTS - 1,), jnp.int32)])
    tail = tail.reshape(1, 1, 4 * N_EXPERTS)
    seg_pad = n_tiles * N_EXPERTS * (SEG_ALIGN - 1)
    n_pad = (2 * n + seg_pad + N_EXPERTS * (MOE_ROWS - SEG_ALIGN) + MOE_ROWS - 1) // MOE_ROWS * MOE_ROWS
    n_blk = n_pad // MOE_ROWS
    rows_l = (2 * tm + N_EXPERTS * (SEG_ALIGN - 1) + MXU_DIM - 1) // MXU_DIM * MXU_DIM
    blk_start = jnp.arange(n_blk, dtype=jnp.int32) * MOE_ROWS
    blk_expert = jnp.minimum(jnp.sum(blk_start[:, None] >= pend[None, :], axis=-1), N_EXPERTS - 1)

    xs = _dispatch(seg, tail, route_t, f_lat, n_pad, rows_l, tm)
    ys = _experts(blk_expert.astype(jnp.int32), n_used.astype(jnp.int32), xs,
                  w_exp_gate[0], w_exp_up[0], w_exp_down[0])
    out = _combine(seg, route, h, mod3, final_norm_g.reshape(1, d), ys, rows_l, s, tm)
    return out.reshape(b, s, d)
```

```python
import functools

import jax
import jax.numpy as jnp
import numpy as np
from jax import lax
from jax.experimental import pallas as pl
from jax.experimental.pallas import tpu as pltpu

F32 = jnp.float32
BF16 = jnp.bfloat16

D_MODEL = 1024
GRID_W = 64
NA_HEADS = 8
NA_HEAD_DIM = 64
NA_WIDTH = NA_HEADS * NA_HEAD_DIM
NA_WIN_H = 8
NA_WIN_W = 16
GLA_HEADS = 4
GLA_DK = D_MODEL // 2
GLA_DV = D_MODEL
GLA_HK = GLA_DK // GLA_HEADS
GLA_HV = GLA_DV // GLA_HEADS
GLA_GATE_RANK = 16
GLA_GATE_NORM = 16.0
GLA_CHUNK = 64
GLA_GROUP = 256
ROPE_BASE = 10000.0
N_GROUPS = 4
EXPERTS_PER_GROUP = 8
N_EXPERTS = N_GROUPS * EXPERTS_PER_GROUP
D_EXPERT = D_MODEL // 2
EPS = 1e-6
NEG_INF = -1e30
LOG2E = 1.4426950408889634

LANES = 128
MXU_DIM = 256
LR_PAD = LANES
ROUTE_LANES = LANES
EXPERT_LANE0 = N_GROUPS
ROUTE_ROWS = 48
MOE_ROWS = 512
SEG_ALIGN = 8
PIECE_SIZES = (8, 16, 24, 32, 40, 48, 56, 64)
PIECE_SLOTS = (32, 32, 32, 32, 32, 32, 32, 24)
MAX_BIG_PER_RUN = 8
PIECE_HEAD = 16
PIECE_TABLE = 512
WAIT_SIZES = (1024, 512, 256, 128, 64, 32, 16, 8)
TAIL_SIZES = (256, 128, 64, 32, 16, 8)
VMEM_LIMIT = 56 * 1024 * 1024

_NT = (((1,), (1,)), ((), ()))
_TN = (((0,), (0,)), ((), ()))


def _params(sem, vmem=VMEM_LIMIT):
    return pltpu.CompilerParams(dimension_semantics=sem, vmem_limit_bytes=vmem)


def _mod_kernel(c_ref, w_ref, b_ref, o_ref):
    c = c_ref[...]
    s = c * jax.nn.sigmoid(c)
    o_ref[...] = jnp.dot(s.astype(BF16), w_ref[...].astype(BF16),
                         preferred_element_type=F32) + b_ref[...]


def _modulation(cs, w_mod, b_mod):
    rows, d = cs.shape
    n = w_mod.shape[1]
    bn = 1024
    return pl.pallas_call(
        _mod_kernel,
        grid=(n // bn,),
        in_specs=[pl.BlockSpec((rows, d), lambda j: (0, 0)),
                  pl.BlockSpec((d, bn), lambda j: (0, j)),
                  pl.BlockSpec((1, bn), lambda j: (0, j))],
        out_specs=pl.BlockSpec((rows, bn), lambda j: (0, j)),
        out_shape=jax.ShapeDtypeStruct((rows, n), F32),
        compiler_params=_params(("arbitrary",)),
        name="mod",
    )(cs, w_mod, b_mod.reshape(1, n))


def _inproj_kernel(specs, use_rope, n_w, n_out, x_ref, mod_ref, g_ref, *rest):
    if use_rope:
        cos_ref, sin_ref = rest[:2]
        rest = rest[2:]
    w_refs = rest[:n_w]
    outs = rest[n_w:n_w + n_out]
    a_scr = rest[n_w + n_out]
    x = x_ref[...]
    ms = jnp.mean(x * x, axis=-1, keepdims=True)
    a = x * lax.rsqrt(ms + EPS) * g_ref[...] * (1.0 + mod_ref[0, 1:2, :]) + mod_ref[0, 0:1, :]
    a_scr[...] = a.astype(BF16)
    for (oi, oc, wi, wc, width, kind) in specs:
        acc = jnp.dot(a_scr[...], w_refs[wi][:, wc:wc + width], preferred_element_type=F32)
        if kind in ("rope", "rope_scaled"):
            if kind == "rope_scaled":
                acc = acc * (GLA_HK ** -0.5)
            lane = lax.broadcasted_iota(jnp.int32, acc.shape, 1)
            rot = jnp.where((lane & 63) < 32,
                            pltpu.roll(acc, width - 32, axis=1),
                            pltpu.roll(acc, 32, axis=1))
            acc = acc * cos_ref[...] + rot * sin_ref[...]
        elif kind == "silu":
            acc = acc * jax.nn.sigmoid(acc)
        elif kind == "sigmoid":
            acc = jax.nn.sigmoid(acc)
        outs[oi][:, oc:oc + width] = acc.astype(outs[oi].dtype)


def _inproj(x2d, mod3, mod_row_fn, g, weights, specs, out_widths, tm, rope=None, name="inproj"):
    n, d = x2d.shape
    use_rope = rope is not None
    in_specs = [pl.BlockSpec((tm, d), lambda i: (i, 0)),
                pl.BlockSpec((1, 6, d), lambda i: (mod_row_fn(i), 0, 0)),
                pl.BlockSpec((1, d), lambda i: (0, 0))]
    args = [x2d, mod3, g.reshape(1, d)]
    if use_rope:
        cos, sin = rope
        nb = cos.shape[0] // tm
        in_specs += [pl.BlockSpec((tm, cos.shape[1]), lambda i: (i % nb, 0)),
                     pl.BlockSpec((tm, cos.shape[1]), lambda i: (i % nb, 0))]
        args += [cos, sin]
    in_specs += [pl.BlockSpec((d, cols), lambda i: (0, 0), pipeline_mode=pl.Buffered(1)) for _, cols in weights]
    args += [w for w, _ in weights]
    out_specs = [pl.BlockSpec((tm, w), lambda i: (i, 0)) for w in out_widths]
    out_shape = [jax.ShapeDtypeStruct((n, w), BF16) for w in out_widths]
    return pl.pallas_call(
        functools.partial(_inproj_kernel, specs, use_rope, len(weights), len(out_widths)),
        grid=(n // tm,),
        in_specs=in_specs,
        out_specs=out_specs,
        out_shape=out_shape,
        scratch_shapes=[pltpu.VMEM((tm, d), BF16)],
        compiler_params=_params(("arbitrary",)),
        name=name,
    )(*args)


def _na_kernel(rows_per_step, n_rows, q_ref, k_ref, v_ref, kc_ref, vc_ref, bias_ref, o_ref, sw_scr, sc_scr):
    j = pl.program_id(1)
    lane = lax.broadcasted_iota(jnp.int32, (GRID_W, LANES), 1)
    first_head = lane < NA_HEAD_DIM
    scale = NA_HEAD_DIM ** -0.5
    win_keys = NA_WIN_H * GRID_W
    pairs = range(NA_HEADS // 2)
    lanes = [slice(p * LANES, (p + 1) * LANES) for p in pairs]
    stack = 2 * GRID_W

    def window(rr):
        r = j * rows_per_step + rr
        rs = jnp.clip(r - NA_WIN_H // 2, 0, n_rows - NA_WIN_H)
        return r - rs, pl.multiple_of(rr * GRID_W, GRID_W), pl.multiple_of(rs * GRID_W, GRID_W)

    def scores(rr, slot):
        _, t0, ks = window(rr)
        for p in pairs:
            qp = q_ref[0, pl.ds(t0, GRID_W), lanes[p]] * scale
            zero = jnp.zeros_like(qp)
            qs = jnp.concatenate([jnp.where(first_head, qp, zero), jnp.where(first_head, zero, qp)], axis=0)
            sw_scr[slot, p * stack:(p + 1) * stack, :] = lax.dot_general(
                qs, k_ref[0, pl.ds(ks, win_keys), lanes[p]], _NT, preferred_element_type=F32)
            sc_scr[slot, p * stack:(p + 1) * stack, :] = lax.dot_general(
                qs, kc_ref[0, :, lanes[p]], _NT, preferred_element_type=F32)

    def attend(rr, slot):
        d, t0, ks = window(rr)
        pws, pcs, dens = [], [], []
        for p in pairs:
            bias = jnp.concatenate([bias_ref[(NA_WIN_H - 1) - d + 2 * i, p] for i in range(NA_WIN_H // 2)], axis=1)
            sw = sw_scr[slot, p * stack:(p + 1) * stack, :] * LOG2E + bias
            sc = sc_scr[slot, p * stack:(p + 1) * stack, :] * LOG2E
            m = jnp.maximum(jnp.max(sw, axis=-1, keepdims=True), jnp.max(sc, axis=-1, keepdims=True))
            pw = jnp.exp2(sw - m)
            pc = jnp.exp2(sc - m)
            dens.append(jnp.sum(pw, axis=-1, keepdims=True) + jnp.sum(pc, axis=-1, keepdims=True))
            pws.append(pw.astype(BF16))
            pcs.append(pc.astype(BF16))
        for p in pairs:
            o2 = (jnp.dot(pws[p], v_ref[0, pl.ds(ks, win_keys), lanes[p]], preferred_element_type=F32)
                  + jnp.dot(pcs[p], vc_ref[0, :, lanes[p]], preferred_element_type=F32)) / dens[p]
            o_ref[0, pl.ds(t0, GRID_W), lanes[p]] = jnp.where(first_head, o2[:GRID_W],
                                                             o2[GRID_W:]).astype(o_ref.dtype)

    scores(0, 0)

    def two_rows(k, carry):
        scores(2 * k + 1, 1)
        attend(2 * k, 0)
        scores(2 * k + 2, 0)
        attend(2 * k + 1, 1)
        return carry

    lax.fori_loop(0, rows_per_step // 2 - 1, two_rows, 0)
    scores(rows_per_step - 1, 1)
    attend(rows_per_step - 2, 0)
    attend(rows_per_step - 1, 1)


def _na_bias_table(rpb):
    n_heads, n_dr, n_dc = rpb.shape
    c = np.arange(GRID_W)
    cstart = np.clip(c - NA_WIN_W // 2, 0, GRID_W - NA_WIN_W)
    kc = np.arange(GRID_W)
    valid = (kc[None, :] >= cstart[:, None]) & (kc[None, :] < cstart[:, None] + NA_WIN_W)
    col_idx = np.clip(kc[None, :] - c[:, None] + NA_WIN_W - 1, 0, n_dc - 1)
    pick = np.zeros((n_dc, GRID_W * GRID_W), np.float32)
    pick[col_idx.reshape(-1), np.arange(GRID_W * GRID_W)] = 1.0
    t = jnp.dot(rpb.reshape(n_heads * n_dr, n_dc).astype(F32), jnp.asarray(pick),
                precision=lax.Precision.HIGHEST).reshape(n_heads, n_dr, GRID_W, GRID_W)
    t = jnp.where(jnp.asarray(valid)[None, None], t * LOG2E, NEG_INF)
    t2 = jnp.concatenate([t[:, :-1], t[:, 1:]], axis=-1)
    return t2.transpose(1, 0, 2, 3).reshape(n_dr - 1, n_heads // 2, 2 * GRID_W, 2 * GRID_W)


def _na_attention(q, k, v, kc, vc, bias, rows_per_step=32):
    b, s, w = q.shape
    n_rows = s // GRID_W
    ctx_len = kc.shape[1]
    blk = rows_per_step * GRID_W
    return pl.pallas_call(
        functools.partial(_na_kernel, rows_per_step, n_rows),
        grid=(b, n_rows // rows_per_step),
        in_specs=[pl.BlockSpec((1, blk, w), lambda bi, j: (bi, j, 0)),
                  pl.BlockSpec((1, s, w), lambda bi, j: (bi, 0, 0)),
                  pl.BlockSpec((1, s, w), lambda bi, j: (bi, 0, 0)),
                  pl.BlockSpec((1, ctx_len, w), lambda bi, j: (bi, 0, 0)),
                  pl.BlockSpec((1, ctx_len, w), lambda bi, j: (bi, 0, 0)),
                  pl.BlockSpec(bias.shape, lambda bi, j: (0, 0, 0, 0), pipeline_mode=pl.Buffered(1))],
        out_specs=pl.BlockSpec((1, blk, w), lambda bi, j: (bi, j, 0)),
        out_shape=jax.ShapeDtypeStruct((b, s, w), BF16),
        scratch_shapes=[pltpu.VMEM((2, NA_HEADS * GRID_W, NA_WIN_H * GRID_W), F32),
                        pltpu.VMEM((2, NA_HEADS * GRID_W, ctx_len), F32)],
        compiler_params=_params(("arbitrary", "arbitrary")),
        name="na_attn",
    )(q, k, v, kc, vc, bias)


def _gla_kernel(n_groups, gq_ref, gk_ref, gv_ref, lr_ref, sog_ref, ck_ref, cv_ref, clr_ref,
                wa_ref, ba_ref, gn_ref, o_ref, st_scr, of_scr, qe_scr, kd_scr, ckd_scr, dec_scr, cdec_scr,
                qk_a):
    c_len = GLA_CHUNK
    grp = GLA_GROUP
    cpg = grp // c_len
    row = lax.broadcasted_iota(jnp.int32, (grp, grp), 0)
    col = lax.broadcasted_iota(jnp.int32, (grp, grp), 1)
    same_chunk = (row // c_len) == (col // c_len)
    masks = (same_chunk & (col <= row), same_chunk & (col >= row))

    dirs = range(2)

    def gates(lr_rows, k_rows):
        zs = [jnp.dot(lr_rows, wa_ref[t], preferred_element_type=F32) + ba_ref[t:t + 1, :] for t in dirs]
        gs = [(jnp.minimum(z, 0.0) * LOG2E - jnp.log2(1.0 + jnp.exp2(jnp.abs(z) * (-LOG2E)))) * (1.0 / GLA_GATE_NORM)
              for z in zs]
        g1s = [g.astype(BF16) for g in gs]
        g2s = [(g - g1.astype(F32)).astype(BF16) for g, g1 in zip(gs, g1s)]
        tris = [masks[t].astype(BF16) for t in dirs]
        bs = [jnp.dot(tris[t], g1s[t], preferred_element_type=F32)
              + jnp.dot(tris[t], g2s[t], preferred_element_type=F32) for t in dirs]
        out = []
        for t in dirs:
            edge = c_len - 1 if t == 0 else 0
            decs = [jnp.exp2(bs[t][j * c_len + edge:j * c_len + edge + 1, :]) for j in range(cpg)]
            ke = k_rows * jnp.exp2(-bs[t])
            kd = (ke * jnp.concatenate([jnp.broadcast_to(v, (c_len, v.shape[1])) for v in decs],
                                       axis=0)).astype(BF16)
            dec = jnp.concatenate(decs + [jnp.ones_like(v) for v in decs], axis=0)
            out.append((bs[t], ke.astype(BF16), kd, dec))
        return out

    for t, (_, _, kd, dec) in enumerate(gates(clr_ref[0], ck_ref[0].astype(F32))):
        ckd_scr[t] = kd
        cdec_scr[t] = dec

    def group_gates(gi, qk_buf):
        rows = pl.ds(pl.multiple_of(gi * grp, grp), grp)
        qf = gq_ref[0, rows, :].astype(F32)
        for t, (b, ke, kd, dec) in enumerate(gates(lr_ref[0, rows, :], gk_ref[0, rows, :].astype(F32))):
            qe = (qf * jnp.exp2(b)).astype(BF16)
            qk_buf[t] = qe
            qk_buf[2 + t] = ke
            qe_scr[t, rows, :] = qe
            kd_scr[t, rows, :] = kd
            dec_scr[t, gi] = dec

    def group_attend(gi, qk_buf):
        rows = pl.ds(pl.multiple_of(gi * grp, grp), grp)
        heads = range(GLA_HEADS)
        kls = [slice(h * GLA_HK, (h + 1) * GLA_HK) for h in heads]
        raw = [[lax.dot_general(qk_buf[t, :, kls[h]], qk_buf[2 + t, :, kls[h]], _NT, preferred_element_type=F32)
                for t in dirs] for h in heads]
        atts = [(jnp.where(masks[0], raw[h][0], 0.0) + jnp.where(masks[1], raw[h][1], 0.0)).astype(BF16)
                for h in heads]
        for h in heads:
            vl = slice(h * GLA_HV, (h + 1) * GLA_HV)
            of_scr[rows, vl] = jnp.dot(atts[h], gv_ref[0, rows, vl], preferred_element_type=F32)

    def state_step(t, h, v_rows, kd_rows, dec_row):
        kl = slice(h * GLA_HK, (h + 1) * GLA_HK)
        upd = lax.dot_general(v_rows, kd_rows[:, kl], _TN, preferred_element_type=F32)
        st_scr[t, h] = st_scr[t, h] * dec_row[:, kl] + upd

    def chunk_order(t):
        return range(cpg) if t == 0 else range(cpg - 1, -1, -1)

    st_scr[...] = jnp.zeros_like(st_scr)
    for t in dirs:
        for j in chunk_order(t):
            rows = slice(j * c_len, (j + 1) * c_len)
            for h in range(GLA_HEADS):
                state_step(t, h, cv_ref[0, rows, h * GLA_HV:(h + 1) * GLA_HV],
                           ckd_scr[t, rows, :], cdec_scr[t, j:j + 1, :])

    def chunk_step(t, gi, j, finish):
        rows = pl.ds(pl.multiple_of(gi * grp + j * c_len, c_len), c_len)
        qe_rows = qe_scr[t, rows, :]
        kd_rows = kd_scr[t, rows, :]
        dec_row = dec_scr[t, gi, j:j + 1, :]
        for h in range(GLA_HEADS):
            kl = slice(h * GLA_HK, (h + 1) * GLA_HK)
            vl = slice(h * GLA_HV, (h + 1) * GLA_HV)
            o_t = of_scr[rows, vl] + lax.dot_general(qe_rows[:, kl], st_scr[t, h].astype(BF16), _NT,
                                                     preferred_element_type=F32)
            if finish:
                ms = jnp.mean(o_t * o_t, axis=-1, keepdims=True)
                o_n = o_t * lax.rsqrt(ms + EPS) * gn_ref[:, vl] * sog_ref[0, rows, vl].astype(F32)
                o_ref[0, rows, vl] = o_n.astype(o_ref.dtype)
            else:
                of_scr[rows, vl] = o_t
            state_step(t, h, gv_ref[0, rows, vl], kd_rows, dec_row)

    def scan_steps(i, finish):
        steps = []
        for jf, jb in zip(chunk_order(0), chunk_order(1)):
            steps.append(functools.partial(chunk_step, 0, i, jf, finish))
            steps.append(functools.partial(chunk_step, 1, n_groups - 1 - i, jb, finish))
        return steps

    half = n_groups // 2
    for g in (0, n_groups - 1):
        group_gates(g, qk_a)
        group_attend(g, qk_a)

    def prepare_and_scan(i, carry):
        steps = scan_steps(i, False)
        per_phase = len(steps) // 4
        for n_phase, phase in enumerate((functools.partial(group_gates, i + 1, qk_a),
                                         functools.partial(group_attend, i + 1, qk_a),
                                         functools.partial(group_gates, n_groups - 2 - i, qk_a),
                                         functools.partial(group_attend, n_groups - 2 - i, qk_a))):
            phase()
            for step in steps[n_phase * per_phase:(n_phase + 1) * per_phase]:
                step()
        return carry

    def scan_only(finish):
        def body(i, carry):
            for step in scan_steps(i, finish):
                step()
            return carry
        return body

    lax.fori_loop(0, half - 1, prepare_and_scan, 0)
    lax.fori_loop(half - 1, half, scan_only(False), 0)
    lax.fori_loop(half, n_groups, scan_only(True), 0)


def _gla(gq, gk, gv, lr, sog, ck, cv, clr, wa, ba, gn):
    b, s, _ = gq.shape
    ctx_len = ck.shape[1]
    assert ctx_len == GLA_GROUP and s % (2 * GLA_GROUP) == 0
    n_groups = s // GLA_GROUP
    cpg = GLA_GROUP // GLA_CHUNK

    def per_batch(t, w):
        return pl.BlockSpec((1, t, w), lambda bi: (bi, 0, 0))

    return pl.pallas_call(
        functools.partial(_gla_kernel, n_groups),
        grid=(b,),
        in_specs=[per_batch(s, GLA_DK), per_batch(s, GLA_DK), per_batch(s, GLA_DV),
                  per_batch(s, LR_PAD), per_batch(s, GLA_DV),
                  per_batch(ctx_len, GLA_DK), per_batch(ctx_len, GLA_DV), per_batch(ctx_len, LR_PAD),
                  pl.BlockSpec(wa.shape, lambda bi: (0, 0, 0)),
                  pl.BlockSpec(ba.shape, lambda bi: (0, 0)),
                  pl.BlockSpec(gn.shape, lambda bi: (0, 0))],
        out_specs=per_batch(s, GLA_DV),
        out_shape=jax.ShapeDtypeStruct((b, s, GLA_DV), BF16),
        scratch_shapes=[pltpu.VMEM((2, GLA_HEADS, GLA_HV, GLA_HK), F32),
                        pltpu.VMEM((s, GLA_DV), F32),
                        pltpu.VMEM((2, s, GLA_DK), BF16),
                        pltpu.VMEM((2, s, GLA_DK), BF16),
                        pltpu.VMEM((2, ctx_len, GLA_DK), BF16),
                        pltpu.VMEM((2, n_groups, 2 * cpg, GLA_DK), F32),
                        pltpu.VMEM((2, 2 * cpg, GLA_DK), F32),
                        pltpu.VMEM((4, GLA_GROUP, GLA_DK), BF16)],
        compiler_params=_params(("arbitrary",), vmem=60 * 1024 * 1024),
        name="gla",
    )(gq, gk, gv, lr, sog, ck, cv, clr, wa, ba, gn)


def _merge_kernel(ona_ref, ogla_ref, sgn_ref, sgg_ref, x_ref, mod_ref, gffn_ref,
                  wna_ref, wgla_ref, wout_ref, wr_ref, br_ref,
                  h_ref, f_ref, route_ref, route_t_ref, cnt_ref):
    rows = x_ref.shape[0]
    y1 = (sgn_ref[...].astype(F32) * jnp.dot(ona_ref[...], wna_ref[...], preferred_element_type=F32)
          + sgg_ref[...].astype(F32) * jnp.dot(ogla_ref[...], wgla_ref[...], preferred_element_type=F32))
    y = jnp.dot(y1.astype(BF16), wout_ref[...], preferred_element_type=F32)
    h = x_ref[...] + mod_ref[0, 2:3, :] * y
    h_ref[...] = h
    ms = jnp.mean(h * h, axis=-1, keepdims=True)
    f = h * lax.rsqrt(ms + EPS) * gffn_ref[...] * (1.0 + mod_ref[0, 4:5, :]) + mod_ref[0, 3:4, :]
    f_ref[...] = f.astype(f_ref.dtype)

    f_hi = f.astype(BF16)
    f_lo = (f - f_hi.astype(F32)).astype(BF16)
    both = jnp.dot(f_hi, wr_ref[...], preferred_element_type=F32)
    logits = (both[:, :ROUTE_LANES] + both[:, ROUTE_LANES:]
              + jnp.dot(f_lo, wr_ref[:, :ROUTE_LANES], preferred_element_type=F32)) + br_ref[...]

    lt = logits.T[0:ROUTE_ROWS, :]
    er = lax.broadcasted_iota(jnp.int32, (ROUTE_ROWS, rows), 0)
    big = jnp.int32(ROUTE_ROWS)

    def first_argmax(vals):
        m = jnp.max(vals, axis=0, keepdims=True)
        idx = jnp.min(jnp.where(vals == m, er, big), axis=0, keepdims=True)
        return m, idx

    is_grp = er < N_GROUPS
    lg = jnp.where(is_grp, lt, -jnp.inf)
    mg, grp = first_argmax(lg)
    p_grp = 1.0 / jnp.sum(jnp.where(is_grp, jnp.exp(lg - mg), 0.0), axis=0, keepdims=True)
    lo = EXPERT_LANE0 + grp * EXPERTS_PER_GROUP
    in_grp = (er >= lo) & (er < lo + EXPERTS_PER_GROUP)
    le = jnp.where(in_grp, lt, -jnp.inf)
    m1, i1 = first_argmax(le)
    m2, i2 = first_argmax(jnp.where(er == i1, -jnp.inf, le))
    t = jnp.exp(m2 - m1)
    w1 = p_grp / (1.0 + t)
    w2 = p_grp * t / (1.0 + t)

    hot1 = er == i1
    hot2 = er == i2
    onehot = (hot1 | hot2).astype(BF16)
    s_i = lax.broadcasted_iota(jnp.int32, (rows, rows), 0)
    t_i = lax.broadcasted_iota(jnp.int32, (rows, rows), 1)
    rank = jnp.dot(onehot, (s_i < t_i).astype(BF16), preferred_element_type=F32)
    cnt = jnp.sum(onehot.astype(F32), axis=1, keepdims=True)
    cnt_al = jnp.floor((cnt + (SEG_ALIGN - 1.0)) * (1.0 / SEG_ALIGN)) * SEG_ALIGN
    e_i = lax.broadcasted_iota(jnp.int32, (ROUTE_ROWS, ROUTE_ROWS), 0)
    e_j = lax.broadcasted_iota(jnp.int32, (ROUTE_ROWS, ROUTE_ROWS), 1)
    lstart = jnp.dot((e_j < e_i).astype(BF16), jnp.broadcast_to(cnt_al, (ROUTE_ROWS, LANES)).astype(BF16),
                     preferred_element_type=F32)[:, 0:1]
    pos = rank + lstart
    pos1 = jnp.sum(jnp.where(hot1, pos, 0.0), axis=0, keepdims=True)
    pos2 = jnp.sum(jnp.where(hot2, pos, 0.0), axis=0, keepdims=True)

    e1 = (i1 - EXPERT_LANE0).astype(F32)
    e2 = (i2 - EXPERT_LANE0).astype(F32)
    fields = jnp.concatenate([e1, e2, w1, w2, pos1, pos2, jnp.zeros((2, rows), F32)], axis=0)
    route_t_ref[0] = fields
    route_ref[...] = jnp.concatenate([fields, jnp.zeros((ROUTE_LANES - 8, rows), F32)], axis=0).T
    cnt_rows = jnp.concatenate([jnp.broadcast_to(cnt, (ROUTE_ROWS, ROUTE_LANES)),
                                jnp.zeros((ROUTE_LANES - ROUTE_ROWS, ROUTE_LANES), F32)], axis=0)
    cnt_ref[0] = cnt_rows.T[0:1, :]


def _merge_route(ona, ogla, sgn, sgg, x2d, mod3, gffn, wna, wgla, wout, wr, br, tokens_per_batch, tm):
    n, d = x2d.shape
    per_b = tokens_per_batch // tm

    def tile(w):
        return pl.BlockSpec((tm, w), lambda i: (i, 0))

    def whole(a):
        return pl.BlockSpec(a.shape, lambda i: (0,) * a.ndim)

    return pl.pallas_call(
        _merge_kernel,
        grid=(n // tm,),
        in_specs=[tile(NA_WIDTH), tile(GLA_DV), tile(d), tile(d), tile(d),
                  pl.BlockSpec((1, 6, d), lambda i: (i // per_b, 0, 0)),
                  whole(gffn), whole(wna), whole(wgla), whole(wout), whole(wr), whole(br)],
        out_specs=[tile(d), tile(d), tile(ROUTE_LANES),
                   pl.BlockSpec((1, 8, tm), lambda i: (i, 0, 0)),
                   pl.BlockSpec((1, 1, ROUTE_LANES), lambda i: (i, 0, 0))],
        out_shape=[jax.ShapeDtypeStruct((n, d), F32),
                   jax.ShapeDtypeStruct((n, d), BF16),
                   jax.ShapeDtypeStruct((n, ROUTE_LANES), F32),
                   jax.ShapeDtypeStruct((n // tm, 8, tm), F32),
                   jax.ShapeDtypeStruct((n // tm, 1, ROUTE_LANES), F32)],
        compiler_params=_params(("arbitrary",)),
        name="merge_route",
    )(ona, ogla, sgn, sgg, x2d, mod3, gffn, wna, wgla, wout, wr, br)


HI_MASK = -65536


def _pack_rows(a):
    half = a.shape[1] // 2
    hi = lax.bitcast_convert_type(a[:, :half], jnp.int32)
    lo = lax.shift_right_logical(lax.bitcast_convert_type(a[:, half:], jnp.int32), 16)
    return hi | lo


def _unpack_rows(w):
    hi = lax.bitcast_convert_type(w & HI_MASK, F32).astype(BF16)
    lo = lax.bitcast_convert_type(lax.shift_left(w, 16), F32).astype(BF16)
    return hi, lo


def _piece_table(cnt_al, lstart, seg_dst):
    n_tiles = cnt_al.shape[0]

    def compact(mask, local, glob, slots):
        pos = jnp.cumsum(mask.astype(jnp.int32), axis=1) - 1
        hot = mask[:, :, None] & (pos[:, :, None] == jnp.arange(slots, dtype=jnp.int32))
        pick = lambda v: jnp.sum(jnp.where(hot, v[:, :, None], 0), axis=1)
        return jnp.sum(mask, axis=1).astype(jnp.int32), pick(local), pick(glob)

    counts, lists = [], []
    big = PIECE_SIZES[-1]
    for size, slots in zip(PIECE_SIZES[:-1], PIECE_SLOTS[:-1]):
        n, lo, gl = compact((cnt_al & (big - 1)) == size, lstart, seg_dst, slots)
        counts.append(n)
        lists += [lo, gl]
    k = jnp.arange(MAX_BIG_PER_RUN, dtype=jnp.int32)
    off = (cnt_al & (big - 1))[:, :, None] + big * k
    mask = k < (cnt_al // big)[:, :, None]
    flat = lambda a: a.reshape(n_tiles, -1)
    n, lo, gl = compact(flat(mask), flat(lstart[:, :, None] + off), flat(seg_dst[:, :, None] + off), PIECE_SLOTS[-1])
    counts.append(n)
    lists += [lo, gl]
    head = jnp.stack(counts + [jnp.sum(cnt_al, axis=1)], axis=1)
    head = jnp.pad(head, ((0, 0), (0, PIECE_HEAD - head.shape[1])))
    table = jnp.concatenate([head] + lists, axis=1)
    table = jnp.pad(table, ((0, 0), (0, PIECE_TABLE - table.shape[1])))
    return table.reshape(n_tiles, 1, PIECE_TABLE).astype(jnp.int32)


def _segment_copies(seg_ref, make_copy, action):
    if action == "wait":
        total = seg_ref[0, 0, len(PIECE_SIZES)]
        for size in WAIT_SIZES:
            @pl.when((total & size) != 0)
            def _():
                make_copy(0, 0, size).wait()
        return

    base = PIECE_HEAD
    for c, (size, slots) in enumerate(zip(PIECE_SIZES, PIECE_SLOTS)):
        def piece(k, carry, size=size, lo=base, gl=base + slots):
            make_copy(pl.multiple_of(seg_ref[0, 0, lo + k], SEG_ALIGN),
                      pl.multiple_of(seg_ref[0, 0, gl + k], SEG_ALIGN), size).start()
            return carry

        lax.fori_loop(0, seg_ref[0, 0, c], piece, 0)
        base += 2 * slots


def _dispatch_kernel(seg_ref, seg_prev_ref, tail_ref, lpos_ref, f_ref, xs_ref, sbuf, zbuf, sems):
    i = pl.program_id(0)
    last = pl.num_programs(0) - 1
    tm = f_ref.shape[0]
    rows_l = sbuf.shape[1]
    slot = i % 2
    fields = lpos_ref[0]
    lp = fields.astype(jnp.int32)
    half = f_ref.shape[1] // 2
    row = lax.broadcasted_iota(jnp.int32, (rows_l, tm), 0)
    first = row == lp[4:5, :]
    second = row == lp[5:6, :]
    sbuf[slot, :, :half] = _pack_rows(jnp.dot((first | second).astype(BF16), f_ref[...], preferred_element_type=F32))
    w_sorted = jnp.sum(jnp.where(first, fields[2:3, :], 0.0) + jnp.where(second, fields[3:4, :], 0.0),
                       axis=1, keepdims=True)
    sbuf[slot, :, half:] = lax.bitcast_convert_type(jnp.broadcast_to(w_sorted, (rows_l, LANES)), jnp.int32)

    def copier(s):
        def copy(local, glob, size):
            return pltpu.make_async_copy(sbuf.at[s, pl.ds(local, size), :], xs_ref.at[pl.ds(glob, size), :],
                                         sems.at[s])
        return copy

    _segment_copies(seg_ref, copier(slot), "start")

    @pl.when(i > 0)
    def _():
        _segment_copies(seg_prev_ref, copier(1 - slot), "wait")

    @pl.when(i == last)
    def _():
        _segment_copies(seg_ref, copier(slot), "wait")
        sem = sems.at[slot]
        zbuf[...] = jnp.zeros_like(zbuf)

        def tail(action):
            def body(e, carry):
                start = tail_ref[0, 0, e]
                length = tail_ref[0, 0, N_EXPERTS + e]
                off = jnp.int32(0)
                for size in TAIL_SIZES:
                    @pl.when((length & size) != 0)
                    def _():
                        cp = pltpu.make_async_copy(zbuf.at[pl.ds(0, size), :],
                                                   xs_ref.at[pl.ds(pl.multiple_of(start + off, SEG_ALIGN), size), :],
                                                   sem)
                        getattr(cp, action)()
                    off = off + (length & size)
                return carry
            lax.fori_loop(0, N_EXPERTS, body, 0)

        tail("start")
        tail("wait")

        def unused(action):
            def body(blk, carry):
                cp = pltpu.make_async_copy(zbuf, xs_ref.at[pl.ds(pl.multiple_of(blk * MOE_ROWS, MOE_ROWS),
                                                                  MOE_ROWS), :], sem)
                getattr(cp, action)()
                return carry
            lax.fori_loop(tail_ref[0, 0, 2 * N_EXPERTS], xs_ref.shape[0] // MOE_ROWS, body, 0)

        unused("start")
        unused("wait")


def _dispatch(seg, tail, lpos_rows, f2d, n_pad, rows_l, tm):
    n, d = f2d.shape
    return pl.pallas_call(
        _dispatch_kernel,
        grid=(n // tm,),
        in_specs=[pl.BlockSpec((1, 1, seg.shape[2]), lambda i: (i, 0, 0), memory_space=pltpu.SMEM),
                  pl.BlockSpec((1, 1, seg.shape[2]), lambda i: (jnp.maximum(i - 1, 0), 0, 0),
                               memory_space=pltpu.SMEM),
                  pl.BlockSpec(tail.shape, lambda i: (0, 0, 0), memory_space=pltpu.SMEM),
                  pl.BlockSpec((1, 8, tm), lambda i: (i, 0, 0)),
                  pl.BlockSpec((tm, d), lambda i: (i, 0))],
        out_specs=pl.BlockSpec(memory_space=pl.ANY),
        out_shape=jax.ShapeDtypeStruct((n_pad, d // 2 + LANES), jnp.int32),
        scratch_shapes=[pltpu.VMEM((2, rows_l, d // 2 + LANES), jnp.int32),
                        pltpu.VMEM((MOE_ROWS, d // 2 + LANES), jnp.int32),
                        pltpu.SemaphoreType.DMA((2,))],
        compiler_params=_params(("arbitrary",)),
        name="dispatch",
    )(seg, seg, tail, lpos_rows, f2d)


def _expert_kernel(be_ref, nu_ref, x_ref, wg_ref, wu_ref, wd_ref, y_ref, wgu_scr, wd_scr):
    i = pl.program_id(0)
    prev = be_ref[jnp.maximum(i - 1, 0)]

    @pl.when((i == 0) | (be_ref[i] != prev))
    def _():
        wgu_scr[:, :D_EXPERT] = wg_ref[0].astype(BF16)
        wgu_scr[:, D_EXPERT:] = wu_ref[0].astype(BF16)
        wd_scr[...] = wd_ref[0].astype(BF16)

    @pl.when(i < nu_ref[0])
    def _():
        half = y_ref.shape[1]
        x = jnp.concatenate(_unpack_rows(x_ref[:, :half]), axis=1)
        gu = jnp.dot(x, wgu_scr[...], preferred_element_type=F32)
        gate = gu[:, :D_EXPERT]
        hdn = gate * jax.nn.sigmoid(gate) * gu[:, D_EXPERT:]
        y = jnp.dot(hdn.astype(BF16), wd_scr[...], preferred_element_type=F32)
        y = y * lax.bitcast_convert_type(x_ref[:, half:half + 1], F32)
        y_ref[...] = _pack_rows(y.astype(BF16).astype(F32))

    @pl.when(i >= nu_ref[0])
    def _():
        y_ref[...] = jnp.zeros_like(y_ref)


def _experts(blk_expert, n_used, xs, wg, wu, wd):
    n_pad, row_words = xs.shape
    half = row_words - LANES
    d = 2 * half
    n_blk = n_pad // MOE_ROWS

    def used_block(i, be, nu):
        return (jnp.maximum(jnp.minimum(i, nu[0] - 1), 0), 0)

    grid_spec = pltpu.PrefetchScalarGridSpec(
        num_scalar_prefetch=2,
        grid=(n_blk,),
        in_specs=[pl.BlockSpec((MOE_ROWS, row_words), used_block),
                  pl.BlockSpec((1, d, D_EXPERT), lambda i, be, nu: (be[i], 0, 0)),
                  pl.BlockSpec((1, d, D_EXPERT), lambda i, be, nu: (be[i], 0, 0)),
                  pl.BlockSpec((1, D_EXPERT, d), lambda i, be, nu: (be[i], 0, 0))],
        out_specs=pl.BlockSpec((MOE_ROWS, half), lambda i, be, nu: (i, 0)),
        scratch_shapes=[pltpu.VMEM((d, 2 * D_EXPERT), BF16),
                        pltpu.VMEM((D_EXPERT, d), BF16)],
    )
    return pl.pallas_call(
        _expert_kernel,
        grid_spec=grid_spec,
        out_shape=jax.ShapeDtypeStruct((n_pad, half), jnp.int32),
        compiler_params=_params(("arbitrary",)),
        name="experts",
    )(blk_expert, n_used, xs, wg, wu, wd)


def _combine_kernel(seg_ref, seg_next_ref, col_ref, h_ref, mod_ref, fg_ref, ys_ref, o_ref, ybuf, sems):
    i = pl.program_id(0)
    tm = h_ref.shape[0]
    rows_l = ybuf.shape[1]
    slot = i % 2

    def copier(s):
        def copy(local, glob, size):
            return pltpu.make_async_copy(ys_ref.at[pl.ds(glob, size), :], ybuf.at[s, pl.ds(local, size), :],
                                         sems.at[s])
        return copy

    @pl.when(i == 0)
    def _():
        ybuf[...] = jnp.zeros_like(ybuf)
        _segment_copies(seg_ref, copier(slot), "start")

    @pl.when(i + 1 < pl.num_programs(0))
    def _():
        _segment_copies(seg_next_ref, copier(1 - slot), "start")

    _segment_copies(seg_ref, copier(slot), "wait")
    y_hi, y_lo = _unpack_rows(ybuf[slot])
    y_all = jnp.concatenate([y_hi, y_lo], axis=1)
    col = lax.broadcasted_iota(jnp.int32, (tm, rows_l), 1)
    info = col_ref[...]
    pick = (col == info[:, 4:5].astype(jnp.int32)) | (col == info[:, 5:6].astype(jnp.int32))
    moe = jnp.dot(pick.astype(BF16), y_all, preferred_element_type=F32)
    h = h_ref[...] + mod_ref[0, 5:6, :] * moe
    ms = jnp.mean(h * h, axis=-1, keepdims=True)
    o_ref[...] = h * lax.rsqrt(ms + EPS) * fg_ref[...]


def _combine(seg, colinfo, h2d, mod3, fg, ys, rows_l, tokens_per_batch, tm):
    n, d = h2d.shape
    per_b = tokens_per_batch // tm
    return pl.pallas_call(
        _combine_kernel,
        grid=(n // tm,),
        in_specs=[pl.BlockSpec((1, 1, seg.shape[2]), lambda i: (i, 0, 0), memory_space=pltpu.SMEM),
                  pl.BlockSpec((1, 1, seg.shape[2]), lambda i: (jnp.minimum(i + 1, n // tm - 1), 0, 0),
                               memory_space=pltpu.SMEM),
                  pl.BlockSpec((tm, colinfo.shape[1]), lambda i: (i, 0)),
                  pl.BlockSpec((tm, d), lambda i: (i, 0)),
                  pl.BlockSpec((1, 6, d), lambda i: (i // per_b, 0, 0)),
                  pl.BlockSpec((1, d), lambda i: (0, 0)),
                  pl.BlockSpec(memory_space=pl.ANY)],
        out_specs=pl.BlockSpec((tm, d), lambda i: (i, 0)),
        out_shape=jax.ShapeDtypeStruct((n, d), F32),
        scratch_shapes=[pltpu.VMEM((2, rows_l, d // 2), jnp.int32), pltpu.SemaphoreType.DMA((2,))],
        compiler_params=_params(("arbitrary",)),
        name="combine",
    )(seg, seg, colinfo, h2d, mod3, fg, ys)


def _rope_tables(s):
    t = np.arange(s)
    pos_r = (t // GRID_W).astype(np.float32)
    pos_c = (t % GRID_W).astype(np.float32)
    nf = GLA_HK // 4
    inv = (np.float32(ROPE_BASE) ** (-np.arange(nf, dtype=np.float32) / np.float32(nf))).astype(np.float32)
    ang_r = pos_r[:, None] * inv
    ang_c = pos_c[:, None] * inv

    def half(ang):
        return (np.concatenate([np.cos(ang), np.cos(ang)], axis=-1),
                np.concatenate([-np.sin(ang), np.sin(ang)], axis=-1))

    cr, sr = half(ang_r)
    cc, sc = half(ang_c)
    cos = np.tile(np.concatenate([cr, cc], axis=-1), (1, GLA_HEADS)).astype(np.float32)
    sin = np.tile(np.concatenate([sr, sc], axis=-1), (1, GLA_HEADS)).astype(np.float32)
    return jnp.asarray(cos), jnp.asarray(sin)


def kernel(x, c, ctx, c_ctx, w_mod, b_mod, norm_attn_g, norm_ffn_g, w_in, w_gla_a2, b_gla_a2, gla_norm_g, na_rpb, w_na_o, w_gla_o, w_out, w_group, b_group, w_expert, b_expert, w_exp_gate, w_exp_up, w_exp_down, final_norm_g):
    b, s, d = x.shape
    ctx_len = ctx.shape[1]
    n = b * s
    assert w_mod.shape[0] == 1, "single-layer block"
    assert s % (GRID_W * NA_WIN_H) == 0 and ctx_len % GLA_CHUNK == 0

    mod_rows = -(-(b + 1) // 8) * 8
    cs = jnp.zeros((mod_rows, d), F32).at[:b].set(c).at[b].set(c_ctx)
    mod3 = _modulation(cs, w_mod[0], b_mod[0]).reshape(mod_rows, 6, d)

    o_q, o_k, o_v = 0, NA_WIDTH, 2 * NA_WIDTH
    o_gq = 3 * NA_WIDTH
    o_gk = o_gq + GLA_DK
    o_gv = o_gk + GLA_DK
    o_og = o_gv + GLA_DV
    o_lr = o_og + GLA_DV
    o_mn = o_lr + 2 * GLA_GATE_RANK
    w_bf = w_in[0].astype(BF16)
    w_a = (w_bf, o_lr)
    w_g = (w_bf[:, o_mn:], 2 * d)
    w_lr = (jnp.pad(w_bf[:, o_lr:o_mn], ((0, 0), (0, LR_PAD - 2 * GLA_GATE_RANK))), LR_PAD)
    cw = 512
    lat_specs = [(0, 0, 0, o_q, cw, "plain"), (1, 0, 0, o_k, cw, "plain"), (2, 0, 0, o_v, cw, "plain"),
                 (3, 0, 0, o_gq, cw, "rope_scaled"), (4, 0, 0, o_gk, cw, "rope"),
                 (5, 0, 0, o_gv, cw, "plain"), (5, cw, 0, o_gv + cw, cw, "plain"),
                 (6, 0, 0, o_og, cw, "silu"), (6, cw, 0, o_og + cw, cw, "silu"),
                 (7, 0, 1, 0, cw, "sigmoid"), (7, cw, 1, cw, cw, "sigmoid"),
                 (8, 0, 1, d, cw, "sigmoid"), (8, cw, 1, d + cw, cw, "sigmoid"),
                 (9, 0, 2, 0, LR_PAD, "plain")]
    lat_widths = [NA_WIDTH, NA_WIDTH, NA_WIDTH, GLA_DK, GLA_DK, GLA_DV, GLA_DV, d, d, LR_PAD]
    tm = 512
    per_b = s // tm
    q_na, k_na, v_na, gq, gk, gv, sog, sgn, sgg, lr = _inproj(
        x.reshape(n, d), mod3, lambda i: i // per_b, norm_attn_g[0], [w_a, w_g, w_lr], lat_specs, lat_widths, tm,
        rope=_rope_tables(s), name="inproj_lat")

    ctx_specs = [(0, 0, 0, o_k, cw, "plain"), (1, 0, 0, o_v, cw, "plain"), (2, 0, 0, o_gk, cw, "plain"),
                 (3, 0, 0, o_gv, cw, "plain"), (3, cw, 0, o_gv + cw, cw, "plain"),
                 (4, 0, 1, 0, LR_PAD, "plain")]
    ctx_widths = [NA_WIDTH, NA_WIDTH, GLA_DK, GLA_DV, LR_PAD]
    kc_na, vc_na, ck, cv, clr = _inproj(
        ctx.reshape(b * ctx_len, d), mod3, lambda i: b, norm_attn_g[0], [w_a, w_lr], ctx_specs, ctx_widths,
        tm if (b * ctx_len) % tm == 0 else ctx_len, name="inproj_ctx")

    def lat3(a):
        return a.reshape(b, s, a.shape[-1])

    def ctx3(a):
        return a.reshape(b, ctx_len, a.shape[-1])

    o_na = _na_attention(lat3(q_na), lat3(k_na), lat3(v_na), ctx3(kc_na), ctx3(vc_na),
                         _na_bias_table(na_rpb[0]))

    wa = jnp.zeros((2, LR_PAD, GLA_DK), F32)
    wa = wa.at[0, :GLA_GATE_RANK].set(w_gla_a2[0, 0]).at[1, GLA_GATE_RANK:2 * GLA_GATE_RANK].set(w_gla_a2[0, 1])
    gn = jnp.tile(gla_norm_g[0], GLA_HEADS).reshape(1, GLA_DV)
    o_gla = _gla(lat3(gq), lat3(gk), lat3(gv), lat3(lr), lat3(sog), ctx3(ck), ctx3(cv), ctx3(clr),
                 wa.astype(BF16), b_gla_a2[0], gn)

    wr = jnp.zeros((d, ROUTE_LANES), F32)
    wr = wr.at[:, :N_GROUPS].set(w_group[0]).at[:, EXPERT_LANE0:EXPERT_LANE0 + N_EXPERTS].set(w_expert[0])
    wr_hi = wr.astype(BF16)
    br = jnp.zeros((1, ROUTE_LANES), F32)
    br = br.at[0, :N_GROUPS].set(b_group[0]).at[0, EXPERT_LANE0:EXPERT_LANE0 + N_EXPERTS].set(b_expert[0])
    h, f_lat, route, route_t, counts = _merge_route(
        o_na.reshape(n, NA_WIDTH), o_gla.reshape(n, GLA_DV), sgn, sgg, x.reshape(n, d), mod3, norm_ffn_g[0].reshape(1, d),
        w_na_o[0].astype(BF16), w_gla_o[0].astype(BF16), w_out[0].astype(BF16),
        jnp.concatenate([wr_hi, (wr - wr_hi.astype(F32)).astype(BF16)], axis=1), br, s, tm)

    n_tiles = n // tm
    cnt = counts[:, 0, EXPERT_LANE0:EXPERT_LANE0 + N_EXPERTS].astype(jnp.int32)
    cnt_al = (cnt + SEG_ALIGN - 1) // SEG_ALIGN * SEG_ALIGN
    lstart = jnp.cumsum(cnt_al, axis=1) - cnt_al
    total = jnp.sum(cnt_al, axis=0)
    padded = (total + MOE_ROWS - 1) // MOE_ROWS * MOE_ROWS
    pend = jnp.cumsum(padded)
    pstart = pend - padded
    seg_dst = pstart[None, :] + jnp.cumsum(cnt_al, axis=0) - cnt_al
    assert tm // PIECE_SIZES[-1] <= MAX_BIG_PER_RUN
    assert (2 * tm + N_EXPERTS * (SEG_ALIGN - 1)) // PIECE_SIZES[-1] <= PIECE_SLOTS[-1]
    seg = _piece_table(cnt_al, lstart, seg_dst)
    n_used = (pend[-1] // MOE_ROWS).reshape(1)
    tail = jnp.concatenate([pstart + total, padded - total, n_used, jnp.zeros((2 * N_EXPERTS - 1,), jnp.int32)])
    tail = tail.reshape(1, 1, 4 * N_EXPERTS)
    seg_pad = n_tiles * N_EXPERTS * (SEG_ALIGN - 1)
    n_pad = (2 * n + seg_pad + N_EXPERTS * (MOE_ROWS - SEG_ALIGN) + MOE_ROWS - 1) // MOE_ROWS * MOE_ROWS
    n_blk = n_pad // MOE_ROWS
    rows_l = (2 * tm + N_EXPERTS * (SEG_ALIGN - 1) + MXU_DIM - 1) // MXU_DIM * MXU_DIM
    blk_start = jnp.arange(n_blk, dtype=jnp.int32) * MOE_ROWS
    blk_expert = jnp.minimum(jnp.sum(blk_start[:, None] >= pend[None, :], axis=-1), N_EXPERTS - 1)

    xs = _dispatch(seg, tail, route_t, f_lat, n_pad, rows_l, tm)
    ys = _experts(blk_expert.astype(jnp.int32), n_used.astype(jnp.int32), xs,
                  w_exp_gate[0], w_exp_up[0], w_exp_down[0])
    out = _combine(seg, route, h, mod3, final_norm_g.reshape(1, d), ys, rows_l, s, tm)
    return out.reshape(b, s, d)
```

```python
import functools

import jax
import jax.numpy as jnp
import numpy as np
from jax import lax
from jax.experimental import pallas as pl
from jax.experimental.pallas import tpu as pltpu

F32 = jnp.float32
BF16 = jnp.bfloat16

D_MODEL = 1024
GRID_W = 64
NA_HEADS = 8
NA_HEAD_DIM = 64
NA_WIDTH = NA_HEADS * NA_HEAD_DIM
NA_WIN_H = 8
NA_WIN_W = 16
GLA_HEADS = 4
GLA_DK = D_MODEL // 2
GLA_DV = D_MODEL
GLA_HK = GLA_DK // GLA_HEADS
GLA_HV = GLA_DV // GLA_HEADS
GLA_GATE_RANK = 16
GLA_GATE_NORM = 16.0
GLA_CHUNK = 64
GLA_GROUP = 256
ROPE_BASE = 10000.0
N_GROUPS = 4
EXPERTS_PER_GROUP = 8
N_EXPERTS = N_GROUPS * EXPERTS_PER_GROUP
D_EXPERT = D_MODEL // 2
EPS = 1e-6
NEG_INF = -1e30
LOG2E = 1.4426950408889634

LANES = 128
MXU_DIM = 256
LR_PAD = LANES
ROUTE_LANES = LANES
EXPERT_LANE0 = N_GROUPS
ROUTE_ROWS = 48
MERGE_TILES = 2
MOE_ROWS = 512
SEG_ALIGN = 8
PIECE_SIZES = (8, 16, 24, 32, 40, 48, 56, 64)
PIECE_SLOTS = (32, 32, 32, 32, 32, 32, 32, 24)
MAX_BIG_PER_RUN = 8
PIECE_HEAD = 16
PIECE_TABLE = 512
WAIT_SIZES = (1024, 512, 256, 128, 64, 32, 16, 8)
TAIL_SIZES = (256, 128, 64, 32, 16, 8)
VMEM_LIMIT = 56 * 1024 * 1024

_NT = (((1,), (1,)), ((), ()))
_TN = (((0,), (0,)), ((), ()))


def _params(sem, vmem=VMEM_LIMIT):
    return pltpu.CompilerParams(dimension_semantics=sem, vmem_limit_bytes=vmem)


def _mod_kernel(c_ref, w_ref, b_ref, o_ref):
    c = c_ref[...]
    s = c * jax.nn.sigmoid(c)
    o_ref[...] = jnp.dot(s.astype(BF16), w_ref[...].astype(BF16),
                         preferred_element_type=F32) + b_ref[...]


def _modulation(cs, w_mod, b_mod):
    rows, d = cs.shape
    n = w_mod.shape[1]
    bn = 1024
    return pl.pallas_call(
        _mod_kernel,
        grid=(n // bn,),
        in_specs=[pl.BlockSpec((rows, d), lambda j: (0, 0)),
                  pl.BlockSpec((d, bn), lambda j: (0, j)),
                  pl.BlockSpec((1, bn), lambda j: (0, j))],
        out_specs=pl.BlockSpec((rows, bn), lambda j: (0, j)),
        out_shape=jax.ShapeDtypeStruct((rows, n), F32),
        compiler_params=_params(("arbitrary",)),
        name="mod",
    )(cs, w_mod, b_mod.reshape(1, n))


def _inproj_kernel(specs, use_rope, n_w, n_out, x_ref, mod_ref, g_ref, *rest):
    if use_rope:
        cos_ref, sin_ref = rest[:2]
        rest = rest[2:]
    w_refs = rest[:n_w]
    outs = rest[n_w:n_w + n_out]
    a_scr = rest[n_w + n_out]
    x = x_ref[...]
    ms = jnp.mean(x * x, axis=-1, keepdims=True)
    a = x * lax.rsqrt(ms + EPS) * g_ref[...] * (1.0 + mod_ref[0, 1:2, :]) + mod_ref[0, 0:1, :]
    a_scr[...] = a.astype(BF16)
    for (oi, oc, wi, wc, width, kind) in specs:
        acc = jnp.dot(a_scr[...], w_refs[wi][:, wc:wc + width], preferred_element_type=F32)
        if kind in ("rope", "rope_scaled"):
            if kind == "rope_scaled":
                acc = acc * (GLA_HK ** -0.5)
            lane = lax.broadcasted_iota(jnp.int32, acc.shape, 1)
            rot = jnp.where((lane & 63) < 32,
                            pltpu.roll(acc, width - 32, axis=1),
                            pltpu.roll(acc, 32, axis=1))
            acc = acc * cos_ref[...] + rot * sin_ref[...]
        elif kind == "silu":
            acc = acc * jax.nn.sigmoid(acc)
        elif kind == "sigmoid":
            acc = jax.nn.sigmoid(acc)
        outs[oi][:, oc:oc + width] = acc.astype(outs[oi].dtype)


def _inproj(x2d, mod3, mod_row_fn, g, weights, specs, out_widths, tm, rope=None, name="inproj"):
    n, d = x2d.shape
    use_rope = rope is not None
    in_specs = [pl.BlockSpec((tm, d), lambda i: (i, 0)),
                pl.BlockSpec((1, 6, d), lambda i: (mod_row_fn(i), 0, 0)),
                pl.BlockSpec((1, d), lambda i: (0, 0))]
    args = [x2d, mod3, g.reshape(1, d)]
    if use_rope:
        cos, sin = rope
        nb = cos.shape[0] // tm
        in_specs += [pl.BlockSpec((tm, cos.shape[1]), lambda i: (i % nb, 0)),
                     pl.BlockSpec((tm, cos.shape[1]), lambda i: (i % nb, 0))]
        args += [cos, sin]
    in_specs += [pl.BlockSpec((d, cols), lambda i: (0, 0), pipeline_mode=pl.Buffered(1)) for _, cols in weights]
    args += [w for w, _ in weights]
    out_specs = [pl.BlockSpec((tm, w), lambda i: (i, 0)) for w in out_widths]
    out_shape = [jax.ShapeDtypeStruct((n, w), BF16) for w in out_widths]
    return pl.pallas_call(
        functools.partial(_inproj_kernel, specs, use_rope, len(weights), len(out_widths)),
        grid=(n // tm,),
        in_specs=in_specs,
        out_specs=out_specs,
        out_shape=out_shape,
        scratch_shapes=[pltpu.VMEM((tm, d), BF16)],
        compiler_params=_params(("arbitrary",)),
        name=name,
    )(*args)


def _na_kernel(rows_per_step, n_rows, q_ref, k_ref, v_ref, kc_ref, vc_ref, bias_ref, o_ref, sw_scr, sc_scr):
    j = pl.program_id(1)
    lane = lax.broadcasted_iota(jnp.int32, (GRID_W, LANES), 1)
    first_head = lane < NA_HEAD_DIM
    scale = NA_HEAD_DIM ** -0.5
    win_keys = NA_WIN_H * GRID_W
    pairs = range(NA_HEADS // 2)
    lanes = [slice(p * LANES, (p + 1) * LANES) for p in pairs]
    stack = 2 * GRID_W

    def window(rr):
        r = j * rows_per_step + rr
        rs = jnp.clip(r - NA_WIN_H // 2, 0, n_rows - NA_WIN_H)
        return r - rs, pl.multiple_of(rr * GRID_W, GRID_W), pl.multiple_of(rs * GRID_W, GRID_W)

    def scores(rr, slot):
        _, t0, ks = window(rr)
        for p in pairs:
            qp = q_ref[0, pl.ds(t0, GRID_W), lanes[p]] * scale
            zero = jnp.zeros_like(qp)
            qs = jnp.concatenate([jnp.where(first_head, qp, zero), jnp.where(first_head, zero, qp)], axis=0)
            sw_scr[slot, p * stack:(p + 1) * stack, :] = lax.dot_general(
                qs, k_ref[0, pl.ds(ks, win_keys), lanes[p]], _NT, preferred_element_type=F32)
            sc_scr[slot, p * stack:(p + 1) * stack, :] = lax.dot_general(
                qs, kc_ref[0, :, lanes[p]], _NT, preferred_element_type=F32)

    def attend(rr, slot):
        d, t0, ks = window(rr)
        pws, pcs, dens = [], [], []
        for p in pairs:
            bias = jnp.concatenate([bias_ref[(NA_WIN_H - 1) - d + 2 * i, p] for i in range(NA_WIN_H // 2)], axis=1)
            sw = sw_scr[slot, p * stack:(p + 1) * stack, :] * LOG2E + bias
            sc = sc_scr[slot, p * stack:(p + 1) * stack, :] * LOG2E
            m = jnp.maximum(jnp.max(sw, axis=-1, keepdims=True), jnp.max(sc, axis=-1, keepdims=True))
            pw = jnp.exp2(sw - m)
            pc = jnp.exp2(sc - m)
            dens.append(jnp.sum(pw, axis=-1, keepdims=True) + jnp.sum(pc, axis=-1, keepdims=True))
            pws.append(pw.astype(BF16))
            pcs.append(pc.astype(BF16))
        for p in pairs:
            o2 = (jnp.dot(pws[p], v_ref[0, pl.ds(ks, win_keys), lanes[p]], preferred_element_type=F32)
                  + jnp.dot(pcs[p], vc_ref[0, :, lanes[p]], preferred_element_type=F32)) / dens[p]
            o_ref[0, pl.ds(t0, GRID_W), lanes[p]] = jnp.where(first_head, o2[:GRID_W],
                                                             o2[GRID_W:]).astype(o_ref.dtype)

    scores(0, 0)

    def two_rows(k, carry):
        scores(2 * k + 1, 1)
        attend(2 * k, 0)
        scores(2 * k + 2, 0)
        attend(2 * k + 1, 1)
        return carry

    lax.fori_loop(0, rows_per_step // 2 - 1, two_rows, 0)
    scores(rows_per_step - 1, 1)
    attend(rows_per_step - 2, 0)
    attend(rows_per_step - 1, 1)


def _na_bias_table(rpb):
    n_heads, n_dr, n_dc = rpb.shape
    c = np.arange(GRID_W)
    cstart = np.clip(c - NA_WIN_W // 2, 0, GRID_W - NA_WIN_W)
    kc = np.arange(GRID_W)
    valid = (kc[None, :] >= cstart[:, None]) & (kc[None, :] < cstart[:, None] + NA_WIN_W)
    col_idx = np.clip(kc[None, :] - c[:, None] + NA_WIN_W - 1, 0, n_dc - 1)
    pick = np.zeros((n_dc, GRID_W * GRID_W), np.float32)
    pick[col_idx.reshape(-1), np.arange(GRID_W * GRID_W)] = 1.0
    t = jnp.dot(rpb.reshape(n_heads * n_dr, n_dc).astype(F32), jnp.asarray(pick),
                precision=lax.Precision.HIGHEST).reshape(n_heads, n_dr, GRID_W, GRID_W)
    t = jnp.where(jnp.asarray(valid)[None, None], t * LOG2E, NEG_INF)
    t2 = jnp.concatenate([t[:, :-1], t[:, 1:]], axis=-1)
    return t2.transpose(1, 0, 2, 3).reshape(n_dr - 1, n_heads // 2, 2 * GRID_W, 2 * GRID_W)


def _na_attention(q, k, v, kc, vc, bias, rows_per_step=32):
    b, s, w = q.shape
    n_rows = s // GRID_W
    ctx_len = kc.shape[1]
    blk = rows_per_step * GRID_W
    return pl.pallas_call(
        functools.partial(_na_kernel, rows_per_step, n_rows),
        grid=(b, n_rows // rows_per_step),
        in_specs=[pl.BlockSpec((1, blk, w), lambda bi, j: (bi, j, 0)),
                  pl.BlockSpec((1, s, w), lambda bi, j: (bi, 0, 0)),
                  pl.BlockSpec((1, s, w), lambda bi, j: (bi, 0, 0)),
                  pl.BlockSpec((1, ctx_len, w), lambda bi, j: (bi, 0, 0)),
                  pl.BlockSpec((1, ctx_len, w), lambda bi, j: (bi, 0, 0)),
                  pl.BlockSpec(bias.shape, lambda bi, j: (0, 0, 0, 0), pipeline_mode=pl.Buffered(1))],
        out_specs=pl.BlockSpec((1, blk, w), lambda bi, j: (bi, j, 0)),
        out_shape=jax.ShapeDtypeStruct((b, s, w), BF16),
        scratch_shapes=[pltpu.VMEM((2, NA_HEADS * GRID_W, NA_WIN_H * GRID_W), F32),
                        pltpu.VMEM((2, NA_HEADS * GRID_W, ctx_len), F32)],
        compiler_params=_params(("arbitrary", "arbitrary")),
        name="na_attn",
    )(q, k, v, kc, vc, bias)


def _gla_kernel(n_groups, gq_ref, gk_ref, gv_ref, lr_ref, sog_ref, ck_ref, cv_ref, clr_ref,
                wa_ref, ba_ref, gn_ref, o_ref, st_scr, of_scr, qe_scr, kd_scr, ckd_scr, dec_scr, cdec_scr,
                qk_a):
    c_len = GLA_CHUNK
    grp = GLA_GROUP
    cpg = grp // c_len
    row = lax.broadcasted_iota(jnp.int32, (grp, grp), 0)
    col = lax.broadcasted_iota(jnp.int32, (grp, grp), 1)
    same_chunk = (row // c_len) == (col // c_len)
    masks = (same_chunk & (col <= row), same_chunk & (col >= row))

    dirs = range(2)

    def gates(lr_rows, k_rows):
        zs = [jnp.dot(lr_rows, wa_ref[t], preferred_element_type=F32) + ba_ref[t:t + 1, :] for t in dirs]
        gs = [(jnp.minimum(z, 0.0) * LOG2E - jnp.log2(1.0 + jnp.exp2(jnp.abs(z) * (-LOG2E)))) * (1.0 / GLA_GATE_NORM)
              for z in zs]
        g1s = [g.astype(BF16) for g in gs]
        g2s = [(g - g1.astype(F32)).astype(BF16) for g, g1 in zip(gs, g1s)]
        tris = [masks[t].astype(BF16) for t in dirs]
        bs = [jnp.dot(tris[t], g1s[t], preferred_element_type=F32)
              + jnp.dot(tris[t], g2s[t], preferred_element_type=F32) for t in dirs]
        out = []
        for t in dirs:
            edge = c_len - 1 if t == 0 else 0
            decs = [jnp.exp2(bs[t][j * c_len + edge:j * c_len + edge + 1, :]) for j in range(cpg)]
            ke = k_rows * jnp.exp2(-bs[t])
            kd = (ke * jnp.concatenate([jnp.broadcast_to(v, (c_len, v.shape[1])) for v in decs],
                                       axis=0)).astype(BF16)
            dec = jnp.concatenate(decs + [jnp.ones_like(v) for v in decs], axis=0)
            out.append((bs[t], ke.astype(BF16), kd, dec))
        return out

    for t, (_, _, kd, dec) in enumerate(gates(clr_ref[0], ck_ref[0].astype(F32))):
        ckd_scr[t] = kd
        cdec_scr[t] = dec

    def group_gates(gi, qk_buf):
        rows = pl.ds(pl.multiple_of(gi * grp, grp), grp)
        qf = gq_ref[0, rows, :].astype(F32)
        for t, (b, ke, kd, dec) in enumerate(gates(lr_ref[0, rows, :], gk_ref[0, rows, :].astype(F32))):
            qe = (qf * jnp.exp2(b)).astype(BF16)
            qk_buf[t] = qe
            qk_buf[2 + t] = ke
            qe_scr[t, rows, :] = qe
            kd_scr[t, rows, :] = kd
            dec_scr[t, gi] = dec

    def group_attend(gi, qk_buf):
        rows = pl.ds(pl.multiple_of(gi * grp, grp), grp)
        heads = range(GLA_HEADS)
        kls = [slice(h * GLA_HK, (h + 1) * GLA_HK) for h in heads]
        raw = [[lax.dot_general(qk_buf[t, :, kls[h]], qk_buf[2 + t, :, kls[h]], _NT, preferred_element_type=F32)
                for t in dirs] for h in heads]
        atts = [(jnp.where(masks[0], raw[h][0], 0.0) + jnp.where(masks[1], raw[h][1], 0.0)).astype(BF16)
                for h in heads]
        for h in heads:
            vl = slice(h * GLA_HV, (h + 1) * GLA_HV)
            of_scr[rows, vl] = jnp.dot(atts[h], gv_ref[0, rows, vl], preferred_element_type=F32)

    def state_step(t, h, v_rows, kd_rows, dec_row):
        kl = slice(h * GLA_HK, (h + 1) * GLA_HK)
        upd = lax.dot_general(v_rows, kd_rows[:, kl], _TN, preferred_element_type=F32)
        st_scr[t, h] = st_scr[t, h] * dec_row[:, kl] + upd

    def chunk_order(t):
        return range(cpg) if t == 0 else range(cpg - 1, -1, -1)

    st_scr[...] = jnp.zeros_like(st_scr)
    for t in dirs:
        for j in chunk_order(t):
            rows = slice(j * c_len, (j + 1) * c_len)
            for h in range(GLA_HEADS):
                state_step(t, h, cv_ref[0, rows, h * GLA_HV:(h + 1) * GLA_HV],
                           ckd_scr[t, rows, :], cdec_scr[t, j:j + 1, :])

    def chunk_step(t, gi, j, finish):
        rows = pl.ds(pl.multiple_of(gi * grp + j * c_len, c_len), c_len)
        qe_rows = qe_scr[t, rows, :]
        kd_rows = kd_scr[t, rows, :]
        dec_row = dec_scr[t, gi, j:j + 1, :]
        for h in range(GLA_HEADS):
            kl = slice(h * GLA_HK, (h + 1) * GLA_HK)
            vl = slice(h * GLA_HV, (h + 1) * GLA_HV)
            o_t = of_scr[rows, vl] + lax.dot_general(qe_rows[:, kl], st_scr[t, h].astype(BF16), _NT,
                                                     preferred_element_type=F32)
            if finish:
                ms = jnp.mean(o_t * o_t, axis=-1, keepdims=True)
                o_n = o_t * lax.rsqrt(ms + EPS) * gn_ref[:, vl] * sog_ref[0, rows, vl].astype(F32)
                o_ref[0, rows, vl] = o_n.astype(o_ref.dtype)
            else:
                of_scr[rows, vl] = o_t
            state_step(t, h, gv_ref[0, rows, vl], kd_rows, dec_row)

    def scan_steps(i, finish):
        steps = []
        for jf, jb in zip(chunk_order(0), chunk_order(1)):
            steps.append(functools.partial(chunk_step, 0, i, jf, finish))
            steps.append(functools.partial(chunk_step, 1, n_groups - 1 - i, jb, finish))
        return steps

    half = n_groups // 2
    for g in (0, n_groups - 1):
        group_gates(g, qk_a)
        group_attend(g, qk_a)

    def prepare_and_scan(i, carry):
        steps = scan_steps(i, False)
        per_phase = len(steps) // 4
        for n_phase, phase in enumerate((functools.partial(group_gates, i + 1, qk_a),
                                         functools.partial(group_attend, i + 1, qk_a),
                                         functools.partial(group_gates, n_groups - 2 - i, qk_a),
                                         functools.partial(group_attend, n_groups - 2 - i, qk_a))):
            phase()
            for step in steps[n_phase * per_phase:(n_phase + 1) * per_phase]:
                step()
        return carry

    def scan_only(finish):
        def body(i, carry):
            for step in scan_steps(i, finish):
                step()
            return carry
        return body

    lax.fori_loop(0, half - 1, prepare_and_scan, 0)
    lax.fori_loop(half - 1, half, scan_only(False), 0)
    lax.fori_loop(half, n_groups, scan_only(True), 0)


def _gla(gq, gk, gv, lr, sog, ck, cv, clr, wa, ba, gn):
    b, s, _ = gq.shape
    ctx_len = ck.shape[1]
    assert ctx_len == GLA_GROUP and s % (2 * GLA_GROUP) == 0
    n_groups = s // GLA_GROUP
    cpg = GLA_GROUP // GLA_CHUNK

    def per_batch(t, w):
        return pl.BlockSpec((1, t, w), lambda bi: (bi, 0, 0))

    return pl.pallas_call(
        functools.partial(_gla_kernel, n_groups),
        grid=(b,),
        in_specs=[per_batch(s, GLA_DK), per_batch(s, GLA_DK), per_batch(s, GLA_DV),
                  per_batch(s, LR_PAD), per_batch(s, GLA_DV),
                  per_batch(ctx_len, GLA_DK), per_batch(ctx_len, GLA_DV), per_batch(ctx_len, LR_PAD),
                  pl.BlockSpec(wa.shape, lambda bi: (0, 0, 0)),
                  pl.BlockSpec(ba.shape, lambda bi: (0, 0)),
                  pl.BlockSpec(gn.shape, lambda bi: (0, 0))],
        out_specs=per_batch(s, GLA_DV),
        out_shape=jax.ShapeDtypeStruct((b, s, GLA_DV), BF16),
        scratch_shapes=[pltpu.VMEM((2, GLA_HEADS, GLA_HV, GLA_HK), F32),
                        pltpu.VMEM((s, GLA_DV), F32),
                        pltpu.VMEM((2, s, GLA_DK), BF16),
                        pltpu.VMEM((2, s, GLA_DK), BF16),
                        pltpu.VMEM((2, ctx_len, GLA_DK), BF16),
                        pltpu.VMEM((2, n_groups, 2 * cpg, GLA_DK), F32),
                        pltpu.VMEM((2, 2 * cpg, GLA_DK), F32),
                        pltpu.VMEM((4, GLA_GROUP, GLA_DK), BF16)],
        compiler_params=_params(("arbitrary",), vmem=60 * 1024 * 1024),
        name="gla",
    )(gq, gk, gv, lr, sog, ck, cv, clr, wa, ba, gn)


def _merge_kernel(ona_ref, ogla_ref, sgn_ref, sgg_ref, x_ref, mod_ref, gffn_ref,
                  wna_ref, wgla_ref, wout_ref, wr_ref, br_ref,
                  h_ref, f_ref, route_ref, route_t_ref, cnt_ref):
    rows = x_ref.shape[0] // MERGE_TILES

    def tile_rows(u):
        return slice(u * rows, (u + 1) * rows)

    def branches(u):
        rs = tile_rows(u)
        return (sgn_ref[rs, :].astype(F32) * jnp.dot(ona_ref[rs, :], wna_ref[...], preferred_element_type=F32)
                + sgg_ref[rs, :].astype(F32) * jnp.dot(ogla_ref[rs, :], wgla_ref[...], preferred_element_type=F32))

    def project(y1):
        return jnp.dot(y1.astype(BF16), wout_ref[...], preferred_element_type=F32)

    def residual(u, y):
        rs = tile_rows(u)
        h = x_ref[rs, :] + mod_ref[0, 2:3, :] * y
        h_ref[rs, :] = h
        ms = jnp.mean(h * h, axis=-1, keepdims=True)
        f = h * lax.rsqrt(ms + EPS) * gffn_ref[...] * (1.0 + mod_ref[0, 4:5, :]) + mod_ref[0, 3:4, :]
        f_ref[rs, :] = f.astype(f_ref.dtype)
        return f

    def router(f):
        f_hi = f.astype(BF16)
        f_lo = (f - f_hi.astype(F32)).astype(BF16)
        both = jnp.dot(f_hi, wr_ref[...], preferred_element_type=F32)
        return (both[:, :ROUTE_LANES] + both[:, ROUTE_LANES:]
                + jnp.dot(f_lo, wr_ref[:, :ROUTE_LANES], preferred_element_type=F32)) + br_ref[...]

    y_a = project(branches(0))
    y1_b = branches(1)
    f_a = residual(0, y_a)
    y_b = project(y1_b)
    logits_a = router(f_a)
    f_b = residual(1, y_b)
    _route_tile(0, tile_rows(0), logits_a, route_ref, route_t_ref, cnt_ref)
    _route_tile(1, tile_rows(1), router(f_b), route_ref, route_t_ref, cnt_ref)


def _route_tile(u, rs, logits, route_ref, route_t_ref, cnt_ref):
    rows = logits.shape[0]
    lt = logits.T[0:ROUTE_ROWS, :]
    er = lax.broadcasted_iota(jnp.int32, (ROUTE_ROWS, rows), 0)
    big = jnp.int32(ROUTE_ROWS)

    def first_argmax(vals):
        m = jnp.max(vals, axis=0, keepdims=True)
        idx = jnp.min(jnp.where(vals == m, er, big), axis=0, keepdims=True)
        return m, idx

    is_grp = er < N_GROUPS
    lg = jnp.where(is_grp, lt, -jnp.inf)
    mg, grp = first_argmax(lg)
    p_grp = 1.0 / jnp.sum(jnp.where(is_grp, jnp.exp(lg - mg), 0.0), axis=0, keepdims=True)
    lo = EXPERT_LANE0 + grp * EXPERTS_PER_GROUP
    in_grp = (er >= lo) & (er < lo + EXPERTS_PER_GROUP)
    le = jnp.where(in_grp, lt, -jnp.inf)
    m1, i1 = first_argmax(le)
    m2, i2 = first_argmax(jnp.where(er == i1, -jnp.inf, le))
    t = jnp.exp(m2 - m1)
    w1 = p_grp / (1.0 + t)
    w2 = p_grp * t / (1.0 + t)

    hot1 = er == i1
    hot2 = er == i2
    onehot = (hot1 | hot2).astype(BF16)
    s_i = lax.broadcasted_iota(jnp.int32, (rows, rows), 0)
    t_i = lax.broadcasted_iota(jnp.int32, (rows, rows), 1)
    rank = jnp.dot(onehot, (s_i < t_i).astype(BF16), preferred_element_type=F32)
    cnt = jnp.sum(onehot.astype(F32), axis=1, keepdims=True)
    cnt_al = jnp.floor((cnt + (SEG_ALIGN - 1.0)) * (1.0 / SEG_ALIGN)) * SEG_ALIGN
    e_i = lax.broadcasted_iota(jnp.int32, (ROUTE_ROWS, ROUTE_ROWS), 0)
    e_j = lax.broadcasted_iota(jnp.int32, (ROUTE_ROWS, ROUTE_ROWS), 1)
    lstart = jnp.dot((e_j < e_i).astype(BF16), jnp.broadcast_to(cnt_al, (ROUTE_ROWS, LANES)).astype(BF16),
                     preferred_element_type=F32)[:, 0:1]
    pos = rank + lstart
    pos1 = jnp.sum(jnp.where(hot1, pos, 0.0), axis=0, keepdims=True)
    pos2 = jnp.sum(jnp.where(hot2, pos, 0.0), axis=0, keepdims=True)

    e1 = (i1 - EXPERT_LANE0).astype(F32)
    e2 = (i2 - EXPERT_LANE0).astype(F32)
    fields = jnp.concatenate([e1, e2, w1, w2, pos1, pos2, jnp.zeros((2, rows), F32)], axis=0)
    route_t_ref[u] = fields
    route_ref[rs, :] = jnp.concatenate([fields, jnp.zeros((ROUTE_LANES - 8, rows), F32)], axis=0).T
    cnt_rows = jnp.concatenate([jnp.broadcast_to(cnt, (ROUTE_ROWS, ROUTE_LANES)),
                                jnp.zeros((ROUTE_LANES - ROUTE_ROWS, ROUTE_LANES), F32)], axis=0)
    cnt_ref[u] = cnt_rows.T[0:1, :]


def _merge_route(ona, ogla, sgn, sgg, x2d, mod3, gffn, wna, wgla, wout, wr, br, tokens_per_batch, tm):
    n, d = x2d.shape
    step_rows = MERGE_TILES * tm
    assert tokens_per_batch % step_rows == 0
    per_b = tokens_per_batch // step_rows

    def tile(w):
        return pl.BlockSpec((step_rows, w), lambda i: (i, 0))

    def whole(a):
        return pl.BlockSpec(a.shape, lambda i: (0,) * a.ndim)

    return pl.pallas_call(
        _merge_kernel,
        grid=(n // step_rows,),
        in_specs=[tile(NA_WIDTH), tile(GLA_DV), tile(d), tile(d), tile(d),
                  pl.BlockSpec((1, 6, d), lambda i: (i // per_b, 0, 0)),
                  whole(gffn), whole(wna), whole(wgla), whole(wout), whole(wr), whole(br)],
        out_specs=[tile(d), tile(d), tile(ROUTE_LANES),
                   pl.BlockSpec((MERGE_TILES, 8, tm), lambda i: (i, 0, 0)),
                   pl.BlockSpec((MERGE_TILES, 1, ROUTE_LANES), lambda i: (i, 0, 0))],
        out_shape=[jax.ShapeDtypeStruct((n, d), F32),
                   jax.ShapeDtypeStruct((n, d), BF16),
                   jax.ShapeDtypeStruct((n, ROUTE_LANES), F32),
                   jax.ShapeDtypeStruct((n // tm, 8, tm), F32),
                   jax.ShapeDtypeStruct((n // tm, 1, ROUTE_LANES), F32)],
        compiler_params=_params(("arbitrary",)),
        name="merge_route",
    )(ona, ogla, sgn, sgg, x2d, mod3, gffn, wna, wgla, wout, wr, br)


HI_MASK = -65536


def _pack_rows(a):
    half = a.shape[1] // 2
    hi = lax.bitcast_convert_type(a[:, :half], jnp.int32)
    lo = lax.shift_right_logical(lax.bitcast_convert_type(a[:, half:], jnp.int32), 16)
    return hi | lo


def _unpack_rows(w):
    hi = lax.bitcast_convert_type(w & HI_MASK, F32).astype(BF16)
    lo = lax.bitcast_convert_type(lax.shift_left(w, 16), F32).astype(BF16)
    return hi, lo


def _piece_table(cnt_al, lstart, seg_dst):
    n_tiles = cnt_al.shape[0]

    def compact(mask, local, glob, slots):
        pos = jnp.cumsum(mask.astype(jnp.int32), axis=1) - 1
        hot = mask[:, :, None] & (pos[:, :, None] == jnp.arange(slots, dtype=jnp.int32))
        pick = lambda v: jnp.sum(jnp.where(hot, v[:, :, None], 0), axis=1)
        return jnp.sum(mask, axis=1).astype(jnp.int32), pick(local), pick(glob)

    counts, lists = [], []
    big = PIECE_SIZES[-1]
    for size, slots in zip(PIECE_SIZES[:-1], PIECE_SLOTS[:-1]):
        n, lo, gl = compact((cnt_al & (big - 1)) == size, lstart, seg_dst, slots)
        counts.append(n)
        lists += [lo, gl]
    k = jnp.arange(MAX_BIG_PER_RUN, dtype=jnp.int32)
    off = (cnt_al & (big - 1))[:, :, None] + big * k
    mask = k < (cnt_al // big)[:, :, None]
    flat = lambda a: a.reshape(n_tiles, -1)
    n, lo, gl = compact(flat(mask), flat(lstart[:, :, None] + off), flat(seg_dst[:, :, None] + off), PIECE_SLOTS[-1])
    counts.append(n)
    lists += [lo, gl]
    head = jnp.stack(counts + [jnp.sum(cnt_al, axis=1)], axis=1)
    head = jnp.pad(head, ((0, 0), (0, PIECE_HEAD - head.shape[1])))
    table = jnp.concatenate([head] + lists, axis=1)
    table = jnp.pad(table, ((0, 0), (0, PIECE_TABLE - table.shape[1])))
    return table.reshape(n_tiles, 1, PIECE_TABLE).astype(jnp.int32)


def _segment_copies(seg_ref, make_copy, action):
    if action == "wait":
        total = seg_ref[0, 0, len(PIECE_SIZES)]
        for size in WAIT_SIZES:
            @pl.when((total & size) != 0)
            def _():
                make_copy(0, 0, size).wait()
        return

    base = PIECE_HEAD
    for c, (size, slots) in enumerate(zip(PIECE_SIZES, PIECE_SLOTS)):
        def piece(k, carry, size=size, lo=base, gl=base + slots):
            make_copy(pl.multiple_of(seg_ref[0, 0, lo + k], SEG_ALIGN),
                      pl.multiple_of(seg_ref[0, 0, gl + k], SEG_ALIGN), size).start()
            return carry

        lax.fori_loop(0, seg_ref[0, 0, c], piece, 0)
        base += 2 * slots


def _dispatch_kernel(seg_ref, seg_prev_ref, tail_ref, lpos_ref, f_ref, xs_ref, sbuf, zbuf, sems):
    i = pl.program_id(0)
    last = pl.num_programs(0) - 1
    tm = f_ref.shape[0]
    rows_l = sbuf.shape[1]
    slot = i % 2
    fields = lpos_ref[0]
    lp = fields.astype(jnp.int32)
    half = f_ref.shape[1] // 2
    row = lax.broadcasted_iota(jnp.int32, (rows_l, tm), 0)
    first = row == lp[4:5, :]
    second = row == lp[5:6, :]
    sbuf[slot, :, :half] = _pack_rows(jnp.dot((first | second).astype(BF16), f_ref[...], preferred_element_type=F32))
    w_sorted = jnp.sum(jnp.where(first, fields[2:3, :], 0.0) + jnp.where(second, fields[3:4, :], 0.0),
                       axis=1, keepdims=True)
    sbuf[slot, :, half:] = lax.bitcast_convert_type(jnp.broadcast_to(w_sorted, (rows_l, LANES)), jnp.int32)

    def copier(s):
        def copy(local, glob, size):
            return pltpu.make_async_copy(sbuf.at[s, pl.ds(local, size), :], xs_ref.at[pl.ds(glob, size), :],
                                         sems.at[s])
        return copy

    _segment_copies(seg_ref, copier(slot), "start")

    @pl.when(i > 0)
    def _():
        _segment_copies(seg_prev_ref, copier(1 - slot), "wait")

    @pl.when(i == last)
    def _():
        _segment_copies(seg_ref, copier(slot), "wait")
        sem = sems.at[slot]
        zbuf[...] = jnp.zeros_like(zbuf)

        def tail(action):
            def body(e, carry):
                start = tail_ref[0, 0, e]
                length = tail_ref[0, 0, N_EXPERTS + e]
                off = jnp.int32(0)
                for size in TAIL_SIZES:
                    @pl.when((length & size) != 0)
                    def _():
                        cp = pltpu.make_async_copy(zbuf.at[pl.ds(0, size), :],
                                                   xs_ref.at[pl.ds(pl.multiple_of(start + off, SEG_ALIGN), size), :],
                                                   sem)
                        getattr(cp, action)()
                    off = off + (length & size)
                return carry
            lax.fori_loop(0, N_EXPERTS, body, 0)

        tail("start")
        tail("wait")

        def unused(action):
            def body(blk, carry):
                cp = pltpu.make_async_copy(zbuf, xs_ref.at[pl.ds(pl.multiple_of(blk * MOE_ROWS, MOE_ROWS),
                                                                  MOE_ROWS), :], sem)
                getattr(cp, action)()
                return carry
            lax.fori_loop(tail_ref[0, 0, 2 * N_EXPERTS], xs_ref.shape[0] // MOE_ROWS, body, 0)

        unused("start")
        unused("wait")


def _dispatch(seg, tail, lpos_rows, f2d, n_pad, rows_l, tm):
    n, d = f2d.shape
    return pl.pallas_call(
        _dispatch_kernel,
        grid=(n // tm,),
        in_specs=[pl.BlockSpec((1, 1, seg.shape[2]), lambda i: (i, 0, 0), memory_space=pltpu.SMEM),
                  pl.BlockSpec((1, 1, seg.shape[2]), lambda i: (jnp.maximum(i - 1, 0), 0, 0),
                               memory_space=pltpu.SMEM),
                  pl.BlockSpec(tail.shape, lambda i: (0, 0, 0), memory_space=pltpu.SMEM),
                  pl.BlockSpec((1, 8, tm), lambda i: (i, 0, 0)),
                  pl.BlockSpec((tm, d), lambda i: (i, 0))],
        out_specs=pl.BlockSpec(memory_space=pl.ANY),
        out_shape=jax.ShapeDtypeStruct((n_pad, d // 2 + LANES), jnp.int32),
        scratch_shapes=[pltpu.VMEM((2, rows_l, d // 2 + LANES), jnp.int32),
                        pltpu.VMEM((MOE_ROWS, d // 2 + LANES), jnp.int32),
                        pltpu.SemaphoreType.DMA((2,))],
        compiler_params=_params(("arbitrary",)),
        name="dispatch",
    )(seg, seg, tail, lpos_rows, f2d)


def _expert_kernel(be_ref, nu_ref, x_ref, wg_ref, wu_ref, wd_ref, y_ref, wgu_scr, wd_scr):
    i = pl.program_id(0)
    prev = be_ref[jnp.maximum(i - 1, 0)]

    @pl.when((i == 0) | (be_ref[i] != prev))
    def _():
        wgu_scr[:, :D_EXPERT] = wg_ref[0].astype(BF16)
        wgu_scr[:, D_EXPERT:] = wu_ref[0].astype(BF16)
        wd_scr[...] = wd_ref[0].astype(BF16)

    @pl.when(i < nu_ref[0])
    def _():
        half = y_ref.shape[1]
        x = jnp.concatenate(_unpack_rows(x_ref[:, :half]), axis=1)
        gu = jnp.dot(x, wgu_scr[...], preferred_element_type=F32)
        gate = gu[:, :D_EXPERT]
        hdn = gate * jax.nn.sigmoid(gate) * gu[:, D_EXPERT:]
        y = jnp.dot(hdn.astype(BF16), wd_scr[...], preferred_element_type=F32)
        y = y * lax.bitcast_convert_type(x_ref[:, half:half + 1], F32)
        y_ref[...] = _pack_rows(y.astype(BF16).astype(F32))

    @pl.when(i >= nu_ref[0])
    def _():
        y_ref[...] = jnp.zeros_like(y_ref)


def _experts(blk_expert, n_used, xs, wg, wu, wd):
    n_pad, row_words = xs.shape
    half = row_words - LANES
    d = 2 * half
    n_blk = n_pad // MOE_ROWS

    def used_block(i, be, nu):
        return (jnp.maximum(jnp.minimum(i, nu[0] - 1), 0), 0)

    grid_spec = pltpu.PrefetchScalarGridSpec(
        num_scalar_prefetch=2,
        grid=(n_blk,),
        in_specs=[pl.BlockSpec((MOE_ROWS, row_words), used_block),
                  pl.BlockSpec((1, d, D_EXPERT), lambda i, be, nu: (be[i], 0, 0)),
                  pl.BlockSpec((1, d, D_EXPERT), lambda i, be, nu: (be[i], 0, 0)),
                  pl.BlockSpec((1, D_EXPERT, d), lambda i, be, nu: (be[i], 0, 0))],
        out_specs=pl.BlockSpec((MOE_ROWS, half), lambda i, be, nu: (i, 0)),
        scratch_shapes=[pltpu.VMEM((d, 2 * D_EXPERT), BF16),
                        pltpu.VMEM((D_EXPERT, d), BF16)],
    )
    return pl.pallas_call(
        _expert_kernel,
        grid_spec=grid_spec,
        out_shape=jax.ShapeDtypeStruct((n_pad, half), jnp.int32),
        compiler_params=_params(("arbitrary",)),
        name="experts",
    )(blk_expert, n_used, xs, wg, wu, wd)


def _combine_kernel(seg_ref, seg_next_ref, col_ref, h_ref, mod_ref, fg_ref, ys_ref, o_ref, ybuf, sems):
    i = pl.program_id(0)
    tm = h_ref.shape[0]
    rows_l = ybuf.shape[1]
    slot = i % 2

    def copier(s):
        def copy(local, glob, size):
            return pltpu.make_async_copy(ys_ref.at[pl.ds(glob, size), :], ybuf.at[s, pl.ds(local, size), :],
                                         sems.at[s])
        return copy

    @pl.when(i == 0)
    def _():
        ybuf[...] = jnp.zeros_like(ybuf)
        _segment_copies(seg_ref, copier(slot), "start")

    @pl.when(i + 1 < pl.num_programs(0))
    def _():
        _segment_copies(seg_next_ref, copier(1 - slot), "start")

    _segment_copies(seg_ref, copier(slot), "wait")
    y_hi, y_lo = _unpack_rows(ybuf[slot])
    y_all = jnp.concatenate([y_hi, y_lo], axis=1)
    col = lax.broadcasted_iota(jnp.int32, (tm, rows_l), 1)
    info = col_ref[...]
    pick = (col == info[:, 4:5].astype(jnp.int32)) | (col == info[:, 5:6].astype(jnp.int32))
    moe = jnp.dot(pick.astype(BF16), y_all, preferred_element_type=F32)
    h = h_ref[...] + mod_ref[0, 5:6, :] * moe
    ms = jnp.mean(h * h, axis=-1, keepdims=True)
    o_ref[...] = h * lax.rsqrt(ms + EPS) * fg_ref[...]


def _combine(seg, colinfo, h2d, mod3, fg, ys, rows_l, tokens_per_batch, tm):
    n, d = h2d.shape
    per_b = tokens_per_batch // tm
    return pl.pallas_call(
        _combine_kernel,
        grid=(n // tm,),
        in_specs=[pl.BlockSpec((1, 1, seg.shape[2]), lambda i: (i, 0, 0), memory_space=pltpu.SMEM),
                  pl.BlockSpec((1, 1, seg.shape[2]), lambda i: (jnp.minimum(i + 1, n // tm - 1), 0, 0),
                               memory_space=pltpu.SMEM),
                  pl.BlockSpec((tm, colinfo.shape[1]), lambda i: (i, 0)),
                  pl.BlockSpec((tm, d), lambda i: (i, 0)),
                  pl.BlockSpec((1, 6, d), lambda i: (i // per_b, 0, 0)),
                  pl.BlockSpec((1, d), lambda i: (0, 0)),
                  pl.BlockSpec(memory_space=pl.ANY)],
        out_specs=pl.BlockSpec((tm, d), lambda i: (i, 0)),
        out_shape=jax.ShapeDtypeStruct((n, d), F32),
        scratch_shapes=[pltpu.VMEM((2, rows_l, d // 2), jnp.int32), pltpu.SemaphoreType.DMA((2,))],
        compiler_params=_params(("arbitrary",)),
        name="combine",
    )(seg, seg, colinfo, h2d, mod3, fg, ys)


def _rope_tables(s):
    t = np.arange(s)
    pos_r = (t // GRID_W).astype(np.float32)
    pos_c = (t % GRID_W).astype(np.float32)
    nf = GLA_HK // 4
    inv = (np.float32(ROPE_BASE) ** (-np.arange(nf, dtype=np.float32) / np.float32(nf))).astype(np.float32)
    ang_r = pos_r[:, None] * inv
    ang_c = pos_c[:, None] * inv

    def half(ang):
        return (np.concatenate([np.cos(ang), np.cos(ang)], axis=-1),
                np.concatenate([-np.sin(ang), np.sin(ang)], axis=-1))

    cr, sr = half(ang_r)
    cc, sc = half(ang_c)
    cos = np.tile(np.concatenate([cr, cc], axis=-1), (1, GLA_HEADS)).astype(np.float32)
    sin = np.tile(np.concatenate([sr, sc], axis=-1), (1, GLA_HEADS)).astype(np.float32)
    return jnp.asarray(cos), jnp.asarray(sin)


def kernel(x, c, ctx, c_ctx, w_mod, b_mod, norm_attn_g, norm_ffn_g, w_in, w_gla_a2, b_gla_a2, gla_norm_g, na_rpb, w_na_o, w_gla_o, w_out, w_group, b_group, w_expert, b_expert, w_exp_gate, w_exp_up, w_exp_down, final_norm_g):
    b, s, d = x.shape
    ctx_len = ctx.shape[1]
    n = b * s
    assert w_mod.shape[0] == 1, "single-layer block"
    assert s % (GRID_W * NA_WIN_H) == 0 and ctx_len % GLA_CHUNK == 0

    mod_rows = -(-(b + 1) // 8) * 8
    cs = jnp.zeros((mod_rows, d), F32).at[:b].set(c).at[b].set(c_ctx)
    mod3 = _modulation(cs, w_mod[0], b_mod[0]).reshape(mod_rows, 6, d)

    o_q, o_k, o_v = 0, NA_WIDTH, 2 * NA_WIDTH
    o_gq = 3 * NA_WIDTH
    o_gk = o_gq + GLA_DK
    o_gv = o_gk + GLA_DK
    o_og = o_gv + GLA_DV
    o_lr = o_og + GLA_DV
    o_mn = o_lr + 2 * GLA_GATE_RANK
    w_bf = w_in[0].astype(BF16)
    w_a = (w_bf, o_lr)
    w_g = (w_bf[:, o_mn:], 2 * d)
    w_lr = (jnp.pad(w_bf[:, o_lr:o_mn], ((0, 0), (0, LR_PAD - 2 * GLA_GATE_RANK))), LR_PAD)
    cw = 512
    lat_specs = [(0, 0, 0, o_q, cw, "plain"), (1, 0, 0, o_k, cw, "plain"), (2, 0, 0, o_v, cw, "plain"),
                 (3, 0, 0, o_gq, cw, "rope_scaled"), (4, 0, 0, o_gk, cw, "rope"),
                 (5, 0, 0, o_gv, cw, "plain"), (5, cw, 0, o_gv + cw, cw, "plain"),
                 (6, 0, 0, o_og, cw, "silu"), (6, cw, 0, o_og + cw, cw, "silu"),
                 (7, 0, 1, 0, cw, "sigmoid"), (7, cw, 1, cw, cw, "sigmoid"),
                 (8, 0, 1, d, cw, "sigmoid"), (8, cw, 1, d + cw, cw, "sigmoid"),
                 (9, 0, 2, 0, LR_PAD, "plain")]
    lat_widths = [NA_WIDTH, NA_WIDTH, NA_WIDTH, GLA_DK, GLA_DK, GLA_DV, GLA_DV, d, d, LR_PAD]
    tm = 512
    per_b = s // tm
    q_na, k_na, v_na, gq, gk, gv, sog, sgn, sgg, lr = _inproj(
        x.reshape(n, d), mod3, lambda i: i // per_b, norm_attn_g[0], [w_a, w_g, w_lr], lat_specs, lat_widths, tm,
        rope=_rope_tables(s), name="inproj_lat")

    ctx_specs = [(0, 0, 0, o_k, cw, "plain"), (1, 0, 0, o_v, cw, "plain"), (2, 0, 0, o_gk, cw, "plain"),
                 (3, 0, 0, o_gv, cw, "plain"), (3, cw, 0, o_gv + cw, cw, "plain"),
                 (4, 0, 1, 0, LR_PAD, "plain")]
    ctx_widths = [NA_WIDTH, NA_WIDTH, GLA_DK, GLA_DV, LR_PAD]
    kc_na, vc_na, ck, cv, clr = _inproj(
        ctx.reshape(b * ctx_len, d), mod3, lambda i: b, norm_attn_g[0], [w_a, w_lr], ctx_specs, ctx_widths,
        tm if (b * ctx_len) % tm == 0 else ctx_len, name="inproj_ctx")

    def lat3(a):
        return a.reshape(b, s, a.shape[-1])

    def ctx3(a):
        return a.reshape(b, ctx_len, a.shape[-1])

    o_na = _na_attention(lat3(q_na), lat3(k_na), lat3(v_na), ctx3(kc_na), ctx3(vc_na),
                         _na_bias_table(na_rpb[0]))

    wa = jnp.zeros((2, LR_PAD, GLA_DK), F32)
    wa = wa.at[0, :GLA_GATE_RANK].set(w_gla_a2[0, 0]).at[1, GLA_GATE_RANK:2 * GLA_GATE_RANK].set(w_gla_a2[0, 1])
    gn = jnp.tile(gla_norm_g[0], GLA_HEADS).reshape(1, GLA_DV)
    o_gla = _gla(lat3(gq), lat3(gk), lat3(gv), lat3(lr), lat3(sog), ctx3(ck), ctx3(cv), ctx3(clr),
                 wa.astype(BF16), b_gla_a2[0], gn)

    wr = jnp.zeros((d, ROUTE_LANES), F32)
    wr = wr.at[:, :N_GROUPS].set(w_group[0]).at[:, EXPERT_LANE0:EXPERT_LANE0 + N_EXPERTS].set(w_expert[0])
    wr_hi = wr.astype(BF16)
    br = jnp.zeros((1, ROUTE_LANES), F32)
    br = br.at[0, :N_GROUPS].set(b_group[0]).at[0, EXPERT_LANE0:EXPERT_LANE0 + N_EXPERTS].set(b_expert[0])
    h, f_lat, route, route_t, counts = _merge_route(
        o_na.reshape(n, NA_WIDTH), o_gla.reshape(n, GLA_DV), sgn, sgg, x.reshape(n, d), mod3, norm_ffn_g[0].reshape(1, d),
        w_na_o[0].astype(BF16), w_gla_o[0].astype(BF16), w_out[0].astype(BF16),
        jnp.concatenate([wr_hi, (wr - wr_hi.astype(F32)).astype(BF16)], axis=1), br, s, tm)

    n_tiles = n // tm
    cnt = counts[:, 0, EXPERT_LANE0:EXPERT_LANE0 + N_EXPERTS].astype(jnp.int32)
    cnt_al = (cnt + SEG_ALIGN - 1) // SEG_ALIGN * SEG_ALIGN
    lstart = jnp.cumsum(cnt_al, axis=1) - cnt_al
    total = jnp.sum(cnt_al, axis=0)
    padded = (total + MOE_ROWS - 1) // MOE_ROWS * MOE_ROWS
    pend = jnp.cumsum(padded)
    pstart = pend - padded
    seg_dst = pstart[None, :] + jnp.cumsum(cnt_al, axis=0) - cnt_al
    assert tm // PIECE_SIZES[-1] <= MAX_BIG_PER_RUN
    assert (2 * tm + N_EXPERTS * (SEG_ALIGN - 1)) // PIECE_SIZES[-1] <= PIECE_SLOTS[-1]
    seg = _piece_table(cnt_al, lstart, seg_dst)
    n_used = (pend[-1] // MOE_ROWS).reshape(1)
    tail = jnp.concatenate([pstart + total, padded - total, n_used, jnp.zeros((2 * N_EXPERTS - 1,), jnp.int32)])
    tail = tail.reshape(1, 1, 4 * N_EXPERTS)
    seg_pad = n_tiles * N_EXPERTS * (SEG_ALIGN - 1)
    n_pad = (2 * n + seg_pad + N_EXPERTS * (MOE_ROWS - SEG_ALIGN) + MOE_ROWS - 1) // MOE_ROWS * MOE_ROWS
    n_blk = n_pad // MOE_ROWS
    rows_l = (2 * tm + N_EXPERTS * (SEG_ALIGN - 1) + MXU_DIM - 1) // MXU_DIM * MXU_DIM
    blk_start = jnp.arange(n_blk, dtype=jnp.int32) * MOE_ROWS
    blk_expert = jnp.minimum(jnp.sum(blk_start[:, None] >= pend[None, :], axis=-1), N_EXPERTS - 1)

    xs = _dispatch(seg, tail, route_t, f_lat, n_pad, rows_l, tm)
    ys = _experts(blk_expert.astype(jnp.int32), n_used.astype(jnp.int32), xs,
                  w_exp_gate[0], w_exp_up[0], w_exp_down[0])
    out = _combine(seg, route, h, mod3, final_norm_g.reshape(1, d), ys, rows_l, s, tm)
    return out.reshape(b, s, d)
```

```python
import functools

import jax
import jax.numpy as jnp
import numpy as np
from jax import lax
from jax.experimental import pallas as pl
from jax.experimental.pallas import tpu as pltpu

F32 = jnp.float32
BF16 = jnp.bfloat16

D_MODEL = 1024
GRID_W = 64
NA_HEADS = 8
NA_HEAD_DIM = 64
NA_WIDTH = NA_HEADS * NA_HEAD_DIM
NA_WIN_H = 8
NA_WIN_W = 16
GLA_HEADS = 4
GLA_DK = D_MODEL // 2
GLA_DV = D_MODEL
GLA_HK = GLA_DK // GLA_HEADS
GLA_HV = GLA_DV // GLA_HEADS
GLA_GATE_RANK = 16
GLA_GATE_NORM = 16.0
GLA_CHUNK = 64
GLA_GROUP = 256
ROPE_BASE = 10000.0
N_GROUPS = 4
EXPERTS_PER_GROUP = 8
N_EXPERTS = N_GROUPS * EXPERTS_PER_GROUP
D_EXPERT = D_MODEL // 2
EPS = 1e-6
NEG_INF = -1e30
LOG2E = 1.4426950408889634

LANES = 128
MXU_DIM = 256
LR_PAD = LANES
ROUTE_LANES = LANES
EXPERT_LANE0 = N_GROUPS
ROUTE_ROWS = 48
MERGE_TILES = 2
COMBINE_TILES = 2
MOE_ROWS = 512
SEG_ALIGN = 8
PIECE_SIZES = (8, 16, 24, 32, 40, 48, 56, 64)
PIECE_SLOTS = (32, 32, 32, 32, 32, 32, 32, 24)
MAX_BIG_PER_RUN = 8
PIECE_HEAD = 16
PIECE_TABLE = 512
WAIT_SIZES = (1024, 512, 256, 128, 64, 32, 16, 8)
TAIL_SIZES = (256, 128, 64, 32, 16, 8)
VMEM_LIMIT = 56 * 1024 * 1024

_NT = (((1,), (1,)), ((), ()))
_TN = (((0,), (0,)), ((), ()))


def _params(sem, vmem=VMEM_LIMIT):
    return pltpu.CompilerParams(dimension_semantics=sem, vmem_limit_bytes=vmem)


def _mod_kernel(c_ref, w_ref, b_ref, o_ref):
    c = c_ref[...]
    s = c * jax.nn.sigmoid(c)
    o_ref[...] = jnp.dot(s.astype(BF16), w_ref[...].astype(BF16),
                         preferred_element_type=F32) + b_ref[...]


def _modulation(cs, w_mod, b_mod):
    rows, d = cs.shape
    n = w_mod.shape[1]
    bn = 1024
    return pl.pallas_call(
        _mod_kernel,
        grid=(n // bn,),
        in_specs=[pl.BlockSpec((rows, d), lambda j: (0, 0)),
                  pl.BlockSpec((d, bn), lambda j: (0, j)),
                  pl.BlockSpec((1, bn), lambda j: (0, j))],
        out_specs=pl.BlockSpec((rows, bn), lambda j: (0, j)),
        out_shape=jax.ShapeDtypeStruct((rows, n), F32),
        compiler_params=_params(("arbitrary",)),
        name="mod",
    )(cs, w_mod, b_mod.reshape(1, n))


def _inproj_kernel(specs, use_rope, n_w, n_out, x_ref, mod_ref, g_ref, *rest):
    if use_rope:
        cos_ref, sin_ref = rest[:2]
        rest = rest[2:]
    w_refs = rest[:n_w]
    outs = rest[n_w:n_w + n_out]
    a_scr = rest[n_w + n_out]
    x = x_ref[...]
    ms = jnp.mean(x * x, axis=-1, keepdims=True)
    a = x * lax.rsqrt(ms + EPS) * g_ref[...] * (1.0 + mod_ref[0, 1:2, :]) + mod_ref[0, 0:1, :]
    a_scr[...] = a.astype(BF16)
    for (oi, oc, wi, wc, width, kind) in specs:
        acc = jnp.dot(a_scr[...], w_refs[wi][:, wc:wc + width], preferred_element_type=F32)
        if kind in ("rope", "rope_scaled"):
            if kind == "rope_scaled":
                acc = acc * (GLA_HK ** -0.5)
            lane = lax.broadcasted_iota(jnp.int32, acc.shape, 1)
            rot = jnp.where((lane & 63) < 32,
                            pltpu.roll(acc, width - 32, axis=1),
                            pltpu.roll(acc, 32, axis=1))
            acc = acc * cos_ref[...] + rot * sin_ref[...]
        elif kind == "silu":
            acc = acc * jax.nn.sigmoid(acc)
        elif kind == "sigmoid":
            acc = jax.nn.sigmoid(acc)
        outs[oi][:, oc:oc + width] = acc.astype(outs[oi].dtype)


def _inproj(x2d, mod3, mod_row_fn, g, weights, specs, out_widths, tm, rope=None, name="inproj"):
    n, d = x2d.shape
    use_rope = rope is not None
    in_specs = [pl.BlockSpec((tm, d), lambda i: (i, 0)),
                pl.BlockSpec((1, 6, d), lambda i: (mod_row_fn(i), 0, 0)),
                pl.BlockSpec((1, d), lambda i: (0, 0))]
    args = [x2d, mod3, g.reshape(1, d)]
    if use_rope:
        cos, sin = rope
        nb = cos.shape[0] // tm
        in_specs += [pl.BlockSpec((tm, cos.shape[1]), lambda i: (i % nb, 0)),
                     pl.BlockSpec((tm, cos.shape[1]), lambda i: (i % nb, 0))]
        args += [cos, sin]
    in_specs += [pl.BlockSpec((d, cols), lambda i: (0, 0), pipeline_mode=pl.Buffered(1)) for _, cols in weights]
    args += [w for w, _ in weights]
    out_specs = [pl.BlockSpec((tm, w), lambda i: (i, 0)) for w in out_widths]
    out_shape = [jax.ShapeDtypeStruct((n, w), BF16) for w in out_widths]
    return pl.pallas_call(
        functools.partial(_inproj_kernel, specs, use_rope, len(weights), len(out_widths)),
        grid=(n // tm,),
        in_specs=in_specs,
        out_specs=out_specs,
        out_shape=out_shape,
        scratch_shapes=[pltpu.VMEM((tm, d), BF16)],
        compiler_params=_params(("arbitrary",)),
        name=name,
    )(*args)


def _na_kernel(rows_per_step, n_rows, q_ref, k_ref, v_ref, kc_ref, vc_ref, bias_ref, o_ref, sw_scr, sc_scr):
    j = pl.program_id(1)
    lane = lax.broadcasted_iota(jnp.int32, (GRID_W, LANES), 1)
    first_head = lane < NA_HEAD_DIM
    scale = NA_HEAD_DIM ** -0.5
    win_keys = NA_WIN_H * GRID_W
    pairs = range(NA_HEADS // 2)
    lanes = [slice(p * LANES, (p + 1) * LANES) for p in pairs]
    stack = 2 * GRID_W

    def window(rr):
        r = j * rows_per_step + rr
        rs = jnp.clip(r - NA_WIN_H // 2, 0, n_rows - NA_WIN_H)
        return r - rs, pl.multiple_of(rr * GRID_W, GRID_W), pl.multiple_of(rs * GRID_W, GRID_W)

    def scores(rr, slot):
        _, t0, ks = window(rr)
        for p in pairs:
            qp = q_ref[0, pl.ds(t0, GRID_W), lanes[p]] * scale
            zero = jnp.zeros_like(qp)
            qs = jnp.concatenate([jnp.where(first_head, qp, zero), jnp.where(first_head, zero, qp)], axis=0)
            sw_scr[slot, p * stack:(p + 1) * stack, :] = lax.dot_general(
                qs, k_ref[0, pl.ds(ks, win_keys), lanes[p]], _NT, preferred_element_type=F32)
            sc_scr[slot, p * stack:(p + 1) * stack, :] = lax.dot_general(
                qs, kc_ref[0, :, lanes[p]], _NT, preferred_element_type=F32)

    def attend(rr, slot):
        d, t0, ks = window(rr)
        pws, pcs, dens = [], [], []
        for p in pairs:
            bias = jnp.concatenate([bias_ref[(NA_WIN_H - 1) - d + 2 * i, p] for i in range(NA_WIN_H // 2)], axis=1)
            sw = sw_scr[slot, p * stack:(p + 1) * stack, :] * LOG2E + bias
            sc = sc_scr[slot, p * stack:(p + 1) * stack, :] * LOG2E
            m = jnp.maximum(jnp.max(sw, axis=-1, keepdims=True), jnp.max(sc, axis=-1, keepdims=True))
            pw = jnp.exp2(sw - m)
            pc = jnp.exp2(sc - m)
            dens.append(jnp.sum(pw, axis=-1, keepdims=True) + jnp.sum(pc, axis=-1, keepdims=True))
            pws.append(pw.astype(BF16))
            pcs.append(pc.astype(BF16))
        for p in pairs:
            o2 = (jnp.dot(pws[p], v_ref[0, pl.ds(ks, win_keys), lanes[p]], preferred_element_type=F32)
                  + jnp.dot(pcs[p], vc_ref[0, :, lanes[p]], preferred_element_type=F32)) / dens[p]
            o_ref[0, pl.ds(t0, GRID_W), lanes[p]] = jnp.where(first_head, o2[:GRID_W],
                                                             o2[GRID_W:]).astype(o_ref.dtype)

    scores(0, 0)

    def two_rows(k, carry):
        scores(2 * k + 1, 1)
        attend(2 * k, 0)
        scores(2 * k + 2, 0)
        attend(2 * k + 1, 1)
        return carry

    lax.fori_loop(0, rows_per_step // 2 - 1, two_rows, 0)
    scores(rows_per_step - 1, 1)
    attend(rows_per_step - 2, 0)
    attend(rows_per_step - 1, 1)


def _na_bias_table(rpb):
    n_heads, n_dr, n_dc = rpb.shape
    c = np.arange(GRID_W)
    cstart = np.clip(c - NA_WIN_W // 2, 0, GRID_W - NA_WIN_W)
    kc = np.arange(GRID_W)
    valid = (kc[None, :] >= cstart[:, None]) & (kc[None, :] < cstart[:, None] + NA_WIN_W)
    col_idx = np.clip(kc[None, :] - c[:, None] + NA_WIN_W - 1, 0, n_dc - 1)
    pick = np.zeros((n_dc, GRID_W * GRID_W), np.float32)
    pick[col_idx.reshape(-1), np.arange(GRID_W * GRID_W)] = 1.0
    t = jnp.dot(rpb.reshape(n_heads * n_dr, n_dc).astype(F32), jnp.asarray(pick),
                precision=lax.Precision.HIGHEST).reshape(n_heads, n_dr, GRID_W, GRID_W)
    t = jnp.where(jnp.asarray(valid)[None, None], t * LOG2E, NEG_INF)
    t2 = jnp.concatenate([t[:, :-1], t[:, 1:]], axis=-1)
    return t2.transpose(1, 0, 2, 3).reshape(n_dr - 1, n_heads // 2, 2 * GRID_W, 2 * GRID_W)


def _na_attention(q, k, v, kc, vc, bias, rows_per_step=32):
    b, s, w = q.shape
    n_rows = s // GRID_W
    ctx_len = kc.shape[1]
    blk = rows_per_step * GRID_W
    return pl.pallas_call(
        functools.partial(_na_kernel, rows_per_step, n_rows),
        grid=(b, n_rows // rows_per_step),
        in_specs=[pl.BlockSpec((1, blk, w), lambda bi, j: (bi, j, 0)),
                  pl.BlockSpec((1, s, w), lambda bi, j: (bi, 0, 0)),
                  pl.BlockSpec((1, s, w), lambda bi, j: (bi, 0, 0)),
                  pl.BlockSpec((1, ctx_len, w), lambda bi, j: (bi, 0, 0)),
                  pl.BlockSpec((1, ctx_len, w), lambda bi, j: (bi, 0, 0)),
                  pl.BlockSpec(bias.shape, lambda bi, j: (0, 0, 0, 0), pipeline_mode=pl.Buffered(1))],
        out_specs=pl.BlockSpec((1, blk, w), lambda bi, j: (bi, j, 0)),
        out_shape=jax.ShapeDtypeStruct((b, s, w), BF16),
        scratch_shapes=[pltpu.VMEM((2, NA_HEADS * GRID_W, NA_WIN_H * GRID_W), F32),
                        pltpu.VMEM((2, NA_HEADS * GRID_W, ctx_len), F32)],
        compiler_params=_params(("arbitrary", "arbitrary")),
        name="na_attn",
    )(q, k, v, kc, vc, bias)


def _gla_kernel(n_groups, gq_ref, gk_ref, gv_ref, lr_ref, sog_ref, ck_ref, cv_ref, clr_ref,
                wa_ref, ba_ref, gn_ref, o_ref, st_scr, of_scr, qe_scr, kd_scr, ckd_scr, dec_scr, cdec_scr,
                qk_a):
    c_len = GLA_CHUNK
    grp = GLA_GROUP
    cpg = grp // c_len
    row = lax.broadcasted_iota(jnp.int32, (grp, grp), 0)
    col = lax.broadcasted_iota(jnp.int32, (grp, grp), 1)
    same_chunk = (row // c_len) == (col // c_len)
    masks = (same_chunk & (col <= row), same_chunk & (col >= row))

    dirs = range(2)

    def gates(lr_rows, k_rows):
        zs = [jnp.dot(lr_rows, wa_ref[t], preferred_element_type=F32) + ba_ref[t:t + 1, :] for t in dirs]
        gs = [(jnp.minimum(z, 0.0) * LOG2E - jnp.log2(1.0 + jnp.exp2(jnp.abs(z) * (-LOG2E)))) * (1.0 / GLA_GATE_NORM)
              for z in zs]
        g1s = [g.astype(BF16) for g in gs]
        g2s = [(g - g1.astype(F32)).astype(BF16) for g, g1 in zip(gs, g1s)]
        tris = [masks[t].astype(BF16) for t in dirs]
        bs = [jnp.dot(tris[t], g1s[t], preferred_element_type=F32)
              + jnp.dot(tris[t], g2s[t], preferred_element_type=F32) for t in dirs]
        out = []
        for t in dirs:
            edge = c_len - 1 if t == 0 else 0
            decs = [jnp.exp2(bs[t][j * c_len + edge:j * c_len + edge + 1, :]) for j in range(cpg)]
            ke = k_rows * jnp.exp2(-bs[t])
            kd = (ke * jnp.concatenate([jnp.broadcast_to(v, (c_len, v.shape[1])) for v in decs],
                                       axis=0)).astype(BF16)
            dec = jnp.concatenate(decs + [jnp.ones_like(v) for v in decs], axis=0)
            out.append((bs[t], ke.astype(BF16), kd, dec))
        return out

    for t, (_, _, kd, dec) in enumerate(gates(clr_ref[0], ck_ref[0].astype(F32))):
        ckd_scr[t] = kd
        cdec_scr[t] = dec

    def group_gates(gi, qk_buf):
        rows = pl.ds(pl.multiple_of(gi * grp, grp), grp)
        qf = gq_ref[0, rows, :].astype(F32)
        for t, (b, ke, kd, dec) in enumerate(gates(lr_ref[0, rows, :], gk_ref[0, rows, :].astype(F32))):
            qe = (qf * jnp.exp2(b)).astype(BF16)
            qk_buf[t] = qe
            qk_buf[2 + t] = ke
            qe_scr[t, rows, :] = qe
            kd_scr[t, rows, :] = kd
            dec_scr[t, gi] = dec

    def group_attend(gi, qk_buf):
        rows = pl.ds(pl.multiple_of(gi * grp, grp), grp)
        heads = range(GLA_HEADS)
        kls = [slice(h * GLA_HK, (h + 1) * GLA_HK) for h in heads]
        raw = [[lax.dot_general(qk_buf[t, :, kls[h]], qk_buf[2 + t, :, kls[h]], _NT, preferred_element_type=F32)
                for t in dirs] for h in heads]
        atts = [(jnp.where(masks[0], raw[h][0], 0.0) + jnp.where(masks[1], raw[h][1], 0.0)).astype(BF16)
                for h in heads]
        for h in heads:
            vl = slice(h * GLA_HV, (h + 1) * GLA_HV)
            of_scr[rows, vl] = jnp.dot(atts[h], gv_ref[0, rows, vl], preferred_element_type=F32)

    def state_step(t, h, v_rows, kd_rows, dec_row):
        kl = slice(h * GLA_HK, (h + 1) * GLA_HK)
        upd = lax.dot_general(v_rows, kd_rows[:, kl], _TN, preferred_element_type=F32)
        st_scr[t, h] = st_scr[t, h] * dec_row[:, kl] + upd

    def chunk_order(t):
        return range(cpg) if t == 0 else range(cpg - 1, -1, -1)

    st_scr[...] = jnp.zeros_like(st_scr)
    for t in dirs:
        for j in chunk_order(t):
            rows = slice(j * c_len, (j + 1) * c_len)
            for h in range(GLA_HEADS):
                state_step(t, h, cv_ref[0, rows, h * GLA_HV:(h + 1) * GLA_HV],
                           ckd_scr[t, rows, :], cdec_scr[t, j:j + 1, :])

    def chunk_step(t, gi, j, finish):
        rows = pl.ds(pl.multiple_of(gi * grp + j * c_len, c_len), c_len)
        qe_rows = qe_scr[t, rows, :]
        kd_rows = kd_scr[t, rows, :]
        dec_row = dec_scr[t, gi, j:j + 1, :]
        for h in range(GLA_HEADS):
            kl = slice(h * GLA_HK, (h + 1) * GLA_HK)
            vl = slice(h * GLA_HV, (h + 1) * GLA_HV)
            o_t = of_scr[rows, vl] + lax.dot_general(qe_rows[:, kl], st_scr[t, h].astype(BF16), _NT,
                                                     preferred_element_type=F32)
            if finish:
                ms = jnp.mean(o_t * o_t, axis=-1, keepdims=True)
                o_n = o_t * lax.rsqrt(ms + EPS) * gn_ref[:, vl] * sog_ref[0, rows, vl].astype(F32)
                o_ref[0, rows, vl] = o_n.astype(o_ref.dtype)
            else:
                of_scr[rows, vl] = o_t
            state_step(t, h, gv_ref[0, rows, vl], kd_rows, dec_row)

    def scan_steps(i, finish):
        steps = []
        for jf, jb in zip(chunk_order(0), chunk_order(1)):
            steps.append(functools.partial(chunk_step, 0, i, jf, finish))
            steps.append(functools.partial(chunk_step, 1, n_groups - 1 - i, jb, finish))
        return steps

    half = n_groups // 2
    for g in (0, n_groups - 1):
        group_gates(g, qk_a)
        group_attend(g, qk_a)

    def prepare_and_scan(i, carry):
        steps = scan_steps(i, False)
        per_phase = len(steps) // 4
        for n_phase, phase in enumerate((functools.partial(group_gates, i + 1, qk_a),
                                         functools.partial(group_attend, i + 1, qk_a),
                                         functools.partial(group_gates, n_groups - 2 - i, qk_a),
                                         functools.partial(group_attend, n_groups - 2 - i, qk_a))):
            phase()
            for step in steps[n_phase * per_phase:(n_phase + 1) * per_phase]:
                step()
        return carry

    def scan_only(finish):
        def body(i, carry):
            for step in scan_steps(i, finish):
                step()
            return carry
        return body

    lax.fori_loop(0, half - 1, prepare_and_scan, 0)
    lax.fori_loop(half - 1, half, scan_only(False), 0)
    lax.fori_loop(half, n_groups, scan_only(True), 0)


def _gla(gq, gk, gv, lr, sog, ck, cv, clr, wa, ba, gn):
    b, s, _ = gq.shape
    ctx_len = ck.shape[1]
    assert ctx_len == GLA_GROUP and s % (2 * GLA_GROUP) == 0
    n_groups = s // GLA_GROUP
    cpg = GLA_GROUP // GLA_CHUNK

    def per_batch(t, w):
        return pl.BlockSpec((1, t, w), lambda bi: (bi, 0, 0))

    return pl.pallas_call(
        functools.partial(_gla_kernel, n_groups),
        grid=(b,),
        in_specs=[per_batch(s, GLA_DK), per_batch(s, GLA_DK), per_batch(s, GLA_DV),
                  per_batch(s, LR_PAD), per_batch(s, GLA_DV),
                  per_batch(ctx_len, GLA_DK), per_batch(ctx_len, GLA_DV), per_batch(ctx_len, LR_PAD),
                  pl.BlockSpec(wa.shape, lambda bi: (0, 0, 0)),
                  pl.BlockSpec(ba.shape, lambda bi: (0, 0)),
                  pl.BlockSpec(gn.shape, lambda bi: (0, 0))],
        out_specs=per_batch(s, GLA_DV),
        out_shape=jax.ShapeDtypeStruct((b, s, GLA_DV), BF16),
        scratch_shapes=[pltpu.VMEM((2, GLA_HEADS, GLA_HV, GLA_HK), F32),
                        pltpu.VMEM((s, GLA_DV), F32),
                        pltpu.VMEM((2, s, GLA_DK), BF16),
                        pltpu.VMEM((2, s, GLA_DK), BF16),
                        pltpu.VMEM((2, ctx_len, GLA_DK), BF16),
                        pltpu.VMEM((2, n_groups, 2 * cpg, GLA_DK), F32),
                        pltpu.VMEM((2, 2 * cpg, GLA_DK), F32),
                        pltpu.VMEM((4, GLA_GROUP, GLA_DK), BF16)],
        compiler_params=_params(("arbitrary",), vmem=60 * 1024 * 1024),
        name="gla",
    )(gq, gk, gv, lr, sog, ck, cv, clr, wa, ba, gn)


def _merge_kernel(ona_ref, ogla_ref, sgn_ref, sgg_ref, x_ref, mod_ref, gffn_ref,
                  wna_ref, wgla_ref, wout_ref, wr_ref, br_ref,
                  h_ref, f_ref, route_ref, route_t_ref, cnt_ref):
    rows = x_ref.shape[0] // MERGE_TILES

    def tile_rows(u):
        return slice(u * rows, (u + 1) * rows)

    def branches(u):
        rs = tile_rows(u)
        return (sgn_ref[rs, :].astype(F32) * jnp.dot(ona_ref[rs, :], wna_ref[...], preferred_element_type=F32)
                + sgg_ref[rs, :].astype(F32) * jnp.dot(ogla_ref[rs, :], wgla_ref[...], preferred_element_type=F32))

    def project(y1):
        return jnp.dot(y1.astype(BF16), wout_ref[...], preferred_element_type=F32)

    def residual(u, y):
        rs = tile_rows(u)
        h = x_ref[rs, :] + mod_ref[0, 2:3, :] * y
        h_ref[rs, :] = h
        ms = jnp.mean(h * h, axis=-1, keepdims=True)
        f = h * lax.rsqrt(ms + EPS) * gffn_ref[...] * (1.0 + mod_ref[0, 4:5, :]) + mod_ref[0, 3:4, :]
        f_ref[rs, :] = f.astype(f_ref.dtype)
        return f

    def router(f):
        f_hi = f.astype(BF16)
        f_lo = (f - f_hi.astype(F32)).astype(BF16)
        both = jnp.dot(f_hi, wr_ref[...], preferred_element_type=F32)
        return (both[:, :ROUTE_LANES] + both[:, ROUTE_LANES:]
                + jnp.dot(f_lo, wr_ref[:, :ROUTE_LANES], preferred_element_type=F32)) + br_ref[...]

    y_a = project(branches(0))
    y1_b = branches(1)
    f_a = residual(0, y_a)
    y_b = project(y1_b)
    logits_a = router(f_a)
    f_b = residual(1, y_b)
    _route_tile(0, tile_rows(0), logits_a, route_ref, route_t_ref, cnt_ref)
    _route_tile(1, tile_rows(1), router(f_b), route_ref, route_t_ref, cnt_ref)


def _route_tile(u, rs, logits, route_ref, route_t_ref, cnt_ref):
    rows = logits.shape[0]
    lt = logits.T[0:ROUTE_ROWS, :]
    er = lax.broadcasted_iota(jnp.int32, (ROUTE_ROWS, rows), 0)
    big = jnp.int32(ROUTE_ROWS)

    def first_argmax(vals):
        m = jnp.max(vals, axis=0, keepdims=True)
        idx = jnp.min(jnp.where(vals == m, er, big), axis=0, keepdims=True)
        return m, idx

    is_grp = er < N_GROUPS
    lg = jnp.where(is_grp, lt, -jnp.inf)
    mg, grp = first_argmax(lg)
    p_grp = 1.0 / jnp.sum(jnp.where(is_grp, jnp.exp(lg - mg), 0.0), axis=0, keepdims=True)
    lo = EXPERT_LANE0 + grp * EXPERTS_PER_GROUP
    in_grp = (er >= lo) & (er < lo + EXPERTS_PER_GROUP)
    le = jnp.where(in_grp, lt, -jnp.inf)
    m1, i1 = first_argmax(le)
    m2, i2 = first_argmax(jnp.where(er == i1, -jnp.inf, le))
    t = jnp.exp(m2 - m1)
    w1 = p_grp / (1.0 + t)
    w2 = p_grp * t / (1.0 + t)

    hot1 = er == i1
    hot2 = er == i2
    onehot = (hot1 | hot2).astype(BF16)
    s_i = lax.broadcasted_iota(jnp.int32, (rows, rows), 0)
    t_i = lax.broadcasted_iota(jnp.int32, (rows, rows), 1)
    rank = jnp.dot(onehot, (s_i < t_i).astype(BF16), preferred_element_type=F32)
    cnt = jnp.sum(onehot.astype(F32), axis=1, keepdims=True)
    cnt_al = jnp.floor((cnt + (SEG_ALIGN - 1.0)) * (1.0 / SEG_ALIGN)) * SEG_ALIGN
    e_i = lax.broadcasted_iota(jnp.int32, (ROUTE_ROWS, ROUTE_ROWS), 0)
    e_j = lax.broadcasted_iota(jnp.int32, (ROUTE_ROWS, ROUTE_ROWS), 1)
    lstart = jnp.dot((e_j < e_i).astype(BF16), jnp.broadcast_to(cnt_al, (ROUTE_ROWS, LANES)).astype(BF16),
                     preferred_element_type=F32)[:, 0:1]
    pos = rank + lstart
    pos1 = jnp.sum(jnp.where(hot1, pos, 0.0), axis=0, keepdims=True)
    pos2 = jnp.sum(jnp.where(hot2, pos, 0.0), axis=0, keepdims=True)

    e1 = (i1 - EXPERT_LANE0).astype(F32)
    e2 = (i2 - EXPERT_LANE0).astype(F32)
    fields = jnp.concatenate([e1, e2, w1, w2, pos1, pos2, jnp.zeros((2, rows), F32)], axis=0)
    route_t_ref[u] = fields
    route_ref[rs, :] = jnp.concatenate([fields, jnp.zeros((ROUTE_LANES - 8, rows), F32)], axis=0).T
    cnt_rows = jnp.concatenate([jnp.broadcast_to(cnt, (ROUTE_ROWS, ROUTE_LANES)),
                                jnp.zeros((ROUTE_LANES - ROUTE_ROWS, ROUTE_LANES), F32)], axis=0)
    cnt_ref[u] = cnt_rows.T[0:1, :]


def _merge_route(ona, ogla, sgn, sgg, x2d, mod3, gffn, wna, wgla, wout, wr, br, tokens_per_batch, tm):
    n, d = x2d.shape
    step_rows = MERGE_TILES * tm
    assert tokens_per_batch % step_rows == 0
    per_b = tokens_per_batch // step_rows

    def tile(w):
        return pl.BlockSpec((step_rows, w), lambda i: (i, 0))

    def whole(a):
        return pl.BlockSpec(a.shape, lambda i: (0,) * a.ndim)

    return pl.pallas_call(
        _merge_kernel,
        grid=(n // step_rows,),
        in_specs=[tile(NA_WIDTH), tile(GLA_DV), tile(d), tile(d), tile(d),
                  pl.BlockSpec((1, 6, d), lambda i: (i // per_b, 0, 0)),
                  whole(gffn), whole(wna), whole(wgla), whole(wout), whole(wr), whole(br)],
        out_specs=[tile(d), tile(d), tile(ROUTE_LANES),
                   pl.BlockSpec((MERGE_TILES, 8, tm), lambda i: (i, 0, 0)),
                   pl.BlockSpec((MERGE_TILES, 1, ROUTE_LANES), lambda i: (i, 0, 0))],
        out_shape=[jax.ShapeDtypeStruct((n, d), F32),
                   jax.ShapeDtypeStruct((n, d), BF16),
                   jax.ShapeDtypeStruct((n, ROUTE_LANES), F32),
                   jax.ShapeDtypeStruct((n // tm, 8, tm), F32),
                   jax.ShapeDtypeStruct((n // tm, 1, ROUTE_LANES), F32)],
        compiler_params=_params(("arbitrary",)),
        name="merge_route",
    )(ona, ogla, sgn, sgg, x2d, mod3, gffn, wna, wgla, wout, wr, br)


HI_MASK = -65536


def _pack_rows(a):
    half = a.shape[1] // 2
    hi = lax.bitcast_convert_type(a[:, :half], jnp.int32)
    lo = lax.shift_right_logical(lax.bitcast_convert_type(a[:, half:], jnp.int32), 16)
    return hi | lo


def _unpack_rows(w):
    hi = lax.bitcast_convert_type(w & HI_MASK, F32).astype(BF16)
    lo = lax.bitcast_convert_type(lax.shift_left(w, 16), F32).astype(BF16)
    return hi, lo


def _piece_table(cnt_al, lstart, seg_dst):
    n_tiles = cnt_al.shape[0]

    def compact(mask, local, glob, slots):
        pos = jnp.cumsum(mask.astype(jnp.int32), axis=1) - 1
        hot = mask[:, :, None] & (pos[:, :, None] == jnp.arange(slots, dtype=jnp.int32))
        pick = lambda v: jnp.sum(jnp.where(hot, v[:, :, None], 0), axis=1)
        return jnp.sum(mask, axis=1).astype(jnp.int32), pick(local), pick(glob)

    counts, lists = [], []
    big = PIECE_SIZES[-1]
    for size, slots in zip(PIECE_SIZES[:-1], PIECE_SLOTS[:-1]):
        n, lo, gl = compact((cnt_al & (big - 1)) == size, lstart, seg_dst, slots)
        counts.append(n)
        lists += [lo, gl]
    k = jnp.arange(MAX_BIG_PER_RUN, dtype=jnp.int32)
    off = (cnt_al & (big - 1))[:, :, None] + big * k
    mask = k < (cnt_al // big)[:, :, None]
    flat = lambda a: a.reshape(n_tiles, -1)
    n, lo, gl = compact(flat(mask), flat(lstart[:, :, None] + off), flat(seg_dst[:, :, None] + off), PIECE_SLOTS[-1])
    counts.append(n)
    lists += [lo, gl]
    head = jnp.stack(counts + [jnp.sum(cnt_al, axis=1)], axis=1)
    head = jnp.pad(head, ((0, 0), (0, PIECE_HEAD - head.shape[1])))
    table = jnp.concatenate([head] + lists, axis=1)
    table = jnp.pad(table, ((0, 0), (0, PIECE_TABLE - table.shape[1])))
    return table.reshape(n_tiles, 1, PIECE_TABLE).astype(jnp.int32)


def _segment_copies(seg_ref, make_copy, action, tile=0):
    if action == "wait":
        total = seg_ref[tile, 0, len(PIECE_SIZES)]
        for size in WAIT_SIZES:
            @pl.when((total & size) != 0)
            def _():
                make_copy(0, 0, size).wait()
        return

    base = PIECE_HEAD
    for c, (size, slots) in enumerate(zip(PIECE_SIZES, PIECE_SLOTS)):
        def piece(k, carry, size=size, lo=base, gl=base + slots):
            make_copy(pl.multiple_of(seg_ref[tile, 0, lo + k], SEG_ALIGN),
                      pl.multiple_of(seg_ref[tile, 0, gl + k], SEG_ALIGN), size).start()
            return carry

        lax.fori_loop(0, seg_ref[tile, 0, c], piece, 0)
        base += 2 * slots


def _dispatch_kernel(seg_ref, seg_prev_ref, tail_ref, lpos_ref, f_ref, xs_ref, sbuf, zbuf, sems):
    i = pl.program_id(0)
    last = pl.num_programs(0) - 1
    tm = f_ref.shape[0]
    rows_l = sbuf.shape[1]
    slot = i % 2
    fields = lpos_ref[0]
    lp = fields.astype(jnp.int32)
    half = f_ref.shape[1] // 2
    row = lax.broadcasted_iota(jnp.int32, (rows_l, tm), 0)
    first = row == lp[4:5, :]
    second = row == lp[5:6, :]
    sbuf[slot, :, :half] = _pack_rows(jnp.dot((first | second).astype(BF16), f_ref[...], preferred_element_type=F32))
    w_sorted = jnp.sum(jnp.where(first, fields[2:3, :], 0.0) + jnp.where(second, fields[3:4, :], 0.0),
                       axis=1, keepdims=True)
    sbuf[slot, :, half:] = lax.bitcast_convert_type(jnp.broadcast_to(w_sorted, (rows_l, LANES)), jnp.int32)

    def copier(s):
        def copy(local, glob, size):
            return pltpu.make_async_copy(sbuf.at[s, pl.ds(local, size), :], xs_ref.at[pl.ds(glob, size), :],
                                         sems.at[s])
        return copy

    _segment_copies(seg_ref, copier(slot), "start")

    @pl.when(i > 0)
    def _():
        _segment_copies(seg_prev_ref, copier(1 - slot), "wait")

    @pl.when(i == last)
    def _():
        _segment_copies(seg_ref, copier(slot), "wait")
        sem = sems.at[slot]
        zbuf[...] = jnp.zeros_like(zbuf)

        def tail(action):
            def body(e, carry):
                start = tail_ref[0, 0, e]
                length = tail_ref[0, 0, N_EXPERTS + e]
                off = jnp.int32(0)
                for size in TAIL_SIZES:
                    @pl.when((length & size) != 0)
                    def _():
                        cp = pltpu.make_async_copy(zbuf.at[pl.ds(0, size), :],
                                                   xs_ref.at[pl.ds(pl.multiple_of(start + off, SEG_ALIGN), size), :],
                                                   sem)
                        getattr(cp, action)()
                    off = off + (length & size)
                return carry
            lax.fori_loop(0, N_EXPERTS, body, 0)

        tail("start")
        tail("wait")

        def unused(action):
            def body(blk, carry):
                cp = pltpu.make_async_copy(zbuf, xs_ref.at[pl.ds(pl.multiple_of(blk * MOE_ROWS, MOE_ROWS),
                                                                  MOE_ROWS), :], sem)
                getattr(cp, action)()
                return carry
            lax.fori_loop(tail_ref[0, 0, 2 * N_EXPERTS], xs_ref.shape[0] // MOE_ROWS, body, 0)

        unused("start")
        unused("wait")


def _dispatch(seg, tail, lpos_rows, f2d, n_pad, rows_l, tm):
    n, d = f2d.shape
    return pl.pallas_call(
        _dispatch_kernel,
        grid=(n // tm,),
        in_specs=[pl.BlockSpec((1, 1, seg.shape[2]), lambda i: (i, 0, 0), memory_space=pltpu.SMEM),
                  pl.BlockSpec((1, 1, seg.shape[2]), lambda i: (jnp.maximum(i - 1, 0), 0, 0),
                               memory_space=pltpu.SMEM),
                  pl.BlockSpec(tail.shape, lambda i: (0, 0, 0), memory_space=pltpu.SMEM),
                  pl.BlockSpec((1, 8, tm), lambda i: (i, 0, 0)),
                  pl.BlockSpec((tm, d), lambda i: (i, 0))],
        out_specs=pl.BlockSpec(memory_space=pl.ANY),
        out_shape=jax.ShapeDtypeStruct((n_pad, d // 2 + LANES), jnp.int32),
        scratch_shapes=[pltpu.VMEM((2, rows_l, d // 2 + LANES), jnp.int32),
                        pltpu.VMEM((MOE_ROWS, d // 2 + LANES), jnp.int32),
                        pltpu.SemaphoreType.DMA((2,))],
        compiler_params=_params(("arbitrary",)),
        name="dispatch",
    )(seg, seg, tail, lpos_rows, f2d)


def _expert_kernel(be_ref, nu_ref, x_ref, wg_ref, wu_ref, wd_ref, y_ref, wgu_scr, wd_scr):
    i = pl.program_id(0)
    prev = be_ref[jnp.maximum(i - 1, 0)]

    @pl.when((i == 0) | (be_ref[i] != prev))
    def _():
        wgu_scr[:, :D_EXPERT] = wg_ref[0].astype(BF16)
        wgu_scr[:, D_EXPERT:] = wu_ref[0].astype(BF16)
        wd_scr[...] = wd_ref[0].astype(BF16)

    @pl.when(i < nu_ref[0])
    def _():
        half = y_ref.shape[1]
        x = jnp.concatenate(_unpack_rows(x_ref[:, :half]), axis=1)
        gu = jnp.dot(x, wgu_scr[...], preferred_element_type=F32)
        gate = gu[:, :D_EXPERT]
        hdn = gate * jax.nn.sigmoid(gate) * gu[:, D_EXPERT:]
        y = jnp.dot(hdn.astype(BF16), wd_scr[...], preferred_element_type=F32)
        y = y * lax.bitcast_convert_type(x_ref[:, half:half + 1], F32)
        y_ref[...] = _pack_rows(y.astype(BF16).astype(F32))

    @pl.when(i >= nu_ref[0])
    def _():
        y_ref[...] = jnp.zeros_like(y_ref)


def _experts(blk_expert, n_used, xs, wg, wu, wd):
    n_pad, row_words = xs.shape
    half = row_words - LANES
    d = 2 * half
    n_blk = n_pad // MOE_ROWS

    def used_block(i, be, nu):
        return (jnp.maximum(jnp.minimum(i, nu[0] - 1), 0), 0)

    grid_spec = pltpu.PrefetchScalarGridSpec(
        num_scalar_prefetch=2,
        grid=(n_blk,),
        in_specs=[pl.BlockSpec((MOE_ROWS, row_words), used_block),
                  pl.BlockSpec((1, d, D_EXPERT), lambda i, be, nu: (be[i], 0, 0)),
                  pl.BlockSpec((1, d, D_EXPERT), lambda i, be, nu: (be[i], 0, 0)),
                  pl.BlockSpec((1, D_EXPERT, d), lambda i, be, nu: (be[i], 0, 0))],
        out_specs=pl.BlockSpec((MOE_ROWS, half), lambda i, be, nu: (i, 0)),
        scratch_shapes=[pltpu.VMEM((d, 2 * D_EXPERT), BF16),
                        pltpu.VMEM((D_EXPERT, d), BF16)],
    )
    return pl.pallas_call(
        _expert_kernel,
        grid_spec=grid_spec,
        out_shape=jax.ShapeDtypeStruct((n_pad, half), jnp.int32),
        compiler_params=_params(("arbitrary",)),
        name="experts",
    )(blk_expert, n_used, xs, wg, wu, wd)


def _combine_kernel(seg_ref, seg_next_ref, col_ref, h_ref, mod_ref, fg_ref, ys_ref, o_ref, ybuf, sems):
    i = pl.program_id(0)
    tm = h_ref.shape[0] // COMBINE_TILES
    rows_l = ybuf.shape[1]
    slot = i % 2
    tiles = range(COMBINE_TILES)

    def copier(s, u):
        def copy(local, glob, size):
            return pltpu.make_async_copy(ys_ref.at[pl.ds(glob, size), :],
                                         ybuf.at[COMBINE_TILES * s + u, pl.ds(local, size), :],
                                         sems.at[COMBINE_TILES * s + u])
        return copy

    @pl.when(i == 0)
    def _():
        ybuf[...] = jnp.zeros_like(ybuf)
        for u in tiles:
            _segment_copies(seg_ref, copier(slot, u), "start", u)

    @pl.when(i + 1 < pl.num_programs(0))
    def _():
        for u in tiles:
            _segment_copies(seg_next_ref, copier(1 - slot, u), "start", u)

    def gather(u):
        _segment_copies(seg_ref, copier(slot, u), "wait", u)
        y_hi, y_lo = _unpack_rows(ybuf[COMBINE_TILES * slot + u])
        y_all = jnp.concatenate([y_hi, y_lo], axis=1)
        col = lax.broadcasted_iota(jnp.int32, (tm, rows_l), 1)
        info = col_ref[u * tm:(u + 1) * tm, :]
        pick = (col == info[:, 4:5].astype(jnp.int32)) | (col == info[:, 5:6].astype(jnp.int32))
        return jnp.dot(pick.astype(BF16), y_all, preferred_element_type=F32)

    def finish(u, moe):
        rs = slice(u * tm, (u + 1) * tm)
        h = h_ref[rs, :] + mod_ref[0, 5:6, :] * moe
        ms = jnp.mean(h * h, axis=-1, keepdims=True)
        o_ref[rs, :] = h * lax.rsqrt(ms + EPS) * fg_ref[...]

    moe_a = gather(0)
    moe_b = gather(1)
    finish(0, moe_a)
    finish(1, moe_b)


def _combine(seg, colinfo, h2d, mod3, fg, ys, rows_l, tokens_per_batch, tm):
    n, d = h2d.shape
    step_rows = COMBINE_TILES * tm
    assert tokens_per_batch % step_rows == 0
    per_b = tokens_per_batch // step_rows
    n_steps = n // step_rows
    return pl.pallas_call(
        _combine_kernel,
        grid=(n_steps,),
        in_specs=[pl.BlockSpec((COMBINE_TILES, 1, seg.shape[2]), lambda i: (i, 0, 0), memory_space=pltpu.SMEM),
                  pl.BlockSpec((COMBINE_TILES, 1, seg.shape[2]), lambda i: (jnp.minimum(i + 1, n_steps - 1), 0, 0),
                               memory_space=pltpu.SMEM),
                  pl.BlockSpec((step_rows, colinfo.shape[1]), lambda i: (i, 0)),
                  pl.BlockSpec((step_rows, d), lambda i: (i, 0)),
                  pl.BlockSpec((1, 6, d), lambda i: (i // per_b, 0, 0)),
                  pl.BlockSpec((1, d), lambda i: (0, 0)),
                  pl.BlockSpec(memory_space=pl.ANY)],
        out_specs=pl.BlockSpec((step_rows, d), lambda i: (i, 0)),
        out_shape=jax.ShapeDtypeStruct((n, d), F32),
        scratch_shapes=[pltpu.VMEM((2 * COMBINE_TILES, rows_l, d // 2), jnp.int32),
                        pltpu.SemaphoreType.DMA((2 * COMBINE_TILES,))],
        compiler_params=_params(("arbitrary",)),
        name="combine",
    )(seg, seg, colinfo, h2d, mod3, fg, ys)


def _rope_tables(s):
    t = np.arange(s)
    pos_r = (t // GRID_W).astype(np.float32)
    pos_c = (t % GRID_W).astype(np.float32)
    nf = GLA_HK // 4
    inv = (np.float32(ROPE_BASE) ** (-np.arange(nf, dtype=np.float32) / np.float32(nf))).astype(np.float32)
    ang_r = pos_r[:, None] * inv
    ang_c = pos_c[:, None] * inv

    def half(ang):
        return (np.concatenate([np.cos(ang), np.cos(ang)], axis=-1),
                np.concatenate([-np.sin(ang), np.sin(ang)], axis=-1))

    cr, sr = half(ang_r)
    cc, sc = half(ang_c)
    cos = np.tile(np.concatenate([cr, cc], axis=-1), (1, GLA_HEADS)).astype(np.float32)
    sin = np.tile(np.concatenate([sr, sc], axis=-1), (1, GLA_HEADS)).astype(np.float32)
    return jnp.asarray(cos), jnp.asarray(sin)


def kernel(x, c, ctx, c_ctx, w_mod, b_mod, norm_attn_g, norm_ffn_g, w_in, w_gla_a2, b_gla_a2, gla_norm_g, na_rpb, w_na_o, w_gla_o, w_out, w_group, b_group, w_expert, b_expert, w_exp_gate, w_exp_up, w_exp_down, final_norm_g):
    b, s, d = x.shape
    ctx_len = ctx.shape[1]
    n = b * s
    assert w_mod.shape[0] == 1, "single-layer block"
    assert s % (GRID_W * NA_WIN_H) == 0 and ctx_len % GLA_CHUNK == 0

    mod_rows = -(-(b + 1) // 8) * 8
    cs = jnp.zeros((mod_rows, d), F32).at[:b].set(c).at[b].set(c_ctx)
    mod3 = _modulation(cs, w_mod[0], b_mod[0]).reshape(mod_rows, 6, d)

    o_q, o_k, o_v = 0, NA_WIDTH, 2 * NA_WIDTH
    o_gq = 3 * NA_WIDTH
    o_gk = o_gq + GLA_DK
    o_gv = o_gk + GLA_DK
    o_og = o_gv + GLA_DV
    o_lr = o_og + GLA_DV
    o_mn = o_lr + 2 * GLA_GATE_RANK
    w_bf = w_in[0].astype(BF16)
    w_a = (w_bf, o_lr)
    w_g = (w_bf[:, o_mn:], 2 * d)
    w_lr = (jnp.pad(w_bf[:, o_lr:o_mn], ((0, 0), (0, LR_PAD - 2 * GLA_GATE_RANK))), LR_PAD)
    cw = 512
    lat_specs = [(0, 0, 0, o_q, cw, "plain"), (1, 0, 0, o_k, cw, "plain"), (2, 0, 0, o_v, cw, "plain"),
                 (3, 0, 0, o_gq, cw, "rope_scaled"), (4, 0, 0, o_gk, cw, "rope"),
                 (5, 0, 0, o_gv, cw, "plain"), (5, cw, 0, o_gv + cw, cw, "plain"),
                 (6, 0, 0, o_og, cw, "silu"), (6, cw, 0, o_og + cw, cw, "silu"),
                 (7, 0, 1, 0, cw, "sigmoid"), (7, cw, 1, cw, cw, "sigmoid"),
                 (8, 0, 1, d, cw, "sigmoid"), (8, cw, 1, d + cw, cw, "sigmoid"),
                 (9, 0, 2, 0, LR_PAD, "plain")]
    lat_widths = [NA_WIDTH, NA_WIDTH, NA_WIDTH, GLA_DK, GLA_DK, GLA_DV, GLA_DV, d, d, LR_PAD]
    tm = 512
    per_b = s // tm
    q_na, k_na, v_na, gq, gk, gv, sog, sgn, sgg, lr = _inproj(
        x.reshape(n, d), mod3, lambda i: i // per_b, norm_attn_g[0], [w_a, w_g, w_lr], lat_specs, lat_widths, tm,
        rope=_rope_tables(s), name="inproj_lat")

    ctx_specs = [(0, 0, 0, o_k, cw, "plain"), (1, 0, 0, o_v, cw, "plain"), (2, 0, 0, o_gk, cw, "plain"),
                 (3, 0, 0, o_gv, cw, "plain"), (3, cw, 0, o_gv + cw, cw, "plain"),
                 (4, 0, 1, 0, LR_PAD, "plain")]
    ctx_widths = [NA_WIDTH, NA_WIDTH, GLA_DK, GLA_DV, LR_PAD]
    kc_na, vc_na, ck, cv, clr = _inproj(
        ctx.reshape(b * ctx_len, d), mod3, lambda i: b, norm_attn_g[0], [w_a, w_lr], ctx_specs, ctx_widths,
        tm if (b * ctx_len) % tm == 0 else ctx_len, name="inproj_ctx")

    def lat3(a):
        return a.reshape(b, s, a.shape[-1])

    def ctx3(a):
        return a.reshape(b, ctx_len, a.shape[-1])

    o_na = _na_attention(lat3(q_na), lat3(k_na), lat3(v_na), ctx3(kc_na), ctx3(vc_na),
                         _na_bias_table(na_rpb[0]))

    wa = jnp.zeros((2, LR_PAD, GLA_DK), F32)
    wa = wa.at[0, :GLA_GATE_RANK].set(w_gla_a2[0, 0]).at[1, GLA_GATE_RANK:2 * GLA_GATE_RANK].set(w_gla_a2[0, 1])
    gn = jnp.tile(gla_norm_g[0], GLA_HEADS).reshape(1, GLA_DV)
    o_gla = _gla(lat3(gq), lat3(gk), lat3(gv), lat3(lr), lat3(sog), ctx3(ck), ctx3(cv), ctx3(clr),
                 wa.astype(BF16), b_gla_a2[0], gn)

    wr = jnp.zeros((d, ROUTE_LANES), F32)
    wr = wr.at[:, :N_GROUPS].set(w_group[0]).at[:, EXPERT_LANE0:EXPERT_LANE0 + N_EXPERTS].set(w_expert[0])
    wr_hi = wr.astype(BF16)
    br = jnp.zeros((1, ROUTE_LANES), F32)
    br = br.at[0, :N_GROUPS].set(b_group[0]).at[0, EXPERT_LANE0:EXPERT_LANE0 + N_EXPERTS].set(b_expert[0])
    h, f_lat, route, route_t, counts = _merge_route(
        o_na.reshape(n, NA_WIDTH), o_gla.reshape(n, GLA_DV), sgn, sgg, x.reshape(n, d), mod3, norm_ffn_g[0].reshape(1, d),
        w_na_o[0].astype(BF16), w_gla_o[0].astype(BF16), w_out[0].astype(BF16),
        jnp.concatenate([wr_hi, (wr - wr_hi.astype(F32)).astype(BF16)], axis=1), br, s, tm)

    n_tiles = n // tm
    cnt = counts[:, 0, EXPERT_LANE0:EXPERT_LANE0 + N_EXPERTS].astype(jnp.int32)
    cnt_al = (cnt + SEG_ALIGN - 1) // SEG_ALIGN * SEG_ALIGN
    lstart = jnp.cumsum(cnt_al, axis=1) - cnt_al
    total = jnp.sum(cnt_al, axis=0)
    padded = (total + MOE_ROWS - 1) // MOE_ROWS * MOE_ROWS
    pend = jnp.cumsum(padded)
    pstart = pend - padded
    seg_dst = pstart[None, :] + jnp.cumsum(cnt_al, axis=0) - cnt_al
    assert tm // PIECE_SIZES[-1] <= MAX_BIG_PER_RUN
    assert (2 * tm + N_EXPERTS * (SEG_ALIGN - 1)) // PIECE_SIZES[-1] <= PIECE_SLOTS[-1]
    seg = _piece_table(cnt_al, lstart, seg_dst)
    n_used = (pend[-1] // MOE_ROWS).reshape(1)
    tail = jnp.concatenate([pstart + total, padded - total, n_used, jnp.zeros((2 * N_EXPERTS - 1,), jnp.int32)])
    tail = tail.reshape(1, 1, 4 * N_EXPERTS)
    seg_pad = n_tiles * N_EXPERTS * (SEG_ALIGN - 1)
    n_pad = (2 * n + seg_pad + N_EXPERTS * (MOE_ROWS - SEG_ALIGN) + MOE_ROWS - 1) // MOE_ROWS * MOE_ROWS
    n_blk = n_pad // MOE_ROWS
    rows_l = (2 * tm + N_EXPERTS * (SEG_ALIGN - 1) + MXU_DIM - 1) // MXU_DIM * MXU_DIM
    blk_start = jnp.arange(n_blk, dtype=jnp.int32) * MOE_ROWS
    blk_expert = jnp.minimum(jnp.sum(blk_start[:, None] >= pend[None, :], axis=-1), N_EXPERTS - 1)

    xs = _dispatch(seg, tail, route_t, f_lat, n_pad, rows_l, tm)
    ys = _experts(blk_expert.astype(jnp.int32), n_used.astype(jnp.int32), xs,
                  w_exp_gate[0], w_exp_up[0], w_exp_down[0])
    out = _combine(seg, route, h, mod3, final_norm_g.reshape(1, d), ys, rows_l, s, tm)
    return out.reshape(b, s, d)
```

```python
import functools

import jax
import jax.numpy as jnp
import numpy as np
from jax import lax
from jax.experimental import pallas as pl
from jax.experimental.pallas import tpu as pltpu

F32 = jnp.float32
BF16 = jnp.bfloat16

D_MODEL = 1024
GRID_W = 64
NA_HEADS = 8
NA_HEAD_DIM = 64
NA_WIDTH = NA_HEADS * NA_HEAD_DIM
NA_WIN_H = 8
NA_WIN_W = 16
GLA_HEADS = 4
GLA_DK = D_MODEL // 2
GLA_DV = D_MODEL
GLA_HK = GLA_DK // GLA_HEADS
GLA_HV = GLA_DV // GLA_HEADS
GLA_GATE_RANK = 16
GLA_GATE_NORM = 16.0
GLA_CHUNK = 64
GLA_GROUP = 256
ROPE_BASE = 10000.0
N_GROUPS = 4
EXPERTS_PER_GROUP = 8
N_EXPERTS = N_GROUPS * EXPERTS_PER_GROUP
D_EXPERT = D_MODEL // 2
EPS = 1e-6
NEG_INF = -1e30
LOG2E = 1.4426950408889634

LANES = 128
MXU_DIM = 256
LR_PAD = LANES
ROUTE_LANES = LANES
EXPERT_LANE0 = N_GROUPS
ROUTE_ROWS = 48
MERGE_TILES = 2
COMBINE_TILES = 2
DISPATCH_TILES = 2
MOE_ROWS = 512
SEG_ALIGN = 8
PIECE_SIZES = (8, 16, 24, 32, 40, 48, 56, 64)
PIECE_SLOTS = (32, 32, 32, 32, 32, 32, 32, 24)
MAX_BIG_PER_RUN = 8
PIECE_HEAD = 16
PIECE_TABLE = 512
WAIT_SIZES = (1024, 512, 256, 128, 64, 32, 16, 8)
TAIL_SIZES = (256, 128, 64, 32, 16, 8)
VMEM_LIMIT = 56 * 1024 * 1024

_NT = (((1,), (1,)), ((), ()))
_TN = (((0,), (0,)), ((), ()))


def _params(sem, vmem=VMEM_LIMIT):
    return pltpu.CompilerParams(dimension_semantics=sem, vmem_limit_bytes=vmem)


def _mod_kernel(c_ref, w_ref, b_ref, o_ref):
    c = c_ref[...]
    s = c * jax.nn.sigmoid(c)
    o_ref[...] = jnp.dot(s.astype(BF16), w_ref[...].astype(BF16),
                         preferred_element_type=F32) + b_ref[...]


def _modulation(cs, w_mod, b_mod):
    rows, d = cs.shape
    n = w_mod.shape[1]
    bn = 1024
    return pl.pallas_call(
        _mod_kernel,
        grid=(n // bn,),
        in_specs=[pl.BlockSpec((rows, d), lambda j: (0, 0)),
                  pl.BlockSpec((d, bn), lambda j: (0, j)),
                  pl.BlockSpec((1, bn), lambda j: (0, j))],
        out_specs=pl.BlockSpec((rows, bn), lambda j: (0, j)),
        out_shape=jax.ShapeDtypeStruct((rows, n), F32),
        compiler_params=_params(("arbitrary",)),
        name="mod",
    )(cs, w_mod, b_mod.reshape(1, n))


def _inproj_kernel(specs, use_rope, n_w, n_out, x_ref, mod_ref, g_ref, *rest):
    if use_rope:
        cos_ref, sin_ref = rest[:2]
        rest = rest[2:]
    w_refs = rest[:n_w]
    outs = rest[n_w:n_w + n_out]
    a_scr = rest[n_w + n_out]
    x = x_ref[...]
    ms = jnp.mean(x * x, axis=-1, keepdims=True)
    a = x * lax.rsqrt(ms + EPS) * g_ref[...] * (1.0 + mod_ref[0, 1:2, :]) + mod_ref[0, 0:1, :]
    a_scr[...] = a.astype(BF16)
    for (oi, oc, wi, wc, width, kind) in specs:
        acc = jnp.dot(a_scr[...], w_refs[wi][:, wc:wc + width], preferred_element_type=F32)
        if kind in ("rope", "rope_scaled"):
            if kind == "rope_scaled":
                acc = acc * (GLA_HK ** -0.5)
            lane = lax.broadcasted_iota(jnp.int32, acc.shape, 1)
            rot = jnp.where((lane & 63) < 32,
                            pltpu.roll(acc, width - 32, axis=1),
                            pltpu.roll(acc, 32, axis=1))
            acc = acc * cos_ref[...] + rot * sin_ref[...]
        elif kind == "silu":
            acc = acc * jax.nn.sigmoid(acc)
        elif kind == "sigmoid":
            acc = jax.nn.sigmoid(acc)
        outs[oi][:, oc:oc + width] = acc.astype(outs[oi].dtype)


def _inproj(x2d, mod3, mod_row_fn, g, weights, specs, out_widths, tm, rope=None, name="inproj"):
    n, d = x2d.shape
    use_rope = rope is not None
    in_specs = [pl.BlockSpec((tm, d), lambda i: (i, 0)),
                pl.BlockSpec((1, 6, d), lambda i: (mod_row_fn(i), 0, 0)),
                pl.BlockSpec((1, d), lambda i: (0, 0))]
    args = [x2d, mod3, g.reshape(1, d)]
    if use_rope:
        cos, sin = rope
        nb = cos.shape[0] // tm
        in_specs += [pl.BlockSpec((tm, cos.shape[1]), lambda i: (i % nb, 0)),
                     pl.BlockSpec((tm, cos.shape[1]), lambda i: (i % nb, 0))]
        args += [cos, sin]
    in_specs += [pl.BlockSpec((d, cols), lambda i: (0, 0), pipeline_mode=pl.Buffered(1)) for _, cols in weights]
    args += [w for w, _ in weights]
    out_specs = [pl.BlockSpec((tm, w), lambda i: (i, 0)) for w in out_widths]
    out_shape = [jax.ShapeDtypeStruct((n, w), BF16) for w in out_widths]
    return pl.pallas_call(
        functools.partial(_inproj_kernel, specs, use_rope, len(weights), len(out_widths)),
        grid=(n // tm,),
        in_specs=in_specs,
        out_specs=out_specs,
        out_shape=out_shape,
        scratch_shapes=[pltpu.VMEM((tm, d), BF16)],
        compiler_params=_params(("arbitrary",)),
        name=name,
    )(*args)


def _na_kernel(rows_per_step, n_rows, q_ref, k_ref, v_ref, kc_ref, vc_ref, bias_ref, o_ref, sw_scr, sc_scr):
    j = pl.program_id(1)
    lane = lax.broadcasted_iota(jnp.int32, (GRID_W, LANES), 1)
    first_head = lane < NA_HEAD_DIM
    scale = NA_HEAD_DIM ** -0.5
    win_keys = NA_WIN_H * GRID_W
    pairs = range(NA_HEADS // 2)
    lanes = [slice(p * LANES, (p + 1) * LANES) for p in pairs]
    stack = 2 * GRID_W

    def window(rr):
        r = j * rows_per_step + rr
        rs = jnp.clip(r - NA_WIN_H // 2, 0, n_rows - NA_WIN_H)
        return r - rs, pl.multiple_of(rr * GRID_W, GRID_W), pl.multiple_of(rs * GRID_W, GRID_W)

    def scores(rr, slot):
        _, t0, ks = window(rr)
        for p in pairs:
            qp = q_ref[0, pl.ds(t0, GRID_W), lanes[p]] * scale
            zero = jnp.zeros_like(qp)
            qs = jnp.concatenate([jnp.where(first_head, qp, zero), jnp.where(first_head, zero, qp)], axis=0)
            sw_scr[slot, p * stack:(p + 1) * stack, :] = lax.dot_general(
                qs, k_ref[0, pl.ds(ks, win_keys), lanes[p]], _NT, preferred_element_type=F32)
            sc_scr[slot, p * stack:(p + 1) * stack, :] = lax.dot_general(
                qs, kc_ref[0, :, lanes[p]], _NT, preferred_element_type=F32)

    def attend(rr, slot):
        d, t0, ks = window(rr)
        pws, pcs, dens = [], [], []
        for p in pairs:
            bias = jnp.concatenate([bias_ref[(NA_WIN_H - 1) - d + 2 * i, p] for i in range(NA_WIN_H // 2)], axis=1)
            sw = sw_scr[slot, p * stack:(p + 1) * stack, :] * LOG2E + bias
            sc = sc_scr[slot, p * stack:(p + 1) * stack, :] * LOG2E
            m = jnp.maximum(jnp.max(sw, axis=-1, keepdims=True), jnp.max(sc, axis=-1, keepdims=True))
            pw = jnp.exp2(sw - m)
            pc = jnp.exp2(sc - m)
            dens.append(jnp.sum(pw, axis=-1, keepdims=True) + jnp.sum(pc, axis=-1, keepdims=True))
            pws.append(pw.astype(BF16))
            pcs.append(pc.astype(BF16))
        for p in pairs:
            o2 = (jnp.dot(pws[p], v_ref[0, pl.ds(ks, win_keys), lanes[p]], preferred_element_type=F32)
                  + jnp.dot(pcs[p], vc_ref[0, :, lanes[p]], preferred_element_type=F32)) / dens[p]
            o_ref[0, pl.ds(t0, GRID_W), lanes[p]] = jnp.where(first_head, o2[:GRID_W],
                                                             o2[GRID_W:]).astype(o_ref.dtype)

    scores(0, 0)

    def two_rows(k, carry):
        scores(2 * k + 1, 1)
        attend(2 * k, 0)
        scores(2 * k + 2, 0)
        attend(2 * k + 1, 1)
        return carry

    lax.fori_loop(0, rows_per_step // 2 - 1, two_rows, 0)
    scores(rows_per_step - 1, 1)
    attend(rows_per_step - 2, 0)
    attend(rows_per_step - 1, 1)


def _na_bias_table(rpb):
    n_heads, n_dr, n_dc = rpb.shape
    c = np.arange(GRID_W)
    cstart = np.clip(c - NA_WIN_W // 2, 0, GRID_W - NA_WIN_W)
    kc = np.arange(GRID_W)
    valid = (kc[None, :] >= cstart[:, None]) & (kc[None, :] < cstart[:, None] + NA_WIN_W)
    col_idx = np.clip(kc[None, :] - c[:, None] + NA_WIN_W - 1, 0, n_dc - 1)
    pick = np.zeros((n_dc, GRID_W * GRID_W), np.float32)
    pick[col_idx.reshape(-1), np.arange(GRID_W * GRID_W)] = 1.0
    t = jnp.dot(rpb.reshape(n_heads * n_dr, n_dc).astype(F32), jnp.asarray(pick),
                precision=lax.Precision.HIGHEST).reshape(n_heads, n_dr, GRID_W, GRID_W)
    t = jnp.where(jnp.asarray(valid)[None, None], t * LOG2E, NEG_INF)
    t2 = jnp.concatenate([t[:, :-1], t[:, 1:]], axis=-1)
    return t2.transpose(1, 0, 2, 3).reshape(n_dr - 1, n_heads // 2, 2 * GRID_W, 2 * GRID_W)


def _na_attention(q, k, v, kc, vc, bias, rows_per_step=32):
    b, s, w = q.shape
    n_rows = s // GRID_W
    ctx_len = kc.shape[1]
    blk = rows_per_step * GRID_W
    return pl.pallas_call(
        functools.partial(_na_kernel, rows_per_step, n_rows),
        grid=(b, n_rows // rows_per_step),
        in_specs=[pl.BlockSpec((1, blk, w), lambda bi, j: (bi, j, 0)),
                  pl.BlockSpec((1, s, w), lambda bi, j: (bi, 0, 0)),
                  pl.BlockSpec((1, s, w), lambda bi, j: (bi, 0, 0)),
                  pl.BlockSpec((1, ctx_len, w), lambda bi, j: (bi, 0, 0)),
                  pl.BlockSpec((1, ctx_len, w), lambda bi, j: (bi, 0, 0)),
                  pl.BlockSpec(bias.shape, lambda bi, j: (0, 0, 0, 0), pipeline_mode=pl.Buffered(1))],
        out_specs=pl.BlockSpec((1, blk, w), lambda bi, j: (bi, j, 0)),
        out_shape=jax.ShapeDtypeStruct((b, s, w), BF16),
        scratch_shapes=[pltpu.VMEM((2, NA_HEADS * GRID_W, NA_WIN_H * GRID_W), F32),
                        pltpu.VMEM((2, NA_HEADS * GRID_W, ctx_len), F32)],
        compiler_params=_params(("arbitrary", "arbitrary")),
        name="na_attn",
    )(q, k, v, kc, vc, bias)


def _gla_kernel(n_groups, gq_ref, gk_ref, gv_ref, lr_ref, sog_ref, ck_ref, cv_ref, clr_ref,
                wa_ref, ba_ref, gn_ref, o_ref, st_scr, of_scr, qe_scr, kd_scr, ckd_scr, dec_scr, cdec_scr,
                qk_a):
    c_len = GLA_CHUNK
    grp = GLA_GROUP
    cpg = grp // c_len
    row = lax.broadcasted_iota(jnp.int32, (grp, grp), 0)
    col = lax.broadcasted_iota(jnp.int32, (grp, grp), 1)
    same_chunk = (row // c_len) == (col // c_len)
    masks = (same_chunk & (col <= row), same_chunk & (col >= row))

    dirs = range(2)

    def gates(lr_rows, k_rows):
        zs = [jnp.dot(lr_rows, wa_ref[t], preferred_element_type=F32) + ba_ref[t:t + 1, :] for t in dirs]
        gs = [(jnp.minimum(z, 0.0) * LOG2E - jnp.log2(1.0 + jnp.exp2(jnp.abs(z) * (-LOG2E)))) * (1.0 / GLA_GATE_NORM)
              for z in zs]
        g1s = [g.astype(BF16) for g in gs]
        g2s = [(g - g1.astype(F32)).astype(BF16) for g, g1 in zip(gs, g1s)]
        tris = [masks[t].astype(BF16) for t in dirs]
        bs = [jnp.dot(tris[t], g1s[t], preferred_element_type=F32)
              + jnp.dot(tris[t], g2s[t], preferred_element_type=F32) for t in dirs]
        out = []
        for t in dirs:
            edge = c_len - 1 if t == 0 else 0
            decs = [jnp.exp2(bs[t][j * c_len + edge:j * c_len + edge + 1, :]) for j in range(cpg)]
            ke = k_rows * jnp.exp2(-bs[t])
            kd = (ke * jnp.concatenate([jnp.broadcast_to(v, (c_len, v.shape[1])) for v in decs],
                                       axis=0)).astype(BF16)
            dec = jnp.concatenate(decs + [jnp.ones_like(v) for v in decs], axis=0)
            out.append((bs[t], ke.astype(BF16), kd, dec))
        return out

    for t, (_, _, kd, dec) in enumerate(gates(clr_ref[0], ck_ref[0].astype(F32))):
        ckd_scr[t] = kd
        cdec_scr[t] = dec

    def group_gates(gi, qk_buf):
        rows = pl.ds(pl.multiple_of(gi * grp, grp), grp)
        qf = gq_ref[0, rows, :].astype(F32)
        for t, (b, ke, kd, dec) in enumerate(gates(lr_ref[0, rows, :], gk_ref[0, rows, :].astype(F32))):
            qe = (qf * jnp.exp2(b)).astype(BF16)
            qk_buf[t] = qe
            qk_buf[2 + t] = ke
            qe_scr[t, rows, :] = qe
            kd_scr[t, rows, :] = kd
            dec_scr[t, gi] = dec

    def group_attend(gi, qk_buf):
        rows = pl.ds(pl.multiple_of(gi * grp, grp), grp)
        heads = range(GLA_HEADS)
        kls = [slice(h * GLA_HK, (h + 1) * GLA_HK) for h in heads]
        raw = [[lax.dot_general(qk_buf[t, :, kls[h]], qk_buf[2 + t, :, kls[h]], _NT, preferred_element_type=F32)
                for t in dirs] for h in heads]
        atts = [(jnp.where(masks[0], raw[h][0], 0.0) + jnp.where(masks[1], raw[h][1], 0.0)).astype(BF16)
                for h in heads]
        for h in heads:
            vl = slice(h * GLA_HV, (h + 1) * GLA_HV)
            of_scr[rows, vl] = jnp.dot(atts[h], gv_ref[0, rows, vl], preferred_element_type=F32)

    def state_step(t, h, v_rows, kd_rows, dec_row):
        kl = slice(h * GLA_HK, (h + 1) * GLA_HK)
        upd = lax.dot_general(v_rows, kd_rows[:, kl], _TN, preferred_element_type=F32)
        st_scr[t, h] = st_scr[t, h] * dec_row[:, kl] + upd

    def chunk_order(t):
        return range(cpg) if t == 0 else range(cpg - 1, -1, -1)

    st_scr[...] = jnp.zeros_like(st_scr)
    for t in dirs:
        for j in chunk_order(t):
            rows = slice(j * c_len, (j + 1) * c_len)
            for h in range(GLA_HEADS):
                state_step(t, h, cv_ref[0, rows, h * GLA_HV:(h + 1) * GLA_HV],
                           ckd_scr[t, rows, :], cdec_scr[t, j:j + 1, :])

    def chunk_step(t, gi, j, finish):
        rows = pl.ds(pl.multiple_of(gi * grp + j * c_len, c_len), c_len)
        qe_rows = qe_scr[t, rows, :]
        kd_rows = kd_scr[t, rows, :]
        dec_row = dec_scr[t, gi, j:j + 1, :]
        for h in range(GLA_HEADS):
            kl = slice(h * GLA_HK, (h + 1) * GLA_HK)
            vl = slice(h * GLA_HV, (h + 1) * GLA_HV)
            o_t = of_scr[rows, vl] + lax.dot_general(qe_rows[:, kl], st_scr[t, h].astype(BF16), _NT,
                                                     preferred_element_type=F32)
            if finish:
                ms = jnp.mean(o_t * o_t, axis=-1, keepdims=True)
                o_n = o_t * lax.rsqrt(ms + EPS) * gn_ref[:, vl] * sog_ref[0, rows, vl].astype(F32)
                o_ref[0, rows, vl] = o_n.astype(o_ref.dtype)
            else:
                of_scr[rows, vl] = o_t
            state_step(t, h, gv_ref[0, rows, vl], kd_rows, dec_row)

    def scan_steps(i, finish):
        steps = []
        for jf, jb in zip(chunk_order(0), chunk_order(1)):
            steps.append(functools.partial(chunk_step, 0, i, jf, finish))
            steps.append(functools.partial(chunk_step, 1, n_groups - 1 - i, jb, finish))
        return steps

    half = n_groups // 2
    for g in (0, n_groups - 1):
        group_gates(g, qk_a)
        group_attend(g, qk_a)

    def prepare_and_scan(i, carry):
        steps = scan_steps(i, False)
        per_phase = len(steps) // 4
        for n_phase, phase in enumerate((functools.partial(group_gates, i + 1, qk_a),
                                         functools.partial(group_attend, i + 1, qk_a),
                                         functools.partial(group_gates, n_groups - 2 - i, qk_a),
                                         functools.partial(group_attend, n_groups - 2 - i, qk_a))):
            phase()
            for step in steps[n_phase * per_phase:(n_phase + 1) * per_phase]:
                step()
        return carry

    def scan_only(finish):
        def body(i, carry):
            for step in scan_steps(i, finish):
                step()
            return carry
        return body

    lax.fori_loop(0, half - 1, prepare_and_scan, 0)
    lax.fori_loop(half - 1, half, scan_only(False), 0)
    lax.fori_loop(half, n_groups, scan_only(True), 0)


def _gla(gq, gk, gv, lr, sog, ck, cv, clr, wa, ba, gn):
    b, s, _ = gq.shape
    ctx_len = ck.shape[1]
    assert ctx_len == GLA_GROUP and s % (2 * GLA_GROUP) == 0
    n_groups = s // GLA_GROUP
    cpg = GLA_GROUP // GLA_CHUNK

    def per_batch(t, w):
        return pl.BlockSpec((1, t, w), lambda bi: (bi, 0, 0))

    return pl.pallas_call(
        functools.partial(_gla_kernel, n_groups),
        grid=(b,),
        in_specs=[per_batch(s, GLA_DK), per_batch(s, GLA_DK), per_batch(s, GLA_DV),
                  per_batch(s, LR_PAD), per_batch(s, GLA_DV),
                  per_batch(ctx_len, GLA_DK), per_batch(ctx_len, GLA_DV), per_batch(ctx_len, LR_PAD),
                  pl.BlockSpec(wa.shape, lambda bi: (0, 0, 0)),
                  pl.BlockSpec(ba.shape, lambda bi: (0, 0)),
                  pl.BlockSpec(gn.shape, lambda bi: (0, 0))],
        out_specs=per_batch(s, GLA_DV),
        out_shape=jax.ShapeDtypeStruct((b, s, GLA_DV), BF16),
        scratch_shapes=[pltpu.VMEM((2, GLA_HEADS, GLA_HV, GLA_HK), F32),
                        pltpu.VMEM((s, GLA_DV), F32),
                        pltpu.VMEM((2, s, GLA_DK), BF16),
                        pltpu.VMEM((2, s, GLA_DK), BF16),
                        pltpu.VMEM((2, ctx_len, GLA_DK), BF16),
                        pltpu.VMEM((2, n_groups, 2 * cpg, GLA_DK), F32),
                        pltpu.VMEM((2, 2 * cpg, GLA_DK), F32),
                        pltpu.VMEM((4, GLA_GROUP, GLA_DK), BF16)],
        compiler_params=_params(("arbitrary",), vmem=60 * 1024 * 1024),
        name="gla",
    )(gq, gk, gv, lr, sog, ck, cv, clr, wa, ba, gn)


def _merge_kernel(ona_ref, ogla_ref, sgn_ref, sgg_ref, x_ref, mod_ref, gffn_ref,
                  wna_ref, wgla_ref, wout_ref, wr_ref, br_ref,
                  h_ref, f_ref, route_ref, route_t_ref, cnt_ref):
    rows = x_ref.shape[0] // MERGE_TILES

    def tile_rows(u):
        return slice(u * rows, (u + 1) * rows)

    def branches(u):
        rs = tile_rows(u)
        return (sgn_ref[rs, :].astype(F32) * jnp.dot(ona_ref[rs, :], wna_ref[...], preferred_element_type=F32)
                + sgg_ref[rs, :].astype(F32) * jnp.dot(ogla_ref[rs, :], wgla_ref[...], preferred_element_type=F32))

    def project(y1):
        return jnp.dot(y1.astype(BF16), wout_ref[...], preferred_element_type=F32)

    def residual(u, y):
        rs = tile_rows(u)
        h = x_ref[rs, :] + mod_ref[0, 2:3, :] * y
        h_ref[rs, :] = h
        ms = jnp.mean(h * h, axis=-1, keepdims=True)
        f = h * lax.rsqrt(ms + EPS) * gffn_ref[...] * (1.0 + mod_ref[0, 4:5, :]) + mod_ref[0, 3:4, :]
        f_ref[rs, :] = f.astype(f_ref.dtype)
        return f

    def router(f):
        f_hi = f.astype(BF16)
        f_lo = (f - f_hi.astype(F32)).astype(BF16)
        both = jnp.dot(f_hi, wr_ref[...], preferred_element_type=F32)
        return (both[:, :ROUTE_LANES] + both[:, ROUTE_LANES:]
                + jnp.dot(f_lo, wr_ref[:, :ROUTE_LANES], preferred_element_type=F32)) + br_ref[...]

    y_a = project(branches(0))
    y1_b = branches(1)
    f_a = residual(0, y_a)
    y_b = project(y1_b)
    logits_a = router(f_a)
    f_b = residual(1, y_b)
    _route_tile(0, tile_rows(0), logits_a, route_ref, route_t_ref, cnt_ref)
    _route_tile(1, tile_rows(1), router(f_b), route_ref, route_t_ref, cnt_ref)


def _route_tile(u, rs, logits, route_ref, route_t_ref, cnt_ref):
    rows = logits.shape[0]
    lt = logits.T[0:ROUTE_ROWS, :]
    er = lax.broadcasted_iota(jnp.int32, (ROUTE_ROWS, rows), 0)
    big = jnp.int32(ROUTE_ROWS)

    def first_argmax(vals):
        m = jnp.max(vals, axis=0, keepdims=True)
        idx = jnp.min(jnp.where(vals == m, er, big), axis=0, keepdims=True)
        return m, idx

    is_grp = er < N_GROUPS
    lg = jnp.where(is_grp, lt, -jnp.inf)
    mg, grp = first_argmax(lg)
    p_grp = 1.0 / jnp.sum(jnp.where(is_grp, jnp.exp(lg - mg), 0.0), axis=0, keepdims=True)
    lo = EXPERT_LANE0 + grp * EXPERTS_PER_GROUP
    in_grp = (er >= lo) & (er < lo + EXPERTS_PER_GROUP)
    le = jnp.where(in_grp, lt, -jnp.inf)
    m1, i1 = first_argmax(le)
    m2, i2 = first_argmax(jnp.where(er == i1, -jnp.inf, le))
    t = jnp.exp(m2 - m1)
    w1 = p_grp / (1.0 + t)
    w2 = p_grp * t / (1.0 + t)

    hot1 = er == i1
    hot2 = er == i2
    onehot = (hot1 | hot2).astype(BF16)
    s_i = lax.broadcasted_iota(jnp.int32, (rows, rows), 0)
    t_i = lax.broadcasted_iota(jnp.int32, (rows, rows), 1)
    rank = jnp.dot(onehot, (s_i < t_i).astype(BF16), preferred_element_type=F32)
    cnt = jnp.sum(onehot.astype(F32), axis=1, keepdims=True)
    cnt_al = jnp.floor((cnt + (SEG_ALIGN - 1.0)) * (1.0 / SEG_ALIGN)) * SEG_ALIGN
    e_i = lax.broadcasted_iota(jnp.int32, (ROUTE_ROWS, ROUTE_ROWS), 0)
    e_j = lax.broadcasted_iota(jnp.int32, (ROUTE_ROWS, ROUTE_ROWS), 1)
    lstart = jnp.dot((e_j < e_i).astype(BF16), jnp.broadcast_to(cnt_al, (ROUTE_ROWS, LANES)).astype(BF16),
                     preferred_element_type=F32)[:, 0:1]
    pos = rank + lstart
    pos1 = jnp.sum(jnp.where(hot1, pos, 0.0), axis=0, keepdims=True)
    pos2 = jnp.sum(jnp.where(hot2, pos, 0.0), axis=0, keepdims=True)

    e1 = (i1 - EXPERT_LANE0).astype(F32)
    e2 = (i2 - EXPERT_LANE0).astype(F32)
    fields = jnp.concatenate([e1, e2, w1, w2, pos1, pos2, jnp.zeros((2, rows), F32)], axis=0)
    route_t_ref[u] = fields
    route_ref[rs, :] = jnp.concatenate([fields, jnp.zeros((ROUTE_LANES - 8, rows), F32)], axis=0).T
    cnt_rows = jnp.concatenate([jnp.broadcast_to(cnt, (ROUTE_ROWS, ROUTE_LANES)),
                                jnp.zeros((ROUTE_LANES - ROUTE_ROWS, ROUTE_LANES), F32)], axis=0)
    cnt_ref[u] = cnt_rows.T[0:1, :]


def _merge_route(ona, ogla, sgn, sgg, x2d, mod3, gffn, wna, wgla, wout, wr, br, tokens_per_batch, tm):
    n, d = x2d.shape
    step_rows = MERGE_TILES * tm
    assert tokens_per_batch % step_rows == 0
    per_b = tokens_per_batch // step_rows

    def tile(w):
        return pl.BlockSpec((step_rows, w), lambda i: (i, 0))

    def whole(a):
        return pl.BlockSpec(a.shape, lambda i: (0,) * a.ndim)

    return pl.pallas_call(
        _merge_kernel,
        grid=(n // step_rows,),
        in_specs=[tile(NA_WIDTH), tile(GLA_DV), tile(d), tile(d), tile(d),
                  pl.BlockSpec((1, 6, d), lambda i: (i // per_b, 0, 0)),
                  whole(gffn), whole(wna), whole(wgla), whole(wout), whole(wr), whole(br)],
        out_specs=[tile(d), tile(d), tile(ROUTE_LANES),
                   pl.BlockSpec((MERGE_TILES, 8, tm), lambda i: (i, 0, 0)),
                   pl.BlockSpec((MERGE_TILES, 1, ROUTE_LANES), lambda i: (i, 0, 0))],
        out_shape=[jax.ShapeDtypeStruct((n, d), F32),
                   jax.ShapeDtypeStruct((n, d), BF16),
                   jax.ShapeDtypeStruct((n, ROUTE_LANES), F32),
                   jax.ShapeDtypeStruct((n // tm, 8, tm), F32),
                   jax.ShapeDtypeStruct((n // tm, 1, ROUTE_LANES), F32)],
        compiler_params=_params(("arbitrary",)),
        name="merge_route",
    )(ona, ogla, sgn, sgg, x2d, mod3, gffn, wna, wgla, wout, wr, br)


HI_MASK = -65536


def _pack_rows(a):
    half = a.shape[1] // 2
    hi = lax.bitcast_convert_type(a[:, :half], jnp.int32)
    lo = lax.shift_right_logical(lax.bitcast_convert_type(a[:, half:], jnp.int32), 16)
    return hi | lo


def _unpack_rows(w):
    hi = lax.bitcast_convert_type(w & HI_MASK, F32).astype(BF16)
    lo = lax.bitcast_convert_type(lax.shift_left(w, 16), F32).astype(BF16)
    return hi, lo


def _piece_table(cnt_al, lstart, seg_dst):
    n_tiles = cnt_al.shape[0]

    def compact(mask, local, glob, slots):
        pos = jnp.cumsum(mask.astype(jnp.int32), axis=1) - 1
        hot = mask[:, :, None] & (pos[:, :, None] == jnp.arange(slots, dtype=jnp.int32))
        pick = lambda v: jnp.sum(jnp.where(hot, v[:, :, None], 0), axis=1)
        return jnp.sum(mask, axis=1).astype(jnp.int32), pick(local), pick(glob)

    counts, lists = [], []
    big = PIECE_SIZES[-1]
    for size, slots in zip(PIECE_SIZES[:-1], PIECE_SLOTS[:-1]):
        n, lo, gl = compact((cnt_al & (big - 1)) == size, lstart, seg_dst, slots)
        counts.append(n)
        lists += [lo, gl]
    k = jnp.arange(MAX_BIG_PER_RUN, dtype=jnp.int32)
    off = (cnt_al & (big - 1))[:, :, None] + big * k
    mask = k < (cnt_al // big)[:, :, None]
    flat = lambda a: a.reshape(n_tiles, -1)
    n, lo, gl = compact(flat(mask), flat(lstart[:, :, None] + off), flat(seg_dst[:, :, None] + off), PIECE_SLOTS[-1])
    counts.append(n)
    lists += [lo, gl]
    head = jnp.stack(counts + [jnp.sum(cnt_al, axis=1)], axis=1)
    head = jnp.pad(head, ((0, 0), (0, PIECE_HEAD - head.shape[1])))
    table = jnp.concatenate([head] + lists, axis=1)
    table = jnp.pad(table, ((0, 0), (0, PIECE_TABLE - table.shape[1])))
    return table.reshape(n_tiles, 1, PIECE_TABLE).astype(jnp.int32)


def _segment_copies(seg_ref, make_copy, action, tile=0):
    if action == "wait":
        total = seg_ref[tile, 0, len(PIECE_SIZES)]
        for size in WAIT_SIZES:
            @pl.when((total & size) != 0)
            def _():
                make_copy(0, 0, size).wait()
        return

    base = PIECE_HEAD
    for c, (size, slots) in enumerate(zip(PIECE_SIZES, PIECE_SLOTS)):
        def piece(k, carry, size=size, lo=base, gl=base + slots):
            make_copy(pl.multiple_of(seg_ref[tile, 0, lo + k], SEG_ALIGN),
                      pl.multiple_of(seg_ref[tile, 0, gl + k], SEG_ALIGN), size).start()
            return carry

        lax.fori_loop(0, seg_ref[tile, 0, c], piece, 0)
        base += 2 * slots


def _dispatch_kernel(seg_ref, seg_prev_ref, tail_ref, lpos_ref, f_ref, xs_ref, sbuf, zbuf, sems):
    i = pl.program_id(0)
    last = pl.num_programs(0) - 1
    tm = f_ref.shape[0] // DISPATCH_TILES
    rows_l = sbuf.shape[1]
    slot = i % 2
    half = f_ref.shape[1] // 2
    tiles = range(DISPATCH_TILES)
    row = lax.broadcasted_iota(jnp.int32, (rows_l, tm), 0)

    def copier(s, u):
        def copy(local, glob, size):
            return pltpu.make_async_copy(sbuf.at[DISPATCH_TILES * s + u, pl.ds(local, size), :],
                                         xs_ref.at[pl.ds(glob, size), :], sems.at[DISPATCH_TILES * s + u])
        return copy

    def permute(u):
        fields = lpos_ref[u]
        lp = fields.astype(jnp.int32)
        first = row == lp[4:5, :]
        second = row == lp[5:6, :]
        sorted_rows = jnp.dot((first | second).astype(BF16), f_ref[u * tm:(u + 1) * tm, :],
                              preferred_element_type=F32)
        w_sorted = jnp.sum(jnp.where(first, fields[2:3, :], 0.0) + jnp.where(second, fields[3:4, :], 0.0),
                           axis=1, keepdims=True)
        return sorted_rows, w_sorted

    def emit(u, sorted_rows, w_sorted):
        buf = DISPATCH_TILES * slot + u
        sbuf[buf, :, :half] = _pack_rows(sorted_rows)
        sbuf[buf, :, half:] = lax.bitcast_convert_type(jnp.broadcast_to(w_sorted, (rows_l, LANES)), jnp.int32)
        _segment_copies(seg_ref, copier(slot, u), "start", u)

    perm_a = permute(0)
    perm_b = permute(1)
    emit(0, *perm_a)
    emit(1, *perm_b)

    @pl.when(i > 0)
    def _():
        for u in tiles:
            _segment_copies(seg_prev_ref, copier(1 - slot, u), "wait", u)

    @pl.when(i == last)
    def _():
        for u in tiles:
            _segment_copies(seg_ref, copier(slot, u), "wait", u)
        sem = sems.at[DISPATCH_TILES * slot]
        zbuf[...] = jnp.zeros_like(zbuf)

        def tail(action):
            def body(e, carry):
                start = tail_ref[0, 0, e]
                length = tail_ref[0, 0, N_EXPERTS + e]
                off = jnp.int32(0)
                for size in TAIL_SIZES:
                    @pl.when((length & size) != 0)
                    def _():
                        cp = pltpu.make_async_copy(zbuf.at[pl.ds(0, size), :],
                                                   xs_ref.at[pl.ds(pl.multiple_of(start + off, SEG_ALIGN), size), :],
                                                   sem)
                        getattr(cp, action)()
                    off = off + (length & size)
                return carry
            lax.fori_loop(0, N_EXPERTS, body, 0)

        tail("start")
        tail("wait")

        def unused(action):
            def body(blk, carry):
                cp = pltpu.make_async_copy(zbuf, xs_ref.at[pl.ds(pl.multiple_of(blk * MOE_ROWS, MOE_ROWS),
                                                                  MOE_ROWS), :], sem)
                getattr(cp, action)()
                return carry
            lax.fori_loop(tail_ref[0, 0, 2 * N_EXPERTS], xs_ref.shape[0] // MOE_ROWS, body, 0)

        unused("start")
        unused("wait")


def _dispatch(seg, tail, lpos_rows, f2d, n_pad, rows_l, tm):
    n, d = f2d.shape
    step_rows = DISPATCH_TILES * tm
    assert n % step_rows == 0
    return pl.pallas_call(
        _dispatch_kernel,
        grid=(n // step_rows,),
        in_specs=[pl.BlockSpec((DISPATCH_TILES, 1, seg.shape[2]), lambda i: (i, 0, 0), memory_space=pltpu.SMEM),
                  pl.BlockSpec((DISPATCH_TILES, 1, seg.shape[2]), lambda i: (jnp.maximum(i - 1, 0), 0, 0),
                               memory_space=pltpu.SMEM),
                  pl.BlockSpec(tail.shape, lambda i: (0, 0, 0), memory_space=pltpu.SMEM),
                  pl.BlockSpec((DISPATCH_TILES, 8, tm), lambda i: (i, 0, 0)),
                  pl.BlockSpec((step_rows, d), lambda i: (i, 0))],
        out_specs=pl.BlockSpec(memory_space=pl.ANY),
        out_shape=jax.ShapeDtypeStruct((n_pad, d // 2 + LANES), jnp.int32),
        scratch_shapes=[pltpu.VMEM((2 * DISPATCH_TILES, rows_l, d // 2 + LANES), jnp.int32),
                        pltpu.VMEM((MOE_ROWS, d // 2 + LANES), jnp.int32),
                        pltpu.SemaphoreType.DMA((2 * DISPATCH_TILES,))],
        compiler_params=_params(("arbitrary",)),
        name="dispatch",
    )(seg, seg, tail, lpos_rows, f2d)


def _expert_kernel(be_ref, nu_ref, x_ref, wg_ref, wu_ref, wd_ref, y_ref, wgu_scr, wd_scr):
    i = pl.program_id(0)
    prev = be_ref[jnp.maximum(i - 1, 0)]

    @pl.when((i == 0) | (be_ref[i] != prev))
    def _():
        wgu_scr[:, :D_EXPERT] = wg_ref[0].astype(BF16)
        wgu_scr[:, D_EXPERT:] = wu_ref[0].astype(BF16)
        wd_scr[...] = wd_ref[0].astype(BF16)

    @pl.when(i < nu_ref[0])
    def _():
        half = y_ref.shape[1]
        x = jnp.concatenate(_unpack_rows(x_ref[:, :half]), axis=1)
        gu = jnp.dot(x, wgu_scr[...], preferred_element_type=F32)
        gate = gu[:, :D_EXPERT]
        hdn = gate * jax.nn.sigmoid(gate) * gu[:, D_EXPERT:]
        y = jnp.dot(hdn.astype(BF16), wd_scr[...], preferred_element_type=F32)
        y = y * lax.bitcast_convert_type(x_ref[:, half:half + 1], F32)
        y_ref[...] = _pack_rows(y.astype(BF16).astype(F32))

    @pl.when(i >= nu_ref[0])
    def _():
        y_ref[...] = jnp.zeros_like(y_ref)


def _experts(blk_expert, n_used, xs, wg, wu, wd):
    n_pad, row_words = xs.shape
    half = row_words - LANES
    d = 2 * half
    n_blk = n_pad // MOE_ROWS

    def used_block(i, be, nu):
        return (jnp.maximum(jnp.minimum(i, nu[0] - 1), 0), 0)

    grid_spec = pltpu.PrefetchScalarGridSpec(
        num_scalar_prefetch=2,
        grid=(n_blk,),
        in_specs=[pl.BlockSpec((MOE_ROWS, row_words), used_block),
                  pl.BlockSpec((1, d, D_EXPERT), lambda i, be, nu: (be[i], 0, 0)),
                  pl.BlockSpec((1, d, D_EXPERT), lambda i, be, nu: (be[i], 0, 0)),
                  pl.BlockSpec((1, D_EXPERT, d), lambda i, be, nu: (be[i], 0, 0))],
        out_specs=pl.BlockSpec((MOE_ROWS, half), lambda i, be, nu: (i, 0)),
        scratch_shapes=[pltpu.VMEM((d, 2 * D_EXPERT), BF16),
                        pltpu.VMEM((D_EXPERT, d), BF16)],
    )
    return pl.pallas_call(
        _expert_kernel,
        grid_spec=grid_spec,
        out_shape=jax.ShapeDtypeStruct((n_pad, half), jnp.int32),
        compiler_params=_params(("arbitrary",)),
        name="experts",
    )(blk_expert, n_used, xs, wg, wu, wd)


def _combine_kernel(seg_ref, seg_next_ref, col_ref, h_ref, mod_ref, fg_ref, ys_ref, o_ref, ybuf, sems):
    i = pl.program_id(0)
    tm = h_ref.shape[0] // COMBINE_TILES
    rows_l = ybuf.shape[1]
    slot = i % 2
    tiles = range(COMBINE_TILES)

    def copier(s, u):
        def copy(local, glob, size):
            return pltpu.make_async_copy(ys_ref.at[pl.ds(glob, size), :],
                                         ybuf.at[COMBINE_TILES * s + u, pl.ds(local, size), :],
                                         sems.at[COMBINE_TILES * s + u])
        return copy

    @pl.when(i == 0)
    def _():
        ybuf[...] = jnp.zeros_like(ybuf)
        for u in tiles:
            _segment_copies(seg_ref, copier(slot, u), "start", u)

    @pl.when(i + 1 < pl.num_programs(0))
    def _():
        for u in tiles:
            _segment_copies(seg_next_ref, copier(1 - slot, u), "start", u)

    def gather(u):
        _segment_copies(seg_ref, copier(slot, u), "wait", u)
        y_hi, y_lo = _unpack_rows(ybuf[COMBINE_TILES * slot + u])
        y_all = jnp.concatenate([y_hi, y_lo], axis=1)
        col = lax.broadcasted_iota(jnp.int32, (tm, rows_l), 1)
        info = col_ref[u * tm:(u + 1) * tm, :]
        pick = (col == info[:, 4:5].astype(jnp.int32)) | (col == info[:, 5:6].astype(jnp.int32))
        return jnp.dot(pick.astype(BF16), y_all, preferred_element_type=F32)

    def finish(u, moe):
        rs = slice(u * tm, (u + 1) * tm)
        h = h_ref[rs, :] + mod_ref[0, 5:6, :] * moe
        ms = jnp.mean(h * h, axis=-1, keepdims=True)
        o_ref[rs, :] = h * lax.rsqrt(ms + EPS) * fg_ref[...]

    moe_a = gather(0)
    moe_b = gather(1)
    finish(0, moe_a)
    finish(1, moe_b)


def _combine(seg, colinfo, h2d, mod3, fg, ys, rows_l, tokens_per_batch, tm):
    n, d = h2d.shape
    step_rows = COMBINE_TILES * tm
    assert tokens_per_batch % step_rows == 0
    per_b = tokens_per_batch // step_rows
    n_steps = n // step_rows
    return pl.pallas_call(
        _combine_kernel,
        grid=(n_steps,),
        in_specs=[pl.BlockSpec((COMBINE_TILES, 1, seg.shape[2]), lambda i: (i, 0, 0), memory_space=pltpu.SMEM),
                  pl.BlockSpec((COMBINE_TILES, 1, seg.shape[2]), lambda i: (jnp.minimum(i + 1, n_steps - 1), 0, 0),
                               memory_space=pltpu.SMEM),
                  pl.BlockSpec((step_rows, colinfo.shape[1]), lambda i: (i, 0)),
                  pl.BlockSpec((step_rows, d), lambda i: (i, 0)),
                  pl.BlockSpec((1, 6, d), lambda i: (i // per_b, 0, 0)),
                  pl.BlockSpec((1, d), lambda i: (0, 0)),
                  pl.BlockSpec(memory_space=pl.ANY)],
        out_specs=pl.BlockSpec((step_rows, d), lambda i: (i, 0)),
        out_shape=jax.ShapeDtypeStruct((n, d), F32),
        scratch_shapes=[pltpu.VMEM((2 * COMBINE_TILES, rows_l, d // 2), jnp.int32),
                        pltpu.SemaphoreType.DMA((2 * COMBINE_TILES,))],
        compiler_params=_params(("arbitrary",)),
        name="combine",
    )(seg, seg, colinfo, h2d, mod3, fg, ys)


def _rope_tables(s):
    t = np.arange(s)
    pos_r = (t // GRID_W).astype(np.float32)
    pos_c = (t % GRID_W).astype(np.float32)
    nf = GLA_HK // 4
    inv = (np.float32(ROPE_BASE) ** (-np.arange(nf, dtype=np.float32) / np.float32(nf))).astype(np.float32)
    ang_r = pos_r[:, None] * inv
    ang_c = pos_c[:, None] * inv

    def half(ang):
        return (np.concatenate([np.cos(ang), np.cos(ang)], axis=-1),
                np.concatenate([-np.sin(ang), np.sin(ang)], axis=-1))

    cr, sr = half(ang_r)
    cc, sc = half(ang_c)
    cos = np.tile(np.concatenate([cr, cc], axis=-1), (1, GLA_HEADS)).astype(np.float32)
    sin = np.tile(np.concatenate([sr, sc], axis=-1), (1, GLA_HEADS)).astype(np.float32)
    return jnp.asarray(cos), jnp.asarray(sin)


def kernel(x, c, ctx, c_ctx, w_mod, b_mod, norm_attn_g, norm_ffn_g, w_in, w_gla_a2, b_gla_a2, gla_norm_g, na_rpb, w_na_o, w_gla_o, w_out, w_group, b_group, w_expert, b_expert, w_exp_gate, w_exp_up, w_exp_down, final_norm_g):
    b, s, d = x.shape
    ctx_len = ctx.shape[1]
    n = b * s
    assert w_mod.shape[0] == 1, "single-layer block"
    assert s % (GRID_W * NA_WIN_H) == 0 and ctx_len % GLA_CHUNK == 0

    mod_rows = -(-(b + 1) // 8) * 8
    cs = jnp.zeros((mod_rows, d), F32).at[:b].set(c).at[b].set(c_ctx)
    mod3 = _modulation(cs, w_mod[0], b_mod[0]).reshape(mod_rows, 6, d)

    o_q, o_k, o_v = 0, NA_WIDTH, 2 * NA_WIDTH
    o_gq = 3 * NA_WIDTH
    o_gk = o_gq + GLA_DK
    o_gv = o_gk + GLA_DK
    o_og = o_gv + GLA_DV
    o_lr = o_og + GLA_DV
    o_mn = o_lr + 2 * GLA_GATE_RANK
    w_bf = w_in[0].astype(BF16)
    w_a = (w_bf, o_lr)
    w_g = (w_bf[:, o_mn:], 2 * d)
    w_lr = (jnp.pad(w_bf[:, o_lr:o_mn], ((0, 0), (0, LR_PAD - 2 * GLA_GATE_RANK))), LR_PAD)
    cw = 512
    lat_specs = [(0, 0, 0, o_q, cw, "plain"), (1, 0, 0, o_k, cw, "plain"), (2, 0, 0, o_v, cw, "plain"),
                 (3, 0, 0, o_gq, cw, "rope_scaled"), (4, 0, 0, o_gk, cw, "rope"),
                 (5, 0, 0, o_gv, cw, "plain"), (5, cw, 0, o_gv + cw, cw, "plain"),
                 (6, 0, 0, o_og, cw, "silu"), (6, cw, 0, o_og + cw, cw, "silu"),
                 (7, 0, 1, 0, cw, "sigmoid"), (7, cw, 1, cw, cw, "sigmoid"),
                 (8, 0, 1, d, cw, "sigmoid"), (8, cw, 1, d + cw, cw, "sigmoid"),
                 (9, 0, 2, 0, LR_PAD, "plain")]
    lat_widths = [NA_WIDTH, NA_WIDTH, NA_WIDTH, GLA_DK, GLA_DK, GLA_DV, GLA_DV, d, d, LR_PAD]
    tm = 512
    per_b = s // tm
    q_na, k_na, v_na, gq, gk, gv, sog, sgn, sgg, lr = _inproj(
        x.reshape(n, d), mod3, lambda i: i // per_b, norm_attn_g[0], [w_a, w_g, w_lr], lat_specs, lat_widths, tm,
        rope=_rope_tables(s), name="inproj_lat")

    ctx_specs = [(0, 0, 0, o_k, cw, "plain"), (1, 0, 0, o_v, cw, "plain"), (2, 0, 0, o_gk, cw, "plain"),
                 (3, 0, 0, o_gv, cw, "plain"), (3, cw, 0, o_gv + cw, cw, "plain"),
                 (4, 0, 1, 0, LR_PAD, "plain")]
    ctx_widths = [NA_WIDTH, NA_WIDTH, GLA_DK, GLA_DV, LR_PAD]
    kc_na, vc_na, ck, cv, clr = _inproj(
        ctx.reshape(b * ctx_len, d), mod3, lambda i: b, norm_attn_g[0], [w_a, w_lr], ctx_specs, ctx_widths,
        tm if (b * ctx_len) % tm == 0 else ctx_len, name="inproj_ctx")

    def lat3(a):
        return a.reshape(b, s, a.shape[-1])

    def ctx3(a):
        return a.reshape(b, ctx_len, a.shape[-1])

    o_na = _na_attention(lat3(q_na), lat3(k_na), lat3(v_na), ctx3(kc_na), ctx3(vc_na),
                         _na_bias_table(na_rpb[0]))

    wa = jnp.zeros((2, LR_PAD, GLA_DK), F32)
    wa = wa.at[0, :GLA_GATE_RANK].set(w_gla_a2[0, 0]).at[1, GLA_GATE_RANK:2 * GLA_GATE_RANK].set(w_gla_a2[0, 1])
    gn = jnp.tile(gla_norm_g[0], GLA_HEADS).reshape(1, GLA_DV)
    o_gla = _gla(lat3(gq), lat3(gk), lat3(gv), lat3(lr), lat3(sog), ctx3(ck), ctx3(cv), ctx3(clr),
                 wa.astype(BF16), b_gla_a2[0], gn)

    wr = jnp.zeros((d, ROUTE_LANES), F32)
    wr = wr.at[:, :N_GROUPS].set(w_group[0]).at[:, EXPERT_LANE0:EXPERT_LANE0 + N_EXPERTS].set(w_expert[0])
    wr_hi = wr.astype(BF16)
    br = jnp.zeros((1, ROUTE_LANES), F32)
    br = br.at[0, :N_GROUPS].set(b_group[0]).at[0, EXPERT_LANE0:EXPERT_LANE0 + N_EXPERTS].set(b_expert[0])
    h, f_lat, route, route_t, counts = _merge_route(
        o_na.reshape(n, NA_WIDTH), o_gla.reshape(n, GLA_DV), sgn, sgg, x.reshape(n, d), mod3, norm_ffn_g[0].reshape(1, d),
        w_na_o[0].astype(BF16), w_gla_o[0].astype(BF16), w_out[0].astype(BF16),
        jnp.concatenate([wr_hi, (wr - wr_hi.astype(F32)).astype(BF16)], axis=1), br, s, tm)

    n_tiles = n // tm
    cnt = counts[:, 0, EXPERT_LANE0:EXPERT_LANE0 + N_EXPERTS].astype(jnp.int32)
    cnt_al = (cnt + SEG_ALIGN - 1) // SEG_ALIGN * SEG_ALIGN
    lstart = jnp.cumsum(cnt_al, axis=1) - cnt_al
    total = jnp.sum(cnt_al, axis=0)
    padded = (total + MOE_ROWS - 1) // MOE_ROWS * MOE_ROWS
    pend = jnp.cumsum(padded)
    pstart = pend - padded
    seg_dst = pstart[None, :] + jnp.cumsum(cnt_al, axis=0) - cnt_al
    assert tm // PIECE_SIZES[-1] <= MAX_BIG_PER_RUN
    assert (2 * tm + N_EXPERTS * (SEG_ALIGN - 1)) // PIECE_SIZES[-1] <= PIECE_SLOTS[-1]
    seg = _piece_table(cnt_al, lstart, seg_dst)
    n_used = (pend[-1] // MOE_ROWS).reshape(1)
    tail = jnp.concatenate([pstart + total, padded - total, n_used, jnp.zeros((2 * N_EXPERTS - 1,), jnp.int32)])
    tail = tail.reshape(1, 1, 4 * N_EXPERTS)
    seg_pad = n_tiles * N_EXPERTS * (SEG_ALIGN - 1)
    n_pad = (2 * n + seg_pad + N_EXPERTS * (MOE_ROWS - SEG_ALIGN) + MOE_ROWS - 1) // MOE_ROWS * MOE_ROWS
    n_blk = n_pad // MOE_ROWS
    rows_l = (2 * tm + N_EXPERTS * (SEG_ALIGN - 1) + MXU_DIM - 1) // MXU_DIM * MXU_DIM
    blk_start = jnp.arange(n_blk, dtype=jnp.int32) * MOE_ROWS
    blk_expert = jnp.minimum(jnp.sum(blk_start[:, None] >= pend[None, :], axis=-1), N_EXPERTS - 1)

    xs = _dispatch(seg, tail, route_t, f_lat, n_pad, rows_l, tm)
    ys = _experts(blk_expert.astype(jnp.int32), n_used.astype(jnp.int32), xs,
                  w_exp_gate[0], w_exp_up[0], w_exp_down[0])
    out = _combine(seg, route, h, mod3, final_norm_g.reshape(1, d), ys, rows_l, s, tm)
    return out.reshape(b, s, d)
```

```python
import functools

import jax
import jax.numpy as jnp
import numpy as np
from jax import lax
from jax.experimental import pallas as pl
from jax.experimental.pallas import tpu as pltpu

F32 = jnp.float32
BF16 = jnp.bfloat16

D_MODEL = 1024
GRID_W = 64
NA_HEADS = 8
NA_HEAD_DIM = 64
NA_WIDTH = NA_HEADS * NA_HEAD_DIM
NA_WIN_H = 8
NA_WIN_W = 16
GLA_HEADS = 4
GLA_DK = D_MODEL // 2
GLA_DV = D_MODEL
GLA_HK = GLA_DK // GLA_HEADS
GLA_HV = GLA_DV // GLA_HEADS
GLA_GATE_RANK = 16
GLA_GATE_NORM = 16.0
GLA_CHUNK = 64
GLA_GROUP = 256
ROPE_BASE = 10000.0
N_GROUPS = 4
EXPERTS_PER_GROUP = 8
N_EXPERTS = N_GROUPS * EXPERTS_PER_GROUP
D_EXPERT = D_MODEL // 2
EPS = 1e-6
NEG_INF = -1e30
LOG2E = 1.4426950408889634

LANES = 128
MXU_DIM = 256
LR_PAD = LANES
ROUTE_LANES = LANES
EXPERT_LANE0 = N_GROUPS
ROUTE_ROWS = 48
MERGE_TILES = 2
COMBINE_TILES = 2
DISPATCH_TILES = 2
MOE_ROWS = 512
SEG_ALIGN = 8
PIECE_SIZES = (8, 16, 24, 32, 40, 48, 56, 64)
PIECE_SLOTS = (32, 32, 32, 32, 32, 32, 32, 24)
MAX_BIG_PER_RUN = 8
PIECE_HEAD = 16
PIECE_TABLE = 512
WAIT_SIZES = (1024, 512, 256, 128, 64, 32, 16, 8)
TAIL_SIZES = (256, 128, 64, 32, 16, 8)
VMEM_LIMIT = 56 * 1024 * 1024

_NT = (((1,), (1,)), ((), ()))
_TN = (((0,), (0,)), ((), ()))


def _params(sem, vmem=VMEM_LIMIT):
    return pltpu.CompilerParams(dimension_semantics=sem, vmem_limit_bytes=vmem)


def _mod_kernel(c_ref, w_ref, b_ref, o_ref):
    c = c_ref[...]
    s = c * jax.nn.sigmoid(c)
    o_ref[...] = jnp.dot(s.astype(BF16), w_ref[...].astype(BF16),
                         preferred_element_type=F32) + b_ref[...]


def _modulation(cs, w_mod, b_mod):
    rows, d = cs.shape
    n = w_mod.shape[1]
    bn = 1024
    return pl.pallas_call(
        _mod_kernel,
        grid=(n // bn,),
        in_specs=[pl.BlockSpec((rows, d), lambda j: (0, 0)),
                  pl.BlockSpec((d, bn), lambda j: (0, j)),
                  pl.BlockSpec((1, bn), lambda j: (0, j))],
        out_specs=pl.BlockSpec((rows, bn), lambda j: (0, j)),
        out_shape=jax.ShapeDtypeStruct((rows, n), F32),
        compiler_params=_params(("arbitrary",)),
        name="mod",
    )(cs, w_mod, b_mod.reshape(1, n))


def _inproj_kernel(specs, use_rope, n_w, n_out, x_ref, mod_ref, g_ref, *rest):
    if use_rope:
        cos_ref, sin_ref = rest[:2]
        rest = rest[2:]
    w_refs = rest[:n_w]
    outs = rest[n_w:n_w + n_out]
    a_scr = rest[n_w + n_out]
    x = x_ref[...]
    ms = jnp.mean(x * x, axis=-1, keepdims=True)
    a = x * lax.rsqrt(ms + EPS) * g_ref[...] * (1.0 + mod_ref[0, 1:2, :]) + mod_ref[0, 0:1, :]
    a_scr[...] = a.astype(BF16)
    for (oi, oc, wi, wc, width, kind) in specs:
        acc = jnp.dot(a_scr[...], w_refs[wi][:, wc:wc + width], preferred_element_type=F32)
        if kind in ("rope", "rope_scaled"):
            if kind == "rope_scaled":
                acc = acc * (GLA_HK ** -0.5)
            lane = lax.broadcasted_iota(jnp.int32, acc.shape, 1)
            rot = jnp.where((lane & 63) < 32,
                            pltpu.roll(acc, width - 32, axis=1),
                            pltpu.roll(acc, 32, axis=1))
            acc = acc * cos_ref[...] + rot * sin_ref[...]
        elif kind == "silu":
            acc = acc * jax.nn.sigmoid(acc)
        elif kind == "sigmoid":
            acc = jax.nn.sigmoid(acc)
        outs[oi][:, oc:oc + width] = acc.astype(outs[oi].dtype)


def _inproj(x2d, mod3, mod_row_fn, g, weights, specs, out_widths, tm, rope=None, name="inproj"):
    n, d = x2d.shape
    use_rope = rope is not None
    in_specs = [pl.BlockSpec((tm, d), lambda i: (i, 0)),
                pl.BlockSpec((1, 6, d), lambda i: (mod_row_fn(i), 0, 0)),
                pl.BlockSpec((1, d), lambda i: (0, 0))]
    args = [x2d, mod3, g.reshape(1, d)]
    if use_rope:
        cos, sin = rope
        nb = cos.shape[0] // tm
        in_specs += [pl.BlockSpec((tm, cos.shape[1]), lambda i: (i % nb, 0)),
                     pl.BlockSpec((tm, cos.shape[1]), lambda i: (i % nb, 0))]
        args += [cos, sin]
    in_specs += [pl.BlockSpec((d, cols), lambda i: (0, 0), pipeline_mode=pl.Buffered(1)) for _, cols in weights]
    args += [w for w, _ in weights]
    out_specs = [pl.BlockSpec((tm, w), lambda i: (i, 0)) for w in out_widths]
    out_shape = [jax.ShapeDtypeStruct((n, w), BF16) for w in out_widths]
    return pl.pallas_call(
        functools.partial(_inproj_kernel, specs, use_rope, len(weights), len(out_widths)),
        grid=(n // tm,),
        in_specs=in_specs,
        out_specs=out_specs,
        out_shape=out_shape,
        scratch_shapes=[pltpu.VMEM((tm, d), BF16)],
        compiler_params=_params(("arbitrary",)),
        name=name,
    )(*args)


def _na_kernel(rows_per_step, n_rows, q_ref, k_ref, v_ref, kc_ref, vc_ref, bias_ref, o_ref, sw_scr, sc_scr):
    j = pl.program_id(1)
    lane = lax.broadcasted_iota(jnp.int32, (GRID_W, LANES), 1)
    first_head = lane < NA_HEAD_DIM
    scale = NA_HEAD_DIM ** -0.5
    win_keys = NA_WIN_H * GRID_W
    pairs = range(NA_HEADS // 2)
    lanes = [slice(p * LANES, (p + 1) * LANES) for p in pairs]
    stack = 2 * GRID_W

    def window(rr):
        r = j * rows_per_step + rr
        rs = jnp.clip(r - NA_WIN_H // 2, 0, n_rows - NA_WIN_H)
        return r - rs, pl.multiple_of(rr * GRID_W, GRID_W), pl.multiple_of(rs * GRID_W, GRID_W)

    def scores(rr, slot):
        _, t0, ks = window(rr)
        for p in pairs:
            qp = q_ref[0, pl.ds(t0, GRID_W), lanes[p]] * scale
            zero = jnp.zeros_like(qp)
            qs = jnp.concatenate([jnp.where(first_head, qp, zero), jnp.where(first_head, zero, qp)], axis=0)
            sw_scr[slot, p * stack:(p + 1) * stack, :] = lax.dot_general(
                qs, k_ref[0, pl.ds(ks, win_keys), lanes[p]], _NT, preferred_element_type=F32)
            sc_scr[slot, p * stack:(p + 1) * stack, :] = lax.dot_general(
                qs, kc_ref[0, :, lanes[p]], _NT, preferred_element_type=F32)

    def attend(rr, slot):
        d, t0, ks = window(rr)
        pws, pcs, dens = [], [], []
        for p in pairs:
            bias = jnp.concatenate([bias_ref[(NA_WIN_H - 1) - d + 2 * i, p] for i in range(NA_WIN_H // 2)], axis=1)
            sw = sw_scr[slot, p * stack:(p + 1) * stack, :] * LOG2E + bias
            sc = sc_scr[slot, p * stack:(p + 1) * stack, :] * LOG2E
            m = jnp.maximum(jnp.max(sw, axis=-1, keepdims=True), jnp.max(sc, axis=-1, keepdims=True))
            pw = jnp.exp2(sw - m)
            pc = jnp.exp2(sc - m)
            dens.append(jnp.sum(pw, axis=-1, keepdims=True) + jnp.sum(pc, axis=-1, keepdims=True))
            pws.append(pw.astype(BF16))
            pcs.append(pc.astype(BF16))
        for p in pairs:
            o2 = (jnp.dot(pws[p], v_ref[0, pl.ds(ks, win_keys), lanes[p]], preferred_element_type=F32)
                  + jnp.dot(pcs[p], vc_ref[0, :, lanes[p]], preferred_element_type=F32)) / dens[p]
            o_ref[0, pl.ds(t0, GRID_W), lanes[p]] = jnp.where(first_head, o2[:GRID_W],
                                                             o2[GRID_W:]).astype(o_ref.dtype)

    scores(0, 0)

    def two_rows(k, carry):
        scores(2 * k + 1, 1)
        attend(2 * k, 0)
        scores(2 * k + 2, 0)
        attend(2 * k + 1, 1)
        return carry

    lax.fori_loop(0, rows_per_step // 2 - 1, two_rows, 0)
    scores(rows_per_step - 1, 1)
    attend(rows_per_step - 2, 0)
    attend(rows_per_step - 1, 1)


def _na_bias_table(rpb):
    n_heads, n_dr, n_dc = rpb.shape
    c = np.arange(GRID_W)
    cstart = np.clip(c - NA_WIN_W // 2, 0, GRID_W - NA_WIN_W)
    kc = np.arange(GRID_W)
    valid = (kc[None, :] >= cstart[:, None]) & (kc[None, :] < cstart[:, None] + NA_WIN_W)
    col_idx = np.clip(kc[None, :] - c[:, None] + NA_WIN_W - 1, 0, n_dc - 1)
    pick = np.zeros((n_dc, GRID_W * GRID_W), np.float32)
    pick[col_idx.reshape(-1), np.arange(GRID_W * GRID_W)] = 1.0
    t = jnp.dot(rpb.reshape(n_heads * n_dr, n_dc).astype(F32), jnp.asarray(pick),
                precision=lax.Precision.HIGHEST).reshape(n_heads, n_dr, GRID_W, GRID_W)
    t = jnp.where(jnp.asarray(valid)[None, None], t * LOG2E, NEG_INF)
    t2 = jnp.concatenate([t[:, :-1], t[:, 1:]], axis=-1)
    return t2.transpose(1, 0, 2, 3).reshape(n_dr - 1, n_heads // 2, 2 * GRID_W, 2 * GRID_W)


def _na_attention(q, k, v, kc, vc, bias, rows_per_step=32):
    b, s, w = q.shape
    n_rows = s // GRID_W
    ctx_len = kc.shape[1]
    blk = rows_per_step * GRID_W
    return pl.pallas_call(
        functools.partial(_na_kernel, rows_per_step, n_rows),
        grid=(b, n_rows // rows_per_step),
        in_specs=[pl.BlockSpec((1, blk, w), lambda bi, j: (bi, j, 0)),
                  pl.BlockSpec((1, s, w), lambda bi, j: (bi, 0, 0)),
                  pl.BlockSpec((1, s, w), lambda bi, j: (bi, 0, 0)),
                  pl.BlockSpec((1, ctx_len, w), lambda bi, j: (bi, 0, 0)),
                  pl.BlockSpec((1, ctx_len, w), lambda bi, j: (bi, 0, 0)),
                  pl.BlockSpec(bias.shape, lambda bi, j: (0, 0, 0, 0), pipeline_mode=pl.Buffered(1))],
        out_specs=pl.BlockSpec((1, blk, w), lambda bi, j: (bi, j, 0)),
        out_shape=jax.ShapeDtypeStruct((b, s, w), BF16),
        scratch_shapes=[pltpu.VMEM((2, NA_HEADS * GRID_W, NA_WIN_H * GRID_W), F32),
                        pltpu.VMEM((2, NA_HEADS * GRID_W, ctx_len), F32)],
        compiler_params=_params(("arbitrary", "arbitrary")),
        name="na_attn",
    )(q, k, v, kc, vc, bias)


def _gla_kernel(n_groups, gq_ref, gk_ref, gv_ref, lr_ref, sog_ref, ck_ref, cv_ref, clr_ref,
                wa_ref, ba_ref, gn_ref, o_ref, st_scr, of_scr, qe_scr, kd_scr, ckd_scr, dec_scr, cdec_scr,
                qk_a):
    c_len = GLA_CHUNK
    grp = GLA_GROUP
    cpg = grp // c_len
    row = lax.broadcasted_iota(jnp.int32, (grp, grp), 0)
    col = lax.broadcasted_iota(jnp.int32, (grp, grp), 1)
    same_chunk = (row // c_len) == (col // c_len)
    masks = (same_chunk & (col <= row), same_chunk & (col >= row))

    dirs = range(2)

    def gates(lr_rows, k_rows):
        zs = [jnp.dot(lr_rows, wa_ref[t], preferred_element_type=F32) + ba_ref[t:t + 1, :] for t in dirs]
        gs = [(jnp.minimum(z, 0.0) * LOG2E - jnp.log2(1.0 + jnp.exp2(jnp.abs(z) * (-LOG2E)))) * (1.0 / GLA_GATE_NORM)
              for z in zs]
        g1s = [g.astype(BF16) for g in gs]
        g2s = [(g - g1.astype(F32)).astype(BF16) for g, g1 in zip(gs, g1s)]
        tris = [masks[t].astype(BF16) for t in dirs]
        bs = [jnp.dot(tris[t], g1s[t], preferred_element_type=F32)
              + jnp.dot(tris[t], g2s[t], preferred_element_type=F32) for t in dirs]
        out = []
        for t in dirs:
            edge = c_len - 1 if t == 0 else 0
            decs = [jnp.exp2(bs[t][j * c_len + edge:j * c_len + edge + 1, :]) for j in range(cpg)]
            ke = k_rows * jnp.exp2(-bs[t])
            kd = (ke * jnp.concatenate([jnp.broadcast_to(v, (c_len, v.shape[1])) for v in decs],
                                       axis=0)).astype(BF16)
            dec = jnp.concatenate(decs + [jnp.ones_like(v) for v in decs], axis=0)
            out.append((bs[t], ke.astype(BF16), kd, dec))
        return out

    for t, (_, _, kd, dec) in enumerate(gates(clr_ref[0], ck_ref[0].astype(F32))):
        ckd_scr[t] = kd
        cdec_scr[t] = dec

    def group_gates(gi, qk_buf):
        rows = pl.ds(pl.multiple_of(gi * grp, grp), grp)
        qf = gq_ref[0, rows, :].astype(F32)
        for t, (b, ke, kd, dec) in enumerate(gates(lr_ref[0, rows, :], gk_ref[0, rows, :].astype(F32))):
            qe = (qf * jnp.exp2(b)).astype(BF16)
            qk_buf[t] = qe
            qk_buf[2 + t] = ke
            qe_scr[t, rows, :] = qe
            kd_scr[t, rows, :] = kd
            dec_scr[t, gi] = dec

    def group_attend(gi, qk_buf):
        rows = pl.ds(pl.multiple_of(gi * grp, grp), grp)
        heads = range(GLA_HEADS)
        kls = [slice(h * GLA_HK, (h + 1) * GLA_HK) for h in heads]
        raw = [[lax.dot_general(qk_buf[t, :, kls[h]], qk_buf[2 + t, :, kls[h]], _NT, preferred_element_type=F32)
                for t in dirs] for h in heads]
        atts = [(jnp.where(masks[0], raw[h][0], 0.0) + jnp.where(masks[1], raw[h][1], 0.0)).astype(BF16)
                for h in heads]
        for h in heads:
            vl = slice(h * GLA_HV, (h + 1) * GLA_HV)
            of_scr[rows, vl] = jnp.dot(atts[h], gv_ref[0, rows, vl], preferred_element_type=F32)

    def state_step(t, h, v_rows, kd_rows, dec_row):
        kl = slice(h * GLA_HK, (h + 1) * GLA_HK)
        upd = lax.dot_general(v_rows, kd_rows[:, kl], _TN, preferred_element_type=F32)
        st_scr[t, h] = st_scr[t, h] * dec_row[:, kl] + upd

    def chunk_order(t):
        return range(cpg) if t == 0 else range(cpg - 1, -1, -1)

    st_scr[...] = jnp.zeros_like(st_scr)
    for t in dirs:
        for j in chunk_order(t):
            rows = slice(j * c_len, (j + 1) * c_len)
            for h in range(GLA_HEADS):
                state_step(t, h, cv_ref[0, rows, h * GLA_HV:(h + 1) * GLA_HV],
                           ckd_scr[t, rows, :], cdec_scr[t, j:j + 1, :])

    def chunk_step(t, gi, j, finish):
        rows = pl.ds(pl.multiple_of(gi * grp + j * c_len, c_len), c_len)
        qe_rows = qe_scr[t, rows, :]
        kd_rows = kd_scr[t, rows, :]
        dec_row = dec_scr[t, gi, j:j + 1, :]
        for h in range(GLA_HEADS):
            kl = slice(h * GLA_HK, (h + 1) * GLA_HK)
            vl = slice(h * GLA_HV, (h + 1) * GLA_HV)
            o_t = of_scr[rows, vl] + lax.dot_general(qe_rows[:, kl], st_scr[t, h].astype(BF16), _NT,
                                                     preferred_element_type=F32)
            if finish:
                ms = jnp.mean(o_t * o_t, axis=-1, keepdims=True)
                o_n = o_t * lax.rsqrt(ms + EPS) * gn_ref[:, vl] * sog_ref[0, rows, vl].astype(F32)
                o_ref[0, rows, vl] = o_n.astype(o_ref.dtype)
            else:
                of_scr[rows, vl] = o_t
            state_step(t, h, gv_ref[0, rows, vl], kd_rows, dec_row)

    def scan_steps(i, finish):
        steps = []
        for jf, jb in zip(chunk_order(0), chunk_order(1)):
            steps.append(functools.partial(chunk_step, 0, i, jf, finish))
            steps.append(functools.partial(chunk_step, 1, n_groups - 1 - i, jb, finish))
        return steps

    half = n_groups // 2
    for g in (0, n_groups - 1):
        group_gates(g, qk_a)
        group_attend(g, qk_a)

    def prepare_and_scan(i, carry):
        steps = scan_steps(i, False)
        per_phase = len(steps) // 4
        for n_phase, phase in enumerate((functools.partial(group_gates, i + 1, qk_a),
                                         functools.partial(group_attend, i + 1, qk_a),
                                         functools.partial(group_gates, n_groups - 2 - i, qk_a),
                                         functools.partial(group_attend, n_groups - 2 - i, qk_a))):
            phase()
            for step in steps[n_phase * per_phase:(n_phase + 1) * per_phase]:
                step()
        return carry

    def scan_only(finish):
        def body(i, carry):
            for step in scan_steps(i, finish):
                step()
            return carry
        return body

    lax.fori_loop(0, half - 1, prepare_and_scan, 0)
    lax.fori_loop(half - 1, half, scan_only(False), 0)
    lax.fori_loop(half, n_groups, scan_only(True), 0)


def _gla(gq, gk, gv, lr, sog, ck, cv, clr, wa, ba, gn):
    b, s, _ = gq.shape
    ctx_len = ck.shape[1]
    assert ctx_len == GLA_GROUP and s % (2 * GLA_GROUP) == 0
    n_groups = s // GLA_GROUP
    cpg = GLA_GROUP // GLA_CHUNK

    def per_batch(t, w):
        return pl.BlockSpec((1, t, w), lambda bi: (bi, 0, 0))

    return pl.pallas_call(
        functools.partial(_gla_kernel, n_groups),
        grid=(b,),
        in_specs=[per_batch(s, GLA_DK), per_batch(s, GLA_DK), per_batch(s, GLA_DV),
                  per_batch(s, LR_PAD), per_batch(s, GLA_DV),
                  per_batch(ctx_len, GLA_DK), per_batch(ctx_len, GLA_DV), per_batch(ctx_len, LR_PAD),
                  pl.BlockSpec(wa.shape, lambda bi: (0, 0, 0)),
                  pl.BlockSpec(ba.shape, lambda bi: (0, 0)),
                  pl.BlockSpec(gn.shape, lambda bi: (0, 0))],
        out_specs=per_batch(s, GLA_DV),
        out_shape=jax.ShapeDtypeStruct((b, s, GLA_DV), BF16),
        scratch_shapes=[pltpu.VMEM((2, GLA_HEADS, GLA_HV, GLA_HK), F32),
                        pltpu.VMEM((s, GLA_DV), F32),
                        pltpu.VMEM((2, s, GLA_DK), BF16),
                        pltpu.VMEM((2, s, GLA_DK), BF16),
                        pltpu.VMEM((2, ctx_len, GLA_DK), BF16),
                        pltpu.VMEM((2, n_groups, 2 * cpg, GLA_DK), F32),
                        pltpu.VMEM((2, 2 * cpg, GLA_DK), F32),
                        pltpu.VMEM((4, GLA_GROUP, GLA_DK), BF16)],
        compiler_params=_params(("arbitrary",), vmem=60 * 1024 * 1024),
        name="gla",
    )(gq, gk, gv, lr, sog, ck, cv, clr, wa, ba, gn)


def _merge_kernel(ona_ref, ogla_ref, sgn_ref, sgg_ref, x_ref, mod_ref, gffn_ref,
                  wna_ref, wgla_ref, wout_ref, wr_ref, br_ref,
                  h_ref, f_ref, route_ref, route_t_ref, cnt_ref):
    rows = x_ref.shape[0] // MERGE_TILES

    def tile_rows(u):
        return slice(u * rows, (u + 1) * rows)

    def branches(u):
        rs = tile_rows(u)
        return (sgn_ref[rs, :].astype(F32) * jnp.dot(ona_ref[rs, :], wna_ref[...], preferred_element_type=F32)
                + sgg_ref[rs, :].astype(F32) * jnp.dot(ogla_ref[rs, :], wgla_ref[...], preferred_element_type=F32))

    def project(y1):
        return jnp.dot(y1.astype(BF16), wout_ref[...], preferred_element_type=F32)

    def residual(u, y):
        rs = tile_rows(u)
        h = x_ref[rs, :] + mod_ref[0, 2:3, :] * y
        h_ref[rs, :] = h
        ms = jnp.mean(h * h, axis=-1, keepdims=True)
        f = h * lax.rsqrt(ms + EPS) * gffn_ref[...] * (1.0 + mod_ref[0, 4:5, :]) + mod_ref[0, 3:4, :]
        f_ref[rs, :] = f.astype(f_ref.dtype)
        return f

    def router(f):
        f_hi = f.astype(BF16)
        f_lo = (f - f_hi.astype(F32)).astype(BF16)
        both = jnp.dot(f_hi, wr_ref[...], preferred_element_type=F32)
        return (both[:, :ROUTE_LANES] + both[:, ROUTE_LANES:]
                + jnp.dot(f_lo, wr_ref[:, :ROUTE_LANES], preferred_element_type=F32)) + br_ref[...]

    y_a = project(branches(0))
    y1_b = branches(1)
    f_a = residual(0, y_a)
    y_b = project(y1_b)
    logits_a = router(f_a)
    f_b = residual(1, y_b)
    _route_tile(0, tile_rows(0), logits_a, route_ref, route_t_ref, cnt_ref)
    _route_tile(1, tile_rows(1), router(f_b), route_ref, route_t_ref, cnt_ref)


def _route_tile(u, rs, logits, route_ref, route_t_ref, cnt_ref):
    rows = logits.shape[0]
    lt = logits.T[0:ROUTE_ROWS, :]
    er = lax.broadcasted_iota(jnp.int32, (ROUTE_ROWS, rows), 0)
    big = jnp.int32(ROUTE_ROWS)

    def first_argmax(vals):
        m = jnp.max(vals, axis=0, keepdims=True)
        idx = jnp.min(jnp.where(vals == m, er, big), axis=0, keepdims=True)
        return m, idx

    is_grp = er < N_GROUPS
    lg = jnp.where(is_grp, lt, -jnp.inf)
    mg, grp = first_argmax(lg)
    p_grp = 1.0 / jnp.sum(jnp.where(is_grp, jnp.exp(lg - mg), 0.0), axis=0, keepdims=True)
    lo = EXPERT_LANE0 + grp * EXPERTS_PER_GROUP
    in_grp = (er >= lo) & (er < lo + EXPERTS_PER_GROUP)
    le = jnp.where(in_grp, lt, -jnp.inf)
    m1, i1 = first_argmax(le)
    m2, i2 = first_argmax(jnp.where(er == i1, -jnp.inf, le))
    t = jnp.exp(m2 - m1)
    w1 = p_grp / (1.0 + t)
    w2 = p_grp * t / (1.0 + t)

    hot1 = er == i1
    hot2 = er == i2
    onehot = (hot1 | hot2).astype(BF16)
    s_i = lax.broadcasted_iota(jnp.int32, (rows, rows), 0)
    t_i = lax.broadcasted_iota(jnp.int32, (rows, rows), 1)
    rank = jnp.dot(onehot, (s_i < t_i).astype(BF16), preferred_element_type=F32)
    cnt = jnp.sum(onehot.astype(F32), axis=1, keepdims=True)
    cnt_al = jnp.floor((cnt + (SEG_ALIGN - 1.0)) * (1.0 / SEG_ALIGN)) * SEG_ALIGN
    e_i = lax.broadcasted_iota(jnp.int32, (ROUTE_ROWS, ROUTE_ROWS), 0)
    e_j = lax.broadcasted_iota(jnp.int32, (ROUTE_ROWS, ROUTE_ROWS), 1)
    lstart = jnp.dot((e_j < e_i).astype(BF16), jnp.broadcast_to(cnt_al, (ROUTE_ROWS, LANES)).astype(BF16),
                     preferred_element_type=F32)[:, 0:1]
    pos = rank + lstart
    pos1 = jnp.sum(jnp.where(hot1, pos, 0.0), axis=0, keepdims=True)
    pos2 = jnp.sum(jnp.where(hot2, pos, 0.0), axis=0, keepdims=True)

    e1 = (i1 - EXPERT_LANE0).astype(F32)
    e2 = (i2 - EXPERT_LANE0).astype(F32)
    fields = jnp.concatenate([e1, e2, w1, w2, pos1, pos2, jnp.zeros((2, rows), F32)], axis=0)
    route_t_ref[u] = fields
    route_ref[rs, :] = jnp.concatenate([fields, jnp.zeros((ROUTE_LANES - 8, rows), F32)], axis=0).T
    cnt_rows = jnp.concatenate([jnp.broadcast_to(cnt, (ROUTE_ROWS, ROUTE_LANES)),
                                jnp.zeros((ROUTE_LANES - ROUTE_ROWS, ROUTE_LANES), F32)], axis=0)
    cnt_ref[u] = cnt_rows.T[0:1, :]


def _merge_route(ona, ogla, sgn, sgg, x2d, mod3, gffn, wna, wgla, wout, wr, br, tokens_per_batch, tm):
    n, d = x2d.shape
    step_rows = MERGE_TILES * tm
    assert tokens_per_batch % step_rows == 0
    per_b = tokens_per_batch // step_rows

    def tile(w):
        return pl.BlockSpec((step_rows, w), lambda i: (i, 0))

    def whole(a):
        return pl.BlockSpec(a.shape, lambda i: (0,) * a.ndim)

    return pl.pallas_call(
        _merge_kernel,
        grid=(n // step_rows,),
        in_specs=[tile(NA_WIDTH), tile(GLA_DV), tile(d), tile(d), tile(d),
                  pl.BlockSpec((1, 6, d), lambda i: (i // per_b, 0, 0)),
                  whole(gffn), whole(wna), whole(wgla), whole(wout), whole(wr), whole(br)],
        out_specs=[tile(d), tile(d), tile(ROUTE_LANES),
                   pl.BlockSpec((MERGE_TILES, 8, tm), lambda i: (i, 0, 0)),
                   pl.BlockSpec((MERGE_TILES, 1, ROUTE_LANES), lambda i: (i, 0, 0))],
        out_shape=[jax.ShapeDtypeStruct((n, d), F32),
                   jax.ShapeDtypeStruct((n, d), BF16),
                   jax.ShapeDtypeStruct((n, ROUTE_LANES), F32),
                   jax.ShapeDtypeStruct((n // tm, 8, tm), F32),
                   jax.ShapeDtypeStruct((n // tm, 1, ROUTE_LANES), F32)],
        compiler_params=_params(("arbitrary",)),
        name="merge_route",
    )(ona, ogla, sgn, sgg, x2d, mod3, gffn, wna, wgla, wout, wr, br)


HI_MASK = -65536


def _pack_rows(a):
    half = a.shape[1] // 2
    hi = lax.bitcast_convert_type(a[:, :half], jnp.int32)
    lo = lax.shift_right_logical(lax.bitcast_convert_type(a[:, half:], jnp.int32), 16)
    return hi | lo


def _unpack_rows(w):
    hi = lax.bitcast_convert_type(w & HI_MASK, F32).astype(BF16)
    lo = lax.bitcast_convert_type(lax.shift_left(w, 16), F32).astype(BF16)
    return hi, lo


def _piece_table(cnt_al, lstart, seg_dst):
    n_tiles = cnt_al.shape[0]

    def compact(mask, local, glob, slots):
        pos = jnp.cumsum(mask.astype(jnp.int32), axis=1) - 1
        hot = mask[:, :, None] & (pos[:, :, None] == jnp.arange(slots, dtype=jnp.int32))
        pick = lambda v: jnp.sum(jnp.where(hot, v[:, :, None], 0), axis=1)
        return jnp.sum(mask, axis=1).astype(jnp.int32), pick(local), pick(glob)

    counts, lists = [], []
    big = PIECE_SIZES[-1]
    for size, slots in zip(PIECE_SIZES[:-1], PIECE_SLOTS[:-1]):
        n, lo, gl = compact((cnt_al & (big - 1)) == size, lstart, seg_dst, slots)
        counts.append(n)
        lists += [lo, gl]
    k = jnp.arange(MAX_BIG_PER_RUN, dtype=jnp.int32)
    off = (cnt_al & (big - 1))[:, :, None] + big * k
    mask = k < (cnt_al // big)[:, :, None]
    flat = lambda a: a.reshape(n_tiles, -1)
    n, lo, gl = compact(flat(mask), flat(lstart[:, :, None] + off), flat(seg_dst[:, :, None] + off), PIECE_SLOTS[-1])
    counts.append(n)
    lists += [lo, gl]
    head = jnp.stack(counts + [jnp.sum(cnt_al, axis=1)], axis=1)
    head = jnp.pad(head, ((0, 0), (0, PIECE_HEAD - head.shape[1])))
    table = jnp.concatenate([head] + lists, axis=1)
    table = jnp.pad(table, ((0, 0), (0, PIECE_TABLE - table.shape[1])))
    return table.reshape(n_tiles, 1, PIECE_TABLE).astype(jnp.int32)


def _segment_copies(seg_ref, make_copy, action, tile=0):
    if action == "wait":
        total = seg_ref[tile, 0, len(PIECE_SIZES)]
        for size in WAIT_SIZES:
            @pl.when((total & size) != 0)
            def _():
                make_copy(0, 0, size).wait()
        return

    base = PIECE_HEAD
    for c, (size, slots) in enumerate(zip(PIECE_SIZES, PIECE_SLOTS)):
        def piece(k, carry, size=size, lo=base, gl=base + slots, priority=c % 2):
            make_copy(pl.multiple_of(seg_ref[tile, 0, lo + k], SEG_ALIGN),
                      pl.multiple_of(seg_ref[tile, 0, gl + k], SEG_ALIGN), size).start(priority=priority)
            return carry

        lax.fori_loop(0, seg_ref[tile, 0, c], piece, 0)
        base += 2 * slots


def _dispatch_kernel(seg_ref, seg_prev_ref, tail_ref, lpos_ref, f_ref, xs_ref, sbuf, zbuf, sems):
    i = pl.program_id(0)
    last = pl.num_programs(0) - 1
    tm = f_ref.shape[0] // DISPATCH_TILES
    rows_l = sbuf.shape[1]
    slot = i % 2
    half = f_ref.shape[1] // 2
    tiles = range(DISPATCH_TILES)
    row = lax.broadcasted_iota(jnp.int32, (rows_l, tm), 0)

    def copier(s, u):
        def copy(local, glob, size):
            return pltpu.make_async_copy(sbuf.at[DISPATCH_TILES * s + u, pl.ds(local, size), :],
                                         xs_ref.at[pl.ds(glob, size), :], sems.at[DISPATCH_TILES * s + u])
        return copy

    def permute(u):
        fields = lpos_ref[u]
        lp = fields.astype(jnp.int32)
        first = row == lp[4:5, :]
        second = row == lp[5:6, :]
        sorted_rows = jnp.dot((first | second).astype(BF16), f_ref[u * tm:(u + 1) * tm, :],
                              preferred_element_type=F32)
        w_sorted = jnp.sum(jnp.where(first, fields[2:3, :], 0.0) + jnp.where(second, fields[3:4, :], 0.0),
                           axis=1, keepdims=True)
        return sorted_rows, w_sorted

    def emit(u, sorted_rows, w_sorted):
        buf = DISPATCH_TILES * slot + u
        sbuf[buf, :, :half] = _pack_rows(sorted_rows)
        sbuf[buf, :, half:] = lax.bitcast_convert_type(jnp.broadcast_to(w_sorted, (rows_l, LANES)), jnp.int32)
        _segment_copies(seg_ref, copier(slot, u), "start", u)

    perm_a = permute(0)
    perm_b = permute(1)
    emit(0, *perm_a)
    emit(1, *perm_b)

    @pl.when(i > 0)
    def _():
        for u in tiles:
            _segment_copies(seg_prev_ref, copier(1 - slot, u), "wait", u)

    @pl.when(i == last)
    def _():
        for u in tiles:
            _segment_copies(seg_ref, copier(slot, u), "wait", u)
        sem = sems.at[DISPATCH_TILES * slot]
        zbuf[...] = jnp.zeros_like(zbuf)

        def tail(action):
            def body(e, carry):
                start = tail_ref[0, 0, e]
                length = tail_ref[0, 0, N_EXPERTS + e]
                off = jnp.int32(0)
                for size in TAIL_SIZES:
                    @pl.when((length & size) != 0)
                    def _():
                        cp = pltpu.make_async_copy(zbuf.at[pl.ds(0, size), :],
                                                   xs_ref.at[pl.ds(pl.multiple_of(start + off, SEG_ALIGN), size), :],
                                                   sem)
                        getattr(cp, action)()
                    off = off + (length & size)
                return carry
            lax.fori_loop(0, N_EXPERTS, body, 0)

        tail("start")
        tail("wait")

        def unused(action):
            def body(blk, carry):
                cp = pltpu.make_async_copy(zbuf, xs_ref.at[pl.ds(pl.multiple_of(blk * MOE_ROWS, MOE_ROWS),
                                                                  MOE_ROWS), :], sem)
                getattr(cp, action)()
                return carry
            lax.fori_loop(tail_ref[0, 0, 2 * N_EXPERTS], xs_ref.shape[0] // MOE_ROWS, body, 0)

        unused("start")
        unused("wait")


def _dispatch(seg, tail, lpos_rows, f2d, n_pad, rows_l, tm):
    n, d = f2d.shape
    step_rows = DISPATCH_TILES * tm
    assert n % step_rows == 0
    return pl.pallas_call(
        _dispatch_kernel,
        grid=(n // step_rows,),
        in_specs=[pl.BlockSpec((DISPATCH_TILES, 1, seg.shape[2]), lambda i: (i, 0, 0), memory_space=pltpu.SMEM),
                  pl.BlockSpec((DISPATCH_TILES, 1, seg.shape[2]), lambda i: (jnp.maximum(i - 1, 0), 0, 0),
                               memory_space=pltpu.SMEM),
                  pl.BlockSpec(tail.shape, lambda i: (0, 0, 0), memory_space=pltpu.SMEM),
                  pl.BlockSpec((DISPATCH_TILES, 8, tm), lambda i: (i, 0, 0)),
                  pl.BlockSpec((step_rows, d), lambda i: (i, 0))],
        out_specs=pl.BlockSpec(memory_space=pl.ANY),
        out_shape=jax.ShapeDtypeStruct((n_pad, d // 2 + LANES), jnp.int32),
        scratch_shapes=[pltpu.VMEM((2 * DISPATCH_TILES, rows_l, d // 2 + LANES), jnp.int32),
                        pltpu.VMEM((MOE_ROWS, d // 2 + LANES), jnp.int32),
                        pltpu.SemaphoreType.DMA((2 * DISPATCH_TILES,))],
        compiler_params=_params(("arbitrary",)),
        name="dispatch",
    )(seg, seg, tail, lpos_rows, f2d)


def _expert_kernel(be_ref, nu_ref, x_ref, wg_ref, wu_ref, wd_ref, y_ref, wgu_scr, wd_scr):
    i = pl.program_id(0)
    prev = be_ref[jnp.maximum(i - 1, 0)]

    @pl.when((i == 0) | (be_ref[i] != prev))
    def _():
        wgu_scr[:, :D_EXPERT] = wg_ref[0].astype(BF16)
        wgu_scr[:, D_EXPERT:] = wu_ref[0].astype(BF16)
        wd_scr[...] = wd_ref[0].astype(BF16)

    @pl.when(i < nu_ref[0])
    def _():
        half = y_ref.shape[1]
        x = jnp.concatenate(_unpack_rows(x_ref[:, :half]), axis=1)
        gu = jnp.dot(x, wgu_scr[...], preferred_element_type=F32)
        gate = gu[:, :D_EXPERT]
        hdn = gate * jax.nn.sigmoid(gate) * gu[:, D_EXPERT:]
        y = jnp.dot(hdn.astype(BF16), wd_scr[...], preferred_element_type=F32)
        y = y * lax.bitcast_convert_type(x_ref[:, half:half + 1], F32)
        y_ref[...] = _pack_rows(y.astype(BF16).astype(F32))

    @pl.when(i >= nu_ref[0])
    def _():
        y_ref[...] = jnp.zeros_like(y_ref)


def _experts(blk_expert, n_used, xs, wg, wu, wd):
    n_pad, row_words = xs.shape
    half = row_words - LANES
    d = 2 * half
    n_blk = n_pad // MOE_ROWS

    def used_block(i, be, nu):
        return (jnp.maximum(jnp.minimum(i, nu[0] - 1), 0), 0)

    grid_spec = pltpu.PrefetchScalarGridSpec(
        num_scalar_prefetch=2,
        grid=(n_blk,),
        in_specs=[pl.BlockSpec((MOE_ROWS, row_words), used_block),
                  pl.BlockSpec((1, d, D_EXPERT), lambda i, be, nu: (be[i], 0, 0)),
                  pl.BlockSpec((1, d, D_EXPERT), lambda i, be, nu: (be[i], 0, 0)),
                  pl.BlockSpec((1, D_EXPERT, d), lambda i, be, nu: (be[i], 0, 0))],
        out_specs=pl.BlockSpec((MOE_ROWS, half), lambda i, be, nu: (i, 0)),
        scratch_shapes=[pltpu.VMEM((d, 2 * D_EXPERT), BF16),
                        pltpu.VMEM((D_EXPERT, d), BF16)],
    )
    return pl.pallas_call(
        _expert_kernel,
        grid_spec=grid_spec,
        out_shape=jax.ShapeDtypeStruct((n_pad, half), jnp.int32),
        compiler_params=_params(("arbitrary",)),
        name="experts",
    )(blk_expert, n_used, xs, wg, wu, wd)


def _combine_kernel(seg_ref, seg_next_ref, col_ref, h_ref, mod_ref, fg_ref, ys_ref, o_ref, ybuf, sems):
    i = pl.program_id(0)
    tm = h_ref.shape[0] // COMBINE_TILES
    rows_l = ybuf.shape[1]
    slot = i % 2
    tiles = range(COMBINE_TILES)

    def copier(s, u):
        def copy(local, glob, size):
            return pltpu.make_async_copy(ys_ref.at[pl.ds(glob, size), :],
                                         ybuf.at[COMBINE_TILES * s + u, pl.ds(local, size), :],
                                         sems.at[COMBINE_TILES * s + u])
        return copy

    @pl.when(i == 0)
    def _():
        ybuf[...] = jnp.zeros_like(ybuf)
        for u in tiles:
            _segment_copies(seg_ref, copier(slot, u), "start", u)

    @pl.when(i + 1 < pl.num_programs(0))
    def _():
        for u in tiles:
            _segment_copies(seg_next_ref, copier(1 - slot, u), "start", u)

    def gather(u):
        _segment_copies(seg_ref, copier(slot, u), "wait", u)
        y_hi, y_lo = _unpack_rows(ybuf[COMBINE_TILES * slot + u])
        y_all = jnp.concatenate([y_hi, y_lo], axis=1)
        col = lax.broadcasted_iota(jnp.int32, (tm, rows_l), 1)
        info = col_ref[u * tm:(u + 1) * tm, :]
        pick = (col == info[:, 4:5].astype(jnp.int32)) | (col == info[:, 5:6].astype(jnp.int32))
        return jnp.dot(pick.astype(BF16), y_all, preferred_element_type=F32)

    def finish(u, moe):
        rs = slice(u * tm, (u + 1) * tm)
        h = h_ref[rs, :] + mod_ref[0, 5:6, :] * moe
        ms = jnp.mean(h * h, axis=-1, keepdims=True)
        o_ref[rs, :] = h * lax.rsqrt(ms + EPS) * fg_ref[...]

    moe_a = gather(0)
    moe_b = gather(1)
    finish(0, moe_a)
    finish(1, moe_b)


def _combine(seg, colinfo, h2d, mod3, fg, ys, rows_l, tokens_per_batch, tm):
    n, d = h2d.shape
    step_rows = COMBINE_TILES * tm
    assert tokens_per_batch % step_rows == 0
    per_b = tokens_per_batch // step_rows
    n_steps = n // step_rows
    return pl.pallas_call(
        _combine_kernel,
        grid=(n_steps,),
        in_specs=[pl.BlockSpec((COMBINE_TILES, 1, seg.shape[2]), lambda i: (i, 0, 0), memory_space=pltpu.SMEM),
                  pl.BlockSpec((COMBINE_TILES, 1, seg.shape[2]), lambda i: (jnp.minimum(i + 1, n_steps - 1), 0, 0),
                               memory_space=pltpu.SMEM),
                  pl.BlockSpec((step_rows, colinfo.shape[1]), lambda i: (i, 0)),
                  pl.BlockSpec((step_rows, d), lambda i: (i, 0)),
                  pl.BlockSpec((1, 6, d), lambda i: (i // per_b, 0, 0)),
                  pl.BlockSpec((1, d), lambda i: (0, 0)),
                  pl.BlockSpec(memory_space=pl.ANY)],
        out_specs=pl.BlockSpec((step_rows, d), lambda i: (i, 0)),
        out_shape=jax.ShapeDtypeStruct((n, d), F32),
        scratch_shapes=[pltpu.VMEM((2 * COMBINE_TILES, rows_l, d // 2), jnp.int32),
                        pltpu.SemaphoreType.DMA((2 * COMBINE_TILES,))],
        compiler_params=_params(("arbitrary",)),
        name="combine",
    )(seg, seg, colinfo, h2d, mod3, fg, ys)


def _rope_tables(s):
    t = np.arange(s)
    pos_r = (t // GRID_W).astype(np.float32)
    pos_c = (t % GRID_W).astype(np.float32)
    nf = GLA_HK // 4
    inv = (np.float32(ROPE_BASE) ** (-np.arange(nf, dtype=np.float32) / np.float32(nf))).astype(np.float32)
    ang_r = pos_r[:, None] * inv
    ang_c = pos_c[:, None] * inv

    def half(ang):
        return (np.concatenate([np.cos(ang), np.cos(ang)], axis=-1),
                np.concatenate([-np.sin(ang), np.sin(ang)], axis=-1))

    cr, sr = half(ang_r)
    cc, sc = half(ang_c)
    cos = np.tile(np.concatenate([cr, cc], axis=-1), (1, GLA_HEADS)).astype(np.float32)
    sin = np.tile(np.concatenate([sr, sc], axis=-1), (1, GLA_HEADS)).astype(np.float32)
    return jnp.asarray(cos), jnp.asarray(sin)


def kernel(x, c, ctx, c_ctx, w_mod, b_mod, norm_attn_g, norm_ffn_g, w_in, w_gla_a2, b_gla_a2, gla_norm_g, na_rpb, w_na_o, w_gla_o, w_out, w_group, b_group, w_expert, b_expert, w_exp_gate, w_exp_up, w_exp_down, final_norm_g):
    b, s, d = x.shape
    ctx_len = ctx.shape[1]
    n = b * s
    assert w_mod.shape[0] == 1, "single-layer block"
    assert s % (GRID_W * NA_WIN_H) == 0 and ctx_len % GLA_CHUNK == 0

    mod_rows = -(-(b + 1) // 8) * 8
    cs = jnp.zeros((mod_rows, d), F32).at[:b].set(c).at[b].set(c_ctx)
    mod3 = _modulation(cs, w_mod[0], b_mod[0]).reshape(mod_rows, 6, d)

    o_q, o_k, o_v = 0, NA_WIDTH, 2 * NA_WIDTH
    o_gq = 3 * NA_WIDTH
    o_gk = o_gq + GLA_DK
    o_gv = o_gk + GLA_DK
    o_og = o_gv + GLA_DV
    o_lr = o_og + GLA_DV
    o_mn = o_lr + 2 * GLA_GATE_RANK
    w_bf = w_in[0].astype(BF16)
    w_a = (w_bf, o_lr)
    w_g = (w_bf[:, o_mn:], 2 * d)
    w_lr = (jnp.pad(w_bf[:, o_lr:o_mn], ((0, 0), (0, LR_PAD - 2 * GLA_GATE_RANK))), LR_PAD)
    cw = 512
    lat_specs = [(0, 0, 0, o_q, cw, "plain"), (1, 0, 0, o_k, cw, "plain"), (2, 0, 0, o_v, cw, "plain"),
                 (3, 0, 0, o_gq, cw, "rope_scaled"), (4, 0, 0, o_gk, cw, "rope"),
                 (5, 0, 0, o_gv, cw, "plain"), (5, cw, 0, o_gv + cw, cw, "plain"),
                 (6, 0, 0, o_og, cw, "silu"), (6, cw, 0, o_og + cw, cw, "silu"),
                 (7, 0, 1, 0, cw, "sigmoid"), (7, cw, 1, cw, cw, "sigmoid"),
                 (8, 0, 1, d, cw, "sigmoid"), (8, cw, 1, d + cw, cw, "sigmoid"),
                 (9, 0, 2, 0, LR_PAD, "plain")]
    lat_widths = [NA_WIDTH, NA_WIDTH, NA_WIDTH, GLA_DK, GLA_DK, GLA_DV, GLA_DV, d, d, LR_PAD]
    tm = 512
    per_b = s // tm
    q_na, k_na, v_na, gq, gk, gv, sog, sgn, sgg, lr = _inproj(
        x.reshape(n, d), mod3, lambda i: i // per_b, norm_attn_g[0], [w_a, w_g, w_lr], lat_specs, lat_widths, tm,
        rope=_rope_tables(s), name="inproj_lat")

    ctx_specs = [(0, 0, 0, o_k, cw, "plain"), (1, 0, 0, o_v, cw, "plain"), (2, 0, 0, o_gk, cw, "plain"),
                 (3, 0, 0, o_gv, cw, "plain"), (3, cw, 0, o_gv + cw, cw, "plain"),
                 (4, 0, 1, 0, LR_PAD, "plain")]
    ctx_widths = [NA_WIDTH, NA_WIDTH, GLA_DK, GLA_DV, LR_PAD]
    kc_na, vc_na, ck, cv, clr = _inproj(
        ctx.reshape(b * ctx_len, d), mod3, lambda i: b, norm_attn_g[0], [w_a, w_lr], ctx_specs, ctx_widths,
        tm if (b * ctx_len) % tm == 0 else ctx_len, name="inproj_ctx")

    def lat3(a):
        return a.reshape(b, s, a.shape[-1])

    def ctx3(a):
        return a.reshape(b, ctx_len, a.shape[-1])

    o_na = _na_attention(lat3(q_na), lat3(k_na), lat3(v_na), ctx3(kc_na), ctx3(vc_na),
                         _na_bias_table(na_rpb[0]))

    wa = jnp.zeros((2, LR_PAD, GLA_DK), F32)
    wa = wa.at[0, :GLA_GATE_RANK].set(w_gla_a2[0, 0]).at[1, GLA_GATE_RANK:2 * GLA_GATE_RANK].set(w_gla_a2[0, 1])
    gn = jnp.tile(gla_norm_g[0], GLA_HEADS).reshape(1, GLA_DV)
    o_gla = _gla(lat3(gq), lat3(gk), lat3(gv), lat3(lr), lat3(sog), ctx3(ck), ctx3(cv), ctx3(clr),
                 wa.astype(BF16), b_gla_a2[0], gn)

    wr = jnp.zeros((d, ROUTE_LANES), F32)
    wr = wr.at[:, :N_GROUPS].set(w_group[0]).at[:, EXPERT_LANE0:EXPERT_LANE0 + N_EXPERTS].set(w_expert[0])
    wr_hi = wr.astype(BF16)
    br = jnp.zeros((1, ROUTE_LANES), F32)
    br = br.at[0, :N_GROUPS].set(b_group[0]).at[0, EXPERT_LANE0:EXPERT_LANE0 + N_EXPERTS].set(b_expert[0])
    h, f_lat, route, route_t, counts = _merge_route(
        o_na.reshape(n, NA_WIDTH), o_gla.reshape(n, GLA_DV), sgn, sgg, x.reshape(n, d), mod3, norm_ffn_g[0].reshape(1, d),
        w_na_o[0].astype(BF16), w_gla_o[0].astype(BF16), w_out[0].astype(BF16),
        jnp.concatenate([wr_hi, (wr - wr_hi.astype(F32)).astype(BF16)], axis=1), br, s, tm)

    n_tiles = n // tm
    cnt = counts[:, 0, EXPERT_LANE0:EXPERT_LANE0 + N_EXPERTS].astype(jnp.int32)
    cnt_al = (cnt + SEG_ALIGN - 1) // SEG_ALIGN * SEG_ALIGN
    lstart = jnp.cumsum(cnt_al, axis=1) - cnt_al
    total = jnp.sum(cnt_al, axis=0)
    padded = (total + MOE_ROWS - 1) // MOE_ROWS * MOE_ROWS
    pend = jnp.cumsum(padded)
    pstart = pend - padded
    seg_dst = pstart[None, :] + jnp.cumsum(cnt_al, axis=0) - cnt_al
    assert tm // PIECE_SIZES[-1] <= MAX_BIG_PER_RUN
    assert (2 * tm + N_EXPERTS * (SEG_ALIGN - 1)) // PIECE_SIZES[-1] <= PIECE_SLOTS[-1]
    seg = _piece_table(cnt_al, lstart, seg_dst)
    n_used = (pend[-1] // MOE_ROWS).reshape(1)
    tail = jnp.concatenate([pstart + total, padded - total, n_used, jnp.zeros((2 * N_EXPERTS - 1,), jnp.int32)])
    tail = tail.reshape(1, 1, 4 * N_EXPERTS)
    seg_pad = n_tiles * N_EXPERTS * (SEG_ALIGN - 1)
    n_pad = (2 * n + seg_pad + N_EXPERTS * (MOE_ROWS - SEG_ALIGN) + MOE_ROWS - 1) // MOE_ROWS * MOE_ROWS
    n_blk = n_pad // MOE_ROWS
    rows_l = (2 * tm + N_EXPERTS * (SEG_ALIGN - 1) + MXU_DIM - 1) // MXU_DIM * MXU_DIM
    blk_start = jnp.arange(n_blk, dtype=jnp.int32) * MOE_ROWS
    blk_expert = jnp.minimum(jnp.sum(blk_start[:, None] >= pend[None, :], axis=-1), N_EXPERTS - 1)

    xs = _dispatch(seg, tail, route_t, f_lat, n_pad, rows_l, tm)
    ys = _experts(blk_expert.astype(jnp.int32), n_used.astype(jnp.int32), xs,
                  w_exp_gate[0], w_exp_up[0], w_exp_down[0])
    out = _combine(seg, route, h, mod3, final_norm_g.reshape(1, d), ys, rows_l, s, tm)
    return out.reshape(b, s, d)
```
